```python
import jax, jax.numpy as jnp
from jax import lax
import numpy as np

D_MODEL = 1024
BATCH = 4
SEQ = 4096
DEPTH = 1

D_MIX = D_MODEL
POOL_WIDTH = D_MIX // 2
POOL_WINDOWS = (2, 4, 8, 16)
POOL_GROUPS = len(POOL_WINDOWS)
POOL_GROUP_DIM = POOL_WIDTH // POOL_GROUPS
GLA_WIDTH = D_MIX - POOL_WIDTH
GLA_HEADS = 4
GLA_DK_TOTAL = GLA_WIDTH // 2
GLA_DK = GLA_DK_TOTAL // GLA_HEADS
GLA_DV = GLA_WIDTH // GLA_HEADS
GLA_GATE_RANK = 16
GLA_GATE_NORMALIZER = 16.0
GLA_CHUNK = 16
IN_SIZES = (POOL_WIDTH, GLA_DK_TOTAL, GLA_DK_TOTAL, GLA_WIDTH, GLA_WIDTH, GLA_GATE_RANK)
D_IN = sum(IN_SIZES)
N_EXPERTS = 256
TOP_K = 8
N_GROUPS = 8
TOPK_GROUPS = 4
EXPERT_DIM = D_MODEL // 4
SHARED_DIM = EXPERT_DIM
ROUTED_SCALE = 2.5
MOE_BLOCK = 128
DEEPNORM_ALPHA = (2.0 * DEPTH) ** 0.25
DEEPNORM_BETA = (8.0 * DEPTH) ** -0.25
LN_EPS = 1e-5
RMS_EPS = 1e-5

kernel_name = 'hybrid_pool_gla_moe_deepnorm'

F32 = jnp.float32


def _layernorm(x, g, b):
    xf = x.astype(F32)
    mu = jnp.mean(xf, axis=-1, keepdims=True)
    var = jnp.mean(jnp.square(xf - mu), axis=-1, keepdims=True)
    return ((xf - mu) * lax.rsqrt(var + LN_EPS) * g.astype(F32) + b.astype(F32)).astype(x.dtype)


def _pool_mixer(p, pool_w_group, pool_scale):
    B, S, _ = p.shape
    pf = p.astype(F32).reshape(B, S, POOL_GROUPS, POOL_GROUP_DIM)
    c0 = jnp.concatenate([jnp.zeros((B, 1, POOL_GROUPS, POOL_GROUP_DIM), F32), jnp.cumsum(pf, axis=1)], axis=1)
    t = jnp.arange(S)
    means = []
    for g, w in enumerate(POOL_WINDOWS):
        cg = c0[:, :, g]
        lower = jnp.concatenate([jnp.zeros((B, w - 1, POOL_GROUP_DIM), F32), cg[:, :S + 1 - w]], axis=1)
        cnt = jnp.minimum(t + 1, w).astype(F32)
        means.append((cg[:, 1:] - lower) / cnt[None, :, None])
    mixed = jnp.stack(means, axis=2) - pf
    out = jnp.einsum('bsgc,gcd->bsgd', mixed, pool_w_group.astype(F32))
    out = out * pool_scale.astype(F32).reshape(POOL_GROUPS, POOL_GROUP_DIM)
    return out.reshape(B, S, POOL_WIDTH)


def _gla_mixer(q, k, v, r, g_low, gate_w2, gate_b, norm_w):
    B, S, _ = q.shape
    H, C = GLA_HEADS, GLA_CHUNK
    N = S // C
    q = q.astype(F32).reshape(B, N, C, H, GLA_DK) * (GLA_DK ** -0.5)
    k = k.astype(F32).reshape(B, N, C, H, GLA_DK)
    v = v.astype(F32).reshape(B, N, C, H, GLA_DV)
    gk = jax.nn.log_sigmoid(g_low.astype(F32) @ gate_w2.astype(F32) + gate_b.astype(F32)) / GLA_GATE_NORMALIZER
    b = jnp.cumsum(gk.reshape(B, N, C, H, GLA_DK), axis=2)
    causal = jnp.tril(jnp.ones((C, C), dtype=bool))
    diff = b[:, :, :, None] - b[:, :, None, :]
    decay = jnp.exp(jnp.where(causal[None, None, :, :, None, None], diff, -jnp.inf))
    scores = jnp.einsum('bnihd,bnjhd,bnijhd->bnhij', q, k, decay)
    o_intra = jnp.einsum('bnhij,bnjhe->bnihe', scores, v)
    b_last = b[:, :, -1]
    q_dec = q * jnp.exp(b)
    k_dec = k * jnp.exp(b_last[:, :, None] - b)
    kv = jnp.einsum('bnjhd,bnjhe->bnhde', k_dec, v)
    chunk_decay = jnp.exp(b_last)

    def step(state, inp):
        q_c, kv_c, dec_c = inp
        o = jnp.einsum('bihd,bhde->bihe', q_c, state)
        return state * dec_c[..., None] + kv_c, o

    state0 = jnp.zeros((B, H, GLA_DK, GLA_DV), F32)
    _, o_inter = lax.scan(step, state0, (jnp.moveaxis(q_dec, 1, 0), jnp.moveaxis(kv, 1, 0), jnp.moveaxis(chunk_decay, 1, 0)))
    o = (o_intra + jnp.moveaxis(o_inter, 0, 1)).reshape(B, S, H, GLA_DV)
    o = o * lax.rsqrt(jnp.mean(jnp.square(o), axis=-1, keepdims=True) + RMS_EPS) * norm_w.astype(F32)
    return o.reshape(B, S, GLA_WIDTH) * jax.nn.silu(r.astype(F32))


def _swiglu(x, wg, wu, wd):
    return (jax.nn.silu(x @ wg) * (x @ wu)) @ wd


def _route(h_flat, router_w, router_bias):
    T = h_flat.shape[0]
    scores = jax.nn.sigmoid(h_flat.astype(F32) @ router_w.astype(F32))
    biased = scores + router_bias.astype(F32)
    grp = biased.reshape(T, N_GROUPS, N_EXPERTS // N_GROUPS)
    grp_score = jnp.sum(lax.top_k(grp, 2)[0], axis=-1)
    _, top_grp = lax.top_k(grp_score, TOPK_GROUPS)
    grp_mask = jnp.sum(jax.nn.one_hot(top_grp, N_GROUPS, dtype=F32), axis=1) > 0
    expert_mask = jnp.repeat(grp_mask, N_EXPERTS // N_GROUPS, axis=1)
    _, idx = lax.top_k(jnp.where(expert_mask, biased, -jnp.inf), TOP_K)
    w = jnp.take_along_axis(scores, idx, axis=1)
    w = w / jnp.sum(w, axis=-1, keepdims=True) * ROUTED_SCALE
    return idx.astype(jnp.int32), w


def _routed_experts(h_flat, idx, w, wg, wu, wd):
    T, D = h_flat.shape
    A = T * TOP_K
    flat_e = idx.reshape(A)
    flat_tok = jnp.arange(A, dtype=jnp.int32) // TOP_K
    flat_w = w.reshape(A)
    order = jnp.argsort(flat_e)
    se, st, sw = flat_e[order], flat_tok[order], flat_w[order]
    counts = jnp.zeros((N_EXPERTS,), jnp.int32).at[flat_e].add(1)
    padded = (counts + MOE_BLOCK - 1) // MOE_BLOCK * MOE_BLOCK
    group_start = jnp.cumsum(counts) - counts
    padded_end = jnp.cumsum(padded)
    padded_start = padded_end - padded
    pos = padded_start[se] + jnp.arange(A, dtype=jnp.int32) - group_start[se]
    n_blocks = (A + N_EXPERTS * (MOE_BLOCK - 1)) // MOE_BLOCK
    P = n_blocks * MOE_BLOCK
    row_tok = jnp.full((P,), T, jnp.int32).at[pos].set(st)
    row_w = jnp.zeros((P,), F32).at[pos].set(sw)
    block_rows = jnp.arange(n_blocks, dtype=jnp.int32) * MOE_BLOCK
    block_e = jnp.minimum(jnp.searchsorted(padded_end, block_rows, side='right'), N_EXPERTS - 1).astype(jnp.int32)
    h_pad = jnp.concatenate([h_flat, jnp.zeros((1, D), h_flat.dtype)], axis=0)

    def block_fn(args):
        tok, e = args
        return _swiglu(h_pad[tok], wg[e], wu[e], wd[e])

    out = lax.map(block_fn, (row_tok.reshape(n_blocks, MOE_BLOCK), block_e))
    out = out.reshape(P, D).astype(F32) * row_w[:, None]
    return jax.ops.segment_sum(out, row_tok, num_segments=T + 1)[:T].astype(h_flat.dtype)


def setup_inputs(seed: int = 0) -> dict:
    key = jax.random.key(seed)
    ks = jax.random.split(key, 20)

    def nrm(k, shape, scale):
        return jax.random.normal(k, shape, F32) * scale

    beta = DEEPNORM_BETA
    col_scale = jnp.concatenate([
        jnp.full((POOL_WIDTH,), beta, F32), jnp.ones((2 * GLA_DK_TOTAL,), F32),
        jnp.full((GLA_WIDTH,), beta, F32), jnp.ones((GLA_WIDTH + GLA_GATE_RANK,), F32)])
    return {
        'x': nrm(ks[0], (BATCH, SEQ, D_MODEL), 1.0),
        'w_in': nrm(ks[1], (DEPTH, D_MODEL, D_IN), D_MODEL ** -0.5) * col_scale,
        'gla_gate_w2': nrm(ks[2], (DEPTH, GLA_GATE_RANK, GLA_DK_TOTAL), GLA_GATE_RANK ** -0.5),
        'gla_gate_b': nrm(ks[3], (DEPTH, GLA_DK_TOTAL), 0.1),
        'gla_norm_w': 1.0 + nrm(ks[4], (DEPTH, GLA_DV), 0.02),
        'pool_w_group': nrm(ks[5], (DEPTH, POOL_GROUPS, POOL_GROUP_DIM, POOL_GROUP_DIM), POOL_GROUP_DIM ** -0.5),
        'pool_scale': 1.0 + nrm(ks[6], (DEPTH, POOL_WIDTH), 0.1),
        'w_out': nrm(ks[7], (DEPTH, D_MIX, D_MODEL), D_MIX ** -0.5 * beta),
        'ln1_g': 1.0 + nrm(ks[8], (DEPTH, D_MODEL), 0.02),
        'ln1_b': nrm(ks[9], (DEPTH, D_MODEL), 0.02),
        'router_w': nrm(ks[10], (DEPTH, D_MODEL, N_EXPERTS), D_MODEL ** -0.5),
        'router_bias': nrm(ks[11], (DEPTH, N_EXPERTS), 0.01),
        'w_exp_gate': nrm(ks[12], (DEPTH, N_EXPERTS, D_MODEL, EXPERT_DIM), D_MODEL ** -0.5 * beta),
        'w_exp_up': nrm(ks[13], (DEPTH, N_EXPERTS, D_MODEL, EXPERT_DIM), D_MODEL ** -0.5 * beta),
        'w_exp_down': nrm(ks[14], (DEPTH, N_EXPERTS, EXPERT_DIM, D_MODEL), EXPERT_DIM ** -0.5 * beta),
        'w_sh_gate': nrm(ks[15], (DEPTH, D_MODEL, SHARED_DIM), D_MODEL ** -0.5 * beta),
        'w_sh_up': nrm(ks[16], (DEPTH, D_MODEL, SHARED_DIM), D_MODEL ** -0.5 * beta),
        'w_sh_down': nrm(ks[17], (DEPTH, SHARED_DIM, D_MODEL), SHARED_DIM ** -0.5 * beta),
        'ln2_g': 1.0 + nrm(ks[18], (DEPTH, D_MODEL), 0.02),
        'ln2_b': nrm(ks[19], (DEPTH, D_MODEL), 0.02),
    }


def reference(x, w_in, gla_gate_w2, gla_gate_b, gla_norm_w, pool_w_group, pool_scale, w_out, ln1_g, ln1_b,
              router_w, router_bias, w_exp_gate, w_exp_up, w_exp_down, w_sh_gate, w_sh_up, w_sh_down, ln2_g, ln2_b):
    B, S, D = x.shape
    offsets = np.cumsum(IN_SIZES)[:-1].tolist()
    h = x
    for l in range(DEPTH):
        proj = h @ w_in[l]
        p, q, k, v, r, g_low = jnp.split(proj, offsets, axis=-1)
        pool_out = _pool_mixer(p, pool_w_group[l], pool_scale[l])
        gla_out = _gla_mixer(q, k, v, r, g_low, gla_gate_w2[l], gla_gate_b[l], gla_norm_w[l])
        mixed = jnp.concatenate([pool_out, gla_out], axis=-1).astype(h.dtype) @ w_out[l]
        h = _layernorm(DEEPNORM_ALPHA * h + mixed, ln1_g[l], ln1_b[l])
        h_flat = h.reshape(B * S, D)
        idx, gate = _route(h_flat, router_w[l], router_bias[l])
        routed = _routed_experts(h_flat, idx, gate.astype(h.dtype), w_exp_gate[l], w_exp_up[l], w_exp_down[l])
        shared = _swiglu(h_flat, w_sh_gate[l], w_sh_up[l], w_sh_down[l])
        h = _layernorm(DEEPNORM_ALPHA * h + (routed + shared).reshape(B, S, D), ln2_g[l], ln2_b[l])
    return h
```

```python
import functools

import jax
import jax.numpy as jnp
from jax import lax
from jax.experimental import pallas as pl
from jax.experimental.pallas import tpu as pltpu

F32 = jnp.float32
BF16 = jnp.bfloat16
I32 = jnp.int32

POOL_WINDOWS = (2, 4, 8, 16)
POOL_GROUP_DIM = 128
POOL_WIDTH = 512
GLA_HEADS = 4
GLA_DK = 64
GLA_DV = 128
GLA_DK_TOTAL = 256
GLA_WIDTH = 512
GLA_GATE_RANK = 16
GLA_GATE_NORMALIZER = 16.0
GLA_CHUNK = 16
N_EXPERTS = 256
TOP_K = 8
N_GROUPS = 8
GROUP_SIZE = N_EXPERTS // N_GROUPS
TOPK_GROUPS = 4
ROUTED_SCALE = 2.5
DEPTH = 1
DEEPNORM_ALPHA = (2.0 * DEPTH) ** 0.25
LN_EPS = 1e-5
RMS_EPS = 1e-5

LANES = 128
SUBLANES = 8
VMEM_LIMIT = 56 * 1024 * 1024

PROJ_TM = 512
MIX_TS = 256
ROUTE_TM = 256
MOE_BM = 128
COMB_TM = 128
D_IN_PAD = 2944
GATE_COL_BLOCK = 16


def _cparams(sem):
    return pltpu.CompilerParams(dimension_semantics=sem, vmem_limit_bytes=VMEM_LIMIT)


def _silu(x):
    return x * (1.0 / (1.0 + jnp.exp(-x)))


def _layernorm(y, g, b):
    mu = jnp.mean(y, axis=-1, keepdims=True)
    yc = y - mu
    var = jnp.mean(yc * yc, axis=-1, keepdims=True)
    return yc * lax.rsqrt(var + LN_EPS) * g + b


def _inproj_body(x_ref, w_ref, o_ref):
    o_ref[...] = jnp.dot(x_ref[...].astype(BF16), w_ref[...], preferred_element_type=F32)


def _inproj(x2d, w_bf):
    t, d = x2d.shape
    n = w_bf.shape[1]
    return pl.pallas_call(
        _inproj_body,
        out_shape=jax.ShapeDtypeStruct((t, n), F32),
        grid=(t // PROJ_TM,),
        in_specs=[pl.BlockSpec((PROJ_TM, d), lambda i: (i, 0)),
                  pl.BlockSpec((d, n), lambda i: (0, 0))],
        out_specs=pl.BlockSpec((PROJ_TM, n), lambda i: (i, 0)),
        compiler_params=_cparams(("arbitrary",)),
        name="inproj",
    )(x2d, w_bf)


def _mixer_body(p_ref, q_ref, k_ref, v_ref, r_ref, gl_ref, w2_ref, gb_ref, nw_ref, pw_ref, ps_ref,
                o_ref, pbuf, state):
    ts = p_ref.shape[0]
    s_idx = pl.program_id(1)
    halo = POOL_WINDOWS[-1]

    @pl.when(s_idx == 0)
    def _():
        pbuf[pl.ds(0, halo), :] = jnp.zeros((halo, POOL_WIDTH), F32)
        state[...] = jnp.zeros(state.shape, F32)

    p = p_ref[...]
    pbuf[pl.ds(halo, ts), :] = p
    pos = s_idx * ts + lax.broadcasted_iota(I32, (ts, 1), 0)
    for g, w in enumerate(POOL_WINDOWS):
        c0 = g * POOL_GROUP_DIM
        acc = pbuf[pl.ds(halo, ts), pl.ds(c0, POOL_GROUP_DIM)]
        for j in range(1, w):
            acc = acc + pbuf[pl.ds(halo - j, ts), pl.ds(c0, POOL_GROUP_DIM)]
        cnt = jnp.minimum(pos + 1, w).astype(F32)
        mixed = acc / cnt - p[:, c0:c0 + POOL_GROUP_DIM]
        og = jnp.dot(mixed.astype(BF16), pw_ref[g], preferred_element_type=F32)
        o_ref[:, pl.ds(c0, POOL_GROUP_DIM)] = (og * ps_ref[:, pl.ds(c0, POOL_GROUP_DIM)]).astype(o_ref.dtype)
    pbuf[pl.ds(0, halo), :] = pbuf[pl.ds(ts, halo), :]

    nchunk = ts // GLA_CHUNK
    glog = jnp.dot(gl_ref[...].astype(BF16), w2_ref[...], preferred_element_type=F32) + gb_ref[...]
    gk = (jnp.minimum(glog, 0.0) - jnp.log(1.0 + jnp.exp(-jnp.abs(glog)))) * (1.0 / GLA_GATE_NORMALIZER)
    row = lax.broadcasted_iota(I32, (ts, 1), 0)
    rin = row % GLA_CHUNK
    b = gk
    sh = 1
    while sh < GLA_CHUNK:
        b = b + jnp.where(rin >= sh, pltpu.roll(b, sh, axis=0), 0.0)
        sh *= 2
    b3 = b.reshape(nchunk, GLA_CHUNK, GLA_DK_TOTAL)
    bmid = b3[:, GLA_CHUNK // 2 - 1:GLA_CHUNK // 2, :]
    blast = b3[:, GLA_CHUNK - 1:GLA_CHUNK, :]
    q3 = (q_ref[...] * (GLA_DK ** -0.5)).reshape(nchunk, GLA_CHUNK, GLA_DK_TOTAL)
    k3 = k_ref[...].reshape(nchunk, GLA_CHUNK, GLA_DK_TOTAL)
    qs = (q3 * jnp.exp(b3 - bmid)).reshape(ts, GLA_DK_TOTAL)
    ks = (k3 * jnp.exp(bmid - b3)).reshape(ts, GLA_DK_TOTAL)
    qd = (q3 * jnp.exp(b3)).reshape(ts, GLA_DK_TOTAL)
    kd = (k3 * jnp.exp(blast - b3)).reshape(ts, GLA_DK_TOTAL).astype(BF16)
    cdec = jnp.exp(blast).reshape(nchunk, GLA_DK_TOTAL)

    v = v_ref[...]
    vb = v.astype(BF16)
    lane = lax.broadcasted_iota(I32, (1, LANES), 1)
    head_lane = [lane < GLA_DK, lane >= GLA_DK]
    prow = lax.broadcasted_iota(I32, (LANES, 1), 0)
    head_row = prow < GLA_DK

    blk = LANES
    ri = lax.broadcasted_iota(I32, (blk, blk), 0)
    ci = lax.broadcasted_iota(I32, (blk, blk), 1)
    causal = (ri // GLA_CHUNK == ci // GLA_CHUNK) & (ri >= ci)
    o_intra = [[None] * (ts // blk) for _ in range(GLA_HEADS)]
    for rb in range(ts // blk):
        rs = slice(rb * blk, (rb + 1) * blk)
        for pair in range(GLA_HEADS // 2):
            ls = slice(pair * LANES, (pair + 1) * LANES)
            ks_p = ks[rs, ls].astype(BF16)
            for sub in range(2):
                h = pair * 2 + sub
                q_m = jnp.where(head_lane[sub], qs[rs, ls], 0.0).astype(BF16)
                sc = lax.dot_general(q_m, ks_p, (((1,), (1,)), ((), ())), preferred_element_type=F32)
                sc = jnp.where(causal, sc, 0.0).astype(BF16)
                o_intra[h][rb] = jnp.dot(sc, vb[rs, h * GLA_DV:(h + 1) * GLA_DV], preferred_element_type=F32)

    o_inter = [[None] * nchunk for _ in range(GLA_HEADS)]
    for pair in range(GLA_HEADS // 2):
        ls = slice(pair * LANES, (pair + 1) * LANES)
        dec_cols = cdec[:, ls].T
        st = state[pair]
        for c in range(nchunk):
            rs = slice(c * GLA_CHUNK, (c + 1) * GLA_CHUNK)
            st_b = st.astype(BF16)
            kv = []
            for sub in range(2):
                h = pair * 2 + sub
                q_m = jnp.where(head_lane[sub], qd[rs, ls], 0.0).astype(BF16)
                o_inter[h][c] = jnp.dot(q_m, st_b, preferred_element_type=F32)
                kv.append(lax.dot_general(kd[rs, ls], vb[rs, h * GLA_DV:(h + 1) * GLA_DV],
                                          (((0,), (0,)), ((), ())), preferred_element_type=F32))
            st = st * dec_cols[:, c:c + 1] + jnp.where(head_row, kv[0], kv[1])
        state[pair] = st

    nw = nw_ref[...]
    r = r_ref[...]
    for h in range(GLA_HEADS):
        o = jnp.concatenate(o_intra[h], axis=0) + jnp.concatenate(o_inter[h], axis=0)
        o = o * lax.rsqrt(jnp.mean(o * o, axis=-1, keepdims=True) + RMS_EPS) * nw
        o = o * _silu(r[:, h * GLA_DV:(h + 1) * GLA_DV])
        o_ref[:, pl.ds(POOL_WIDTH + h * GLA_DV, GLA_DV)] = o.astype(o_ref.dtype)


def _mixer(proj, batch, seq, w2p, gate_b, norm_w, pool_w, pool_scale):
    t = proj.shape[0]
    ts = MIX_TS
    nseq = seq // ts

    def rows(width, col_block):
        return pl.BlockSpec((ts, width), lambda bi, si: (bi * nseq + si, col_block))

    def full(shape):
        return pl.BlockSpec(shape, lambda bi, si: (0,) * len(shape))

    return pl.pallas_call(
        _mixer_body,
        out_shape=jax.ShapeDtypeStruct((t, POOL_WIDTH + GLA_WIDTH), BF16),
        grid=(batch, nseq),
        in_specs=[rows(POOL_WIDTH, 0),
                  rows(GLA_DK_TOTAL, 2),
                  rows(GLA_DK_TOTAL, 3),
                  rows(GLA_WIDTH, 2),
                  rows(GLA_WIDTH, 3),
                  rows(LANES, GATE_COL_BLOCK),
                  full(w2p.shape), full(gate_b.shape), full(norm_w.shape),
                  full(pool_w.shape), full(pool_scale.shape)],
        out_specs=pl.BlockSpec((ts, POOL_WIDTH + GLA_WIDTH), lambda bi, si: (bi * nseq + si, 0)),
        scratch_shapes=[pltpu.VMEM((ts + POOL_WINDOWS[-1], POOL_WIDTH), F32),
                        pltpu.VMEM((GLA_HEADS // 2, LANES, GLA_DV), F32)],
        compiler_params=_cparams(("arbitrary", "arbitrary")),
        name="mixer",
    )(proj, proj, proj, proj, proj, proj, w2p, gate_b, norm_w, pool_w, pool_scale)


def _outproj_body(m_ref, w_ref, x_ref, g_ref, b_ref, h_ref, h8_ref):
    tm, d = x_ref.shape
    y = DEEPNORM_ALPHA * x_ref[...] + jnp.dot(m_ref[...], w_ref[...], preferred_element_type=F32)
    h = _layernorm(y, g_ref[...], b_ref[...])
    h_ref[...] = h
    for s in range(d // LANES):
        h8_ref[pl.ds(s, tm, stride=d // LANES), :] = h[:, s * LANES:(s + 1) * LANES]


def _outproj_ln(mixed, w_bf, x2d, g, b):
    t, d = x2d.shape
    tm = PROJ_TM
    per_row = d // LANES
    return pl.pallas_call(
        _outproj_body,
        out_shape=(jax.ShapeDtypeStruct((t, d), F32), jax.ShapeDtypeStruct((t * per_row, LANES), F32)),
        grid=(t // tm,),
        in_specs=[pl.BlockSpec((tm, d), lambda i: (i, 0)),
                  pl.BlockSpec((d, d), lambda i: (0, 0)),
                  pl.BlockSpec((tm, d), lambda i: (i, 0)),
                  pl.BlockSpec((1, d), lambda i: (0, 0)),
                  pl.BlockSpec((1, d), lambda i: (0, 0))],
        out_specs=(pl.BlockSpec((tm, d), lambda i: (i, 0)),
                   pl.BlockSpec((tm * per_row, LANES), lambda i: (i, 0))),
        compiler_params=_cparams(("arbitrary",)),
        name="outproj_ln",
    )(mixed, w_bf, x2d, g, b)


def _first_argmax_rows(val, rowf, nrows):
    m = jnp.max(val, axis=0, keepdims=True)
    first = jnp.min(jnp.where(val == m, rowf, float(nrows)), axis=0, keepdims=True)
    return m, first, rowf == first


def _router_body(h_ref, wt_ref, bias_ref, idx_ref, gate_ref, rank_ref, cnt_ref, carry):
    tm = h_ref.shape[0]
    i = pl.program_id(0)

    @pl.when(i == 0)
    def _():
        carry[...] = jnp.zeros(carry.shape, F32)

    logits = lax.dot_general(wt_ref[...], h_ref[...], (((1,), (1,)), ((), ())),
                             preferred_element_type=F32, precision=lax.Precision.HIGHEST)
    scores = 1.0 / (1.0 + jnp.exp(-logits))
    biased = scores + bias_ref[...]
    neg = -jnp.inf

    grp = biased.reshape(N_GROUPS, GROUP_SIZE, tm)
    gi = lax.broadcasted_iota(I32, (N_GROUPS, GROUP_SIZE, tm), 1).astype(F32)
    g1 = jnp.max(grp, axis=1, keepdims=True)
    f1 = jnp.min(jnp.where(grp == g1, gi, float(GROUP_SIZE)), axis=1, keepdims=True)
    g2 = jnp.max(jnp.where(gi == f1, neg, grp), axis=1, keepdims=True)
    gscore = (g1 + g2).reshape(N_GROUPS, tm)

    growf = lax.broadcasted_iota(I32, (N_GROUPS, tm), 0).astype(F32)
    gsel = jnp.zeros((N_GROUPS, tm), F32)
    gval = gscore
    for _ in range(TOPK_GROUPS):
        _, _, pick = _first_argmax_rows(gval, growf, N_GROUPS)
        gsel = jnp.where(pick, 1.0, gsel)
        gval = jnp.where(pick, neg, gval)
    emask = jnp.broadcast_to(gsel.reshape(N_GROUPS, 1, tm), (N_GROUPS, GROUP_SIZE, tm)).reshape(N_EXPERTS, tm)

    rowf = lax.broadcasted_iota(I32, (N_EXPERTS, tm), 0).astype(F32)
    val = jnp.where(emask > 0.0, biased, neg)
    onehot = jnp.zeros((N_EXPERTS, tm), F32)
    picks, idxs, ws = [], [], []
    for _ in range(TOP_K):
        _, first, pick = _first_argmax_rows(val, rowf, N_EXPERTS)
        picks.append(pick)
        idxs.append(first)
        ws.append(jnp.sum(jnp.where(pick, scores, 0.0), axis=0, keepdims=True))
        onehot = jnp.where(pick, 1.0, onehot)
        val = jnp.where(pick, neg, val)
    w = jnp.concatenate(ws, axis=0)
    gate_ref[...] = w / jnp.sum(w, axis=0, keepdims=True) * ROUTED_SCALE
    idx_ref[...] = jnp.concatenate(idxs, axis=0).astype(I32)

    ti = lax.broadcasted_iota(I32, (tm, tm), 0)
    tj = lax.broadcasted_iota(I32, (tm, tm), 1)
    upper = jnp.where(ti < tj, 1.0, 0.0).astype(BF16)
    prefix = jnp.dot(onehot.astype(BF16), upper, preferred_element_type=F32) + carry[...]
    ranks = [jnp.sum(jnp.where(pk, prefix, 0.0), axis=0, keepdims=True) for pk in picks]
    rank_ref[...] = jnp.concatenate(ranks, axis=0).astype(I32)
    carry[...] = carry[...] + jnp.sum(onehot, axis=1, keepdims=True)
    cnt_ref[...] = carry[...]


def _router(h, wt, bias_col):
    t, d = h.shape
    tm = ROUTE_TM
    return pl.pallas_call(
        _router_body,
        out_shape=(jax.ShapeDtypeStruct((TOP_K, t), I32), jax.ShapeDtypeStruct((TOP_K, t), F32),
                   jax.ShapeDtypeStruct((TOP_K, t), I32), jax.ShapeDtypeStruct((N_EXPERTS, 1), F32)),
        grid=(t // tm,),
        in_specs=[pl.BlockSpec((tm, d), lambda i: (i, 0)),
                  pl.BlockSpec((N_EXPERTS, d), lambda i: (0, 0)),
                  pl.BlockSpec((N_EXPERTS, 1), lambda i: (0, 0))],
        out_specs=(pl.BlockSpec((TOP_K, tm), lambda i: (0, i)),
                   pl.BlockSpec((TOP_K, tm), lambda i: (0, i)),
                   pl.BlockSpec((TOP_K, tm), lambda i: (0, i)),
                   pl.BlockSpec((N_EXPERTS, 1), lambda i: (0, 0))),
        scratch_shapes=[pltpu.VMEM((N_EXPERTS, 1), F32)],
        compiler_params=_cparams(("arbitrary",)),
        name="router",
    )(h, wt, bias_col)


def _expert_row_copy(h8_ref, buf, sem, slot, tok, r, per_row):
    src = h8_ref.at[pl.ds(pl.multiple_of(tok * per_row, per_row), per_row), :]
    dst = buf.at[slot, pl.ds(r * per_row, per_row), :]
    return pltpu.make_async_copy(src, dst, sem.at[slot])


def _experts_body(be_ref, tok_cur, tok_nxt, h8_ref, wg_ref, wu_ref, wd_ref, y8_ref,
                  buf, sem, wg_b, wu_b, wd_b):
    i = pl.program_id(0)
    nb = pl.num_programs(0)
    bm = MOE_BM
    per_row = buf.shape[1] // bm
    slot = i % 2

    def issue(tok_ref, sl):
        for r in range(bm):
            _expert_row_copy(h8_ref, buf, sem, sl, tok_ref[0, 0, r], r, per_row).start()

    @pl.when(i == 0)
    def _():
        issue(tok_cur, 0)

    @pl.when(i + 1 < nb)
    def _():
        issue(tok_nxt, 1 - slot)

    prev_e = be_ref[jnp.maximum(i - 1, 0)]

    @pl.when((i == 0) | (be_ref[i] != prev_e))
    def _():
        wg_b[...] = wg_ref[0].astype(BF16)
        wu_b[...] = wu_ref[0].astype(BF16)
        wd_b[...] = wd_ref[0].astype(BF16)

    pltpu.make_async_copy(h8_ref.at[pl.ds(0, bm * per_row), :], buf.at[slot], sem.at[slot]).wait()
    x = jnp.concatenate([buf[slot, pl.ds(s, bm, stride=per_row), :] for s in range(per_row)], axis=1).astype(BF16)
    g = jnp.dot(x, wg_b[...], preferred_element_type=F32)
    u = jnp.dot(x, wu_b[...], preferred_element_type=F32)
    a = (_silu(g) * u).astype(BF16)
    y = jnp.dot(a, wd_b[...], preferred_element_type=F32)
    for s in range(per_row):
        y8_ref[pl.ds(s, bm, stride=per_row), :] = y[:, s * LANES:(s + 1) * LANES]


def _experts(block_e, row_tok3, h8, wg, wu, wd):
    nblk = block_e.shape[0]
    bm = MOE_BM
    e, d, de = wg.shape
    per_row = d // LANES
    grid_spec = pltpu.PrefetchScalarGridSpec(
        num_scalar_prefetch=1,
        grid=(nblk,),
        in_specs=[pl.BlockSpec((1, 1, bm), lambda i, be: (i, 0, 0), memory_space=pltpu.SMEM),
                  pl.BlockSpec((1, 1, bm), lambda i, be: (jnp.minimum(i + 1, nblk - 1), 0, 0),
                               memory_space=pltpu.SMEM),
                  pl.BlockSpec(memory_space=pl.ANY),
                  pl.BlockSpec((1, d, de), lambda i, be: (be[i], 0, 0)),
                  pl.BlockSpec((1, d, de), lambda i, be: (be[i], 0, 0)),
                  pl.BlockSpec((1, de, d), lambda i, be: (be[i], 0, 0))],
        out_specs=pl.BlockSpec((bm * per_row, LANES), lambda i, be: (i, 0)),
        scratch_shapes=[pltpu.VMEM((2, bm * per_row, LANES), F32),
                        pltpu.SemaphoreType.DMA((2,)),
                        pltpu.VMEM((d, de), BF16), pltpu.VMEM((d, de), BF16), pltpu.VMEM((de, d), BF16)],
    )
    return pl.pallas_call(
        _experts_body,
        out_shape=jax.ShapeDtypeStruct((nblk * bm * per_row, LANES), F32),
        grid_spec=grid_spec,
        compiler_params=_cparams(("arbitrary",)),
        name="experts",
    )(block_e, row_tok3, row_tok3, h8, wg, wu, wd)


def _combine_body(pos_cur, pos_nxt, y8_ref, h_ref, gate_ref, wsg_ref, wsu_ref, wsd_ref, g_ref, b_ref,
                  o_ref, buf, sem):
    i = pl.program_id(0)
    nb = pl.num_programs(0)
    tm, d = h_ref.shape
    per_row = d // LANES
    slot = i % 2
    tok_rows = TOP_K * per_row

    def issue(pos_ref, sl):
        def body(t, carry):
            for k in range(TOP_K):
                p = pos_ref[0, 0, t * TOP_K + k]
                src = y8_ref.at[pl.ds(pl.multiple_of(p * per_row, per_row), per_row), :]
                dst = buf.at[sl, pl.ds(pl.multiple_of((t * TOP_K + k) * per_row, per_row), per_row), :]
                pltpu.make_async_copy(src, dst, sem.at[sl]).start()
            return carry
        lax.fori_loop(0, tm, body, 0)

    @pl.when(i == 0)
    def _():
        issue(pos_cur, 0)

    @pl.when(i + 1 < nb)
    def _():
        issue(pos_nxt, 1 - slot)

    h = h_ref[...]
    hb = h.astype(BF16)
    sg = jnp.dot(hb, wsg_ref[...], preferred_element_type=F32)
    su = jnp.dot(hb, wsu_ref[...], preferred_element_type=F32)
    shared = jnp.dot((_silu(sg) * su).astype(BF16), wsd_ref[...], preferred_element_type=F32)

    pltpu.make_async_copy(y8_ref.at[pl.ds(0, tm * tok_rows), :], buf.at[slot], sem.at[slot]).wait()
    gates = gate_ref[...]
    cols = []
    for s in range(per_row):
        acc = None
        for k in range(TOP_K):
            term = gates[:, k:k + 1] * buf[slot, pl.ds(k * per_row + s, tm, stride=tok_rows), :]
            acc = term if acc is None else acc + term
        cols.append(acc)
    routed = jnp.concatenate(cols, axis=1)
    o_ref[...] = _layernorm(DEEPNORM_ALPHA * h + (routed + shared), g_ref[...], b_ref[...])


def _combine(pos3, y8, h, gates, wsg, wsu, wsd, g, b):
    t, d = h.shape
    tm = COMB_TM
    nt = t // tm
    per_row = d // LANES
    ds_ = wsg.shape[1]
    return pl.pallas_call(
        _combine_body,
        out_shape=jax.ShapeDtypeStruct((t, d), F32),
        grid=(nt,),
        in_specs=[pl.BlockSpec((1, 1, tm * TOP_K), lambda i: (i, 0, 0), memory_space=pltpu.SMEM),
                  pl.BlockSpec((1, 1, tm * TOP_K), lambda i: (jnp.minimum(i + 1, nt - 1), 0, 0),
                               memory_space=pltpu.SMEM),
                  pl.BlockSpec(memory_space=pl.ANY),
                  pl.BlockSpec((tm, d), lambda i: (i, 0)),
                  pl.BlockSpec((tm, TOP_K), lambda i: (i, 0)),
                  pl.BlockSpec((d, ds_), lambda i: (0, 0)),
                  pl.BlockSpec((d, ds_), lambda i: (0, 0)),
                  pl.BlockSpec((ds_, d), lambda i: (0, 0)),
                  pl.BlockSpec((1, d), lambda i: (0, 0)),
                  pl.BlockSpec((1, d), lambda i: (0, 0))],
        out_specs=pl.BlockSpec((tm, d), lambda i: (i, 0)),
        scratch_shapes=[pltpu.VMEM((2, tm * TOP_K * per_row, LANES), F32),
                        pltpu.SemaphoreType.DMA((2,))],
        compiler_params=_cparams(("arbitrary",)),
        name="combine",
    )(pos3, pos3, y8, h, gates, wsg, wsu, wsd, g, b)


def _routing_tables(idx_t, rank_t, counts, t):
    bm = MOE_BM
    a = t * TOP_K
    nblk = (a + N_EXPERTS * (bm - 1)) // bm
    cnt = counts.reshape(N_EXPERTS).astype(I32)
    padded = (cnt + bm - 1) // bm * bm
    padded_end = jnp.cumsum(padded)
    padded_start = padded_end - padded
    pos_t = padded_start[idx_t] + rank_t
    tok = jnp.broadcast_to(jnp.arange(t, dtype=I32)[None, :], (TOP_K, t))
    row_tok = jnp.zeros((nblk * bm,), I32).at[pos_t.reshape(-1)].set(tok.reshape(-1))
    block_rows = jnp.arange(nblk, dtype=I32) * bm
    block_e = jnp.minimum(jnp.searchsorted(padded_end, block_rows, side='right'), N_EXPERTS - 1).astype(I32)
    pos_flat = pos_t.T.reshape(t // COMB_TM, 1, COMB_TM * TOP_K)
    return block_e, row_tok.reshape(nblk, 1, bm), pos_flat


def kernel(x, w_in, gla_gate_w2, gla_gate_b, gla_norm_w, pool_w_group, pool_scale, w_out, ln1_g, ln1_b,
           router_w, router_bias, w_exp_gate, w_exp_up, w_exp_down, w_sh_gate, w_sh_up, w_sh_down, ln2_g, ln2_b):
    batch, seq, d = x.shape
    t = batch * seq
    h2d = x.reshape(t, d)
    for l in range(DEPTH):
        d_in = w_in.shape[2]
        w_in_b = jnp.pad(w_in[l], ((0, 0), (0, D_IN_PAD - d_in))).astype(BF16)
        w2p = jnp.pad(gla_gate_w2[l], ((0, LANES - GLA_GATE_RANK), (0, 0))).astype(BF16)
        proj = _inproj(h2d, w_in_b)
        mixed = _mixer(proj, batch, seq, w2p, gla_gate_b[l].reshape(1, -1), gla_norm_w[l].reshape(1, -1),
                       pool_w_group[l].astype(BF16), pool_scale[l].reshape(1, -1))
        h, h8 = _outproj_ln(mixed, w_out[l].astype(BF16), h2d, ln1_g[l].reshape(1, -1), ln1_b[l].reshape(1, -1))
        idx_t, gate_t, rank_t, counts = _router(h, router_w[l].T, router_bias[l].reshape(-1, 1))
        block_e, row_tok3, pos3 = _routing_tables(idx_t, rank_t, counts, t)
        y8 = _experts(block_e, row_tok3, h8, w_exp_gate[l], w_exp_up[l], w_exp_down[l])
        h2d = _combine(pos3, y8, h, gate_t.T, w_sh_gate[l].astype(BF16), w_sh_up[l].astype(BF16),
                       w_sh_down[l].astype(BF16), ln2_g[l].reshape(1, -1), ln2_b[l].reshape(1, -1))
    return h2d.reshape(batch, seq, d)
```

```python
import functools

import jax
import jax.numpy as jnp
from jax import lax
from jax.experimental import pallas as pl
from jax.experimental.pallas import tpu as pltpu

F32 = jnp.float32
BF16 = jnp.bfloat16
I32 = jnp.int32

POOL_WINDOWS = (2, 4, 8, 16)
POOL_GROUP_DIM = 128
POOL_WIDTH = 512
GLA_HEADS = 4
GLA_DK = 64
GLA_DV = 128
GLA_DK_TOTAL = 256
GLA_WIDTH = 512
GLA_GATE_RANK = 16
GLA_GATE_NORMALIZER = 16.0
GLA_CHUNK = 16
N_EXPERTS = 256
TOP_K = 8
N_GROUPS = 8
GROUP_SIZE = N_EXPERTS // N_GROUPS
TOPK_GROUPS = 4
ROUTED_SCALE = 2.5
DEPTH = 1
DEEPNORM_ALPHA = (2.0 * DEPTH) ** 0.25
LN_EPS = 1e-5
RMS_EPS = 1e-5

LANES = 128
SUBLANES = 8
VMEM_LIMIT = 56 * 1024 * 1024

PROJ_TM = 512
MIX_TS = 256
ROUTE_TM = 256
MOE_BM = 128
COMB_TM = 128
POS_TM = 512
DISP_TM = 256
ROW_TILES = 8
D_IN_PAD = 2944
GATE_COL_BLOCK = 16


def _cparams(sem):
    return pltpu.CompilerParams(dimension_semantics=sem, vmem_limit_bytes=VMEM_LIMIT)


def _silu(x):
    return x * (1.0 / (1.0 + jnp.exp(-x)))


def _layernorm(y, g, b):
    mu = jnp.mean(y, axis=-1, keepdims=True)
    yc = y - mu
    var = jnp.mean(yc * yc, axis=-1, keepdims=True)
    return yc * lax.rsqrt(var + LN_EPS) * g + b


def _inproj_body(x_ref, w_ref, o_ref):
    o_ref[...] = jnp.dot(x_ref[...].astype(BF16), w_ref[...], preferred_element_type=F32)


def _inproj(x2d, w_bf):
    t, d = x2d.shape
    n = w_bf.shape[1]
    return pl.pallas_call(
        _inproj_body,
        out_shape=jax.ShapeDtypeStruct((t, n), F32),
        grid=(t // PROJ_TM,),
        in_specs=[pl.BlockSpec((PROJ_TM, d), lambda i: (i, 0)),
                  pl.BlockSpec((d, n), lambda i: (0, 0))],
        out_specs=pl.BlockSpec((PROJ_TM, n), lambda i: (i, 0)),
        compiler_params=_cparams(("arbitrary",)),
        name="inproj",
    )(x2d, w_bf)


def _mixer_body(p_ref, q_ref, k_ref, v_ref, r_ref, gl_ref, w2_ref, gb_ref, nw_ref, pw_ref, ps_ref,
                o_ref, pbuf, state):
    ts = p_ref.shape[0]
    s_idx = pl.program_id(1)
    halo = POOL_WINDOWS[-1]

    @pl.when(s_idx == 0)
    def _():
        pbuf[pl.ds(0, halo), :] = jnp.zeros((halo, POOL_WIDTH), F32)
        state[...] = jnp.zeros(state.shape, F32)

    p = p_ref[...]
    pbuf[pl.ds(halo, ts), :] = p
    pos = s_idx * ts + lax.broadcasted_iota(I32, (ts, 1), 0)
    for g, w in enumerate(POOL_WINDOWS):
        c0 = g * POOL_GROUP_DIM
        acc = pbuf[pl.ds(halo, ts), pl.ds(c0, POOL_GROUP_DIM)]
        for j in range(1, w):
            acc = acc + pbuf[pl.ds(halo - j, ts), pl.ds(c0, POOL_GROUP_DIM)]
        cnt = jnp.minimum(pos + 1, w).astype(F32)
        mixed = acc / cnt - p[:, c0:c0 + POOL_GROUP_DIM]
        og = jnp.dot(mixed.astype(BF16), pw_ref[g], preferred_element_type=F32)
        o_ref[:, pl.ds(c0, POOL_GROUP_DIM)] = (og * ps_ref[:, pl.ds(c0, POOL_GROUP_DIM)]).astype(o_ref.dtype)
    pbuf[pl.ds(0, halo), :] = pbuf[pl.ds(ts, halo), :]

    nchunk = ts // GLA_CHUNK
    glog = jnp.dot(gl_ref[...].astype(BF16), w2_ref[...], preferred_element_type=F32) + gb_ref[...]
    gk = (jnp.minimum(glog, 0.0) - jnp.log(1.0 + jnp.exp(-jnp.abs(glog)))) * (1.0 / GLA_GATE_NORMALIZER)
    row = lax.broadcasted_iota(I32, (ts, 1), 0)
    rin = row % GLA_CHUNK
    b = gk
    sh = 1
    while sh < GLA_CHUNK:
        b = b + jnp.where(rin >= sh, pltpu.roll(b, sh, axis=0), 0.0)
        sh *= 2
    b3 = b.reshape(nchunk, GLA_CHUNK, GLA_DK_TOTAL)
    bmid = b3[:, GLA_CHUNK // 2 - 1:GLA_CHUNK // 2, :]
    blast = b3[:, GLA_CHUNK - 1:GLA_CHUNK, :]
    q3 = (q_ref[...] * (GLA_DK ** -0.5)).reshape(nchunk, GLA_CHUNK, GLA_DK_TOTAL)
    k3 = k_ref[...].reshape(nchunk, GLA_CHUNK, GLA_DK_TOTAL)
    qs = (q3 * jnp.exp(b3 - bmid)).reshape(ts, GLA_DK_TOTAL)
    ks = (k3 * jnp.exp(bmid - b3)).reshape(ts, GLA_DK_TOTAL)
    qd = (q3 * jnp.exp(b3)).reshape(ts, GLA_DK_TOTAL)
    kd = (k3 * jnp.exp(blast - b3)).reshape(ts, GLA_DK_TOTAL).astype(BF16)
    cdec = jnp.exp(blast).reshape(nchunk, GLA_DK_TOTAL)

    v = v_ref[...]
    vb = v.astype(BF16)
    lane = lax.broadcasted_iota(I32, (1, LANES), 1)
    head_lane = [lane < GLA_DK, lane >= GLA_DK]
    prow = lax.broadcasted_iota(I32, (LANES, 1), 0)
    head_row = prow < GLA_DK

    blk = LANES
    ri = lax.broadcasted_iota(I32, (blk, blk), 0)
    ci = lax.broadcasted_iota(I32, (blk, blk), 1)
    causal = (ri // GLA_CHUNK == ci // GLA_CHUNK) & (ri >= ci)
    o_intra = [[None] * (ts // blk) for _ in range(GLA_HEADS)]
    for rb in range(ts // blk):
        rs = slice(rb * blk, (rb + 1) * blk)
        for pair in range(GLA_HEADS // 2):
            ls = slice(pair * LANES, (pair + 1) * LANES)
            ks_p = ks[rs, ls].astype(BF16)
            for sub in range(2):
                h = pair * 2 + sub
                q_m = jnp.where(head_lane[sub], qs[rs, ls], 0.0).astype(BF16)
                sc = lax.dot_general(q_m, ks_p, (((1,), (1,)), ((), ())), preferred_element_type=F32)
                sc = jnp.where(causal, sc, 0.0).astype(BF16)
                o_intra[h][rb] = jnp.dot(sc, vb[rs, h * GLA_DV:(h + 1) * GLA_DV], preferred_element_type=F32)

    o_inter = [[None] * nchunk for _ in range(GLA_HEADS)]
    for pair in range(GLA_HEADS // 2):
        ls = slice(pair * LANES, (pair + 1) * LANES)
        dec_cols = cdec[:, ls].T
        st = state[pair]
        for c in range(nchunk):
            rs = slice(c * GLA_CHUNK, (c + 1) * GLA_CHUNK)
            st_b = st.astype(BF16)
            kv = []
            for sub in range(2):
                h = pair * 2 + sub
                q_m = jnp.where(head_lane[sub], qd[rs, ls], 0.0).astype(BF16)
                o_inter[h][c] = jnp.dot(q_m, st_b, preferred_element_type=F32)
                kv.append(lax.dot_general(kd[rs, ls], vb[rs, h * GLA_DV:(h + 1) * GLA_DV],
                                          (((0,), (0,)), ((), ())), preferred_element_type=F32))
            st = st * dec_cols[:, c:c + 1] + jnp.where(head_row, kv[0], kv[1])
        state[pair] = st

    nw = nw_ref[...]
    r = r_ref[...]
    for h in range(GLA_HEADS):
        o = jnp.concatenate(o_intra[h], axis=0) + jnp.concatenate(o_inter[h], axis=0)
        o = o * lax.rsqrt(jnp.mean(o * o, axis=-1, keepdims=True) + RMS_EPS) * nw
        o = o * _silu(r[:, h * GLA_DV:(h + 1) * GLA_DV])
        o_ref[:, pl.ds(POOL_WIDTH + h * GLA_DV, GLA_DV)] = o.astype(o_ref.dtype)


def _mixer(proj, batch, seq, w2p, gate_b, norm_w, pool_w, pool_scale):
    t = proj.shape[0]
    ts = MIX_TS
    nseq = seq // ts

    def rows(width, col_block):
        return pl.BlockSpec((ts, width), lambda bi, si: (bi * nseq + si, col_block))

    def full(shape):
        return pl.BlockSpec(shape, lambda bi, si: (0,) * len(shape))

    return pl.pallas_call(
        _mixer_body,
        out_shape=jax.ShapeDtypeStruct((t, POOL_WIDTH + GLA_WIDTH), BF16),
        grid=(batch, nseq),
        in_specs=[rows(POOL_WIDTH, 0),
                  rows(GLA_DK_TOTAL, 2),
                  rows(GLA_DK_TOTAL, 3),
                  rows(GLA_WIDTH, 2),
                  rows(GLA_WIDTH, 3),
                  rows(LANES, GATE_COL_BLOCK),
                  full(w2p.shape), full(gate_b.shape), full(norm_w.shape),
                  full(pool_w.shape), full(pool_scale.shape)],
        out_specs=pl.BlockSpec((ts, POOL_WIDTH + GLA_WIDTH), lambda bi, si: (bi * nseq + si, 0)),
        scratch_shapes=[pltpu.VMEM((ts + POOL_WINDOWS[-1], POOL_WIDTH), F32),
                        pltpu.VMEM((GLA_HEADS // 2, LANES, GLA_DV), F32)],
        compiler_params=_cparams(("arbitrary", "arbitrary")),
        name="mixer",
    )(proj, proj, proj, proj, proj, proj, w2p, gate_b, norm_w, pool_w, pool_scale)


def _outproj_body(m_ref, w_ref, x_ref, g_ref, b_ref, h_ref, h8_ref):
    tm, d = x_ref.shape
    y = DEEPNORM_ALPHA * x_ref[...] + jnp.dot(m_ref[...], w_ref[...], preferred_element_type=F32)
    h = _layernorm(y, g_ref[...], b_ref[...])
    h_ref[...] = h
    for s in range(d // LANES):
        h8_ref[pl.ds(s, tm, stride=d // LANES), :] = h[:, s * LANES:(s + 1) * LANES]


def _outproj_ln(mixed, w_bf, x2d, g, b):
    t, d = x2d.shape
    tm = PROJ_TM
    per_row = d // LANES
    return pl.pallas_call(
        _outproj_body,
        out_shape=(jax.ShapeDtypeStruct((t, d), F32), jax.ShapeDtypeStruct((t * per_row, LANES), F32)),
        grid=(t // tm,),
        in_specs=[pl.BlockSpec((tm, d), lambda i: (i, 0)),
                  pl.BlockSpec((d, d), lambda i: (0, 0)),
                  pl.BlockSpec((tm, d), lambda i: (i, 0)),
                  pl.BlockSpec((1, d), lambda i: (0, 0)),
                  pl.BlockSpec((1, d), lambda i: (0, 0))],
        out_specs=(pl.BlockSpec((tm, d), lambda i: (i, 0)),
                   pl.BlockSpec((tm * per_row, LANES), lambda i: (i, 0))),
        compiler_params=_cparams(("arbitrary",)),
        name="outproj_ln",
    )(mixed, w_bf, x2d, g, b)


def _first_argmax_rows(val, rowf, nrows):
    m = jnp.max(val, axis=0, keepdims=True)
    first = jnp.min(jnp.where(val == m, rowf, float(nrows)), axis=0, keepdims=True)
    return m, first, rowf == first


def _router_body(h_ref, wt_ref, bias_ref, idx_ref, gate_ref, rank_ref, cnt_ref, carry):
    tm = h_ref.shape[0]
    i = pl.program_id(0)

    @pl.when(i == 0)
    def _():
        carry[...] = jnp.zeros(carry.shape, F32)

    logits = lax.dot_general(wt_ref[...], h_ref[...], (((1,), (1,)), ((), ())),
                             preferred_element_type=F32, precision=lax.Precision.HIGHEST)
    scores = 1.0 / (1.0 + jnp.exp(-logits))
    biased = scores + bias_ref[...]
    neg = -jnp.inf

    grp = biased.reshape(N_GROUPS, GROUP_SIZE, tm)
    gi = lax.broadcasted_iota(I32, (N_GROUPS, GROUP_SIZE, tm), 1).astype(F32)
    g1 = jnp.max(grp, axis=1, keepdims=True)
    f1 = jnp.min(jnp.where(grp == g1, gi, float(GROUP_SIZE)), axis=1, keepdims=True)
    g2 = jnp.max(jnp.where(gi == f1, neg, grp), axis=1, keepdims=True)
    gscore = (g1 + g2).reshape(N_GROUPS, tm)

    growf = lax.broadcasted_iota(I32, (N_GROUPS, tm), 0).astype(F32)
    gsel = jnp.zeros((N_GROUPS, tm), F32)
    gval = gscore
    for _ in range(TOPK_GROUPS):
        _, _, pick = _first_argmax_rows(gval, growf, N_GROUPS)
        gsel = jnp.where(pick, 1.0, gsel)
        gval = jnp.where(pick, neg, gval)
    emask = jnp.broadcast_to(gsel.reshape(N_GROUPS, 1, tm), (N_GROUPS, GROUP_SIZE, tm)).reshape(N_EXPERTS, tm)

    rowf = lax.broadcasted_iota(I32, (N_EXPERTS, tm), 0).astype(F32)
    val = jnp.where(emask > 0.0, biased, neg)
    onehot = jnp.zeros((N_EXPERTS, tm), F32)
    picks, idxs, ws = [], [], []
    for _ in range(TOP_K):
        _, first, pick = _first_argmax_rows(val, rowf, N_EXPERTS)
        picks.append(pick)
        idxs.append(first)
        ws.append(jnp.sum(jnp.where(pick, scores, 0.0), axis=0, keepdims=True))
        onehot = jnp.where(pick, 1.0, onehot)
        val = jnp.where(pick, neg, val)
    w = jnp.concatenate(ws, axis=0)
    gate_ref[...] = w / jnp.sum(w, axis=0, keepdims=True) * ROUTED_SCALE
    idx_ref[...] = jnp.concatenate(idxs, axis=0).astype(I32)

    ti = lax.broadcasted_iota(I32, (tm, tm), 0)
    tj = lax.broadcasted_iota(I32, (tm, tm), 1)
    upper = jnp.where(ti < tj, 1.0, 0.0).astype(BF16)
    prefix = jnp.dot(onehot.astype(BF16), upper, preferred_element_type=F32) + carry[...]
    ranks = [jnp.sum(jnp.where(pk, prefix, 0.0), axis=0, keepdims=True) for pk in picks]
    rank_ref[...] = jnp.concatenate(ranks, axis=0).astype(I32)
    carry[...] = carry[...] + jnp.sum(onehot, axis=1, keepdims=True)
    cnt_ref[...] = carry[...]


def _router(h, wt, bias_col):
    t, d = h.shape
    tm = ROUTE_TM
    return pl.pallas_call(
        _router_body,
        out_shape=(jax.ShapeDtypeStruct((TOP_K, t), I32), jax.ShapeDtypeStruct((TOP_K, t), F32),
                   jax.ShapeDtypeStruct((TOP_K, t), I32), jax.ShapeDtypeStruct((N_EXPERTS, 1), F32)),
        grid=(t // tm,),
        in_specs=[pl.BlockSpec((tm, d), lambda i: (i, 0)),
                  pl.BlockSpec((N_EXPERTS, d), lambda i: (0, 0)),
                  pl.BlockSpec((N_EXPERTS, 1), lambda i: (0, 0))],
        out_specs=(pl.BlockSpec((TOP_K, tm), lambda i: (0, i)),
                   pl.BlockSpec((TOP_K, tm), lambda i: (0, i)),
                   pl.BlockSpec((TOP_K, tm), lambda i: (0, i)),
                   pl.BlockSpec((N_EXPERTS, 1), lambda i: (0, 0))),
        scratch_shapes=[pltpu.VMEM((N_EXPERTS, 1), F32)],
        compiler_params=_cparams(("arbitrary",)),
        name="router",
    )(h, wt, bias_col)


def _positions_body(idx_ref, rank_ref, start_ref, pos_ref):
    tm = idx_ref.shape[1]
    rowi = lax.broadcasted_iota(I32, (N_EXPERTS, tm), 0)
    start = start_ref[...]
    idx = idx_ref[...]
    rows = [jnp.sum(jnp.where(rowi == idx[k:k + 1, :], start, 0.0), axis=0, keepdims=True) for k in range(TOP_K)]
    pos_ref[...] = jnp.concatenate(rows, axis=0).astype(I32) + rank_ref[...]


def _positions(idx_t, rank_t, start_col):
    t = idx_t.shape[1]
    tm = POS_TM
    return pl.pallas_call(
        _positions_body,
        out_shape=jax.ShapeDtypeStruct((TOP_K, t), I32),
        grid=(t // tm,),
        in_specs=[pl.BlockSpec((TOP_K, tm), lambda i: (0, i)),
                  pl.BlockSpec((TOP_K, tm), lambda i: (0, i)),
                  pl.BlockSpec((N_EXPERTS, 1), lambda i: (0, 0))],
        out_specs=pl.BlockSpec((TOP_K, tm), lambda i: (0, i)),
        compiler_params=_cparams(("arbitrary",)),
        name="positions",
    )(idx_t, rank_t, start_col)


def _pad_fill_copy(zeros, xs8_ref, sem, row, nrows, per_row):
    return pltpu.make_async_copy(zeros.at[pl.ds(0, nrows * per_row), :],
                                 xs8_ref.at[pl.ds(pl.multiple_of(row * per_row, per_row), nrows * per_row), :], sem)


def _dispatch_body(pad_row_ref, pad_n_ref, nu_ref, pos_ref, h8_ref, xs8_ref, zeros, sem, pad_sem):
    i = pl.program_id(0)
    tm = h8_ref.shape[0] // ROW_TILES
    per_row = ROW_TILES
    half = MOE_BM // 2
    pad_bits = [1 << j for j in range(MOE_BM.bit_length() - 1)]
    n_half_blocks = xs8_ref.shape[0] // (half * per_row)

    def pad_pass(wait):
        def go(cp):
            if wait:
                cp.wait()
            else:
                cp.start()

        def body(e, carry):
            row = pad_row_ref[e]
            n = pad_n_ref[e]
            for bit in pad_bits:
                @pl.when((n & bit) != 0)
                def _():
                    go(_pad_fill_copy(zeros, xs8_ref, pad_sem, row + (n & (bit - 1)), bit, per_row))
            return carry
        lax.fori_loop(0, N_EXPERTS, body, 0)

        def tail(hb, carry):
            go(_pad_fill_copy(zeros, xs8_ref, pad_sem, hb * half, half, per_row))
            return carry
        lax.fori_loop(nu_ref[0] * 2, n_half_blocks, tail, 0)

    @pl.when(i == 0)
    def _():
        zeros[...] = jnp.zeros(zeros.shape, F32)
        pad_pass(False)

    def body(t, carry):
        src = h8_ref.at[pl.ds(pl.multiple_of(t * per_row, per_row), per_row), :]
        for k in range(TOP_K):
            p = pos_ref[0, 0, t * TOP_K + k]
            dst = xs8_ref.at[pl.ds(pl.multiple_of(p * per_row, per_row), per_row), :]
            pltpu.make_async_copy(src, dst, sem).start()
        return carry
    lax.fori_loop(0, tm, body, 0)

    for k in range(TOP_K):
        pltpu.make_async_copy(h8_ref, xs8_ref.at[pl.ds(0, tm * per_row), :], sem).wait()

    @pl.when(i == pl.num_programs(0) - 1)
    def _():
        pad_pass(True)


def _dispatch(pad_row, pad_n, n_used, pos3, h8, n_rows):
    nt, _, per_step = pos3.shape
    tm = per_step // TOP_K
    grid_spec = pltpu.PrefetchScalarGridSpec(
        num_scalar_prefetch=3,
        grid=(nt,),
        in_specs=[pl.BlockSpec((1, 1, per_step), lambda i, a, b, c: (i, 0, 0), memory_space=pltpu.SMEM),
                  pl.BlockSpec((tm * ROW_TILES, LANES), lambda i, a, b, c: (i, 0))],
        out_specs=pl.BlockSpec(memory_space=pl.ANY),
        scratch_shapes=[pltpu.VMEM((MOE_BM // 2 * ROW_TILES, LANES), F32),
                        pltpu.SemaphoreType.DMA, pltpu.SemaphoreType.DMA],
    )
    return pl.pallas_call(
        _dispatch_body,
        out_shape=jax.ShapeDtypeStruct((n_rows * ROW_TILES, LANES), F32),
        grid_spec=grid_spec,
        compiler_params=_cparams(("arbitrary",)),
        name="dispatch",
    )(pad_row, pad_n, n_used, pos3, h8)


def _experts_body(be_ref, nu_ref, x8_ref, wg_ref, wu_ref, wd_ref, y8_ref, wg_b, wu_b, wd_b):
    i = pl.program_id(0)
    bm = MOE_BM
    per_row = ROW_TILES

    @pl.when(i < nu_ref[0])
    def _():
        prev_e = be_ref[jnp.maximum(i - 1, 0)]

        @pl.when((i == 0) | (be_ref[i] != prev_e))
        def _():
            wg_b[...] = wg_ref[0].astype(BF16)
            wu_b[...] = wu_ref[0].astype(BF16)
            wd_b[...] = wd_ref[0].astype(BF16)

        x = jnp.concatenate([x8_ref[pl.ds(s, bm, stride=per_row), :] for s in range(per_row)], axis=1).astype(BF16)
        g = jnp.dot(x, wg_b[...], preferred_element_type=F32)
        u = jnp.dot(x, wu_b[...], preferred_element_type=F32)
        a = (_silu(g) * u).astype(BF16)
        y = jnp.dot(a, wd_b[...], preferred_element_type=F32)
        for s in range(per_row):
            y8_ref[pl.ds(s, bm, stride=per_row), :] = y[:, s * LANES:(s + 1) * LANES]

    @pl.when(i >= nu_ref[0])
    def _():
        y8_ref[...] = jnp.zeros(y8_ref.shape, F32)


def _experts(block_e, n_used, xs8, wg, wu, wd):
    nblk = block_e.shape[0]
    bm = MOE_BM
    e, d, de = wg.shape
    per_row = ROW_TILES

    def blk(i, be, nu):
        return jnp.minimum(i, nu[0] - 1)

    grid_spec = pltpu.PrefetchScalarGridSpec(
        num_scalar_prefetch=2,
        grid=(nblk,),
        in_specs=[pl.BlockSpec((bm * per_row, LANES), lambda i, be, nu: (blk(i, be, nu), 0)),
                  pl.BlockSpec((1, d, de), lambda i, be, nu: (be[blk(i, be, nu)], 0, 0)),
                  pl.BlockSpec((1, d, de), lambda i, be, nu: (be[blk(i, be, nu)], 0, 0)),
                  pl.BlockSpec((1, de, d), lambda i, be, nu: (be[blk(i, be, nu)], 0, 0))],
        out_specs=pl.BlockSpec((bm * per_row, LANES), lambda i, be, nu: (i, 0)),
        scratch_shapes=[pltpu.VMEM((d, de), BF16), pltpu.VMEM((d, de), BF16), pltpu.VMEM((de, d), BF16)],
    )
    return pl.pallas_call(
        _experts_body,
        out_shape=jax.ShapeDtypeStruct((nblk * bm * per_row, LANES), F32),
        grid_spec=grid_spec,
        compiler_params=_cparams(("arbitrary",)),
        name="experts",
    )(block_e, n_used, xs8, wg, wu, wd)


def _combine_body(pos_cur, pos_nxt, y8_ref, h_ref, gate_ref, wsg_ref, wsu_ref, wsd_ref, g_ref, b_ref,
                  o_ref, buf, sem):
    i = pl.program_id(0)
    nb = pl.num_programs(0)
    tm, d = h_ref.shape
    per_row = d // LANES
    slot = i % 2
    tok_rows = TOP_K * per_row

    def issue(pos_ref, sl):
        def body(t, carry):
            for k in range(TOP_K):
                p = pos_ref[0, 0, t * TOP_K + k]
                src = y8_ref.at[pl.ds(pl.multiple_of(p * per_row, per_row), per_row), :]
                dst = buf.at[sl, pl.ds(pl.multiple_of((t * TOP_K + k) * per_row, per_row), per_row), :]
                pltpu.make_async_copy(src, dst, sem.at[sl]).start()
            return carry
        lax.fori_loop(0, tm, body, 0)

    @pl.when(i == 0)
    def _():
        issue(pos_cur, 0)

    @pl.when(i + 1 < nb)
    def _():
        issue(pos_nxt, 1 - slot)

    h = h_ref[...]
    hb = h.astype(BF16)
    sg = jnp.dot(hb, wsg_ref[...], preferred_element_type=F32)
    su = jnp.dot(hb, wsu_ref[...], preferred_element_type=F32)
    shared = jnp.dot((_silu(sg) * su).astype(BF16), wsd_ref[...], preferred_element_type=F32)

    pltpu.make_async_copy(y8_ref.at[pl.ds(0, tm * tok_rows), :], buf.at[slot], sem.at[slot]).wait()
    gates = gate_ref[...]
    cols = []
    for s in range(per_row):
        acc = None
        for k in range(TOP_K):
            term = gates[:, k:k + 1] * buf[slot, pl.ds(k * per_row + s, tm, stride=tok_rows), :]
            acc = term if acc is None else acc + term
        cols.append(acc)
    routed = jnp.concatenate(cols, axis=1)
    o_ref[...] = _layernorm(DEEPNORM_ALPHA * h + (routed + shared), g_ref[...], b_ref[...])


def _combine(pos3, y8, h, gates, wsg, wsu, wsd, g, b):
    t, d = h.shape
    tm = COMB_TM
    nt = t // tm
    per_row = d // LANES
    ds_ = wsg.shape[1]
    return pl.pallas_call(
        _combine_body,
        out_shape=jax.ShapeDtypeStruct((t, d), F32),
        grid=(nt,),
        in_specs=[pl.BlockSpec((1, 1, tm * TOP_K), lambda i: (i, 0, 0), memory_space=pltpu.SMEM),
                  pl.BlockSpec((1, 1, tm * TOP_K), lambda i: (jnp.minimum(i + 1, nt - 1), 0, 0),
                               memory_space=pltpu.SMEM),
                  pl.BlockSpec(memory_space=pl.ANY),
                  pl.BlockSpec((tm, d), lambda i: (i, 0)),
                  pl.BlockSpec((tm, TOP_K), lambda i: (i, 0)),
                  pl.BlockSpec((d, ds_), lambda i: (0, 0)),
                  pl.BlockSpec((d, ds_), lambda i: (0, 0)),
                  pl.BlockSpec((ds_, d), lambda i: (0, 0)),
                  pl.BlockSpec((1, d), lambda i: (0, 0)),
                  pl.BlockSpec((1, d), lambda i: (0, 0))],
        out_specs=pl.BlockSpec((tm, d), lambda i: (i, 0)),
        scratch_shapes=[pltpu.VMEM((2, tm * TOP_K * per_row, LANES), F32),
                        pltpu.SemaphoreType.DMA((2,))],
        compiler_params=_cparams(("arbitrary",)),
        name="combine",
    )(pos3, pos3, y8, h, gates, wsg, wsu, wsd, g, b)


def _expert_tables(counts, nblk):
    bm = MOE_BM
    cnt = counts.reshape(N_EXPERTS).astype(I32)
    padded = (cnt + bm - 1) // bm * bm
    padded_end = jnp.cumsum(padded)
    padded_start = padded_end - padded
    block_rows = jnp.arange(nblk, dtype=I32) * bm
    block_e = jnp.sum((padded_end[None, :] <= block_rows[:, None]).astype(I32), axis=1)
    block_e = jnp.minimum(block_e, N_EXPERTS - 1)
    n_used = (padded_end[-1:] // bm).astype(I32)
    return padded_start, padded_start + cnt, padded - cnt, block_e, n_used


def kernel(x, w_in, gla_gate_w2, gla_gate_b, gla_norm_w, pool_w_group, pool_scale, w_out, ln1_g, ln1_b,
           router_w, router_bias, w_exp_gate, w_exp_up, w_exp_down, w_sh_gate, w_sh_up, w_sh_down, ln2_g, ln2_b):
    batch, seq, d = x.shape
    t = batch * seq
    h2d = x.reshape(t, d)
    for l in range(DEPTH):
        d_in = w_in.shape[2]
        w_in_b = jnp.pad(w_in[l], ((0, 0), (0, D_IN_PAD - d_in))).astype(BF16)
        w2p = jnp.pad(gla_gate_w2[l], ((0, LANES - GLA_GATE_RANK), (0, 0))).astype(BF16)
        proj = _inproj(h2d, w_in_b)
        mixed = _mixer(proj, batch, seq, w2p, gla_gate_b[l].reshape(1, -1), gla_norm_w[l].reshape(1, -1),
                       pool_w_group[l].astype(BF16), pool_scale[l].reshape(1, -1))
        h, h8 = _outproj_ln(mixed, w_out[l].astype(BF16), h2d, ln1_g[l].reshape(1, -1), ln1_b[l].reshape(1, -1))
        idx_t, gate_t, rank_t, counts = _router(h, router_w[l].T, router_bias[l].reshape(-1, 1))
        nblk = (t * TOP_K + N_EXPERTS * (MOE_BM - 1)) // MOE_BM
        start, pad_row, pad_n, block_e, n_used = _expert_tables(counts, nblk)
        pos_t = _positions(idx_t, rank_t, start.astype(F32).reshape(-1, 1))
        pos_tok = pos_t.T
        xs8 = _dispatch(pad_row, pad_n, n_used, pos_tok.reshape(t // DISP_TM, 1, DISP_TM * TOP_K), h8, nblk * MOE_BM)
        y8 = _experts(block_e, n_used, xs8, w_exp_gate[l], w_exp_up[l], w_exp_down[l])
        h2d = _combine(pos_tok.reshape(t // COMB_TM, 1, COMB_TM * TOP_K), y8, h, gate_t.T,
                       w_sh_gate[l].astype(BF16), w_sh_up[l].astype(BF16), w_sh_down[l].astype(BF16),
                       ln2_g[l].reshape(1, -1), ln2_b[l].reshape(1, -1))
    return h2d.reshape(batch, seq, d)
```

```python
import functools

import jax
import jax.numpy as jnp
from jax import lax
from jax.experimental import pallas as pl
from jax.experimental.pallas import tpu as pltpu

F32 = jnp.float32
BF16 = jnp.bfloat16
I32 = jnp.int32

POOL_WINDOWS = (2, 4, 8, 16)
POOL_GROUP_DIM = 128
POOL_WIDTH = 512
GLA_HEADS = 4
GLA_DK = 64
GLA_DV = 128
GLA_DK_TOTAL = 256
GLA_WIDTH = 512
GLA_GATE_RANK = 16
GLA_GATE_NORMALIZER = 16.0
GLA_CHUNK = 16
N_EXPERTS = 256
TOP_K = 8
N_GROUPS = 8
GROUP_SIZE = N_EXPERTS // N_GROUPS
TOPK_GROUPS = 4
ROUTED_SCALE = 2.5
DEPTH = 1
DEEPNORM_ALPHA = (2.0 * DEPTH) ** 0.25
LN_EPS = 1e-5
RMS_EPS = 1e-5

LANES = 128
SUBLANES = 8
VMEM_LIMIT = 56 * 1024 * 1024

PROJ_TM = 512
MIX_TS = 256
ROUTE_TM = 256
MOE_BM = 256
COMB_TM = 128
POS_TM = 512
DISP_TM = 256
ROW_TILES = 8
D_IN_PAD = 2944
GATE_COL_BLOCK = 16


def _cparams(sem):
    return pltpu.CompilerParams(dimension_semantics=sem, vmem_limit_bytes=VMEM_LIMIT)


def _silu(x):
    return x * (1.0 / (1.0 + jnp.exp(-x)))


def _layernorm(y, g, b):
    mu = jnp.mean(y, axis=-1, keepdims=True)
    yc = y - mu
    var = jnp.mean(yc * yc, axis=-1, keepdims=True)
    return yc * lax.rsqrt(var + LN_EPS) * g + b


def _inproj_body(x_ref, w_ref, o_ref):
    o_ref[...] = jnp.dot(x_ref[...].astype(BF16), w_ref[...], preferred_element_type=F32)


def _inproj(x2d, w_bf):
    t, d = x2d.shape
    n = w_bf.shape[1]
    return pl.pallas_call(
        _inproj_body,
        out_shape=jax.ShapeDtypeStruct((t, n), F32),
        grid=(t // PROJ_TM,),
        in_specs=[pl.BlockSpec((PROJ_TM, d), lambda i: (i, 0)),
                  pl.BlockSpec((d, n), lambda i: (0, 0))],
        out_specs=pl.BlockSpec((PROJ_TM, n), lambda i: (i, 0)),
        compiler_params=_cparams(("arbitrary",)),
        name="inproj",
    )(x2d, w_bf)


def _mixer_body(p_ref, q_ref, k_ref, v_ref, r_ref, gl_ref, w2_ref, gb_ref, nw_ref, pw_ref, ps_ref,
                o_ref, pbuf, state):
    ts = p_ref.shape[0]
    s_idx = pl.program_id(1)
    halo = POOL_WINDOWS[-1]

    @pl.when(s_idx == 0)
    def _():
        pbuf[pl.ds(0, halo), :] = jnp.zeros((halo, POOL_WIDTH), F32)
        state[...] = jnp.zeros(state.shape, F32)

    p = p_ref[...]
    pbuf[pl.ds(halo, ts), :] = p
    pos = s_idx * ts + lax.broadcasted_iota(I32, (ts, 1), 0)
    for g, w in enumerate(POOL_WINDOWS):
        c0 = g * POOL_GROUP_DIM
        acc = pbuf[pl.ds(halo, ts), pl.ds(c0, POOL_GROUP_DIM)]
        for j in range(1, w):
            acc = acc + pbuf[pl.ds(halo - j, ts), pl.ds(c0, POOL_GROUP_DIM)]
        cnt = jnp.minimum(pos + 1, w).astype(F32)
        mixed = acc / cnt - p[:, c0:c0 + POOL_GROUP_DIM]
        og = jnp.dot(mixed.astype(BF16), pw_ref[g], preferred_element_type=F32)
        o_ref[:, pl.ds(c0, POOL_GROUP_DIM)] = (og * ps_ref[:, pl.ds(c0, POOL_GROUP_DIM)]).astype(o_ref.dtype)
    pbuf[pl.ds(0, halo), :] = pbuf[pl.ds(ts, halo), :]

    nchunk = ts // GLA_CHUNK
    glog = jnp.dot(gl_ref[...].astype(BF16), w2_ref[...], preferred_element_type=F32) + gb_ref[...]
    gk = (jnp.minimum(glog, 0.0) - jnp.log(1.0 + jnp.exp(-jnp.abs(glog)))) * (1.0 / GLA_GATE_NORMALIZER)
    row = lax.broadcasted_iota(I32, (ts, 1), 0)
    rin = row % GLA_CHUNK
    b = gk
    sh = 1
    while sh < GLA_CHUNK:
        b = b + jnp.where(rin >= sh, pltpu.roll(b, sh, axis=0), 0.0)
        sh *= 2
    b3 = b.reshape(nchunk, GLA_CHUNK, GLA_DK_TOTAL)
    bmid = b3[:, GLA_CHUNK // 2 - 1:GLA_CHUNK // 2, :]
    blast = b3[:, GLA_CHUNK - 1:GLA_CHUNK, :]
    q3 = (q_ref[...] * (GLA_DK ** -0.5)).reshape(nchunk, GLA_CHUNK, GLA_DK_TOTAL)
    k3 = k_ref[...].reshape(nchunk, GLA_CHUNK, GLA_DK_TOTAL)
    qs = (q3 * jnp.exp(b3 - bmid)).reshape(ts, GLA_DK_TOTAL)
    ks = (k3 * jnp.exp(bmid - b3)).reshape(ts, GLA_DK_TOTAL)
    qd = (q3 * jnp.exp(b3)).reshape(ts, GLA_DK_TOTAL)
    kd = (k3 * jnp.exp(blast - b3)).reshape(ts, GLA_DK_TOTAL).astype(BF16)
    cdec = jnp.exp(blast).reshape(nchunk, GLA_DK_TOTAL)

    v = v_ref[...]
    vb = v.astype(BF16)
    lane = lax.broadcasted_iota(I32, (1, LANES), 1)
    head_lane = [lane < GLA_DK, lane >= GLA_DK]
    prow = lax.broadcasted_iota(I32, (LANES, 1), 0)
    head_row = prow < GLA_DK

    blk = LANES
    ri = lax.broadcasted_iota(I32, (blk, blk), 0)
    ci = lax.broadcasted_iota(I32, (blk, blk), 1)
    causal = (ri // GLA_CHUNK == ci // GLA_CHUNK) & (ri >= ci)
    o_intra = [[None] * (ts // blk) for _ in range(GLA_HEADS)]
    for rb in range(ts // blk):
        rs = slice(rb * blk, (rb + 1) * blk)
        for pair in range(GLA_HEADS // 2):
            ls = slice(pair * LANES, (pair + 1) * LANES)
            ks_p = ks[rs, ls].astype(BF16)
            for sub in range(2):
                h = pair * 2 + sub
                q_m = jnp.where(head_lane[sub], qs[rs, ls], 0.0).astype(BF16)
                sc = lax.dot_general(q_m, ks_p, (((1,), (1,)), ((), ())), preferred_element_type=F32)
                sc = jnp.where(causal, sc, 0.0).astype(BF16)
                o_intra[h][rb] = jnp.dot(sc, vb[rs, h * GLA_DV:(h + 1) * GLA_DV], preferred_element_type=F32)

    o_inter = [[None] * nchunk for _ in range(GLA_HEADS)]
    for pair in range(GLA_HEADS // 2):
        ls = slice(pair * LANES, (pair + 1) * LANES)
        dec_cols = cdec[:, ls].T
        st = state[pair]
        for c in range(nchunk):
            rs = slice(c * GLA_CHUNK, (c + 1) * GLA_CHUNK)
            st_b = st.astype(BF16)
            kv = []
            for sub in range(2):
                h = pair * 2 + sub
                q_m = jnp.where(head_lane[sub], qd[rs, ls], 0.0).astype(BF16)
                o_inter[h][c] = jnp.dot(q_m, st_b, preferred_element_type=F32)
                kv.append(lax.dot_general(kd[rs, ls], vb[rs, h * GLA_DV:(h + 1) * GLA_DV],
                                          (((0,), (0,)), ((), ())), preferred_element_type=F32))
            st = st * dec_cols[:, c:c + 1] + jnp.where(head_row, kv[0], kv[1])
        state[pair] = st

    nw = nw_ref[...]
    r = r_ref[...]
    for h in range(GLA_HEADS):
        o = jnp.concatenate(o_intra[h], axis=0) + jnp.concatenate(o_inter[h], axis=0)
        o = o * lax.rsqrt(jnp.mean(o * o, axis=-1, keepdims=True) + RMS_EPS) * nw
        o = o * _silu(r[:, h * GLA_DV:(h + 1) * GLA_DV])
        o_ref[:, pl.ds(POOL_WIDTH + h * GLA_DV, GLA_DV)] = o.astype(o_ref.dtype)


def _mixer(proj, batch, seq, w2p, gate_b, norm_w, pool_w, pool_scale):
    t = proj.shape[0]
    ts = MIX_TS
    nseq = seq // ts

    def rows(width, col_block):
        return pl.BlockSpec((ts, width), lambda bi, si: (bi * nseq + si, col_block))

    def full(shape):
        return pl.BlockSpec(shape, lambda bi, si: (0,) * len(shape))

    return pl.pallas_call(
        _mixer_body,
        out_shape=jax.ShapeDtypeStruct((t, POOL_WIDTH + GLA_WIDTH), BF16),
        grid=(batch, nseq),
        in_specs=[rows(POOL_WIDTH, 0),
                  rows(GLA_DK_TOTAL, 2),
                  rows(GLA_DK_TOTAL, 3),
                  rows(GLA_WIDTH, 2),
                  rows(GLA_WIDTH, 3),
                  rows(LANES, GATE_COL_BLOCK),
                  full(w2p.shape), full(gate_b.shape), full(norm_w.shape),
                  full(pool_w.shape), full(pool_scale.shape)],
        out_specs=pl.BlockSpec((ts, POOL_WIDTH + GLA_WIDTH), lambda bi, si: (bi * nseq + si, 0)),
        scratch_shapes=[pltpu.VMEM((ts + POOL_WINDOWS[-1], POOL_WIDTH), F32),
                        pltpu.VMEM((GLA_HEADS // 2, LANES, GLA_DV), F32)],
        compiler_params=_cparams(("arbitrary", "arbitrary")),
        name="mixer",
    )(proj, proj, proj, proj, proj, proj, w2p, gate_b, norm_w, pool_w, pool_scale)


def _outproj_body(m_ref, w_ref, x_ref, g_ref, b_ref, h_ref, h8_ref):
    tm, d = x_ref.shape
    y = DEEPNORM_ALPHA * x_ref[...] + jnp.dot(m_ref[...], w_ref[...], preferred_element_type=F32)
    h = _layernorm(y, g_ref[...], b_ref[...])
    h_ref[...] = h
    for s in range(d // LANES):
        h8_ref[pl.ds(s, tm, stride=d // LANES), :] = h[:, s * LANES:(s + 1) * LANES]


def _outproj_ln(mixed, w_bf, x2d, g, b):
    t, d = x2d.shape
    tm = PROJ_TM
    per_row = d // LANES
    return pl.pallas_call(
        _outproj_body,
        out_shape=(jax.ShapeDtypeStruct((t, d), F32), jax.ShapeDtypeStruct((t * per_row, LANES), F32)),
        grid=(t // tm,),
        in_specs=[pl.BlockSpec((tm, d), lambda i: (i, 0)),
                  pl.BlockSpec((d, d), lambda i: (0, 0)),
                  pl.BlockSpec((tm, d), lambda i: (i, 0)),
                  pl.BlockSpec((1, d), lambda i: (0, 0)),
                  pl.BlockSpec((1, d), lambda i: (0, 0))],
        out_specs=(pl.BlockSpec((tm, d), lambda i: (i, 0)),
                   pl.BlockSpec((tm * per_row, LANES), lambda i: (i, 0))),
        compiler_params=_cparams(("arbitrary",)),
        name="outproj_ln",
    )(mixed, w_bf, x2d, g, b)


def _first_argmax_rows(val, rowf, nrows):
    m = jnp.max(val, axis=0, keepdims=True)
    first = jnp.min(jnp.where(val == m, rowf, float(nrows)), axis=0, keepdims=True)
    return m, first, rowf == first


def _router_body(h_ref, wt_ref, bias_ref, idx_ref, gate_ref, rank_ref, cnt_ref, carry):
    tm = h_ref.shape[0]
    i = pl.program_id(0)

    @pl.when(i == 0)
    def _():
        carry[...] = jnp.zeros(carry.shape, F32)

    logits = lax.dot_general(wt_ref[...], h_ref[...], (((1,), (1,)), ((), ())),
                             preferred_element_type=F32, precision=lax.Precision.HIGHEST)
    scores = 1.0 / (1.0 + jnp.exp(-logits))
    biased = scores + bias_ref[...]
    neg = -jnp.inf

    grp = biased.reshape(N_GROUPS, GROUP_SIZE, tm)
    gi = lax.broadcasted_iota(I32, (N_GROUPS, GROUP_SIZE, tm), 1).astype(F32)
    g1 = jnp.max(grp, axis=1, keepdims=True)
    f1 = jnp.min(jnp.where(grp == g1, gi, float(GROUP_SIZE)), axis=1, keepdims=True)
    g2 = jnp.max(jnp.where(gi == f1, neg, grp), axis=1, keepdims=True)
    gscore = (g1 + g2).reshape(N_GROUPS, tm)

    growf = lax.broadcasted_iota(I32, (N_GROUPS, tm), 0).astype(F32)
    gsel = jnp.zeros((N_GROUPS, tm), F32)
    gval = gscore
    for _ in range(TOPK_GROUPS):
        _, _, pick = _first_argmax_rows(gval, growf, N_GROUPS)
        gsel = jnp.where(pick, 1.0, gsel)
        gval = jnp.where(pick, neg, gval)
    emask = jnp.broadcast_to(gsel.reshape(N_GROUPS, 1, tm), (N_GROUPS, GROUP_SIZE, tm)).reshape(N_EXPERTS, tm)

    rowf = lax.broadcasted_iota(I32, (N_EXPERTS, tm), 0).astype(F32)
    val = jnp.where(emask > 0.0, biased, neg)
    onehot = jnp.zeros((N_EXPERTS, tm), F32)
    picks, idxs, ws = [], [], []
    for _ in range(TOP_K):
        _, first, pick = _first_argmax_rows(val, rowf, N_EXPERTS)
        picks.append(pick)
        idxs.append(first)
        ws.append(jnp.sum(jnp.where(pick, scores, 0.0), axis=0, keepdims=True))
        onehot = jnp.where(pick, 1.0, onehot)
        val = jnp.where(pick, neg, val)
    w = jnp.concatenate(ws, axis=0)
    gate_ref[...] = w / jnp.sum(w, axis=0, keepdims=True) * ROUTED_SCALE
    idx_ref[...] = jnp.concatenate(idxs, axis=0).astype(I32)

    ti = lax.broadcasted_iota(I32, (tm, tm), 0)
    tj = lax.broadcasted_iota(I32, (tm, tm), 1)
    upper = jnp.where(ti < tj, 1.0, 0.0).astype(BF16)
    prefix = jnp.dot(onehot.astype(BF16), upper, preferred_element_type=F32) + carry[...]
    ranks = [jnp.sum(jnp.where(pk, prefix, 0.0), axis=0, keepdims=True) for pk in picks]
    rank_ref[...] = jnp.concatenate(ranks, axis=0).astype(I32)
    carry[...] = carry[...] + jnp.sum(onehot, axis=1, keepdims=True)
    cnt_ref[...] = carry[...]


def _router(h, wt, bias_col):
    t, d = h.shape
    tm = ROUTE_TM
    return pl.pallas_call(
        _router_body,
        out_shape=(jax.ShapeDtypeStruct((TOP_K, t), I32), jax.ShapeDtypeStruct((TOP_K, t), F32),
                   jax.ShapeDtypeStruct((TOP_K, t), I32), jax.ShapeDtypeStruct((N_EXPERTS, 1), F32)),
        grid=(t // tm,),
        in_specs=[pl.BlockSpec((tm, d), lambda i: (i, 0)),
                  pl.BlockSpec((N_EXPERTS, d), lambda i: (0, 0)),
                  pl.BlockSpec((N_EXPERTS, 1), lambda i: (0, 0))],
        out_specs=(pl.BlockSpec((TOP_K, tm), lambda i: (0, i)),
                   pl.BlockSpec((TOP_K, tm), lambda i: (0, i)),
                   pl.BlockSpec((TOP_K, tm), lambda i: (0, i)),
                   pl.BlockSpec((N_EXPERTS, 1), lambda i: (0, 0))),
        scratch_shapes=[pltpu.VMEM((N_EXPERTS, 1), F32)],
        compiler_params=_cparams(("arbitrary",)),
        name="router",
    )(h, wt, bias_col)


def _positions_body(idx_ref, rank_ref, start_ref, pos_ref):
    tm = idx_ref.shape[1]
    rowi = lax.broadcasted_iota(I32, (N_EXPERTS, tm), 0)
    start = start_ref[...]
    idx = idx_ref[...]
    rows = [jnp.sum(jnp.where(rowi == idx[k:k + 1, :], start, 0.0), axis=0, keepdims=True) for k in range(TOP_K)]
    pos_ref[...] = jnp.concatenate(rows, axis=0).astype(I32) + rank_ref[...]


def _positions(idx_t, rank_t, start_col):
    t = idx_t.shape[1]
    tm = POS_TM
    return pl.pallas_call(
        _positions_body,
        out_shape=jax.ShapeDtypeStruct((TOP_K, t), I32),
        grid=(t // tm,),
        in_specs=[pl.BlockSpec((TOP_K, tm), lambda i: (0, i)),
                  pl.BlockSpec((TOP_K, tm), lambda i: (0, i)),
                  pl.BlockSpec((N_EXPERTS, 1), lambda i: (0, 0))],
        out_specs=pl.BlockSpec((TOP_K, tm), lambda i: (0, i)),
        compiler_params=_cparams(("arbitrary",)),
        name="positions",
    )(idx_t, rank_t, start_col)


def _pad_fill_copy(zeros, xs8_ref, sem, row, nrows, per_row):
    return pltpu.make_async_copy(zeros.at[pl.ds(0, nrows * per_row), :],
                                 xs8_ref.at[pl.ds(pl.multiple_of(row * per_row, per_row), nrows * per_row), :], sem)


def _dispatch_body(pad_row_ref, pad_n_ref, nu_ref, pos_ref, h8_ref, xs8_ref, zeros, sem, pad_sem):
    i = pl.program_id(0)
    tm = h8_ref.shape[0] // ROW_TILES
    per_row = ROW_TILES
    half = MOE_BM // 2
    pad_bits = [1 << j for j in range(MOE_BM.bit_length() - 1)]
    n_half_blocks = xs8_ref.shape[0] // (half * per_row)

    def pad_pass(wait):
        def go(cp):
            if wait:
                cp.wait()
            else:
                cp.start()

        def body(e, carry):
            row = pad_row_ref[e]
            n = pad_n_ref[e]
            for bit in pad_bits:
                @pl.when((n & bit) != 0)
                def _():
                    go(_pad_fill_copy(zeros, xs8_ref, pad_sem, row + (n & (bit - 1)), bit, per_row))
            return carry
        lax.fori_loop(0, N_EXPERTS, body, 0)

        def tail(hb, carry):
            go(_pad_fill_copy(zeros, xs8_ref, pad_sem, hb * half, half, per_row))
            return carry
        lax.fori_loop(nu_ref[0] * 2, n_half_blocks, tail, 0)

    @pl.when(i == 0)
    def _():
        zeros[...] = jnp.zeros(zeros.shape, F32)
        pad_pass(False)

    def body(t, carry):
        src = h8_ref.at[pl.ds(pl.multiple_of(t * per_row, per_row), per_row), :]
        for k in range(TOP_K):
            p = pos_ref[0, 0, t * TOP_K + k]
            dst = xs8_ref.at[pl.ds(pl.multiple_of(p * per_row, per_row), per_row), :]
            pltpu.make_async_copy(src, dst, sem).start(priority=k % 2)
        return carry
    lax.fori_loop(0, tm, body, 0)

    for k in range(TOP_K):
        pltpu.make_async_copy(h8_ref, xs8_ref.at[pl.ds(0, tm * per_row), :], sem).wait()

    @pl.when(i == pl.num_programs(0) - 1)
    def _():
        pad_pass(True)


def _dispatch(pad_row, pad_n, n_used, pos3, h8, n_rows):
    nt, _, per_step = pos3.shape
    tm = per_step // TOP_K
    grid_spec = pltpu.PrefetchScalarGridSpec(
        num_scalar_prefetch=3,
        grid=(nt,),
        in_specs=[pl.BlockSpec((1, 1, per_step), lambda i, a, b, c: (i, 0, 0), memory_space=pltpu.SMEM),
                  pl.BlockSpec((tm * ROW_TILES, LANES), lambda i, a, b, c: (i, 0))],
        out_specs=pl.BlockSpec(memory_space=pl.ANY),
        scratch_shapes=[pltpu.VMEM((MOE_BM // 2 * ROW_TILES, LANES), F32),
                        pltpu.SemaphoreType.DMA, pltpu.SemaphoreType.DMA],
    )
    return pl.pallas_call(
        _dispatch_body,
        out_shape=jax.ShapeDtypeStruct((n_rows * ROW_TILES, LANES), F32),
        grid_spec=grid_spec,
        compiler_params=_cparams(("arbitrary",)),
        name="dispatch",
    )(pad_row, pad_n, n_used, pos3, h8)


def _experts_body(be_ref, nu_ref, x8_ref, wg_ref, wu_ref, wd_ref, y8_ref, wg_b, wu_b, wd_b):
    i = pl.program_id(0)
    bm = MOE_BM
    per_row = ROW_TILES

    @pl.when(i < nu_ref[0])
    def _():
        prev_e = be_ref[jnp.maximum(i - 1, 0)]

        @pl.when((i == 0) | (be_ref[i] != prev_e))
        def _():
            wg_b[...] = wg_ref[0].astype(BF16)
            wu_b[...] = wu_ref[0].astype(BF16)
            wd_b[...] = wd_ref[0].astype(BF16)

        x = jnp.concatenate([x8_ref[pl.ds(s, bm, stride=per_row), :] for s in range(per_row)], axis=1).astype(BF16)
        g = jnp.dot(x, wg_b[...], preferred_element_type=F32)
        u = jnp.dot(x, wu_b[...], preferred_element_type=F32)
        a = (_silu(g) * u).astype(BF16)
        y = jnp.dot(a, wd_b[...], preferred_element_type=F32)
        for s in range(per_row):
            y8_ref[pl.ds(s, bm, stride=per_row), :] = y[:, s * LANES:(s + 1) * LANES]

    @pl.when(i >= nu_ref[0])
    def _():
        y8_ref[...] = jnp.zeros(y8_ref.shape, F32)


def _experts(block_e, n_used, xs8, wg, wu, wd):
    nblk = block_e.shape[0]
    bm = MOE_BM
    e, d, de = wg.shape
    per_row = ROW_TILES

    def blk(i, be, nu):
        return jnp.minimum(i, nu[0] - 1)

    grid_spec = pltpu.PrefetchScalarGridSpec(
        num_scalar_prefetch=2,
        grid=(nblk,),
        in_specs=[pl.BlockSpec((bm * per_row, LANES), lambda i, be, nu: (blk(i, be, nu), 0)),
                  pl.BlockSpec((1, d, de), lambda i, be, nu: (be[blk(i, be, nu)], 0, 0)),
                  pl.BlockSpec((1, d, de), lambda i, be, nu: (be[blk(i, be, nu)], 0, 0)),
                  pl.BlockSpec((1, de, d), lambda i, be, nu: (be[blk(i, be, nu)], 0, 0))],
        out_specs=pl.BlockSpec((bm * per_row, LANES), lambda i, be, nu: (i, 0)),
        scratch_shapes=[pltpu.VMEM((d, de), BF16), pltpu.VMEM((d, de), BF16), pltpu.VMEM((de, d), BF16)],
    )
    return pl.pallas_call(
        _experts_body,
        out_shape=jax.ShapeDtypeStruct((nblk * bm * per_row, LANES), F32),
        grid_spec=grid_spec,
        compiler_params=_cparams(("arbitrary",)),
        name="experts",
    )(block_e, n_used, xs8, wg, wu, wd)


def _combine_body(pos_cur, pos_nxt, y8_ref, h_ref, gate_ref, wsg_ref, wsu_ref, wsd_ref, g_ref, b_ref,
                  o_ref, buf, sem):
    i = pl.program_id(0)
    nb = pl.num_programs(0)
    tm, d = h_ref.shape
    per_row = d // LANES
    slot = i % 2
    tok_rows = TOP_K * per_row

    def issue(pos_ref, sl):
        def body(t, carry):
            for k in range(TOP_K):
                p = pos_ref[0, 0, t * TOP_K + k]
                src = y8_ref.at[pl.ds(pl.multiple_of(p * per_row, per_row), per_row), :]
                dst = buf.at[sl, pl.ds(pl.multiple_of((t * TOP_K + k) * per_row, per_row), per_row), :]
                pltpu.make_async_copy(src, dst, sem.at[sl]).start(priority=k % 2)
            return carry
        lax.fori_loop(0, tm, body, 0)

    @pl.when(i == 0)
    def _():
        issue(pos_cur, 0)

    @pl.when(i + 1 < nb)
    def _():
        issue(pos_nxt, 1 - slot)

    h = h_ref[...]
    hb = h.astype(BF16)
    sg = jnp.dot(hb, wsg_ref[...], preferred_element_type=F32)
    su = jnp.dot(hb, wsu_ref[...], preferred_element_type=F32)
    shared = jnp.dot((_silu(sg) * su).astype(BF16), wsd_ref[...], preferred_element_type=F32)

    pltpu.make_async_copy(y8_ref.at[pl.ds(0, tm * tok_rows), :], buf.at[slot], sem.at[slot]).wait()
    gates = gate_ref[...]
    cols = []
    for s in range(per_row):
        acc = None
        for k in range(TOP_K):
            term = gates[:, k:k + 1] * buf[slot, pl.ds(k * per_row + s, tm, stride=tok_rows), :]
            acc = term if acc is None else acc + term
        cols.append(acc)
    routed = jnp.concatenate(cols, axis=1)
    o_ref[...] = _layernorm(DEEPNORM_ALPHA * h + (routed + shared), g_ref[...], b_ref[...])


def _combine(pos3, y8, h, gates, wsg, wsu, wsd, g, b):
    t, d = h.shape
    tm = COMB_TM
    nt = t // tm
    per_row = d // LANES
    ds_ = wsg.shape[1]
    return pl.pallas_call(
        _combine_body,
        out_shape=jax.ShapeDtypeStruct((t, d), F32),
        grid=(nt,),
        in_specs=[pl.BlockSpec((1, 1, tm * TOP_K), lambda i: (i, 0, 0), memory_space=pltpu.SMEM),
                  pl.BlockSpec((1, 1, tm * TOP_K), lambda i: (jnp.minimum(i + 1, nt - 1), 0, 0),
                               memory_space=pltpu.SMEM),
                  pl.BlockSpec(memory_space=pl.ANY),
                  pl.BlockSpec((tm, d), lambda i: (i, 0)),
                  pl.BlockSpec((tm, TOP_K), lambda i: (i, 0)),
                  pl.BlockSpec((d, ds_), lambda i: (0, 0)),
                  pl.BlockSpec((d, ds_), lambda i: (0, 0)),
                  pl.BlockSpec((ds_, d), lambda i: (0, 0)),
                  pl.BlockSpec((1, d), lambda i: (0, 0)),
                  pl.BlockSpec((1, d), lambda i: (0, 0))],
        out_specs=pl.BlockSpec((tm, d), lambda i: (i, 0)),
        scratch_shapes=[pltpu.VMEM((2, tm * TOP_K * per_row, LANES), F32),
                        pltpu.SemaphoreType.DMA((2,))],
        compiler_params=_cparams(("arbitrary",)),
        name="combine",
    )(pos3, pos3, y8, h, gates, wsg, wsu, wsd, g, b)


def _expert_tables(counts, nblk):
    bm = MOE_BM
    cnt = counts.reshape(N_EXPERTS).astype(I32)
    padded = (cnt + bm - 1) // bm * bm
    padded_end = jnp.cumsum(padded)
    padded_start = padded_end - padded
    block_rows = jnp.arange(nblk, dtype=I32) * bm
    block_e = jnp.sum((padded_end[None, :] <= block_rows[:, None]).astype(I32), axis=1)
    block_e = jnp.minimum(block_e, N_EXPERTS - 1)
    n_used = (padded_end[-1:] // bm).astype(I32)
    return padded_start, padded_start + cnt, padded - cnt, block_e, n_used


def kernel(x, w_in, gla_gate_w2, gla_gate_b, gla_norm_w, pool_w_group, pool_scale, w_out, ln1_g, ln1_b,
           router_w, router_bias, w_exp_gate, w_exp_up, w_exp_down, w_sh_gate, w_sh_up, w_sh_down, ln2_g, ln2_b):
    batch, seq, d = x.shape
    t = batch * seq
    h2d = x.reshape(t, d)
    for l in range(DEPTH):
        d_in = w_in.shape[2]
        w_in_b = jnp.pad(w_in[l], ((0, 0), (0, D_IN_PAD - d_in))).astype(BF16)
        w2p = jnp.pad(gla_gate_w2[l], ((0, LANES - GLA_GATE_RANK), (0, 0))).astype(BF16)
        proj = _inproj(h2d, w_in_b)
        mixed = _mixer(proj, batch, seq, w2p, gla_gate_b[l].reshape(1, -1), gla_norm_w[l].reshape(1, -1),
                       pool_w_group[l].astype(BF16), pool_scale[l].reshape(1, -1))
        h, h8 = _outproj_ln(mixed, w_out[l].astype(BF16), h2d, ln1_g[l].reshape(1, -1), ln1_b[l].reshape(1, -1))
        idx_t, gate_t, rank_t, counts = _router(h, router_w[l].T, router_bias[l].reshape(-1, 1))
        nblk = (t * TOP_K + N_EXPERTS * (MOE_BM - 1)) // MOE_BM
        start, pad_row, pad_n, block_e, n_used = _expert_tables(counts, nblk)
        pos_t = _positions(idx_t, rank_t, start.astype(F32).reshape(-1, 1))
        pos_tok = pos_t.T
        xs8 = _dispatch(pad_row, pad_n, n_used, pos_tok.reshape(t // DISP_TM, 1, DISP_TM * TOP_K), h8, nblk * MOE_BM)
        y8 = _experts(block_e, n_used, xs8, w_exp_gate[l], w_exp_up[l], w_exp_down[l])
        h2d = _combine(pos_tok.reshape(t // COMB_TM, 1, COMB_TM * TOP_K), y8, h, gate_t.T,
                       w_sh_gate[l].astype(BF16), w_sh_up[l].astype(BF16), w_sh_down[l].astype(BF16),
                       ln2_g[l].reshape(1, -1), ln2_b[l].reshape(1, -1))
    return h2d.reshape(batch, seq, d)
```

```python
import functools

import jax
import jax.numpy as jnp
from jax import lax
from jax.experimental import pallas as pl
from jax.experimental.pallas import tpu as pltpu

F32 = jnp.float32
BF16 = jnp.bfloat16
I32 = jnp.int32

POOL_WINDOWS = (2, 4, 8, 16)
POOL_GROUP_DIM = 128
POOL_WIDTH = 512
GLA_HEADS = 4
GLA_DK = 64
GLA_DV = 128
GLA_DK_TOTAL = 256
GLA_WIDTH = 512
GLA_GATE_RANK = 16
GLA_GATE_NORMALIZER = 16.0
GLA_CHUNK = 16
N_EXPERTS = 256
TOP_K = 8
N_GROUPS = 8
GROUP_SIZE = N_EXPERTS // N_GROUPS
TOPK_GROUPS = 4
ROUTED_SCALE = 2.5
DEPTH = 1
DEEPNORM_ALPHA = (2.0 * DEPTH) ** 0.25
LN_EPS = 1e-5
RMS_EPS = 1e-5

LANES = 128
SUBLANES = 8
VMEM_LIMIT = 56 * 1024 * 1024

PROJ_TM = 512
MIX_TS = 256
ROUTE_TM = 256
MOE_BM = 256
COMB_TM = 128
POS_TM = 512
DISP_TM = 256
ROW_TILES = 8
D_IN_PAD = 2944
GATE_COL_BLOCK = 16


def _cparams(sem):
    return pltpu.CompilerParams(dimension_semantics=sem, vmem_limit_bytes=VMEM_LIMIT)


def _silu(x):
    return x * (1.0 / (1.0 + jnp.exp(-x)))


def _layernorm(y, g, b):
    mu = jnp.mean(y, axis=-1, keepdims=True)
    yc = y - mu
    var = jnp.mean(yc * yc, axis=-1, keepdims=True)
    return yc * lax.rsqrt(var + LN_EPS) * g + b


def _inproj_body(x_ref, w_ref, o_ref):
    o_ref[...] = jnp.dot(x_ref[...].astype(BF16), w_ref[...], preferred_element_type=F32)


def _inproj(x2d, w_bf):
    t, d = x2d.shape
    n = w_bf.shape[1]
    return pl.pallas_call(
        _inproj_body,
        out_shape=jax.ShapeDtypeStruct((t, n), F32),
        grid=(t // PROJ_TM,),
        in_specs=[pl.BlockSpec((PROJ_TM, d), lambda i: (i, 0)),
                  pl.BlockSpec((d, n), lambda i: (0, 0))],
        out_specs=pl.BlockSpec((PROJ_TM, n), lambda i: (i, 0)),
        compiler_params=_cparams(("arbitrary",)),
        name="inproj",
    )(x2d, w_bf)


def _mixer_body(p_ref, q_ref, k_ref, v_ref, r_ref, gl_ref, w2_ref, gb_ref, nw_ref, pw_ref, ps_ref,
                o_ref, pbuf, state):
    ts = p_ref.shape[0]
    s_idx = pl.program_id(1)
    halo = POOL_WINDOWS[-1]

    @pl.when(s_idx == 0)
    def _():
        pbuf[pl.ds(0, halo), :] = jnp.zeros((halo, POOL_WIDTH), F32)
        state[...] = jnp.zeros(state.shape, F32)

    p = p_ref[...]
    pbuf[pl.ds(halo, ts), :] = p
    pos = s_idx * ts + lax.broadcasted_iota(I32, (ts, 1), 0)
    for g, w in enumerate(POOL_WINDOWS):
        c0 = g * POOL_GROUP_DIM
        acc = pbuf[pl.ds(halo, ts), pl.ds(c0, POOL_GROUP_DIM)]
        for j in range(1, w):
            acc = acc + pbuf[pl.ds(halo - j, ts), pl.ds(c0, POOL_GROUP_DIM)]
        cnt = jnp.minimum(pos + 1, w).astype(F32)
        mixed = acc / cnt - p[:, c0:c0 + POOL_GROUP_DIM]
        og = jnp.dot(mixed.astype(BF16), pw_ref[g], preferred_element_type=F32)
        o_ref[:, pl.ds(c0, POOL_GROUP_DIM)] = (og * ps_ref[:, pl.ds(c0, POOL_GROUP_DIM)]).astype(o_ref.dtype)
    pbuf[pl.ds(0, halo), :] = pbuf[pl.ds(ts, halo), :]

    nchunk = ts // GLA_CHUNK
    glog = jnp.dot(gl_ref[...].astype(BF16), w2_ref[...], preferred_element_type=F32) + gb_ref[...]
    gk = (jnp.minimum(glog, 0.0) - jnp.log(1.0 + jnp.exp(-jnp.abs(glog)))) * (1.0 / GLA_GATE_NORMALIZER)
    row = lax.broadcasted_iota(I32, (ts, 1), 0)
    rin = row % GLA_CHUNK
    b = gk
    sh = 1
    while sh < GLA_CHUNK:
        b = b + jnp.where(rin >= sh, pltpu.roll(b, sh, axis=0), 0.0)
        sh *= 2
    b3 = b.reshape(nchunk, GLA_CHUNK, GLA_DK_TOTAL)
    bmid = b3[:, GLA_CHUNK // 2 - 1:GLA_CHUNK // 2, :]
    blast = b3[:, GLA_CHUNK - 1:GLA_CHUNK, :]
    q3 = (q_ref[...] * (GLA_DK ** -0.5)).reshape(nchunk, GLA_CHUNK, GLA_DK_TOTAL)
    k3 = k_ref[...].reshape(nchunk, GLA_CHUNK, GLA_DK_TOTAL)
    qs = (q3 * jnp.exp(b3 - bmid)).reshape(ts, GLA_DK_TOTAL)
    ks = (k3 * jnp.exp(bmid - b3)).reshape(ts, GLA_DK_TOTAL)
    qd = (q3 * jnp.exp(b3)).reshape(ts, GLA_DK_TOTAL)
    kd = (k3 * jnp.exp(blast - b3)).reshape(ts, GLA_DK_TOTAL).astype(BF16)
    cdec = jnp.exp(blast).reshape(nchunk, GLA_DK_TOTAL)

    v = v_ref[...]
    vb = v.astype(BF16)
    lane = lax.broadcasted_iota(I32, (1, LANES), 1)
    head_lane = [lane < GLA_DK, lane >= GLA_DK]
    prow = lax.broadcasted_iota(I32, (LANES, 1), 0)
    head_row = prow < GLA_DK

    blk = LANES
    ri = lax.broadcasted_iota(I32, (blk, blk), 0)
    ci = lax.broadcasted_iota(I32, (blk, blk), 1)
    causal = (ri // GLA_CHUNK == ci // GLA_CHUNK) & (ri >= ci)
    o_intra = [[None] * (ts // blk) for _ in range(GLA_HEADS)]
    for rb in range(ts // blk):
        rs = slice(rb * blk, (rb + 1) * blk)
        for pair in range(GLA_HEADS // 2):
            ls = slice(pair * LANES, (pair + 1) * LANES)
            ks_p = ks[rs, ls].astype(BF16)
            for sub in range(2):
                h = pair * 2 + sub
                q_m = jnp.where(head_lane[sub], qs[rs, ls], 0.0).astype(BF16)
                sc = lax.dot_general(q_m, ks_p, (((1,), (1,)), ((), ())), preferred_element_type=F32)
                sc = jnp.where(causal, sc, 0.0).astype(BF16)
                o_intra[h][rb] = jnp.dot(sc, vb[rs, h * GLA_DV:(h + 1) * GLA_DV], preferred_element_type=F32)

    o_inter = [[None] * nchunk for _ in range(GLA_HEADS)]
    for pair in range(GLA_HEADS // 2):
        ls = slice(pair * LANES, (pair + 1) * LANES)
        dec_cols = cdec[:, ls].T
        st = state[pair]
        for c in range(nchunk):
            rs = slice(c * GLA_CHUNK, (c + 1) * GLA_CHUNK)
            st_b = st.astype(BF16)
            kv = []
            for sub in range(2):
                h = pair * 2 + sub
                q_m = jnp.where(head_lane[sub], qd[rs, ls], 0.0).astype(BF16)
                o_inter[h][c] = jnp.dot(q_m, st_b, preferred_element_type=F32)
                kv.append(lax.dot_general(kd[rs, ls], vb[rs, h * GLA_DV:(h + 1) * GLA_DV],
                                          (((0,), (0,)), ((), ())), preferred_element_type=F32))
            st = st * dec_cols[:, c:c + 1] + jnp.where(head_row, kv[0], kv[1])
        state[pair] = st

    nw = nw_ref[...]
    r = r_ref[...]
    for h in range(GLA_HEADS):
        o = jnp.concatenate(o_intra[h], axis=0) + jnp.concatenate(o_inter[h], axis=0)
        o = o * lax.rsqrt(jnp.mean(o * o, axis=-1, keepdims=True) + RMS_EPS) * nw
        o = o * _silu(r[:, h * GLA_DV:(h + 1) * GLA_DV])
        o_ref[:, pl.ds(POOL_WIDTH + h * GLA_DV, GLA_DV)] = o.astype(o_ref.dtype)


def _mixer(proj, batch, seq, w2p, gate_b, norm_w, pool_w, pool_scale):
    t = proj.shape[0]
    ts = MIX_TS
    nseq = seq // ts

    def rows(width, col_block):
        return pl.BlockSpec((ts, width), lambda bi, si: (bi * nseq + si, col_block))

    def full(shape):
        return pl.BlockSpec(shape, lambda bi, si: (0,) * len(shape))

    return pl.pallas_call(
        _mixer_body,
        out_shape=jax.ShapeDtypeStruct((t, POOL_WIDTH + GLA_WIDTH), BF16),
        grid=(batch, nseq),
        in_specs=[rows(POOL_WIDTH, 0),
                  rows(GLA_DK_TOTAL, 2),
                  rows(GLA_DK_TOTAL, 3),
                  rows(GLA_WIDTH, 2),
                  rows(GLA_WIDTH, 3),
                  rows(LANES, GATE_COL_BLOCK),
                  full(w2p.shape), full(gate_b.shape), full(norm_w.shape),
                  full(pool_w.shape), full(pool_scale.shape)],
        out_specs=pl.BlockSpec((ts, POOL_WIDTH + GLA_WIDTH), lambda bi, si: (bi * nseq + si, 0)),
        scratch_shapes=[pltpu.VMEM((ts + POOL_WINDOWS[-1], POOL_WIDTH), F32),
                        pltpu.VMEM((GLA_HEADS // 2, LANES, GLA_DV), F32)],
        compiler_params=_cparams(("arbitrary", "arbitrary")),
        name="mixer",
    )(proj, proj, proj, proj, proj, proj, w2p, gate_b, norm_w, pool_w, pool_scale)


def _outproj_body(m_ref, w_ref, x_ref, g_ref, b_ref, h_ref, h8_ref):
    tm, d = x_ref.shape
    y = DEEPNORM_ALPHA * x_ref[...] + jnp.dot(m_ref[...], w_ref[...], preferred_element_type=F32)
    h = _layernorm(y, g_ref[...], b_ref[...])
    h_ref[...] = h
    for s in range(d // LANES):
        h8_ref[pl.ds(s, tm, stride=d // LANES), :] = h[:, s * LANES:(s + 1) * LANES]


def _outproj_ln(mixed, w_bf, x2d, g, b):
    t, d = x2d.shape
    tm = PROJ_TM
    per_row = d // LANES
    return pl.pallas_call(
        _outproj_body,
        out_shape=(jax.ShapeDtypeStruct((t, d), F32), jax.ShapeDtypeStruct((t * per_row, LANES), F32)),
        grid=(t // tm,),
        in_specs=[pl.BlockSpec((tm, d), lambda i: (i, 0)),
                  pl.BlockSpec((d, d), lambda i: (0, 0)),
                  pl.BlockSpec((tm, d), lambda i: (i, 0)),
                  pl.BlockSpec((1, d), lambda i: (0, 0)),
                  pl.BlockSpec((1, d), lambda i: (0, 0))],
        out_specs=(pl.BlockSpec((tm, d), lambda i: (i, 0)),
                   pl.BlockSpec((tm * per_row, LANES), lambda i: (i, 0))),
        compiler_params=_cparams(("arbitrary",)),
        name="outproj_ln",
    )(mixed, w_bf, x2d, g, b)


def _first_argmax_rows(val, rowf, nrows):
    m = jnp.max(val, axis=0, keepdims=True)
    first = jnp.min(jnp.where(val == m, rowf, float(nrows)), axis=0, keepdims=True)
    return m, first, rowf == first


def _router_body(h_ref, wt_ref, bias_ref, idx_ref, gate_ref, rank_ref, cnt_ref, carry):
    tm = h_ref.shape[0]
    i = pl.program_id(0)

    @pl.when(i == 0)
    def _():
        carry[...] = jnp.zeros(carry.shape, F32)

    logits = lax.dot_general(wt_ref[...], h_ref[...], (((1,), (1,)), ((), ())),
                             preferred_element_type=F32, precision=lax.Precision.HIGHEST)
    scores = 1.0 / (1.0 + jnp.exp(-logits))
    biased = scores + bias_ref[...]
    neg = -jnp.inf

    grp = biased.reshape(N_GROUPS, GROUP_SIZE, tm)
    gi = lax.broadcasted_iota(I32, (N_GROUPS, GROUP_SIZE, tm), 1).astype(F32)
    g1 = jnp.max(grp, axis=1, keepdims=True)
    f1 = jnp.min(jnp.where(grp == g1, gi, float(GROUP_SIZE)), axis=1, keepdims=True)
    g2 = jnp.max(jnp.where(gi == f1, neg, grp), axis=1, keepdims=True)
    gscore = (g1 + g2).reshape(N_GROUPS, tm)

    growf = lax.broadcasted_iota(I32, (N_GROUPS, tm), 0).astype(F32)
    gsel = jnp.zeros((N_GROUPS, tm), F32)
    gval = gscore
    for _ in range(TOPK_GROUPS):
        _, _, pick = _first_argmax_rows(gval, growf, N_GROUPS)
        gsel = jnp.where(pick, 1.0, gsel)
        gval = jnp.where(pick, neg, gval)
    emask = jnp.broadcast_to(gsel.reshape(N_GROUPS, 1, tm), (N_GROUPS, GROUP_SIZE, tm)).reshape(N_EXPERTS, tm)

    rowf = lax.broadcasted_iota(I32, (N_EXPERTS, tm), 0).astype(F32)
    val = jnp.where(emask > 0.0, biased, neg)
    onehot = jnp.zeros((N_EXPERTS, tm), F32)
    picks, idxs, ws = [], [], []
    for _ in range(TOP_K):
        _, first, pick = _first_argmax_rows(val, rowf, N_EXPERTS)
        picks.append(pick)
        idxs.append(first)
        ws.append(jnp.sum(jnp.where(pick, scores, 0.0), axis=0, keepdims=True))
        onehot = jnp.where(pick, 1.0, onehot)
        val = jnp.where(pick, neg, val)
    w = jnp.concatenate(ws, axis=0)
    gate_ref[...] = w / jnp.sum(w, axis=0, keepdims=True) * ROUTED_SCALE
    idx_ref[...] = jnp.concatenate(idxs, axis=0).astype(I32)

    ti = lax.broadcasted_iota(I32, (tm, tm), 0)
    tj = lax.broadcasted_iota(I32, (tm, tm), 1)
    upper = jnp.where(ti < tj, 1.0, 0.0).astype(BF16)
    prefix = jnp.dot(onehot.astype(BF16), upper, preferred_element_type=F32) + carry[...]
    ranks = [jnp.sum(jnp.where(pk, prefix, 0.0), axis=0, keepdims=True) for pk in picks]
    rank_ref[...] = jnp.concatenate(ranks, axis=0).astype(I32)
    carry[...] = carry[...] + jnp.sum(onehot, axis=1, keepdims=True)
    cnt_ref[...] = carry[...]


def _router(h, wt, bias_col):
    t, d = h.shape
    tm = ROUTE_TM
    return pl.pallas_call(
        _router_body,
        out_shape=(jax.ShapeDtypeStruct((TOP_K, t), I32), jax.ShapeDtypeStruct((TOP_K, t), F32),
                   jax.ShapeDtypeStruct((TOP_K, t), I32), jax.ShapeDtypeStruct((N_EXPERTS, 1), F32)),
        grid=(t // tm,),
        in_specs=[pl.BlockSpec((tm, d), lambda i: (i, 0)),
                  pl.BlockSpec((N_EXPERTS, d), lambda i: (0, 0)),
                  pl.BlockSpec((N_EXPERTS, 1), lambda i: (0, 0))],
        out_specs=(pl.BlockSpec((TOP_K, tm), lambda i: (0, i)),
                   pl.BlockSpec((TOP_K, tm), lambda i: (0, i)),
                   pl.BlockSpec((TOP_K, tm), lambda i: (0, i)),
                   pl.BlockSpec((N_EXPERTS, 1), lambda i: (0, 0))),
        scratch_shapes=[pltpu.VMEM((N_EXPERTS, 1), F32)],
        compiler_params=_cparams(("arbitrary",)),
        name="router",
    )(h, wt, bias_col)


def _positions_body(idx_ref, rank_ref, start_ref, pos_ref):
    tm = idx_ref.shape[1]
    rowi = lax.broadcasted_iota(I32, (N_EXPERTS, tm), 0)
    start = start_ref[...]
    idx = idx_ref[...]
    rows = [jnp.sum(jnp.where(rowi == idx[k:k + 1, :], start, 0.0), axis=0, keepdims=True) for k in range(TOP_K)]
    pos_ref[...] = jnp.concatenate(rows, axis=0).astype(I32) + rank_ref[...]


def _positions(idx_t, rank_t, start_col):
    t = idx_t.shape[1]
    tm = POS_TM
    return pl.pallas_call(
        _positions_body,
        out_shape=jax.ShapeDtypeStruct((TOP_K, t), I32),
        grid=(t // tm,),
        in_specs=[pl.BlockSpec((TOP_K, tm), lambda i: (0, i)),
                  pl.BlockSpec((TOP_K, tm), lambda i: (0, i)),
                  pl.BlockSpec((N_EXPERTS, 1), lambda i: (0, 0))],
        out_specs=pl.BlockSpec((TOP_K, tm), lambda i: (0, i)),
        compiler_params=_cparams(("arbitrary",)),
        name="positions",
    )(idx_t, rank_t, start_col)


def _pad_fill_copy(zeros, xs8_ref, sem, row, nrows, per_row):
    return pltpu.make_async_copy(zeros.at[pl.ds(0, nrows * per_row), :],
                                 xs8_ref.at[pl.ds(pl.multiple_of(row * per_row, per_row), nrows * per_row), :], sem)


def _dispatch_body(pad_row_ref, pad_n_ref, nu_ref, pos_ref, h8_ref, xs8_ref, zeros, sem, pad_sem):
    i = pl.program_id(0)
    tm = h8_ref.shape[0] // ROW_TILES
    per_row = ROW_TILES
    half = MOE_BM // 2
    pad_bits = [1 << j for j in range(MOE_BM.bit_length() - 1)]
    n_half_blocks = xs8_ref.shape[0] // (half * per_row)

    def pad_pass(wait):
        def go(cp):
            if wait:
                cp.wait()
            else:
                cp.start()

        def body(e, carry):
            row = pad_row_ref[e]
            n = pad_n_ref[e]
            for bit in pad_bits:
                @pl.when((n & bit) != 0)
                def _():
                    go(_pad_fill_copy(zeros, xs8_ref, pad_sem, row + (n & (bit - 1)), bit, per_row))
            return carry
        lax.fori_loop(0, N_EXPERTS, body, 0)

        def tail(hb, carry):
            go(_pad_fill_copy(zeros, xs8_ref, pad_sem, hb * half, half, per_row))
            return carry
        lax.fori_loop(nu_ref[0] * 2, n_half_blocks, tail, 0)

    @pl.when(i == 0)
    def _():
        zeros[...] = jnp.zeros(zeros.shape, F32)
        pad_pass(False)

    def body(t, carry):
        src = h8_ref.at[pl.ds(pl.multiple_of(t * per_row, per_row), per_row), :]
        for k in range(TOP_K):
            p = pos_ref[0, 0, t * TOP_K + k]
            dst = xs8_ref.at[pl.ds(pl.multiple_of(p * per_row, per_row), per_row), :]
            pltpu.make_async_copy(src, dst, sem).start(priority=k % 2)
        return carry
    lax.fori_loop(0, tm, body, 0)

    for k in range(TOP_K):
        pltpu.make_async_copy(h8_ref, xs8_ref.at[pl.ds(0, tm * per_row), :], sem).wait()

    @pl.when(i == pl.num_programs(0) - 1)
    def _():
        pad_pass(True)


def _dispatch(pad_row, pad_n, n_used, pos3, h8, n_rows):
    nt, _, per_step = pos3.shape
    tm = per_step // TOP_K
    grid_spec = pltpu.PrefetchScalarGridSpec(
        num_scalar_prefetch=3,
        grid=(nt,),
        in_specs=[pl.BlockSpec((1, 1, per_step), lambda i, a, b, c: (i, 0, 0), memory_space=pltpu.SMEM),
                  pl.BlockSpec((tm * ROW_TILES, LANES), lambda i, a, b, c: (i, 0))],
        out_specs=pl.BlockSpec(memory_space=pl.ANY),
        scratch_shapes=[pltpu.VMEM((MOE_BM // 2 * ROW_TILES, LANES), F32),
                        pltpu.SemaphoreType.DMA, pltpu.SemaphoreType.DMA],
    )
    return pl.pallas_call(
        _dispatch_body,
        out_shape=jax.ShapeDtypeStruct((n_rows * ROW_TILES, LANES), F32),
        grid_spec=grid_spec,
        compiler_params=_cparams(("arbitrary",)),
        name="dispatch",
    )(pad_row, pad_n, n_used, pos3, h8)


def _weight_copies(e, w_hbm, stage, sem):
    return [pltpu.make_async_copy(w.at[e], st, sem.at[j]) for j, (w, st) in enumerate(zip(w_hbm, stage))]


def _experts_body(be_ref, nu_ref, nx_ref, x8_ref, wg_hbm, wu_hbm, wd_hbm, y8_ref,
                  sg, su, sd, wsem, wg_b, wu_b, wd_b):
    i = pl.program_id(0)
    bm = MOE_BM
    per_row = ROW_TILES
    w_hbm = (wg_hbm, wu_hbm, wd_hbm)
    stage = (sg, su, sd)

    @pl.when(i < nu_ref[0])
    def _():
        e = be_ref[i]
        prev_e = be_ref[jnp.maximum(i - 1, 0)]

        @pl.when((i == 0) | (e != prev_e))
        def _():
            @pl.when(i == 0)
            def _():
                for cp in _weight_copies(e, w_hbm, stage, wsem):
                    cp.start()
            for cp in _weight_copies(e, w_hbm, stage, wsem):
                cp.wait()
            wg_b[...] = sg[...].astype(BF16)
            wu_b[...] = su[...].astype(BF16)
            wd_b[...] = sd[...].astype(BF16)
            nxt = nx_ref[e]

            @pl.when(nxt >= 0)
            def _():
                for cp in _weight_copies(nxt, w_hbm, stage, wsem):
                    cp.start()

        x = jnp.concatenate([x8_ref[pl.ds(s, bm, stride=per_row), :] for s in range(per_row)], axis=1).astype(BF16)
        g = jnp.dot(x, wg_b[...], preferred_element_type=F32)
        u = jnp.dot(x, wu_b[...], preferred_element_type=F32)
        a = (_silu(g) * u).astype(BF16)
        y = jnp.dot(a, wd_b[...], preferred_element_type=F32)
        for s in range(per_row):
            y8_ref[pl.ds(s, bm, stride=per_row), :] = y[:, s * LANES:(s + 1) * LANES]

    @pl.when(i >= nu_ref[0])
    def _():
        y8_ref[...] = jnp.zeros(y8_ref.shape, F32)


def _experts(block_e, n_used, next_e, xs8, wg, wu, wd):
    nblk = block_e.shape[0]
    bm = MOE_BM
    e, d, de = wg.shape
    per_row = ROW_TILES

    def blk(i, be, nu, nx):
        return jnp.minimum(i, nu[0] - 1)

    grid_spec = pltpu.PrefetchScalarGridSpec(
        num_scalar_prefetch=3,
        grid=(nblk,),
        in_specs=[pl.BlockSpec((bm * per_row, LANES), lambda i, be, nu, nx: (blk(i, be, nu, nx), 0)),
                  pl.BlockSpec(memory_space=pl.ANY),
                  pl.BlockSpec(memory_space=pl.ANY),
                  pl.BlockSpec(memory_space=pl.ANY)],
        out_specs=pl.BlockSpec((bm * per_row, LANES), lambda i, be, nu, nx: (i, 0)),
        scratch_shapes=[pltpu.VMEM((d, de), F32), pltpu.VMEM((d, de), F32), pltpu.VMEM((de, d), F32),
                        pltpu.SemaphoreType.DMA((3,)),
                        pltpu.VMEM((d, de), BF16), pltpu.VMEM((d, de), BF16), pltpu.VMEM((de, d), BF16)],
    )
    return pl.pallas_call(
        _experts_body,
        out_shape=jax.ShapeDtypeStruct((nblk * bm * per_row, LANES), F32),
        grid_spec=grid_spec,
        compiler_params=_cparams(("arbitrary",)),
        name="experts",
    )(block_e, n_used, next_e, xs8, wg, wu, wd)


def _combine_body(pos_cur, pos_nxt, y8_ref, h_ref, gate_ref, wsg_ref, wsu_ref, wsd_ref, g_ref, b_ref,
                  o_ref, buf, sem):
    i = pl.program_id(0)
    nb = pl.num_programs(0)
    tm, d = h_ref.shape
    per_row = d // LANES
    slot = i % 2
    tok_rows = TOP_K * per_row

    def issue(pos_ref, sl):
        def body(t, carry):
            for k in range(TOP_K):
                p = pos_ref[0, 0, t * TOP_K + k]
                src = y8_ref.at[pl.ds(pl.multiple_of(p * per_row, per_row), per_row), :]
                dst = buf.at[sl, pl.ds(pl.multiple_of((k * tm + t) * per_row, per_row), per_row), :]
                pltpu.make_async_copy(src, dst, sem.at[sl]).start(priority=k % 2)
            return carry
        lax.fori_loop(0, tm, body, 0)

    @pl.when(i == 0)
    def _():
        issue(pos_cur, 0)

    @pl.when(i + 1 < nb)
    def _():
        issue(pos_nxt, 1 - slot)

    h = h_ref[...]
    hb = h.astype(BF16)
    sg = jnp.dot(hb, wsg_ref[...], preferred_element_type=F32)
    su = jnp.dot(hb, wsu_ref[...], preferred_element_type=F32)
    shared = jnp.dot((_silu(sg) * su).astype(BF16), wsd_ref[...], preferred_element_type=F32)

    pltpu.make_async_copy(y8_ref.at[pl.ds(0, tm * tok_rows), :], buf.at[slot], sem.at[slot]).wait()
    gates = gate_ref[...]
    cols = []
    for s in range(per_row):
        acc = None
        for k in range(TOP_K):
            term = gates[:, k:k + 1] * buf[slot, pl.ds(k * tm * per_row + s, tm, stride=per_row), :]
            acc = term if acc is None else acc + term
        cols.append(acc)
    routed = jnp.concatenate(cols, axis=1)
    o_ref[...] = _layernorm(DEEPNORM_ALPHA * h + (routed + shared), g_ref[...], b_ref[...])


def _combine(pos3, y8, h, gates, wsg, wsu, wsd, g, b):
    t, d = h.shape
    tm = COMB_TM
    nt = t // tm
    per_row = d // LANES
    ds_ = wsg.shape[1]
    return pl.pallas_call(
        _combine_body,
        out_shape=jax.ShapeDtypeStruct((t, d), F32),
        grid=(nt,),
        in_specs=[pl.BlockSpec((1, 1, tm * TOP_K), lambda i: (i, 0, 0), memory_space=pltpu.SMEM),
                  pl.BlockSpec((1, 1, tm * TOP_K), lambda i: (jnp.minimum(i + 1, nt - 1), 0, 0),
                               memory_space=pltpu.SMEM),
                  pl.BlockSpec(memory_space=pl.ANY),
                  pl.BlockSpec((tm, d), lambda i: (i, 0)),
                  pl.BlockSpec((tm, TOP_K), lambda i: (i, 0)),
                  pl.BlockSpec((d, ds_), lambda i: (0, 0)),
                  pl.BlockSpec((d, ds_), lambda i: (0, 0)),
                  pl.BlockSpec((ds_, d), lambda i: (0, 0)),
                  pl.BlockSpec((1, d), lambda i: (0, 0)),
                  pl.BlockSpec((1, d), lambda i: (0, 0))],
        out_specs=pl.BlockSpec((tm, d), lambda i: (i, 0)),
        scratch_shapes=[pltpu.VMEM((2, tm * TOP_K * per_row, LANES), F32),
                        pltpu.SemaphoreType.DMA((2,))],
        compiler_params=_cparams(("arbitrary",)),
        name="combine",
    )(pos3, pos3, y8, h, gates, wsg, wsu, wsd, g, b)


def _expert_tables(counts, nblk):
    bm = MOE_BM
    cnt = counts.reshape(N_EXPERTS).astype(I32)
    padded = (cnt + bm - 1) // bm * bm
    padded_end = jnp.cumsum(padded)
    padded_start = padded_end - padded
    block_rows = jnp.arange(nblk, dtype=I32) * bm
    block_e = jnp.sum((padded_end[None, :] <= block_rows[:, None]).astype(I32), axis=1)
    block_e = jnp.minimum(block_e, N_EXPERTS - 1)
    n_used = (padded_end[-1:] // bm).astype(I32)
    ids = jnp.where(cnt > 0, jnp.arange(N_EXPERTS, dtype=I32), N_EXPERTS)
    after = jnp.concatenate([lax.cummin(ids, reverse=True)[1:], jnp.full((1,), N_EXPERTS, I32)])
    next_e = jnp.where(after < N_EXPERTS, after, -1).astype(I32)
    return padded_start, padded_start + cnt, padded - cnt, block_e, n_used, next_e


def kernel(x, w_in, gla_gate_w2, gla_gate_b, gla_norm_w, pool_w_group, pool_scale, w_out, ln1_g, ln1_b,
           router_w, router_bias, w_exp_gate, w_exp_up, w_exp_down, w_sh_gate, w_sh_up, w_sh_down, ln2_g, ln2_b):
    batch, seq, d = x.shape
    t = batch * seq
    h2d = x.reshape(t, d)
    for l in range(DEPTH):
        d_in = w_in.shape[2]
        w_in_b = jnp.pad(w_in[l], ((0, 0), (0, D_IN_PAD - d_in))).astype(BF16)
        w2p = jnp.pad(gla_gate_w2[l], ((0, LANES - GLA_GATE_RANK), (0, 0))).astype(BF16)
        proj = _inproj(h2d, w_in_b)
        mixed = _mixer(proj, batch, seq, w2p, gla_gate_b[l].reshape(1, -1), gla_norm_w[l].reshape(1, -1),
                       pool_w_group[l].astype(BF16), pool_scale[l].reshape(1, -1))
        h, h8 = _outproj_ln(mixed, w_out[l].astype(BF16), h2d, ln1_g[l].reshape(1, -1), ln1_b[l].reshape(1, -1))
        idx_t, gate_t, rank_t, counts = _router(h, router_w[l].T, router_bias[l].reshape(-1, 1))
        nblk = (t * TOP_K + N_EXPERTS * (MOE_BM - 1)) // MOE_BM
        start, pad_row, pad_n, block_e, n_used, next_e = _expert_tables(counts, nblk)
        pos_t = _positions(idx_t, rank_t, start.astype(F32).reshape(-1, 1))
        pos_tok = pos_t.T
        xs8 = _dispatch(pad_row, pad_n, n_used, pos_tok.reshape(t // DISP_TM, 1, DISP_TM * TOP_K), h8, nblk * MOE_BM)
        y8 = _experts(block_e, n_used, next_e, xs8, w_exp_gate[l], w_exp_up[l], w_exp_down[l])
        h2d = _combine(pos_tok.reshape(t // COMB_TM, 1, COMB_TM * TOP_K), y8, h, gate_t.T,
                       w_sh_gate[l].astype(BF16), w_sh_up[l].astype(BF16), w_sh_down[l].astype(BF16),
                       ln2_g[l].reshape(1, -1), ln2_b[l].reshape(1, -1))
    return h2d.reshape(batch, seq, d)
```

```python
import functools

import jax
import jax.numpy as jnp
from jax import lax
from jax.experimental import pallas as pl
from jax.experimental.pallas import tpu as pltpu

F32 = jnp.float32
BF16 = jnp.bfloat16
I32 = jnp.int32

POOL_WINDOWS = (2, 4, 8, 16)
POOL_GROUP_DIM = 128
POOL_WIDTH = 512
GLA_HEADS = 4
GLA_DK = 64
GLA_DV = 128
GLA_DK_TOTAL = 256
GLA_WIDTH = 512
GLA_GATE_RANK = 16
GLA_GATE_NORMALIZER = 16.0
GLA_CHUNK = 16
N_EXPERTS = 256
TOP_K = 8
N_GROUPS = 8
GROUP_SIZE = N_EXPERTS // N_GROUPS
TOPK_GROUPS = 4
ROUTED_SCALE = 2.5
DEPTH = 1
DEEPNORM_ALPHA = (2.0 * DEPTH) ** 0.25
LN_EPS = 1e-5
RMS_EPS = 1e-5

LANES = 128
SUBLANES = 8
VMEM_LIMIT = 56 * 1024 * 1024

PROJ_TM = 512
MIX_TS = 256
ROUTE_TM = 256
MOE_BM = 256
COMB_TM = 128
POS_TM = 512
DISP_TM = 256
ROW_TILES = 8
X_RING = 3
D_IN_PAD = 2944
GATE_COL_BLOCK = 16


def _cparams(sem):
    return pltpu.CompilerParams(dimension_semantics=sem, vmem_limit_bytes=VMEM_LIMIT)


def _silu(x):
    return x * (1.0 / (1.0 + jnp.exp(-x)))


def _layernorm(y, g, b):
    mu = jnp.mean(y, axis=-1, keepdims=True)
    yc = y - mu
    var = jnp.mean(yc * yc, axis=-1, keepdims=True)
    return yc * lax.rsqrt(var + LN_EPS) * g + b


def _inproj_body(x_ref, w_ref, o_ref):
    o_ref[...] = jnp.dot(x_ref[...].astype(BF16), w_ref[...], preferred_element_type=F32)


def _inproj(x2d, w_bf):
    t, d = x2d.shape
    n = w_bf.shape[1]
    return pl.pallas_call(
        _inproj_body,
        out_shape=jax.ShapeDtypeStruct((t, n), F32),
        grid=(t // PROJ_TM,),
        in_specs=[pl.BlockSpec((PROJ_TM, d), lambda i: (i, 0)),
                  pl.BlockSpec((d, n), lambda i: (0, 0))],
        out_specs=pl.BlockSpec((PROJ_TM, n), lambda i: (i, 0)),
        compiler_params=_cparams(("arbitrary",)),
        name="inproj",
    )(x2d, w_bf)


def _mixer_body(p_ref, q_ref, k_ref, v_ref, r_ref, gl_ref, w2_ref, gb_ref, nw_ref, pw_ref, ps_ref,
                o_ref, pbuf, state, kvbuf, sall):
    ts = p_ref.shape[0]
    s_idx = pl.program_id(1)
    halo = POOL_WINDOWS[-1]

    @pl.when(s_idx == 0)
    def _():
        pbuf[pl.ds(0, halo), :] = jnp.zeros((halo, POOL_WIDTH), F32)
        state[...] = jnp.zeros(state.shape, F32)

    p = p_ref[...]
    pbuf[pl.ds(halo, ts), :] = p
    pos = s_idx * ts + lax.broadcasted_iota(I32, (ts, 1), 0)
    for g, w in enumerate(POOL_WINDOWS):
        c0 = g * POOL_GROUP_DIM
        acc = pbuf[pl.ds(halo, ts), pl.ds(c0, POOL_GROUP_DIM)]
        for j in range(1, w):
            acc = acc + pbuf[pl.ds(halo - j, ts), pl.ds(c0, POOL_GROUP_DIM)]
        cnt = jnp.minimum(pos + 1, w).astype(F32)
        mixed = acc / cnt - p[:, c0:c0 + POOL_GROUP_DIM]
        og = jnp.dot(mixed.astype(BF16), pw_ref[g], preferred_element_type=F32)
        o_ref[:, pl.ds(c0, POOL_GROUP_DIM)] = (og * ps_ref[:, pl.ds(c0, POOL_GROUP_DIM)]).astype(o_ref.dtype)
    pbuf[pl.ds(0, halo), :] = pbuf[pl.ds(ts, halo), :]

    nchunk = ts // GLA_CHUNK
    glog = jnp.dot(gl_ref[...].astype(BF16), w2_ref[...], preferred_element_type=F32) + gb_ref[...]
    gk = (jnp.minimum(glog, 0.0) - jnp.log(1.0 + jnp.exp(-jnp.abs(glog)))) * (1.0 / GLA_GATE_NORMALIZER)
    row = lax.broadcasted_iota(I32, (ts, 1), 0)
    rin = row % GLA_CHUNK
    b = gk
    sh = 1
    while sh < GLA_CHUNK:
        b = b + jnp.where(rin >= sh, pltpu.roll(b, sh, axis=0), 0.0)
        sh *= 2
    b3 = b.reshape(nchunk, GLA_CHUNK, GLA_DK_TOTAL)
    bmid = b3[:, GLA_CHUNK // 2 - 1:GLA_CHUNK // 2, :]
    blast = b3[:, GLA_CHUNK - 1:GLA_CHUNK, :]
    q3 = (q_ref[...] * (GLA_DK ** -0.5)).reshape(nchunk, GLA_CHUNK, GLA_DK_TOTAL)
    k3 = k_ref[...].reshape(nchunk, GLA_CHUNK, GLA_DK_TOTAL)
    qs = (q3 * jnp.exp(b3 - bmid)).reshape(ts, GLA_DK_TOTAL)
    ks = (k3 * jnp.exp(bmid - b3)).reshape(ts, GLA_DK_TOTAL)
    qd = (q3 * jnp.exp(b3)).reshape(ts, GLA_DK_TOTAL)
    kd = (k3 * jnp.exp(blast - b3)).reshape(ts, GLA_DK_TOTAL)
    cdec = jnp.exp(blast).reshape(nchunk, GLA_DK_TOTAL)

    v = v_ref[...]
    vb = v.astype(BF16)
    lane = lax.broadcasted_iota(I32, (1, LANES), 1)
    head_lane = [lane < GLA_DK, lane >= GLA_DK]

    blk = LANES
    ri = lax.broadcasted_iota(I32, (blk, blk), 0)
    ci = lax.broadcasted_iota(I32, (blk, blk), 1)
    causal = (ri // GLA_CHUNK == ci // GLA_CHUNK) & (ri >= ci)
    o_intra = [[None] * (ts // blk) for _ in range(GLA_HEADS)]
    for rb in range(ts // blk):
        rs = slice(rb * blk, (rb + 1) * blk)
        for pair in range(GLA_HEADS // 2):
            ls = slice(pair * LANES, (pair + 1) * LANES)
            ks_p = ks[rs, ls].astype(BF16)
            for sub in range(2):
                h = pair * 2 + sub
                q_m = jnp.where(head_lane[sub], qs[rs, ls], 0.0).astype(BF16)
                sc = lax.dot_general(q_m, ks_p, (((1,), (1,)), ((), ())), preferred_element_type=F32)
                sc = jnp.where(causal, sc, 0.0).astype(BF16)
                o_intra[h][rb] = jnp.dot(sc, vb[rs, h * GLA_DV:(h + 1) * GLA_DV], preferred_element_type=F32)

    cpb = blk // GLA_CHUNK
    chunk_of_col = lax.broadcasted_iota(I32, (1, blk), 1) // GLA_CHUNK
    stack_head_row = (lax.broadcasted_iota(I32, (cpb * LANES, 1), 0) % LANES) < GLA_DK
    for rb in range(ts // blk):
        rs = slice(rb * blk, (rb + 1) * blk)
        for pair in range(GLA_HEADS // 2):
            ls = slice(pair * LANES, (pair + 1) * LANES)
            kd_t = kd[rs, ls].T
            lhs = jnp.concatenate([jnp.where(chunk_of_col == c, kd_t, 0.0) for c in range(cpb)],
                                  axis=0).astype(BF16)
            inc = [jnp.dot(lhs, vb[rs, (pair * 2 + sub) * GLA_DV:(pair * 2 + sub + 1) * GLA_DV],
                           preferred_element_type=F32) for sub in range(2)]
            stacked = jnp.where(stack_head_row, inc[0], inc[1])
            for c in range(cpb):
                kvbuf[pair, rb * cpb + c] = stacked[c * LANES:(c + 1) * LANES]
    for pair in range(GLA_HEADS // 2):
        ls = slice(pair * LANES, (pair + 1) * LANES)
        dec_cols = cdec[:, ls].T
        st = state[pair]
        for c in range(nchunk):
            sall[pair, c] = st.astype(BF16)
            st = st * dec_cols[:, c:c + 1] + kvbuf[pair, c]
        state[pair] = st
    o_inter = [[None] * nchunk for _ in range(GLA_HEADS)]
    for pair in range(GLA_HEADS // 2):
        ls = slice(pair * LANES, (pair + 1) * LANES)
        for c in range(nchunk):
            rs = slice(c * GLA_CHUNK, (c + 1) * GLA_CHUNK)
            for sub in range(2):
                q_m = jnp.where(head_lane[sub], qd[rs, ls], 0.0).astype(BF16)
                o_inter[pair * 2 + sub][c] = jnp.dot(q_m, sall[pair, c], preferred_element_type=F32)

    nw = nw_ref[...]
    r = r_ref[...]
    for h in range(GLA_HEADS):
        o = jnp.concatenate(o_intra[h], axis=0) + jnp.concatenate(o_inter[h], axis=0)
        o = o * lax.rsqrt(jnp.mean(o * o, axis=-1, keepdims=True) + RMS_EPS) * nw
        o = o * _silu(r[:, h * GLA_DV:(h + 1) * GLA_DV])
        o_ref[:, pl.ds(POOL_WIDTH + h * GLA_DV, GLA_DV)] = o.astype(o_ref.dtype)


def _mixer(proj, batch, seq, w2p, gate_b, norm_w, pool_w, pool_scale):
    t = proj.shape[0]
    ts = MIX_TS
    nseq = seq // ts

    def rows(width, col_block):
        return pl.BlockSpec((ts, width), lambda bi, si: (bi * nseq + si, col_block))

    def full(shape):
        return pl.BlockSpec(shape, lambda bi, si: (0,) * len(shape))

    return pl.pallas_call(
        _mixer_body,
        out_shape=jax.ShapeDtypeStruct((t, POOL_WIDTH + GLA_WIDTH), BF16),
        grid=(batch, nseq),
        in_specs=[rows(POOL_WIDTH, 0),
                  rows(GLA_DK_TOTAL, 2),
                  rows(GLA_DK_TOTAL, 3),
                  rows(GLA_WIDTH, 2),
                  rows(GLA_WIDTH, 3),
                  rows(LANES, GATE_COL_BLOCK),
                  full(w2p.shape), full(gate_b.shape), full(norm_w.shape),
                  full(pool_w.shape), full(pool_scale.shape)],
        out_specs=pl.BlockSpec((ts, POOL_WIDTH + GLA_WIDTH), lambda bi, si: (bi * nseq + si, 0)),
        scratch_shapes=[pltpu.VMEM((ts + POOL_WINDOWS[-1], POOL_WIDTH), F32),
                        pltpu.VMEM((GLA_HEADS // 2, LANES, GLA_DV), F32),
                        pltpu.VMEM((GLA_HEADS // 2, ts // GLA_CHUNK, LANES, GLA_DV), F32),
                        pltpu.VMEM((GLA_HEADS // 2, ts // GLA_CHUNK, LANES, GLA_DV), BF16)],
        compiler_params=_cparams(("arbitrary", "arbitrary")),
        name="mixer",
    )(proj, proj, proj, proj, proj, proj, w2p, gate_b, norm_w, pool_w, pool_scale)


def _outproj_body(m_ref, w_ref, x_ref, g_ref, b_ref, h_ref, h8_ref):
    tm, d = x_ref.shape
    y = DEEPNORM_ALPHA * x_ref[...] + jnp.dot(m_ref[...], w_ref[...], preferred_element_type=F32)
    h = _layernorm(y, g_ref[...], b_ref[...])
    h_ref[...] = h
    for s in range(d // LANES):
        h8_ref[pl.ds(s, tm, stride=d // LANES), :] = h[:, s * LANES:(s + 1) * LANES]


def _outproj_ln(mixed, w_bf, x2d, g, b):
    t, d = x2d.shape
    tm = PROJ_TM
    per_row = d // LANES
    return pl.pallas_call(
        _outproj_body,
        out_shape=(jax.ShapeDtypeStruct((t, d), F32), jax.ShapeDtypeStruct((t * per_row, LANES), F32)),
        grid=(t // tm,),
        in_specs=[pl.BlockSpec((tm, d), lambda i: (i, 0)),
                  pl.BlockSpec((d, d), lambda i: (0, 0)),
                  pl.BlockSpec((tm, d), lambda i: (i, 0)),
                  pl.BlockSpec((1, d), lambda i: (0, 0)),
                  pl.BlockSpec((1, d), lambda i: (0, 0))],
        out_specs=(pl.BlockSpec((tm, d), lambda i: (i, 0)),
                   pl.BlockSpec((tm * per_row, LANES), lambda i: (i, 0))),
        compiler_params=_cparams(("arbitrary",)),
        name="outproj_ln",
    )(mixed, w_bf, x2d, g, b)


def _first_argmax_rows(val, rowf, nrows):
    m = jnp.max(val, axis=0, keepdims=True)
    first = jnp.min(jnp.where(val == m, rowf, float(nrows)), axis=0, keepdims=True)
    return m, first, rowf == first


def _router_body(h_ref, wt_ref, bias_ref, idx_ref, gate_ref, rank_ref, cnt_ref, carry):
    tm = h_ref.shape[0]
    i = pl.program_id(0)

    @pl.when(i == 0)
    def _():
        carry[...] = jnp.zeros(carry.shape, F32)

    logits = lax.dot_general(wt_ref[...], h_ref[...], (((1,), (1,)), ((), ())),
                             preferred_element_type=F32, precision=lax.Precision.HIGHEST)
    scores = 1.0 / (1.0 + jnp.exp(-logits))
    biased = scores + bias_ref[...]
    neg = -jnp.inf

    grp = biased.reshape(N_GROUPS, GROUP_SIZE, tm)
    gi = lax.broadcasted_iota(I32, (N_GROUPS, GROUP_SIZE, tm), 1).astype(F32)
    g1 = jnp.max(grp, axis=1, keepdims=True)
    f1 = jnp.min(jnp.where(grp == g1, gi, float(GROUP_SIZE)), axis=1, keepdims=True)
    g2 = jnp.max(jnp.where(gi == f1, neg, grp), axis=1, keepdims=True)
    gscore = (g1 + g2).reshape(N_GROUPS, tm)

    growf = lax.broadcasted_iota(I32, (N_GROUPS, tm), 0).astype(F32)
    gsel = jnp.zeros((N_GROUPS, tm), F32)
    gval = gscore
    for _ in range(TOPK_GROUPS):
        _, _, pick = _first_argmax_rows(gval, growf, N_GROUPS)
        gsel = jnp.where(pick, 1.0, gsel)
        gval = jnp.where(pick, neg, gval)
    emask = jnp.broadcast_to(gsel.reshape(N_GROUPS, 1, tm), (N_GROUPS, GROUP_SIZE, tm)).reshape(N_EXPERTS, tm)

    rowf = lax.broadcasted_iota(I32, (N_EXPERTS, tm), 0).astype(F32)
    val = jnp.where(emask > 0.0, biased, neg)
    onehot = jnp.zeros((N_EXPERTS, tm), F32)
    picks, idxs, ws = [], [], []
    for _ in range(TOP_K):
        _, first, pick = _first_argmax_rows(val, rowf, N_EXPERTS)
        picks.append(pick)
        idxs.append(first)
        ws.append(jnp.sum(jnp.where(pick, scores, 0.0), axis=0, keepdims=True))
        onehot = jnp.where(pick, 1.0, onehot)
        val = jnp.where(pick, neg, val)
    w = jnp.concatenate(ws, axis=0)
    gate_ref[...] = w / jnp.sum(w, axis=0, keepdims=True) * ROUTED_SCALE
    idx_ref[...] = jnp.concatenate(idxs, axis=0).astype(I32)

    ti = lax.broadcasted_iota(I32, (tm, tm), 0)
    tj = lax.broadcasted_iota(I32, (tm, tm), 1)
    upper = jnp.where(ti < tj, 1.0, 0.0).astype(BF16)
    prefix = jnp.dot(onehot.astype(BF16), upper, preferred_element_type=F32) + carry[...]
    ranks = [jnp.sum(jnp.where(pk, prefix, 0.0), axis=0, keepdims=True) for pk in picks]
    rank_ref[...] = jnp.concatenate(ranks, axis=0).astype(I32)
    carry[...] = carry[...] + jnp.sum(onehot, axis=1, keepdims=True)
    cnt_ref[...] = carry[...]


def _router(h, wt, bias_col):
    t, d = h.shape
    tm = ROUTE_TM
    return pl.pallas_call(
        _router_body,
        out_shape=(jax.ShapeDtypeStruct((TOP_K, t), I32), jax.ShapeDtypeStruct((TOP_K, t), F32),
                   jax.ShapeDtypeStruct((TOP_K, t), I32), jax.ShapeDtypeStruct((N_EXPERTS, 1), F32)),
        grid=(t // tm,),
        in_specs=[pl.BlockSpec((tm, d), lambda i: (i, 0)),
                  pl.BlockSpec((N_EXPERTS, d), lambda i: (0, 0)),
                  pl.BlockSpec((N_EXPERTS, 1), lambda i: (0, 0))],
        out_specs=(pl.BlockSpec((TOP_K, tm), lambda i: (0, i)),
                   pl.BlockSpec((TOP_K, tm), lambda i: (0, i)),
                   pl.BlockSpec((TOP_K, tm), lambda i: (0, i)),
                   pl.BlockSpec((N_EXPERTS, 1), lambda i: (0, 0))),
        scratch_shapes=[pltpu.VMEM((N_EXPERTS, 1), F32)],
        compiler_params=_cparams(("arbitrary",)),
        name="router",
    )(h, wt, bias_col)


def _positions_body(idx_ref, rank_ref, start_ref, pos_ref):
    tm = idx_ref.shape[1]
    rowi = lax.broadcasted_iota(I32, (N_EXPERTS, tm), 0)
    start = start_ref[...]
    idx = idx_ref[...]
    rows = [jnp.sum(jnp.where(rowi == idx[k:k + 1, :], start, 0.0), axis=0, keepdims=True) for k in range(TOP_K)]
    pos_ref[...] = jnp.concatenate(rows, axis=0).astype(I32) + rank_ref[...]


def _positions(idx_t, rank_t, start_col):
    t = idx_t.shape[1]
    tm = POS_TM
    return pl.pallas_call(
        _positions_body,
        out_shape=jax.ShapeDtypeStruct((TOP_K, t), I32),
        grid=(t // tm,),
        in_specs=[pl.BlockSpec((TOP_K, tm), lambda i: (0, i)),
                  pl.BlockSpec((TOP_K, tm), lambda i: (0, i)),
                  pl.BlockSpec((N_EXPERTS, 1), lambda i: (0, 0))],
        out_specs=pl.BlockSpec((TOP_K, tm), lambda i: (0, i)),
        compiler_params=_cparams(("arbitrary",)),
        name="positions",
    )(idx_t, rank_t, start_col)


def _pad_fill_copy(zeros, xs8_ref, sem, row, nrows, per_row):
    return pltpu.make_async_copy(zeros.at[pl.ds(0, nrows * per_row), :],
                                 xs8_ref.at[pl.ds(pl.multiple_of(row * per_row, per_row), nrows * per_row), :], sem)


def _dispatch_body(pad_row_ref, pad_n_ref, nu_ref, pos_ref, h8_ref, xs8_ref, zeros, sem, pad_sem):
    i = pl.program_id(0)
    tm = h8_ref.shape[0] // ROW_TILES
    per_row = ROW_TILES
    half = MOE_BM // 2
    pad_bits = [1 << j for j in range(MOE_BM.bit_length() - 1)]
    n_half_blocks = xs8_ref.shape[0] // (half * per_row)

    def pad_pass(wait):
        def go(cp):
            if wait:
                cp.wait()
            else:
                cp.start()

        def body(e, carry):
            row = pad_row_ref[e]
            n = pad_n_ref[e]
            for bit in pad_bits:
                @pl.when((n & bit) != 0)
                def _():
                    go(_pad_fill_copy(zeros, xs8_ref, pad_sem, row + (n & (bit - 1)), bit, per_row))
            return carry
        lax.fori_loop(0, N_EXPERTS, body, 0)

        def tail(hb, carry):
            go(_pad_fill_copy(zeros, xs8_ref, pad_sem, hb * half, half, per_row))
            return carry
        lax.fori_loop(nu_ref[0] * 2, n_half_blocks, tail, 0)

    @pl.when(i == 0)
    def _():
        zeros[...] = jnp.zeros(zeros.shape, F32)
        pad_pass(False)

    def body(t, carry):
        src = h8_ref.at[pl.ds(pl.multiple_of(t * per_row, per_row), per_row), :]
        for k in range(TOP_K):
            p = pos_ref[0, 0, t * TOP_K + k]
            dst = xs8_ref.at[pl.ds(pl.multiple_of(p * per_row, per_row), per_row), :]
            pltpu.make_async_copy(src, dst, sem).start(priority=k % 2)
        return carry
    lax.fori_loop(0, tm, body, 0)

    for k in range(TOP_K):
        pltpu.make_async_copy(h8_ref, xs8_ref.at[pl.ds(0, tm * per_row), :], sem).wait()

    @pl.when(i == pl.num_programs(0) - 1)
    def _():
        pad_pass(True)


def _dispatch(pad_row, pad_n, n_used, pos3, h8, n_rows):
    nt, _, per_step = pos3.shape
    tm = per_step // TOP_K
    grid_spec = pltpu.PrefetchScalarGridSpec(
        num_scalar_prefetch=3,
        grid=(nt,),
        in_specs=[pl.BlockSpec((1, 1, per_step), lambda i, a, b, c: (i, 0, 0), memory_space=pltpu.SMEM),
                  pl.BlockSpec((tm * ROW_TILES, LANES), lambda i, a, b, c: (i, 0))],
        out_specs=pl.BlockSpec(memory_space=pl.ANY),
        scratch_shapes=[pltpu.VMEM((MOE_BM // 2 * ROW_TILES, LANES), F32),
                        pltpu.SemaphoreType.DMA, pltpu.SemaphoreType.DMA],
    )
    return pl.pallas_call(
        _dispatch_body,
        out_shape=jax.ShapeDtypeStruct((n_rows * ROW_TILES, LANES), F32),
        grid_spec=grid_spec,
        compiler_params=_cparams(("arbitrary",)),
        name="dispatch",
    )(pad_row, pad_n, n_used, pos3, h8)


def _weight_copies(e, w_hbm, stage, sem):
    return [pltpu.make_async_copy(w.at[e], st, sem.at[j]) for j, (w, st) in enumerate(zip(w_hbm, stage))]


def _row_block_copy(xs8_hbm, xbuf, xsem, j):
    rows = xbuf.shape[1]
    slot = j % X_RING
    return pltpu.make_async_copy(xs8_hbm.at[pl.ds(pl.multiple_of(j * rows, rows), rows), :], xbuf.at[slot],
                                 xsem.at[slot])


def _experts_body(be_ref, nu_ref, nx_ref, xs8_hbm, wg_hbm, wu_hbm, wd_hbm, y8_ref,
                  xbuf, xsem, sg, su, sd, wsem, wg_b, wu_b, wd_b):
    i = pl.program_id(0)
    bm = MOE_BM
    per_row = ROW_TILES
    w_hbm = (wg_hbm, wu_hbm, wd_hbm)
    stage = (sg, su, sd)
    n_used = nu_ref[0]

    @pl.when(i < n_used)
    def _():
        e = be_ref[i]
        prev_e = be_ref[jnp.maximum(i - 1, 0)]

        @pl.when(i == 0)
        def _():
            for j in range(X_RING - 1):
                @pl.when(j < n_used)
                def _():
                    _row_block_copy(xs8_hbm, xbuf, xsem, j).start()

        @pl.when(i + X_RING - 1 < n_used)
        def _():
            _row_block_copy(xs8_hbm, xbuf, xsem, i + X_RING - 1).start()

        @pl.when((i == 0) | (e != prev_e))
        def _():
            @pl.when(i == 0)
            def _():
                for cp in _weight_copies(e, w_hbm, stage, wsem):
                    cp.start()
            for cp in _weight_copies(e, w_hbm, stage, wsem):
                cp.wait()
            wg_b[...] = sg[...].astype(BF16)
            wu_b[...] = su[...].astype(BF16)
            wd_b[...] = sd[...].astype(BF16)
            nxt = nx_ref[e]

            @pl.when(nxt >= 0)
            def _():
                for cp in _weight_copies(nxt, w_hbm, stage, wsem):
                    cp.start()

        _row_block_copy(xs8_hbm, xbuf, xsem, i).wait()
        slot = i % X_RING
        x = jnp.concatenate([xbuf[slot, pl.ds(s, bm, stride=per_row), :] for s in range(per_row)],
                            axis=1).astype(BF16)
        g = jnp.dot(x, wg_b[...], preferred_element_type=F32)
        u = jnp.dot(x, wu_b[...], preferred_element_type=F32)
        a = (_silu(g) * u).astype(BF16)
        y = jnp.dot(a, wd_b[...], preferred_element_type=F32)
        for s in range(per_row):
            y8_ref[pl.ds(s, bm, stride=per_row), :] = y[:, s * LANES:(s + 1) * LANES]

    @pl.when(i >= nu_ref[0])
    def _():
        y8_ref[...] = jnp.zeros(y8_ref.shape, F32)


def _experts(block_e, n_used, next_e, xs8, wg, wu, wd):
    nblk = block_e.shape[0]
    bm = MOE_BM
    e, d, de = wg.shape
    per_row = ROW_TILES

    grid_spec = pltpu.PrefetchScalarGridSpec(
        num_scalar_prefetch=3,
        grid=(nblk,),
        in_specs=[pl.BlockSpec(memory_space=pl.ANY),
                  pl.BlockSpec(memory_space=pl.ANY),
                  pl.BlockSpec(memory_space=pl.ANY),
                  pl.BlockSpec(memory_space=pl.ANY)],
        out_specs=pl.BlockSpec((bm * per_row, LANES), lambda i, be, nu, nx: (i, 0)),
        scratch_shapes=[pltpu.VMEM((X_RING, bm * per_row, LANES), F32), pltpu.SemaphoreType.DMA((X_RING,)),
                        pltpu.VMEM((d, de), F32), pltpu.VMEM((d, de), F32), pltpu.VMEM((de, d), F32),
                        pltpu.SemaphoreType.DMA((3,)),
                        pltpu.VMEM((d, de), BF16), pltpu.VMEM((d, de), BF16), pltpu.VMEM((de, d), BF16)],
    )
    return pl.pallas_call(
        _experts_body,
        out_shape=jax.ShapeDtypeStruct((nblk * bm * per_row, LANES), F32),
        grid_spec=grid_spec,
        compiler_params=_cparams(("arbitrary",)),
        name="experts",
    )(block_e, n_used, next_e, xs8, wg, wu, wd)


def _combine_body(pos_cur, pos_nxt, y8_ref, h_ref, gate_ref, wsg_ref, wsu_ref, wsd_ref, g_ref, b_ref,
                  o_ref, buf, sem):
    i = pl.program_id(0)
    nb = pl.num_programs(0)
    tm, d = h_ref.shape
    per_row = d // LANES
    slot = i % 2
    tok_rows = TOP_K * per_row

    def issue(pos_ref, sl):
        def body(t, carry):
            for k in range(TOP_K):
                p = pos_ref[0, 0, t * TOP_K + k]
                src = y8_ref.at[pl.ds(pl.multiple_of(p * per_row, per_row), per_row), :]
                dst = buf.at[sl, pl.ds(pl.multiple_of((k * tm + t) * per_row, per_row), per_row), :]
                pltpu.make_async_copy(src, dst, sem.at[sl]).start(priority=k % 2)
            return carry
        lax.fori_loop(0, tm, body, 0)

    @pl.when(i == 0)
    def _():
        issue(pos_cur, 0)

    @pl.when(i + 1 < nb)
    def _():
        issue(pos_nxt, 1 - slot)

    h = h_ref[...]
    hb = h.astype(BF16)
    sg = jnp.dot(hb, wsg_ref[...], preferred_element_type=F32)
    su = jnp.dot(hb, wsu_ref[...], preferred_element_type=F32)
    shared = jnp.dot((_silu(sg) * su).astype(BF16), wsd_ref[...], preferred_element_type=F32)

    pltpu.make_async_copy(y8_ref.at[pl.ds(0, tm * tok_rows), :], buf.at[slot], sem.at[slot]).wait()
    gates = gate_ref[...]
    cols = []
    for s in range(per_row):
        acc = None
        for k in range(TOP_K):
            term = gates[:, k:k + 1] * buf[slot, pl.ds(k * tm * per_row + s, tm, stride=per_row), :]
            acc = term if acc is None else acc + term
        cols.append(acc)
    routed = jnp.concatenate(cols, axis=1)
    o_ref[...] = _layernorm(DEEPNORM_ALPHA * h + (routed + shared), g_ref[...], b_ref[...])


def _combine(pos3, y8, h, gates, wsg, wsu, wsd, g, b):
    t, d = h.shape
    tm = COMB_TM
    nt = t // tm
    per_row = d // LANES
    ds_ = wsg.shape[1]
    return pl.pallas_call(
        _combine_body,
        out_shape=jax.ShapeDtypeStruct((t, d), F32),
        grid=(nt,),
        in_specs=[pl.BlockSpec((1, 1, tm * TOP_K), lambda i: (i, 0, 0), memory_space=pltpu.SMEM),
                  pl.BlockSpec((1, 1, tm * TOP_K), lambda i: (jnp.minimum(i + 1, nt - 1), 0, 0),
                               memory_space=pltpu.SMEM),
                  pl.BlockSpec(memory_space=pl.ANY),
                  pl.BlockSpec((tm, d), lambda i: (i, 0)),
                  pl.BlockSpec((tm, TOP_K), lambda i: (i, 0)),
                  pl.BlockSpec((d, ds_), lambda i: (0, 0)),
                  pl.BlockSpec((d, ds_), lambda i: (0, 0)),
                  pl.BlockSpec((ds_, d), lambda i: (0, 0)),
                  pl.BlockSpec((1, d), lambda i: (0, 0)),
                  pl.BlockSpec((1, d), lambda i: (0, 0))],
        out_specs=pl.BlockSpec((tm, d), lambda i: (i, 0)),
        scratch_shapes=[pltpu.VMEM((2, tm * TOP_K * per_row, LANES), F32),
                        pltpu.SemaphoreType.DMA((2,))],
        compiler_params=_cparams(("arbitrary",)),
        name="combine",
    )(pos3, pos3, y8, h, gates, wsg, wsu, wsd, g, b)


def _expert_tables(counts, nblk):
    bm = MOE_BM
    cnt = counts.reshape(N_EXPERTS).astype(I32)
    padded = (cnt + bm - 1) // bm * bm
    padded_end = jnp.cumsum(padded)
    padded_start = padded_end - padded
    block_rows = jnp.arange(nblk, dtype=I32) * bm
    block_e = jnp.sum((padded_end[None, :] <= block_rows[:, None]).astype(I32), axis=1)
    block_e = jnp.minimum(block_e, N_EXPERTS - 1)
    n_used = (padded_end[-1:] // bm).astype(I32)
    ids = jnp.where(cnt > 0, jnp.arange(N_EXPERTS, dtype=I32), N_EXPERTS)
    after = jnp.concatenate([lax.cummin(ids, reverse=True)[1:], jnp.full((1,), N_EXPERTS, I32)])
    next_e = jnp.where(after < N_EXPERTS, after, -1).astype(I32)
    return padded_start, padded_start + cnt, padded - cnt, block_e, n_used, next_e


def kernel(x, w_in, gla_gate_w2, gla_gate_b, gla_norm_w, pool_w_group, pool_scale, w_out, ln1_g, ln1_b,
           router_w, router_bias, w_exp_gate, w_exp_up, w_exp_down, w_sh_gate, w_sh_up, w_sh_down, ln2_g, ln2_b):
    batch, seq, d = x.shape
    t = batch * seq
    h2d = x.reshape(t, d)
    for l in range(DEPTH):
        d_in = w_in.shape[2]
        w_in_b = jnp.pad(w_in[l], ((0, 0), (0, D_IN_PAD - d_in))).astype(BF16)
        w2p = jnp.pad(gla_gate_w2[l], ((0, LANES - GLA_GATE_RANK), (0, 0))).astype(BF16)
        proj = _inproj(h2d, w_in_b)
        mixed = _mixer(proj, batch, seq, w2p, gla_gate_b[l].reshape(1, -1), gla_norm_w[l].reshape(1, -1),
                       pool_w_group[l].astype(BF16), pool_scale[l].reshape(1, -1))
        h, h8 = _outproj_ln(mixed, w_out[l].astype(BF16), h2d, ln1_g[l].reshape(1, -1), ln1_b[l].reshape(1, -1))
        idx_t, gate_t, rank_t, counts = _router(h, router_w[l].T, router_bias[l].reshape(-1, 1))
        nblk = (t * TOP_K + N_EXPERTS * (MOE_BM - 1)) // MOE_BM
        start, pad_row, pad_n, block_e, n_used, next_e = _expert_tables(counts, nblk)
        pos_t = _positions(idx_t, rank_t, start.astype(F32).reshape(-1, 1))
        pos_tok = pos_t.T
        xs8 = _dispatch(pad_row, pad_n, n_used, pos_tok.reshape(t // DISP_TM, 1, DISP_TM * TOP_K), h8, nblk * MOE_BM)
        y8 = _experts(block_e, n_used, next_e, xs8, w_exp_gate[l], w_exp_up[l], w_exp_down[l])
        h2d = _combine(pos_tok.reshape(t // COMB_TM, 1, COMB_TM * TOP_K), y8, h, gate_t.T,
                       w_sh_gate[l].astype(BF16), w_sh_up[l].astype(BF16), w_sh_down[l].astype(BF16),
                       ln2_g[l].reshape(1, -1), ln2_b[l].reshape(1, -1))
    return h2d.reshape(batch, seq, d)
```

```python
import functools

import jax
import jax.numpy as jnp
from jax import lax
from jax.experimental import pallas as pl
from jax.experimental.pallas import tpu as pltpu

F32 = jnp.float32
BF16 = jnp.bfloat16
I32 = jnp.int32

POOL_WINDOWS = (2, 4, 8, 16)
POOL_GROUP_DIM = 128
POOL_WIDTH = 512
GLA_HEADS = 4
GLA_DK = 64
GLA_DV = 128
GLA_DK_TOTAL = 256
GLA_WIDTH = 512
GLA_GATE_RANK = 16
GLA_GATE_NORMALIZER = 16.0
GLA_CHUNK = 16
N_EXPERTS = 256
TOP_K = 8
N_GROUPS = 8
GROUP_SIZE = N_EXPERTS // N_GROUPS
TOPK_GROUPS = 4
ROUTED_SCALE = 2.5
DEPTH = 1
DEEPNORM_ALPHA = (2.0 * DEPTH) ** 0.25
LN_EPS = 1e-5
RMS_EPS = 1e-5

LANES = 128
SUBLANES = 8
VMEM_LIMIT = 56 * 1024 * 1024

PROJ_TM = 512
MIX_TS = 256
ROUTE_TM = 256
MOE_BM = 256
COMB_TM = 128
POS_TM = 512
DISP_TM = 256
ROW_TILES = 8
X_RING = 3
COMB_ISSUE_CHUNKS = 8
D_IN_PAD = 2944
GATE_COL_BLOCK = 16


def _cparams(sem):
    return pltpu.CompilerParams(dimension_semantics=sem, vmem_limit_bytes=VMEM_LIMIT)


def _silu(x):
    return x * (1.0 / (1.0 + jnp.exp(-x)))


def _layernorm(y, g, b):
    mu = jnp.mean(y, axis=-1, keepdims=True)
    yc = y - mu
    var = jnp.mean(yc * yc, axis=-1, keepdims=True)
    return yc * lax.rsqrt(var + LN_EPS) * g + b


def _inproj_body(x_ref, w_ref, o_ref):
    o_ref[...] = jnp.dot(x_ref[...].astype(BF16), w_ref[...], preferred_element_type=F32)


def _inproj(x2d, w_bf):
    t, d = x2d.shape
    n = w_bf.shape[1]
    return pl.pallas_call(
        _inproj_body,
        out_shape=jax.ShapeDtypeStruct((t, n), F32),
        grid=(t // PROJ_TM,),
        in_specs=[pl.BlockSpec((PROJ_TM, d), lambda i: (i, 0)),
                  pl.BlockSpec((d, n), lambda i: (0, 0))],
        out_specs=pl.BlockSpec((PROJ_TM, n), lambda i: (i, 0)),
        compiler_params=_cparams(("arbitrary",)),
        name="inproj",
    )(x2d, w_bf)


def _mixer_body(p_ref, q_ref, k_ref, v_ref, r_ref, gl_ref, w2_ref, gb_ref, nw_ref, pw_ref, ps_ref,
                o_ref, pbuf, state, kvbuf, sall):
    ts = p_ref.shape[0]
    s_idx = pl.program_id(1)
    halo = POOL_WINDOWS[-1]

    @pl.when(s_idx == 0)
    def _():
        pbuf[pl.ds(0, halo), :] = jnp.zeros((halo, POOL_WIDTH), F32)
        state[...] = jnp.zeros(state.shape, F32)

    p = p_ref[...]
    pbuf[pl.ds(halo, ts), :] = p
    pos = s_idx * ts + lax.broadcasted_iota(I32, (ts, 1), 0)
    for g, w in enumerate(POOL_WINDOWS):
        c0 = g * POOL_GROUP_DIM
        acc = pbuf[pl.ds(halo, ts), pl.ds(c0, POOL_GROUP_DIM)]
        for j in range(1, w):
            acc = acc + pbuf[pl.ds(halo - j, ts), pl.ds(c0, POOL_GROUP_DIM)]
        cnt = jnp.minimum(pos + 1, w).astype(F32)
        mixed = acc / cnt - p[:, c0:c0 + POOL_GROUP_DIM]
        og = jnp.dot(mixed.astype(BF16), pw_ref[g], preferred_element_type=F32)
        o_ref[:, pl.ds(c0, POOL_GROUP_DIM)] = (og * ps_ref[:, pl.ds(c0, POOL_GROUP_DIM)]).astype(o_ref.dtype)
    pbuf[pl.ds(0, halo), :] = pbuf[pl.ds(ts, halo), :]

    nchunk = ts // GLA_CHUNK
    glog = jnp.dot(gl_ref[...].astype(BF16), w2_ref[...], preferred_element_type=F32) + gb_ref[...]
    gk = (jnp.minimum(glog, 0.0) - jnp.log(1.0 + jnp.exp(-jnp.abs(glog)))) * (1.0 / GLA_GATE_NORMALIZER)
    row = lax.broadcasted_iota(I32, (ts, 1), 0)
    rin = row % GLA_CHUNK
    b = gk
    sh = 1
    while sh < GLA_CHUNK:
        b = b + jnp.where(rin >= sh, pltpu.roll(b, sh, axis=0), 0.0)
        sh *= 2
    b3 = b.reshape(nchunk, GLA_CHUNK, GLA_DK_TOTAL)
    bmid = b3[:, GLA_CHUNK // 2 - 1:GLA_CHUNK // 2, :]
    blast = b3[:, GLA_CHUNK - 1:GLA_CHUNK, :]
    q3 = (q_ref[...] * (GLA_DK ** -0.5)).reshape(nchunk, GLA_CHUNK, GLA_DK_TOTAL)
    k3 = k_ref[...].reshape(nchunk, GLA_CHUNK, GLA_DK_TOTAL)
    qs = (q3 * jnp.exp(b3 - bmid)).reshape(ts, GLA_DK_TOTAL)
    ks = (k3 * jnp.exp(bmid - b3)).reshape(ts, GLA_DK_TOTAL)
    qd = (q3 * jnp.exp(b3)).reshape(ts, GLA_DK_TOTAL)
    kd = (k3 * jnp.exp(blast - b3)).reshape(ts, GLA_DK_TOTAL)
    cdec = jnp.exp(blast).reshape(nchunk, GLA_DK_TOTAL)

    v = v_ref[...]
    vb = v.astype(BF16)
    lane = lax.broadcasted_iota(I32, (1, LANES), 1)
    head_lane = [lane < GLA_DK, lane >= GLA_DK]

    blk = LANES
    ri = lax.broadcasted_iota(I32, (blk, blk), 0)
    ci = lax.broadcasted_iota(I32, (blk, blk), 1)
    causal = (ri // GLA_CHUNK == ci // GLA_CHUNK) & (ri >= ci)
    o_intra = [[None] * (ts // blk) for _ in range(GLA_HEADS)]
    for rb in range(ts // blk):
        rs = slice(rb * blk, (rb + 1) * blk)
        for pair in range(GLA_HEADS // 2):
            ls = slice(pair * LANES, (pair + 1) * LANES)
            ks_p = ks[rs, ls].astype(BF16)
            for sub in range(2):
                h = pair * 2 + sub
                q_m = jnp.where(head_lane[sub], qs[rs, ls], 0.0).astype(BF16)
                sc = lax.dot_general(q_m, ks_p, (((1,), (1,)), ((), ())), preferred_element_type=F32)
                sc = jnp.where(causal, sc, 0.0).astype(BF16)
                o_intra[h][rb] = jnp.dot(sc, vb[rs, h * GLA_DV:(h + 1) * GLA_DV], preferred_element_type=F32)

    cpb = blk // GLA_CHUNK
    chunk_of_col = lax.broadcasted_iota(I32, (1, blk), 1) // GLA_CHUNK
    stack_head_row = (lax.broadcasted_iota(I32, (cpb * LANES, 1), 0) % LANES) < GLA_DK
    for rb in range(ts // blk):
        rs = slice(rb * blk, (rb + 1) * blk)
        for pair in range(GLA_HEADS // 2):
            ls = slice(pair * LANES, (pair + 1) * LANES)
            kd_t = kd[rs, ls].T
            lhs = jnp.concatenate([jnp.where(chunk_of_col == c, kd_t, 0.0) for c in range(cpb)],
                                  axis=0).astype(BF16)
            inc = [jnp.dot(lhs, vb[rs, (pair * 2 + sub) * GLA_DV:(pair * 2 + sub + 1) * GLA_DV],
                           preferred_element_type=F32) for sub in range(2)]
            stacked = jnp.where(stack_head_row, inc[0], inc[1])
            for c in range(cpb):
                kvbuf[pair, rb * cpb + c] = stacked[c * LANES:(c + 1) * LANES]
    for pair in range(GLA_HEADS // 2):
        ls = slice(pair * LANES, (pair + 1) * LANES)
        dec_cols = cdec[:, ls].T
        st = state[pair]
        for c in range(nchunk):
            sall[pair, c] = st.astype(BF16)
            st = st * dec_cols[:, c:c + 1] + kvbuf[pair, c]
        state[pair] = st
    o_inter = [[None] * nchunk for _ in range(GLA_HEADS)]
    for pair in range(GLA_HEADS // 2):
        ls = slice(pair * LANES, (pair + 1) * LANES)
        for c in range(nchunk):
            rs = slice(c * GLA_CHUNK, (c + 1) * GLA_CHUNK)
            for sub in range(2):
                q_m = jnp.where(head_lane[sub], qd[rs, ls], 0.0).astype(BF16)
                o_inter[pair * 2 + sub][c] = jnp.dot(q_m, sall[pair, c], preferred_element_type=F32)

    nw = nw_ref[...]
    r = r_ref[...]
    for h in range(GLA_HEADS):
        o = jnp.concatenate(o_intra[h], axis=0) + jnp.concatenate(o_inter[h], axis=0)
        o = o * lax.rsqrt(jnp.mean(o * o, axis=-1, keepdims=True) + RMS_EPS) * nw
        o = o * _silu(r[:, h * GLA_DV:(h + 1) * GLA_DV])
        o_ref[:, pl.ds(POOL_WIDTH + h * GLA_DV, GLA_DV)] = o.astype(o_ref.dtype)


def _mixer(proj, batch, seq, w2p, gate_b, norm_w, pool_w, pool_scale):
    t = proj.shape[0]
    ts = MIX_TS
    nseq = seq // ts

    def rows(width, col_block):
        return pl.BlockSpec((ts, width), lambda bi, si: (bi * nseq + si, col_block))

    def full(shape):
        return pl.BlockSpec(shape, lambda bi, si: (0,) * len(shape))

    return pl.pallas_call(
        _mixer_body,
        out_shape=jax.ShapeDtypeStruct((t, POOL_WIDTH + GLA_WIDTH), BF16),
        grid=(batch, nseq),
        in_specs=[rows(POOL_WIDTH, 0),
                  rows(GLA_DK_TOTAL, 2),
                  rows(GLA_DK_TOTAL, 3),
                  rows(GLA_WIDTH, 2),
                  rows(GLA_WIDTH, 3),
                  rows(LANES, GATE_COL_BLOCK),
                  full(w2p.shape), full(gate_b.shape), full(norm_w.shape),
                  full(pool_w.shape), full(pool_scale.shape)],
        out_specs=pl.BlockSpec((ts, POOL_WIDTH + GLA_WIDTH), lambda bi, si: (bi * nseq + si, 0)),
        scratch_shapes=[pltpu.VMEM((ts + POOL_WINDOWS[-1], POOL_WIDTH), F32),
                        pltpu.VMEM((GLA_HEADS // 2, LANES, GLA_DV), F32),
                        pltpu.VMEM((GLA_HEADS // 2, ts // GLA_CHUNK, LANES, GLA_DV), F32),
                        pltpu.VMEM((GLA_HEADS // 2, ts // GLA_CHUNK, LANES, GLA_DV), BF16)],
        compiler_params=_cparams(("arbitrary", "arbitrary")),
        name="mixer",
    )(proj, proj, proj, proj, proj, proj, w2p, gate_b, norm_w, pool_w, pool_scale)


def _outproj_body(m_ref, w_ref, x_ref, g_ref, b_ref, h_ref, h8_ref):
    tm, d = x_ref.shape
    y = DEEPNORM_ALPHA * x_ref[...] + jnp.dot(m_ref[...], w_ref[...], preferred_element_type=F32)
    h = _layernorm(y, g_ref[...], b_ref[...])
    h_ref[...] = h
    for s in range(d // LANES):
        h8_ref[pl.ds(s, tm, stride=d // LANES), :] = h[:, s * LANES:(s + 1) * LANES]


def _outproj_ln(mixed, w_bf, x2d, g, b):
    t, d = x2d.shape
    tm = PROJ_TM
    per_row = d // LANES
    return pl.pallas_call(
        _outproj_body,
        out_shape=(jax.ShapeDtypeStruct((t, d), F32), jax.ShapeDtypeStruct((t * per_row, LANES), F32)),
        grid=(t // tm,),
        in_specs=[pl.BlockSpec((tm, d), lambda i: (i, 0)),
                  pl.BlockSpec((d, d), lambda i: (0, 0)),
                  pl.BlockSpec((tm, d), lambda i: (i, 0)),
                  pl.BlockSpec((1, d), lambda i: (0, 0)),
                  pl.BlockSpec((1, d), lambda i: (0, 0))],
        out_specs=(pl.BlockSpec((tm, d), lambda i: (i, 0)),
                   pl.BlockSpec((tm * per_row, LANES), lambda i: (i, 0))),
        compiler_params=_cparams(("arbitrary",)),
        name="outproj_ln",
    )(mixed, w_bf, x2d, g, b)


def _first_argmax_rows(val, rowf, nrows):
    m = jnp.max(val, axis=0, keepdims=True)
    first = jnp.min(jnp.where(val == m, rowf, float(nrows)), axis=0, keepdims=True)
    return m, first, rowf == first


def _router_body(h_ref, whi_ref, wlo_ref, bias_ref, idx_ref, gate_ref, rank_ref, cnt_ref, carry):
    tm = h_ref.shape[0]
    i = pl.program_id(0)

    @pl.when(i == 0)
    def _():
        carry[...] = jnp.zeros(carry.shape, F32)

    h = h_ref[...]
    h_hi = h.astype(BF16)
    h_lo = (h - h_hi.astype(F32)).astype(BF16)
    nt = (((1,), (1,)), ((), ()))
    logits = (lax.dot_general(whi_ref[...], h_hi, nt, preferred_element_type=F32)
              + lax.dot_general(whi_ref[...], h_lo, nt, preferred_element_type=F32)
              + lax.dot_general(wlo_ref[...], h_hi, nt, preferred_element_type=F32))
    scores = 1.0 / (1.0 + jnp.exp(-logits))
    biased = scores + bias_ref[...]
    neg = -jnp.inf

    grp = biased.reshape(N_GROUPS, GROUP_SIZE, tm)
    gi = lax.broadcasted_iota(I32, (N_GROUPS, GROUP_SIZE, tm), 1).astype(F32)
    g1 = jnp.max(grp, axis=1, keepdims=True)
    f1 = jnp.min(jnp.where(grp == g1, gi, float(GROUP_SIZE)), axis=1, keepdims=True)
    g2 = jnp.max(jnp.where(gi == f1, neg, grp), axis=1, keepdims=True)
    gscore = (g1 + g2).reshape(N_GROUPS, tm)

    growf = lax.broadcasted_iota(I32, (N_GROUPS, tm), 0).astype(F32)
    gsel = jnp.zeros((N_GROUPS, tm), F32)
    gval = gscore
    for _ in range(TOPK_GROUPS):
        _, _, pick = _first_argmax_rows(gval, growf, N_GROUPS)
        gsel = jnp.where(pick, 1.0, gsel)
        gval = jnp.where(pick, neg, gval)
    emask = jnp.broadcast_to(gsel.reshape(N_GROUPS, 1, tm), (N_GROUPS, GROUP_SIZE, tm)).reshape(N_EXPERTS, tm)

    rowf = lax.broadcasted_iota(I32, (N_EXPERTS, tm), 0).astype(F32)
    val = jnp.where(emask > 0.0, biased, neg)
    onehot = jnp.zeros((N_EXPERTS, tm), F32)
    picks, idxs, ws = [], [], []
    for _ in range(TOP_K):
        _, first, pick = _first_argmax_rows(val, rowf, N_EXPERTS)
        picks.append(pick)
        idxs.append(first)
        ws.append(jnp.sum(jnp.where(pick, scores, 0.0), axis=0, keepdims=True))
        onehot = jnp.where(pick, 1.0, onehot)
        val = jnp.where(pick, neg, val)
    w = jnp.concatenate(ws, axis=0)
    gate_ref[...] = w / jnp.sum(w, axis=0, keepdims=True) * ROUTED_SCALE
    idx_ref[...] = jnp.concatenate(idxs, axis=0).astype(I32)

    ti = lax.broadcasted_iota(I32, (tm, tm), 0)
    tj = lax.broadcasted_iota(I32, (tm, tm), 1)
    upper = jnp.where(ti < tj, 1.0, 0.0).astype(BF16)
    prefix = jnp.dot(onehot.astype(BF16), upper, preferred_element_type=F32) + carry[...]
    ranks = [jnp.sum(jnp.where(pk, prefix, 0.0), axis=0, keepdims=True) for pk in picks]
    rank_ref[...] = jnp.concatenate(ranks, axis=0).astype(I32)
    carry[...] = carry[...] + jnp.sum(onehot, axis=1, keepdims=True)
    cnt_ref[...] = carry[...]


def _router(h, wt, bias_col):
    t, d = h.shape
    tm = ROUTE_TM
    wt_hi = wt.astype(BF16)
    wt_lo = (wt - wt_hi.astype(F32)).astype(BF16)
    return pl.pallas_call(
        _router_body,
        out_shape=(jax.ShapeDtypeStruct((TOP_K, t), I32), jax.ShapeDtypeStruct((TOP_K, t), F32),
                   jax.ShapeDtypeStruct((TOP_K, t), I32), jax.ShapeDtypeStruct((N_EXPERTS, 1), F32)),
        grid=(t // tm,),
        in_specs=[pl.BlockSpec((tm, d), lambda i: (i, 0)),
                  pl.BlockSpec((N_EXPERTS, d), lambda i: (0, 0)),
                  pl.BlockSpec((N_EXPERTS, d), lambda i: (0, 0)),
                  pl.BlockSpec((N_EXPERTS, 1), lambda i: (0, 0))],
        out_specs=(pl.BlockSpec((TOP_K, tm), lambda i: (0, i)),
                   pl.BlockSpec((TOP_K, tm), lambda i: (0, i)),
                   pl.BlockSpec((TOP_K, tm), lambda i: (0, i)),
                   pl.BlockSpec((N_EXPERTS, 1), lambda i: (0, 0))),
        scratch_shapes=[pltpu.VMEM((N_EXPERTS, 1), F32)],
        compiler_params=_cparams(("arbitrary",)),
        name="router",
    )(h, wt_hi, wt_lo, bias_col)


def _positions_body(idx_ref, rank_ref, start_ref, pos_ref):
    tm = idx_ref.shape[1]
    rowi = lax.broadcasted_iota(I32, (N_EXPERTS, tm), 0)
    start = start_ref[...]
    idx = idx_ref[...]
    rows = [jnp.sum(jnp.where(rowi == idx[k:k + 1, :], start, 0.0), axis=0, keepdims=True) for k in range(TOP_K)]
    pos_ref[...] = jnp.concatenate(rows, axis=0).astype(I32) + rank_ref[...]


def _positions(idx_t, rank_t, start_col):
    t = idx_t.shape[1]
    tm = POS_TM
    return pl.pallas_call(
        _positions_body,
        out_shape=jax.ShapeDtypeStruct((TOP_K, t), I32),
        grid=(t // tm,),
        in_specs=[pl.BlockSpec((TOP_K, tm), lambda i: (0, i)),
                  pl.BlockSpec((TOP_K, tm), lambda i: (0, i)),
                  pl.BlockSpec((N_EXPERTS, 1), lambda i: (0, 0))],
        out_specs=pl.BlockSpec((TOP_K, tm), lambda i: (0, i)),
        compiler_params=_cparams(("arbitrary",)),
        name="positions",
    )(idx_t, rank_t, start_col)


def _pad_fill_copy(zeros, xs8_ref, sem, row, nrows, per_row):
    return pltpu.make_async_copy(zeros.at[pl.ds(0, nrows * per_row), :],
                                 xs8_ref.at[pl.ds(pl.multiple_of(row * per_row, per_row), nrows * per_row), :], sem)


def _dispatch_body(pad_row_ref, pad_n_ref, nu_ref, pos_ref, h8_ref, xs8_ref, zeros, sem, pad_sem):
    i = pl.program_id(0)
    tm = h8_ref.shape[0] // ROW_TILES
    per_row = ROW_TILES
    half = MOE_BM // 2
    pad_bits = [1 << j for j in range(MOE_BM.bit_length() - 1)]
    n_half_blocks = xs8_ref.shape[0] // (half * per_row)

    def pad_pass(wait):
        def go(cp):
            if wait:
                cp.wait()
            else:
                cp.start()

        def body(e, carry):
            row = pad_row_ref[e]
            n = pad_n_ref[e]
            for bit in pad_bits:
                @pl.when((n & bit) != 0)
                def _():
                    go(_pad_fill_copy(zeros, xs8_ref, pad_sem, row + (n & (bit - 1)), bit, per_row))
            return carry
        lax.fori_loop(0, N_EXPERTS, body, 0)

        def tail(hb, carry):
            go(_pad_fill_copy(zeros, xs8_ref, pad_sem, hb * half, half, per_row))
            return carry
        lax.fori_loop(nu_ref[0] * 2, n_half_blocks, tail, 0)

    @pl.when(i == 0)
    def _():
        zeros[...] = jnp.zeros(zeros.shape, F32)
        pad_pass(False)

    def body(t, carry):
        src = h8_ref.at[pl.ds(pl.multiple_of(t * per_row, per_row), per_row), :]
        for k in range(TOP_K):
            p = pos_ref[0, 0, t * TOP_K + k]
            dst = xs8_ref.at[pl.ds(pl.multiple_of(p * per_row, per_row), per_row), :]
            pltpu.make_async_copy(src, dst, sem).start(priority=k % 2)
        return carry
    lax.fori_loop(0, tm, body, 0)

    for k in range(TOP_K):
        pltpu.make_async_copy(h8_ref, xs8_ref.at[pl.ds(0, tm * per_row), :], sem).wait()

    @pl.when(i == pl.num_programs(0) - 1)
    def _():
        pad_pass(True)


def _dispatch(pad_row, pad_n, n_used, pos3, h8, n_rows):
    nt, _, per_step = pos3.shape
    tm = per_step // TOP_K
    grid_spec = pltpu.PrefetchScalarGridSpec(
        num_scalar_prefetch=3,
        grid=(nt,),
        in_specs=[pl.BlockSpec((1, 1, per_step), lambda i, a, b, c: (i, 0, 0), memory_space=pltpu.SMEM),
                  pl.BlockSpec((tm * ROW_TILES, LANES), lambda i, a, b, c: (i, 0))],
        out_specs=pl.BlockSpec(memory_space=pl.ANY),
        scratch_shapes=[pltpu.VMEM((MOE_BM // 2 * ROW_TILES, LANES), F32),
                        pltpu.SemaphoreType.DMA, pltpu.SemaphoreType.DMA],
    )
    return pl.pallas_call(
        _dispatch_body,
        out_shape=jax.ShapeDtypeStruct((n_rows * ROW_TILES, LANES), F32),
        grid_spec=grid_spec,
        compiler_params=_cparams(("arbitrary",)),
        name="dispatch",
    )(pad_row, pad_n, n_used, pos3, h8)


def _weight_copies(e, w_hbm, stage, sem):
    return [pltpu.make_async_copy(w.at[e], st, sem.at[j]) for j, (w, st) in enumerate(zip(w_hbm, stage))]


def _row_block_copy(xs8_hbm, xbuf, xsem, j):
    rows = xbuf.shape[1]
    slot = j % X_RING
    return pltpu.make_async_copy(xs8_hbm.at[pl.ds(pl.multiple_of(j * rows, rows), rows), :], xbuf.at[slot],
                                 xsem.at[slot])


def _experts_body(be_ref, nu_ref, nx_ref, xs8_hbm, wg_hbm, wu_hbm, wd_hbm, y8_ref,
                  xbuf, xsem, sg, su, sd, wsem, wg_b, wu_b, wd_b):
    i = pl.program_id(0)
    bm = MOE_BM
    per_row = ROW_TILES
    w_hbm = (wg_hbm, wu_hbm, wd_hbm)
    stage = (sg, su, sd)
    n_used = nu_ref[0]

    @pl.when(i < n_used)
    def _():
        e = be_ref[i]
        prev_e = be_ref[jnp.maximum(i - 1, 0)]

        @pl.when(i == 0)
        def _():
            for j in range(X_RING - 1):
                @pl.when(j < n_used)
                def _():
                    _row_block_copy(xs8_hbm, xbuf, xsem, j).start()

        @pl.when(i + X_RING - 1 < n_used)
        def _():
            _row_block_copy(xs8_hbm, xbuf, xsem, i + X_RING - 1).start()

        @pl.when((i == 0) | (e != prev_e))
        def _():
            @pl.when(i == 0)
            def _():
                for cp in _weight_copies(e, w_hbm, stage, wsem):
                    cp.start()
            for cp in _weight_copies(e, w_hbm, stage, wsem):
                cp.wait()
            wg_b[...] = sg[...].astype(BF16)
            wu_b[...] = su[...].astype(BF16)
            wd_b[...] = sd[...].astype(BF16)
            nxt = nx_ref[e]

            @pl.when(nxt >= 0)
            def _():
                for cp in _weight_copies(nxt, w_hbm, stage, wsem):
                    cp.start()

        _row_block_copy(xs8_hbm, xbuf, xsem, i).wait()
        slot = i % X_RING
        x = jnp.concatenate([xbuf[slot, pl.ds(s, bm, stride=per_row), :] for s in range(per_row)],
                            axis=1).astype(BF16)
        g = jnp.dot(x, wg_b[...], preferred_element_type=F32)
        u = jnp.dot(x, wu_b[...], preferred_element_type=F32)
        a = (_silu(g) * u).astype(BF16)
        y = jnp.dot(a, wd_b[...], preferred_element_type=F32)
        for s in range(per_row):
            y8_ref[pl.ds(s, bm, stride=per_row), :] = y[:, s * LANES:(s + 1) * LANES]

    @pl.when(i >= nu_ref[0])
    def _():
        y8_ref[...] = jnp.zeros(y8_ref.shape, F32)


def _experts(block_e, n_used, next_e, xs8, wg, wu, wd):
    nblk = block_e.shape[0]
    bm = MOE_BM
    e, d, de = wg.shape
    per_row = ROW_TILES

    grid_spec = pltpu.PrefetchScalarGridSpec(
        num_scalar_prefetch=3,
        grid=(nblk,),
        in_specs=[pl.BlockSpec(memory_space=pl.ANY),
                  pl.BlockSpec(memory_space=pl.ANY),
                  pl.BlockSpec(memory_space=pl.ANY),
                  pl.BlockSpec(memory_space=pl.ANY)],
        out_specs=pl.BlockSpec((bm * per_row, LANES), lambda i, be, nu, nx: (i, 0)),
        scratch_shapes=[pltpu.VMEM((X_RING, bm * per_row, LANES), F32), pltpu.SemaphoreType.DMA((X_RING,)),
                        pltpu.VMEM((d, de), F32), pltpu.VMEM((d, de), F32), pltpu.VMEM((de, d), F32),
                        pltpu.SemaphoreType.DMA((3,)),
                        pltpu.VMEM((d, de), BF16), pltpu.VMEM((d, de), BF16), pltpu.VMEM((de, d), BF16)],
    )
    return pl.pallas_call(
        _experts_body,
        out_shape=jax.ShapeDtypeStruct((nblk * bm * per_row, LANES), F32),
        grid_spec=grid_spec,
        compiler_params=_cparams(("arbitrary",)),
        name="experts",
    )(block_e, n_used, next_e, xs8, wg, wu, wd)


def _combine_body(pos_cur, pos_nxt, y8_ref, h_ref, gate_ref, wsg_ref, wsu_ref, wsd_ref, g_ref, b_ref,
                  o_ref, buf, sem, act_ref, acc_ref):
    i = pl.program_id(0)
    nb = pl.num_programs(0)
    tm, d = h_ref.shape
    per_row = d // LANES
    slot = i % 2
    tok_rows = TOP_K * per_row
    n_chunks = COMB_ISSUE_CHUNKS
    tpc = tm // n_chunks

    def issue(pos_ref, sl, c):
        def body(t, carry):
            for k in range(TOP_K):
                p = pos_ref[0, 0, t * TOP_K + k]
                src = y8_ref.at[pl.ds(pl.multiple_of(p * per_row, per_row), per_row), :]
                dst = buf.at[sl, pl.ds(pl.multiple_of((k * tm + t) * per_row, per_row), per_row), :]
                pltpu.make_async_copy(src, dst, sem.at[sl]).start(priority=k % 2)
            return carry
        lax.fori_loop(c * tpc, (c + 1) * tpc, body, 0)

    @pl.when(i == 0)
    def _():
        for c in range(n_chunks):
            issue(pos_cur, 0, c)

    pltpu.make_async_copy(y8_ref.at[pl.ds(0, tm * tok_rows), :], buf.at[slot], sem.at[slot]).wait()

    def shared_hidden():
        hb = h_ref[...].astype(BF16)
        sg = jnp.dot(hb, wsg_ref[...], preferred_element_type=F32)
        su = jnp.dot(hb, wsu_ref[...], preferred_element_type=F32)
        act_ref[...] = (_silu(sg) * su).astype(BF16)

    def shared_out():
        acc_ref[...] = DEEPNORM_ALPHA * h_ref[...] + jnp.dot(act_ref[...], wsd_ref[...], preferred_element_type=F32)

    def routed_cols(s_list):
        def piece():
            gates = gate_ref[...]
            for s in s_list:
                acc = acc_ref[:, pl.ds(s * LANES, LANES)]
                for k in range(TOP_K):
                    acc = acc + gates[:, k:k + 1] * buf[slot, pl.ds(k * tm * per_row + s, tm, stride=per_row), :]
                acc_ref[:, pl.ds(s * LANES, LANES)] = acc
        return piece

    def finish():
        o_ref[...] = _layernorm(acc_ref[...], g_ref[...], b_ref[...])

    col_groups = [list(range(j, min(j + 2, per_row))) for j in range(0, per_row, 2)]
    pieces = [shared_hidden, shared_out] + [routed_cols(g) for g in col_groups] + [finish]
    for c in range(max(n_chunks, len(pieces))):
        if c < n_chunks:
            @pl.when(i + 1 < nb)
            def _():
                issue(pos_nxt, 1 - slot, c)
        if c < len(pieces):
            pieces[c]()


def _combine(pos3, y8, h, gates, wsg, wsu, wsd, g, b):
    t, d = h.shape
    tm = COMB_TM
    nt = t // tm
    per_row = d // LANES
    ds_ = wsg.shape[1]
    return pl.pallas_call(
        _combine_body,
        out_shape=jax.ShapeDtypeStruct((t, d), F32),
        grid=(nt,),
        in_specs=[pl.BlockSpec((1, 1, tm * TOP_K), lambda i: (i, 0, 0), memory_space=pltpu.SMEM),
                  pl.BlockSpec((1, 1, tm * TOP_K), lambda i: (jnp.minimum(i + 1, nt - 1), 0, 0),
                               memory_space=pltpu.SMEM),
                  pl.BlockSpec(memory_space=pl.ANY),
                  pl.BlockSpec((tm, d), lambda i: (i, 0)),
                  pl.BlockSpec((tm, TOP_K), lambda i: (i, 0)),
                  pl.BlockSpec((d, ds_), lambda i: (0, 0)),
                  pl.BlockSpec((d, ds_), lambda i: (0, 0)),
                  pl.BlockSpec((ds_, d), lambda i: (0, 0)),
                  pl.BlockSpec((1, d), lambda i: (0, 0)),
                  pl.BlockSpec((1, d), lambda i: (0, 0))],
        out_specs=pl.BlockSpec((tm, d), lambda i: (i, 0)),
        scratch_shapes=[pltpu.VMEM((2, tm * TOP_K * per_row, LANES), F32),
                        pltpu.SemaphoreType.DMA((2,)),
                        pltpu.VMEM((tm, ds_), BF16), pltpu.VMEM((tm, d), F32)],
        compiler_params=_cparams(("arbitrary",)),
        name="combine",
    )(pos3, pos3, y8, h, gates, wsg, wsu, wsd, g, b)


def _expert_tables(counts, nblk):
    bm = MOE_BM
    cnt = counts.reshape(N_EXPERTS).astype(I32)
    padded = (cnt + bm - 1) // bm * bm
    padded_end = jnp.cumsum(padded)
    padded_start = padded_end - padded
    block_rows = jnp.arange(nblk, dtype=I32) * bm
    block_e = jnp.sum((padded_end[None, :] <= block_rows[:, None]).astype(I32), axis=1)
    block_e = jnp.minimum(block_e, N_EXPERTS - 1)
    n_used = (padded_end[-1:] // bm).astype(I32)
    ids = jnp.where(cnt > 0, jnp.arange(N_EXPERTS, dtype=I32), N_EXPERTS)
    after = jnp.concatenate([lax.cummin(ids, reverse=True)[1:], jnp.full((1,), N_EXPERTS, I32)])
    next_e = jnp.where(after < N_EXPERTS, after, -1).astype(I32)
    return padded_start, padded_start + cnt, padded - cnt, block_e, n_used, next_e


def kernel(x, w_in, gla_gate_w2, gla_gate_b, gla_norm_w, pool_w_group, pool_scale, w_out, ln1_g, ln1_b,
           router_w, router_bias, w_exp_gate, w_exp_up, w_exp_down, w_sh_gate, w_sh_up, w_sh_down, ln2_g, ln2_b):
    batch, seq, d = x.shape
    t = batch * seq
    h2d = x.reshape(t, d)
    for l in range(DEPTH):
        d_in = w_in.shape[2]
        w_in_b = jnp.pad(w_in[l], ((0, 0), (0, D_IN_PAD - d_in))).astype(BF16)
        w2p = jnp.pad(gla_gate_w2[l], ((0, LANES - GLA_GATE_RANK), (0, 0))).astype(BF16)
        proj = _inproj(h2d, w_in_b)
        mixed = _mixer(proj, batch, seq, w2p, gla_gate_b[l].reshape(1, -1), gla_norm_w[l].reshape(1, -1),
                       pool_w_group[l].astype(BF16), pool_scale[l].reshape(1, -1))
        h, h8 = _outproj_ln(mixed, w_out[l].astype(BF16), h2d, ln1_g[l].reshape(1, -1), ln1_b[l].reshape(1, -1))
        idx_t, gate_t, rank_t, counts = _router(h, router_w[l].T, router_bias[l].reshape(-1, 1))
        nblk = (t * TOP_K + N_EXPERTS * (MOE_BM - 1)) // MOE_BM
        start, pad_row, pad_n, block_e, n_used, next_e = _expert_tables(counts, nblk)
        pos_t = _positions(idx_t, rank_t, start.astype(F32).reshape(-1, 1))
        pos_tok = pos_t.T
        xs8 = _dispatch(pad_row, pad_n, n_used, pos_tok.reshape(t // DISP_TM, 1, DISP_TM * TOP_K), h8, nblk * MOE_BM)
        y8 = _experts(block_e, n_used, next_e, xs8, w_exp_gate[l], w_exp_up[l], w_exp_down[l])
        h2d = _combine(pos_tok.reshape(t // COMB_TM, 1, COMB_TM * TOP_K), y8, h, gate_t.T,
                       w_sh_gate[l].astype(BF16), w_sh_up[l].astype(BF16), w_sh_down[l].astype(BF16),
                       ln2_g[l].reshape(1, -1), ln2_b[l].reshape(1, -1))
    return h2d.reshape(batch, seq, d)
```

```python
import functools

import jax
import jax.numpy as jnp
from jax import lax
from jax.experimental import pallas as pl
from jax.experimental.pallas import tpu as pltpu

F32 = jnp.float32
BF16 = jnp.bfloat16
I32 = jnp.int32
U32 = jnp.uint32
HIGH_HALF = 0xFFFF0000

POOL_WINDOWS = (2, 4, 8, 16)
POOL_GROUP_DIM = 128
POOL_WIDTH = 512
GLA_HEADS = 4
GLA_DK = 64
GLA_DV = 128
GLA_DK_TOTAL = 256
GLA_WIDTH = 512
GLA_GATE_RANK = 16
GLA_GATE_NORMALIZER = 16.0
GLA_CHUNK = 16
N_EXPERTS = 256
TOP_K = 8
N_GROUPS = 8
GROUP_SIZE = N_EXPERTS // N_GROUPS
TOPK_GROUPS = 4
ROUTED_SCALE = 2.5
DEPTH = 1
DEEPNORM_ALPHA = (2.0 * DEPTH) ** 0.25
LN_EPS = 1e-5
RMS_EPS = 1e-5

LANES = 128
SUBLANES = 8
VMEM_LIMIT = 56 * 1024 * 1024

PROJ_TM = 512
MIX_TS = 256
ROUTE_TM = 256
MOE_BM = 256
COMB_TM = 128
POS_TM = 512
DISP_TM = 256
ROW_WORDS = 4
X_RING = 3
D_IN_PAD = 2944
GATE_COL_BLOCK = 16


def _cparams(sem):
    return pltpu.CompilerParams(dimension_semantics=sem, vmem_limit_bytes=VMEM_LIMIT)


def _silu(x):
    return x * (1.0 / (1.0 + jnp.exp(-x)))


def _layernorm(y, g, b):
    mu = jnp.mean(y, axis=-1, keepdims=True)
    yc = y - mu
    var = jnp.mean(yc * yc, axis=-1, keepdims=True)
    return yc * lax.rsqrt(var + LN_EPS) * g + b


def _inproj_body(x_ref, w_ref, o_ref):
    o_ref[...] = jnp.dot(x_ref[...].astype(BF16), w_ref[...], preferred_element_type=F32)


def _inproj(x2d, w_bf):
    t, d = x2d.shape
    n = w_bf.shape[1]
    return pl.pallas_call(
        _inproj_body,
        out_shape=jax.ShapeDtypeStruct((t, n), F32),
        grid=(t // PROJ_TM,),
        in_specs=[pl.BlockSpec((PROJ_TM, d), lambda i: (i, 0)),
                  pl.BlockSpec((d, n), lambda i: (0, 0))],
        out_specs=pl.BlockSpec((PROJ_TM, n), lambda i: (i, 0)),
        compiler_params=_cparams(("arbitrary",)),
        name="inproj",
    )(x2d, w_bf)


def _mixer_body(p_ref, q_ref, k_ref, v_ref, r_ref, gl_ref, w2_ref, gb_ref, nw_ref, pw_ref, ps_ref,
                o_ref, pbuf, state, kvbuf, sall):
    ts = p_ref.shape[0]
    s_idx = pl.program_id(1)
    halo = POOL_WINDOWS[-1]

    @pl.when(s_idx == 0)
    def _():
        pbuf[pl.ds(0, halo), :] = jnp.zeros((halo, POOL_WIDTH), F32)
        state[...] = jnp.zeros(state.shape, F32)

    p = p_ref[...]
    pbuf[pl.ds(halo, ts), :] = p
    pos = s_idx * ts + lax.broadcasted_iota(I32, (ts, 1), 0)
    for g, w in enumerate(POOL_WINDOWS):
        c0 = g * POOL_GROUP_DIM
        acc = pbuf[pl.ds(halo, ts), pl.ds(c0, POOL_GROUP_DIM)]
        for j in range(1, w):
            acc = acc + pbuf[pl.ds(halo - j, ts), pl.ds(c0, POOL_GROUP_DIM)]
        cnt = jnp.minimum(pos + 1, w).astype(F32)
        mixed = acc / cnt - p[:, c0:c0 + POOL_GROUP_DIM]
        og = jnp.dot(mixed.astype(BF16), pw_ref[g], preferred_element_type=F32)
        o_ref[:, pl.ds(c0, POOL_GROUP_DIM)] = (og * ps_ref[:, pl.ds(c0, POOL_GROUP_DIM)]).astype(o_ref.dtype)
    pbuf[pl.ds(0, halo), :] = pbuf[pl.ds(ts, halo), :]

    nchunk = ts // GLA_CHUNK
    glog = jnp.dot(gl_ref[...].astype(BF16), w2_ref[...], preferred_element_type=F32) + gb_ref[...]
    gk = (jnp.minimum(glog, 0.0) - jnp.log(1.0 + jnp.exp(-jnp.abs(glog)))) * (1.0 / GLA_GATE_NORMALIZER)
    row = lax.broadcasted_iota(I32, (ts, 1), 0)
    rin = row % GLA_CHUNK
    b = gk
    sh = 1
    while sh < GLA_CHUNK:
        b = b + jnp.where(rin >= sh, pltpu.roll(b, sh, axis=0), 0.0)
        sh *= 2
    b3 = b.reshape(nchunk, GLA_CHUNK, GLA_DK_TOTAL)
    bmid = b3[:, GLA_CHUNK // 2 - 1:GLA_CHUNK // 2, :]
    blast = b3[:, GLA_CHUNK - 1:GLA_CHUNK, :]
    q3 = (q_ref[...] * (GLA_DK ** -0.5)).reshape(nchunk, GLA_CHUNK, GLA_DK_TOTAL)
    k3 = k_ref[...].reshape(nchunk, GLA_CHUNK, GLA_DK_TOTAL)
    qs = (q3 * jnp.exp(b3 - bmid)).reshape(ts, GLA_DK_TOTAL)
    ks = (k3 * jnp.exp(bmid - b3)).reshape(ts, GLA_DK_TOTAL)
    qd = (q3 * jnp.exp(b3)).reshape(ts, GLA_DK_TOTAL)
    kd = (k3 * jnp.exp(blast - b3)).reshape(ts, GLA_DK_TOTAL)
    cdec = jnp.exp(blast).reshape(nchunk, GLA_DK_TOTAL)

    v = v_ref[...]
    vb = v.astype(BF16)
    lane = lax.broadcasted_iota(I32, (1, LANES), 1)
    head_lane = [lane < GLA_DK, lane >= GLA_DK]

    blk = LANES
    ri = lax.broadcasted_iota(I32, (blk, blk), 0)
    ci = lax.broadcasted_iota(I32, (blk, blk), 1)
    causal = (ri // GLA_CHUNK == ci // GLA_CHUNK) & (ri >= ci)
    o_intra = [[None] * (ts // blk) for _ in range(GLA_HEADS)]
    for rb in range(ts // blk):
        rs = slice(rb * blk, (rb + 1) * blk)
        for pair in range(GLA_HEADS // 2):
            ls = slice(pair * LANES, (pair + 1) * LANES)
            ks_p = ks[rs, ls].astype(BF16)
            for sub in range(2):
                h = pair * 2 + sub
                q_m = jnp.where(head_lane[sub], qs[rs, ls], 0.0).astype(BF16)
                sc = lax.dot_general(q_m, ks_p, (((1,), (1,)), ((), ())), preferred_element_type=F32)
                sc = jnp.where(causal, sc, 0.0).astype(BF16)
                o_intra[h][rb] = jnp.dot(sc, vb[rs, h * GLA_DV:(h + 1) * GLA_DV], preferred_element_type=F32)

    cpb = blk // GLA_CHUNK
    chunk_of_col = lax.broadcasted_iota(I32, (1, blk), 1) // GLA_CHUNK
    stack_head_row = (lax.broadcasted_iota(I32, (cpb * LANES, 1), 0) % LANES) < GLA_DK
    for rb in range(ts // blk):
        rs = slice(rb * blk, (rb + 1) * blk)
        for pair in range(GLA_HEADS // 2):
            ls = slice(pair * LANES, (pair + 1) * LANES)
            kd_t = kd[rs, ls].T
            lhs = jnp.concatenate([jnp.where(chunk_of_col == c, kd_t, 0.0) for c in range(cpb)],
                                  axis=0).astype(BF16)
            inc = [jnp.dot(lhs, vb[rs, (pair * 2 + sub) * GLA_DV:(pair * 2 + sub + 1) * GLA_DV],
                           preferred_element_type=F32) for sub in range(2)]
            stacked = jnp.where(stack_head_row, inc[0], inc[1])
            for c in range(cpb):
                kvbuf[pair, rb * cpb + c] = stacked[c * LANES:(c + 1) * LANES]
    for pair in range(GLA_HEADS // 2):
        ls = slice(pair * LANES, (pair + 1) * LANES)
        dec_cols = cdec[:, ls].T
        st = state[pair]
        for c in range(nchunk):
            sall[pair, c] = st.astype(BF16)
            st = st * dec_cols[:, c:c + 1] + kvbuf[pair, c]
        state[pair] = st
    o_inter = [[None] * nchunk for _ in range(GLA_HEADS)]
    for pair in range(GLA_HEADS // 2):
        ls = slice(pair * LANES, (pair + 1) * LANES)
        for c in range(nchunk):
            rs = slice(c * GLA_CHUNK, (c + 1) * GLA_CHUNK)
            for sub in range(2):
                q_m = jnp.where(head_lane[sub], qd[rs, ls], 0.0).astype(BF16)
                o_inter[pair * 2 + sub][c] = jnp.dot(q_m, sall[pair, c], preferred_element_type=F32)

    nw = nw_ref[...]
    r = r_ref[...]
    for h in range(GLA_HEADS):
        o = jnp.concatenate(o_intra[h], axis=0) + jnp.concatenate(o_inter[h], axis=0)
        o = o * lax.rsqrt(jnp.mean(o * o, axis=-1, keepdims=True) + RMS_EPS) * nw
        o = o * _silu(r[:, h * GLA_DV:(h + 1) * GLA_DV])
        o_ref[:, pl.ds(POOL_WIDTH + h * GLA_DV, GLA_DV)] = o.astype(o_ref.dtype)


def _mixer(proj, batch, seq, w2p, gate_b, norm_w, pool_w, pool_scale):
    t = proj.shape[0]
    ts = MIX_TS
    nseq = seq // ts

    def rows(width, col_block):
        return pl.BlockSpec((ts, width), lambda bi, si: (bi * nseq + si, col_block))

    def full(shape):
        return pl.BlockSpec(shape, lambda bi, si: (0,) * len(shape))

    return pl.pallas_call(
        _mixer_body,
        out_shape=jax.ShapeDtypeStruct((t, POOL_WIDTH + GLA_WIDTH), BF16),
        grid=(batch, nseq),
        in_specs=[rows(POOL_WIDTH, 0),
                  rows(GLA_DK_TOTAL, 2),
                  rows(GLA_DK_TOTAL, 3),
                  rows(GLA_WIDTH, 2),
                  rows(GLA_WIDTH, 3),
                  rows(LANES, GATE_COL_BLOCK),
                  full(w2p.shape), full(gate_b.shape), full(norm_w.shape),
                  full(pool_w.shape), full(pool_scale.shape)],
        out_specs=pl.BlockSpec((ts, POOL_WIDTH + GLA_WIDTH), lambda bi, si: (bi * nseq + si, 0)),
        scratch_shapes=[pltpu.VMEM((ts + POOL_WINDOWS[-1], POOL_WIDTH), F32),
                        pltpu.VMEM((GLA_HEADS // 2, LANES, GLA_DV), F32),
                        pltpu.VMEM((GLA_HEADS // 2, ts // GLA_CHUNK, LANES, GLA_DV), F32),
                        pltpu.VMEM((GLA_HEADS // 2, ts // GLA_CHUNK, LANES, GLA_DV), BF16)],
        compiler_params=_cparams(("arbitrary", "arbitrary")),
        name="mixer",
    )(proj, proj, proj, proj, proj, proj, w2p, gate_b, norm_w, pool_w, pool_scale)


def _pack_row_words(x):
    half = x.shape[1] // 2
    u = pltpu.bitcast(x.astype(BF16).astype(F32), U32)
    hi_mask = jnp.uint32(HIGH_HALF)
    return [(u[:, half + s * LANES:half + (s + 1) * LANES] & hi_mask) | (u[:, s * LANES:(s + 1) * LANES] >> 16)
            for s in range(ROW_WORDS)]


def _unpack_row_words(w):
    return pltpu.bitcast(w << 16, F32), pltpu.bitcast(w & jnp.uint32(HIGH_HALF), F32)


def _outproj_body(m_ref, w_ref, x_ref, g_ref, b_ref, h_ref, h4_ref):
    y = DEEPNORM_ALPHA * x_ref[...] + jnp.dot(m_ref[...], w_ref[...], preferred_element_type=F32)
    h = _layernorm(y, g_ref[...], b_ref[...])
    h_ref[...] = h
    for s, w in enumerate(_pack_row_words(h)):
        h4_ref[:, s, :] = w


def _outproj_ln(mixed, w_bf, x2d, g, b):
    t, d = x2d.shape
    tm = PROJ_TM
    return pl.pallas_call(
        _outproj_body,
        out_shape=(jax.ShapeDtypeStruct((t, d), F32), jax.ShapeDtypeStruct((t, ROW_WORDS, LANES), U32)),
        grid=(t // tm,),
        in_specs=[pl.BlockSpec((tm, d), lambda i: (i, 0)),
                  pl.BlockSpec((d, d), lambda i: (0, 0)),
                  pl.BlockSpec((tm, d), lambda i: (i, 0)),
                  pl.BlockSpec((1, d), lambda i: (0, 0)),
                  pl.BlockSpec((1, d), lambda i: (0, 0))],
        out_specs=(pl.BlockSpec((tm, d), lambda i: (i, 0)),
                   pl.BlockSpec((tm, ROW_WORDS, LANES), lambda i: (i, 0, 0))),
        compiler_params=_cparams(("arbitrary",)),
        name="outproj_ln",
    )(mixed, w_bf, x2d, g, b)


def _first_argmax_rows(val, rowf, nrows):
    m = jnp.max(val, axis=0, keepdims=True)
    first = jnp.min(jnp.where(val == m, rowf, float(nrows)), axis=0, keepdims=True)
    return m, first, rowf == first


def _router_body(h_ref, whi_ref, wlo_ref, bias_ref, idx_ref, gate_ref, rank_ref, cnt_ref, carry):
    tm = h_ref.shape[0]
    i = pl.program_id(0)

    @pl.when(i == 0)
    def _():
        carry[...] = jnp.zeros(carry.shape, F32)

    h = h_ref[...]
    h_hi = h.astype(BF16)
    h_lo = (h - h_hi.astype(F32)).astype(BF16)
    nt = (((1,), (1,)), ((), ()))
    logits = (lax.dot_general(whi_ref[...], h_hi, nt, preferred_element_type=F32)
              + lax.dot_general(whi_ref[...], h_lo, nt, preferred_element_type=F32)
              + lax.dot_general(wlo_ref[...], h_hi, nt, preferred_element_type=F32))
    scores = 1.0 / (1.0 + jnp.exp(-logits))
    biased = scores + bias_ref[...]
    neg = -jnp.inf

    grp = biased.reshape(N_GROUPS, GROUP_SIZE, tm)
    gi = lax.broadcasted_iota(I32, (N_GROUPS, GROUP_SIZE, tm), 1).astype(F32)
    g1 = jnp.max(grp, axis=1, keepdims=True)
    f1 = jnp.min(jnp.where(grp == g1, gi, float(GROUP_SIZE)), axis=1, keepdims=True)
    g2 = jnp.max(jnp.where(gi == f1, neg, grp), axis=1, keepdims=True)
    gscore = (g1 + g2).reshape(N_GROUPS, tm)

    growf = lax.broadcasted_iota(I32, (N_GROUPS, tm), 0).astype(F32)
    gsel = jnp.zeros((N_GROUPS, tm), F32)
    gval = gscore
    for _ in range(TOPK_GROUPS):
        _, _, pick = _first_argmax_rows(gval, growf, N_GROUPS)
        gsel = jnp.where(pick, 1.0, gsel)
        gval = jnp.where(pick, neg, gval)
    emask = jnp.broadcast_to(gsel.reshape(N_GROUPS, 1, tm), (N_GROUPS, GROUP_SIZE, tm)).reshape(N_EXPERTS, tm)

    rowf = lax.broadcasted_iota(I32, (N_EXPERTS, tm), 0).astype(F32)
    val = jnp.where(emask > 0.0, biased, neg)
    onehot = jnp.zeros((N_EXPERTS, tm), F32)
    picks, idxs, ws = [], [], []
    for _ in range(TOP_K):
        _, first, pick = _first_argmax_rows(val, rowf, N_EXPERTS)
        picks.append(pick)
        idxs.append(first)
        ws.append(jnp.sum(jnp.where(pick, scores, 0.0), axis=0, keepdims=True))
        onehot = jnp.where(pick, 1.0, onehot)
        val = jnp.where(pick, neg, val)
    w = jnp.concatenate(ws, axis=0)
    gate_ref[...] = w / jnp.sum(w, axis=0, keepdims=True) * ROUTED_SCALE
    idx_ref[...] = jnp.concatenate(idxs, axis=0).astype(I32)

    ti = lax.broadcasted_iota(I32, (tm, tm), 0)
    tj = lax.broadcasted_iota(I32, (tm, tm), 1)
    upper = jnp.where(ti < tj, 1.0, 0.0).astype(BF16)
    prefix = jnp.dot(onehot.astype(BF16), upper, preferred_element_type=F32) + carry[...]
    ranks = [jnp.sum(jnp.where(pk, prefix, 0.0), axis=0, keepdims=True) for pk in picks]
    rank_ref[...] = jnp.concatenate(ranks, axis=0).astype(I32)
    carry[...] = carry[...] + jnp.sum(onehot, axis=1, keepdims=True)
    cnt_ref[...] = carry[...]


def _router(h, wt, bias_col):
    t, d = h.shape
    tm = ROUTE_TM
    wt_hi = wt.astype(BF16)
    wt_lo = (wt - wt_hi.astype(F32)).astype(BF16)
    return pl.pallas_call(
        _router_body,
        out_shape=(jax.ShapeDtypeStruct((TOP_K, t), I32), jax.ShapeDtypeStruct((TOP_K, t), F32),
                   jax.ShapeDtypeStruct((TOP_K, t), I32), jax.ShapeDtypeStruct((N_EXPERTS, 1), F32)),
        grid=(t // tm,),
        in_specs=[pl.BlockSpec((tm, d), lambda i: (i, 0)),
                  pl.BlockSpec((N_EXPERTS, d), lambda i: (0, 0)),
                  pl.BlockSpec((N_EXPERTS, d), lambda i: (0, 0)),
                  pl.BlockSpec((N_EXPERTS, 1), lambda i: (0, 0))],
        out_specs=(pl.BlockSpec((TOP_K, tm), lambda i: (0, i)),
                   pl.BlockSpec((TOP_K, tm), lambda i: (0, i)),
                   pl.BlockSpec((TOP_K, tm), lambda i: (0, i)),
                   pl.BlockSpec((N_EXPERTS, 1), lambda i: (0, 0))),
        scratch_shapes=[pltpu.VMEM((N_EXPERTS, 1), F32)],
        compiler_params=_cparams(("arbitrary",)),
        name="router",
    )(h, wt_hi, wt_lo, bias_col)


def _positions_body(idx_ref, rank_ref, start_ref, pos_ref):
    tm = idx_ref.shape[1]
    rowi = lax.broadcasted_iota(I32, (N_EXPERTS, tm), 0)
    start = start_ref[...]
    idx = idx_ref[...]
    rows = [jnp.sum(jnp.where(rowi == idx[k:k + 1, :], start, 0.0), axis=0, keepdims=True) for k in range(TOP_K)]
    pos_ref[...] = jnp.concatenate(rows, axis=0).astype(I32) + rank_ref[...]


def _positions(idx_t, rank_t, start_col):
    t = idx_t.shape[1]
    tm = POS_TM
    return pl.pallas_call(
        _positions_body,
        out_shape=jax.ShapeDtypeStruct((TOP_K, t), I32),
        grid=(t // tm,),
        in_specs=[pl.BlockSpec((TOP_K, tm), lambda i: (0, i)),
                  pl.BlockSpec((TOP_K, tm), lambda i: (0, i)),
                  pl.BlockSpec((N_EXPERTS, 1), lambda i: (0, 0))],
        out_specs=pl.BlockSpec((TOP_K, tm), lambda i: (0, i)),
        compiler_params=_cparams(("arbitrary",)),
        name="positions",
    )(idx_t, rank_t, start_col)


def _pad_fill_copy(zeros, xs_ref, sem, row, nrows):
    return pltpu.make_async_copy(zeros.at[pl.ds(0, nrows)], xs_ref.at[pl.ds(row, nrows)], sem)


def _dispatch_body(pad_row_ref, pad_n_ref, nu_ref, pos_ref, h4_ref, xs_ref, zeros, sem, pad_sem):
    i = pl.program_id(0)
    tm = h4_ref.shape[0]
    half = MOE_BM // 2
    pad_bits = [1 << j for j in range(MOE_BM.bit_length() - 1)]
    n_half_blocks = xs_ref.shape[0] // half

    def pad_pass(wait):
        def go(cp):
            if wait:
                cp.wait()
            else:
                cp.start()

        def body(e, carry):
            row = pad_row_ref[e]
            n = pad_n_ref[e]
            for bit in pad_bits:
                @pl.when((n & bit) != 0)
                def _():
                    go(_pad_fill_copy(zeros, xs_ref, pad_sem, row + (n & (bit - 1)), bit))
            return carry
        lax.fori_loop(0, N_EXPERTS, body, 0)

        def tail(hb, carry):
            go(_pad_fill_copy(zeros, xs_ref, pad_sem, hb * half, half))
            return carry
        lax.fori_loop(nu_ref[0] * 2, n_half_blocks, tail, 0)

    @pl.when(i == 0)
    def _():
        zeros[...] = jnp.zeros(zeros.shape, U32)
        pad_pass(False)

    def body(t, carry):
        for k in range(TOP_K):
            p = pos_ref[0, 0, t * TOP_K + k]
            pltpu.make_async_copy(h4_ref.at[t], xs_ref.at[p], sem).start(priority=k % 2)
        return carry
    lax.fori_loop(0, tm, body, 0)

    for k in range(TOP_K):
        pltpu.make_async_copy(h4_ref, xs_ref.at[pl.ds(0, tm)], sem).wait()

    @pl.when(i == pl.num_programs(0) - 1)
    def _():
        pad_pass(True)


def _dispatch(pad_row, pad_n, n_used, pos3, h4, n_rows):
    nt, _, per_step = pos3.shape
    tm = per_step // TOP_K
    grid_spec = pltpu.PrefetchScalarGridSpec(
        num_scalar_prefetch=3,
        grid=(nt,),
        in_specs=[pl.BlockSpec((1, 1, per_step), lambda i, a, b, c: (i, 0, 0), memory_space=pltpu.SMEM),
                  pl.BlockSpec((tm, ROW_WORDS, LANES), lambda i, a, b, c: (i, 0, 0))],
        out_specs=pl.BlockSpec(memory_space=pl.ANY),
        scratch_shapes=[pltpu.VMEM((MOE_BM // 2, ROW_WORDS, LANES), U32),
                        pltpu.SemaphoreType.DMA, pltpu.SemaphoreType.DMA],
    )
    return pl.pallas_call(
        _dispatch_body,
        out_shape=jax.ShapeDtypeStruct((n_rows, ROW_WORDS, LANES), U32),
        grid_spec=grid_spec,
        compiler_params=_cparams(("arbitrary",)),
        name="dispatch",
    )(pad_row, pad_n, n_used, pos3, h4)


def _weight_copies(e, w_hbm, stage, sem):
    return [pltpu.make_async_copy(w.at[e], st, sem.at[j]) for j, (w, st) in enumerate(zip(w_hbm, stage))]


def _row_block_copy(xs_hbm, xbuf, xsem, j):
    rows = xbuf.shape[1]
    slot = j % X_RING
    return pltpu.make_async_copy(xs_hbm.at[pl.ds(pl.multiple_of(j * rows, rows), rows)], xbuf.at[slot],
                                 xsem.at[slot])


def _experts_body(be_ref, nu_ref, nx_ref, xs_hbm, wg_hbm, wu_hbm, wd_hbm, y4_ref,
                  xbuf, xsem, sg, su, sd, wsem, wg_b, wu_b, wd_b):
    i = pl.program_id(0)
    w_hbm = (wg_hbm, wu_hbm, wd_hbm)
    stage = (sg, su, sd)
    n_used = nu_ref[0]

    @pl.when(i < n_used)
    def _():
        e = be_ref[i]
        prev_e = be_ref[jnp.maximum(i - 1, 0)]

        @pl.when(i == 0)
        def _():
            for j in range(X_RING - 1):
                @pl.when(j < n_used)
                def _():
                    _row_block_copy(xs_hbm, xbuf, xsem, j).start()

        @pl.when(i + X_RING - 1 < n_used)
        def _():
            _row_block_copy(xs_hbm, xbuf, xsem, i + X_RING - 1).start()

        @pl.when((i == 0) | (e != prev_e))
        def _():
            @pl.when(i == 0)
            def _():
                for cp in _weight_copies(e, w_hbm, stage, wsem):
                    cp.start()
            for cp in _weight_copies(e, w_hbm, stage, wsem):
                cp.wait()
            wg_b[...] = sg[...].astype(BF16)
            wu_b[...] = su[...].astype(BF16)
            wd_b[...] = sd[...].astype(BF16)
            nxt = nx_ref[e]

            @pl.when(nxt >= 0)
            def _():
                for cp in _weight_copies(nxt, w_hbm, stage, wsem):
                    cp.start()

        _row_block_copy(xs_hbm, xbuf, xsem, i).wait()
        slot = i % X_RING
        parts = [_unpack_row_words(xbuf[slot, :, s, :]) for s in range(ROW_WORDS)]
        x = jnp.concatenate([p[0] for p in parts] + [p[1] for p in parts], axis=1).astype(BF16)
        g = jnp.dot(x, wg_b[...], preferred_element_type=F32)
        u = jnp.dot(x, wu_b[...], preferred_element_type=F32)
        a = (_silu(g) * u).astype(BF16)
        y = jnp.dot(a, wd_b[...], preferred_element_type=F32)
        for s, w in enumerate(_pack_row_words(y)):
            y4_ref[:, s, :] = w

    @pl.when(i >= nu_ref[0])
    def _():
        y4_ref[...] = jnp.zeros(y4_ref.shape, U32)


def _experts(block_e, n_used, next_e, xs, wg, wu, wd):
    nblk = block_e.shape[0]
    bm = MOE_BM
    e, d, de = wg.shape

    grid_spec = pltpu.PrefetchScalarGridSpec(
        num_scalar_prefetch=3,
        grid=(nblk,),
        in_specs=[pl.BlockSpec(memory_space=pl.ANY),
                  pl.BlockSpec(memory_space=pl.ANY),
                  pl.BlockSpec(memory_space=pl.ANY),
                  pl.BlockSpec(memory_space=pl.ANY)],
        out_specs=pl.BlockSpec((bm, ROW_WORDS, LANES), lambda i, be, nu, nx: (i, 0, 0)),
        scratch_shapes=[pltpu.VMEM((X_RING, bm, ROW_WORDS, LANES), U32), pltpu.SemaphoreType.DMA((X_RING,)),
                        pltpu.VMEM((d, de), F32), pltpu.VMEM((d, de), F32), pltpu.VMEM((de, d), F32),
                        pltpu.SemaphoreType.DMA((3,)),
                        pltpu.VMEM((d, de), BF16), pltpu.VMEM((d, de), BF16), pltpu.VMEM((de, d), BF16)],
    )
    return pl.pallas_call(
        _experts_body,
        out_shape=jax.ShapeDtypeStruct((nblk * bm, ROW_WORDS, LANES), U32),
        grid_spec=grid_spec,
        compiler_params=_cparams(("arbitrary",)),
        name="experts",
    )(block_e, n_used, next_e, xs, wg, wu, wd)


def _combine_body(pos_cur, pos_nxt, y4_ref, h_ref, gate_ref, wsg_ref, wsu_ref, wsd_ref, g_ref, b_ref,
                  o_ref, buf, sem):
    i = pl.program_id(0)
    nb = pl.num_programs(0)
    tm, d = h_ref.shape
    slot = i % 2

    def issue(pos_ref, sl):
        def body(t, carry):
            for k in range(TOP_K):
                p = pos_ref[0, 0, t * TOP_K + k]
                pltpu.make_async_copy(y4_ref.at[p], buf.at[sl, k * tm + t], sem.at[sl]).start(priority=k % 2)
            return carry
        lax.fori_loop(0, tm, body, 0)

    @pl.when(i == 0)
    def _():
        issue(pos_cur, 0)

    @pl.when(i + 1 < nb)
    def _():
        issue(pos_nxt, 1 - slot)

    h = h_ref[...]
    hb = h.astype(BF16)
    sg = jnp.dot(hb, wsg_ref[...], preferred_element_type=F32)
    su = jnp.dot(hb, wsu_ref[...], preferred_element_type=F32)
    shared = jnp.dot((_silu(sg) * su).astype(BF16), wsd_ref[...], preferred_element_type=F32)

    pltpu.make_async_copy(y4_ref.at[pl.ds(0, tm * TOP_K)], buf.at[slot], sem.at[slot]).wait()
    gates = gate_ref[...]
    lo_cols, hi_cols = [], []
    for s in range(ROW_WORDS):
        lo_acc = hi_acc = None
        for k in range(TOP_K):
            lo, hi = _unpack_row_words(buf[slot, pl.ds(k * tm, tm), s, :])
            gk = gates[:, k:k + 1]
            lo_acc = gk * lo if lo_acc is None else lo_acc + gk * lo
            hi_acc = gk * hi if hi_acc is None else hi_acc + gk * hi
        lo_cols.append(lo_acc)
        hi_cols.append(hi_acc)
    routed = jnp.concatenate(lo_cols + hi_cols, axis=1)
    o_ref[...] = _layernorm(DEEPNORM_ALPHA * h + (routed + shared), g_ref[...], b_ref[...])


def _combine(pos3, y4, h, gates, wsg, wsu, wsd, g, b):
    t, d = h.shape
    tm = COMB_TM
    nt = t // tm
    ds_ = wsg.shape[1]
    return pl.pallas_call(
        _combine_body,
        out_shape=jax.ShapeDtypeStruct((t, d), F32),
        grid=(nt,),
        in_specs=[pl.BlockSpec((1, 1, tm * TOP_K), lambda i: (i, 0, 0), memory_space=pltpu.SMEM),
                  pl.BlockSpec((1, 1, tm * TOP_K), lambda i: (jnp.minimum(i + 1, nt - 1), 0, 0),
                               memory_space=pltpu.SMEM),
                  pl.BlockSpec(memory_space=pl.ANY),
                  pl.BlockSpec((tm, d), lambda i: (i, 0)),
                  pl.BlockSpec((tm, TOP_K), lambda i: (i, 0)),
                  pl.BlockSpec((d, ds_), lambda i: (0, 0)),
                  pl.BlockSpec((d, ds_), lambda i: (0, 0)),
                  pl.BlockSpec((ds_, d), lambda i: (0, 0)),
                  pl.BlockSpec((1, d), lambda i: (0, 0)),
                  pl.BlockSpec((1, d), lambda i: (0, 0))],
        out_specs=pl.BlockSpec((tm, d), lambda i: (i, 0)),
        scratch_shapes=[pltpu.VMEM((2, tm * TOP_K, ROW_WORDS, LANES), U32),
                        pltpu.SemaphoreType.DMA((2,))],
        compiler_params=_cparams(("arbitrary",)),
        name="combine",
    )(pos3, pos3, y4, h, gates, wsg, wsu, wsd, g, b)


def _expert_tables(counts, nblk):
    bm = MOE_BM
    cnt = counts.reshape(N_EXPERTS).astype(I32)
    padded = (cnt + bm - 1) // bm * bm
    padded_end = jnp.cumsum(padded)
    padded_start = padded_end - padded
    block_rows = jnp.arange(nblk, dtype=I32) * bm
    block_e = jnp.sum((padded_end[None, :] <= block_rows[:, None]).astype(I32), axis=1)
    block_e = jnp.minimum(block_e, N_EXPERTS - 1)
    n_used = (padded_end[-1:] // bm).astype(I32)
    ids = jnp.where(cnt > 0, jnp.arange(N_EXPERTS, dtype=I32), N_EXPERTS)
    after = jnp.concatenate([lax.cummin(ids, reverse=True)[1:], jnp.full((1,), N_EXPERTS, I32)])
    next_e = jnp.where(after < N_EXPERTS, after, -1).astype(I32)
    return padded_start, padded_start + cnt, padded - cnt, block_e, n_used, next_e


def kernel(x, w_in, gla_gate_w2, gla_gate_b, gla_norm_w, pool_w_group, pool_scale, w_out, ln1_g, ln1_b,
           router_w, router_bias, w_exp_gate, w_exp_up, w_exp_down, w_sh_gate, w_sh_up, w_sh_down, ln2_g, ln2_b):
    batch, seq, d = x.shape
    t = batch * seq
    h2d = x.reshape(t, d)
    for l in range(DEPTH):
        d_in = w_in.shape[2]
        w_in_b = jnp.pad(w_in[l], ((0, 0), (0, D_IN_PAD - d_in))).astype(BF16)
        w2p = jnp.pad(gla_gate_w2[l], ((0, LANES - GLA_GATE_RANK), (0, 0))).astype(BF16)
        proj = _inproj(h2d, w_in_b)
        mixed = _mixer(proj, batch, seq, w2p, gla_gate_b[l].reshape(1, -1), gla_norm_w[l].reshape(1, -1),
                       pool_w_group[l].astype(BF16), pool_scale[l].reshape(1, -1))
        h, h4 = _outproj_ln(mixed, w_out[l].astype(BF16), h2d, ln1_g[l].reshape(1, -1), ln1_b[l].reshape(1, -1))
        idx_t, gate_t, rank_t, counts = _router(h, router_w[l].T, router_bias[l].reshape(-1, 1))
        nblk = (t * TOP_K + N_EXPERTS * (MOE_BM - 1)) // MOE_BM
        start, pad_row, pad_n, block_e, n_used, next_e = _expert_tables(counts, nblk)
        pos_t = _positions(idx_t, rank_t, start.astype(F32).reshape(-1, 1))
        pos_tok = pos_t.T
        xs = _dispatch(pad_row, pad_n, n_used, pos_tok.reshape(t // DISP_TM, 1, DISP_TM * TOP_K), h4, nblk * MOE_BM)
        y4 = _experts(block_e, n_used, next_e, xs, w_exp_gate[l], w_exp_up[l], w_exp_down[l])
        h2d = _combine(pos_tok.reshape(t // COMB_TM, 1, COMB_TM * TOP_K), y4, h, gate_t.T,
                       w_sh_gate[l].astype(BF16), w_sh_up[l].astype(BF16), w_sh_down[l].astype(BF16),
                       ln2_g[l].reshape(1, -1), ln2_b[l].reshape(1, -1))
    return h2d.reshape(batch, seq, d)
```

```python
import functools

import jax
import jax.numpy as jnp
from jax import lax
from jax.experimental import pallas as pl
from jax.experimental.pallas import tpu as pltpu

F32 = jnp.float32
BF16 = jnp.bfloat16
I32 = jnp.int32
U32 = jnp.uint32
HIGH_HALF = 0xFFFF0000

POOL_WINDOWS = (2, 4, 8, 16)
POOL_GROUP_DIM = 128
POOL_WIDTH = 512
GLA_HEADS = 4
GLA_DK = 64
GLA_DV = 128
GLA_DK_TOTAL = 256
GLA_WIDTH = 512
GLA_GATE_RANK = 16
GLA_GATE_NORMALIZER = 16.0
GLA_CHUNK = 16
N_EXPERTS = 256
TOP_K = 8
N_GROUPS = 8
GROUP_SIZE = N_EXPERTS // N_GROUPS
TOPK_GROUPS = 4
ROUTED_SCALE = 2.5
DEPTH = 1
DEEPNORM_ALPHA = (2.0 * DEPTH) ** 0.25
LN_EPS = 1e-5
RMS_EPS = 1e-5

LANES = 128
SUBLANES = 8
VMEM_LIMIT = 56 * 1024 * 1024

PROJ_TM = 512
MIX_TS = 256
ROUTE_TM = 256
MOE_BM = 256
COMB_TM = 128
POS_TM = 512
DISP_TM = 256
ROW_WORDS = 4
X_RING = 3
D_IN_PAD = 2944
GATE_COL_BLOCK = 16


def _cparams(sem):
    return pltpu.CompilerParams(dimension_semantics=sem, vmem_limit_bytes=VMEM_LIMIT)


def _silu(x):
    return x * (1.0 / (1.0 + jnp.exp(-x)))


def _layernorm(y, g, b):
    mu = jnp.mean(y, axis=-1, keepdims=True)
    yc = y - mu
    var = jnp.mean(yc * yc, axis=-1, keepdims=True)
    return yc * lax.rsqrt(var + LN_EPS) * g + b


def _inproj_body(x_ref, w_ref, o_ref):
    o_ref[...] = jnp.dot(x_ref[...].astype(BF16), w_ref[...], preferred_element_type=F32)


def _inproj(x2d, w_bf):
    t, d = x2d.shape
    n = w_bf.shape[1]
    return pl.pallas_call(
        _inproj_body,
        out_shape=jax.ShapeDtypeStruct((t, n), F32),
        grid=(t // PROJ_TM,),
        in_specs=[pl.BlockSpec((PROJ_TM, d), lambda i: (i, 0)),
                  pl.BlockSpec((d, n), lambda i: (0, 0))],
        out_specs=pl.BlockSpec((PROJ_TM, n), lambda i: (i, 0)),
        compiler_params=_cparams(("arbitrary",)),
        name="inproj",
    )(x2d, w_bf)


def _mixer_body(p_ref, q_ref, k_ref, v_ref, r_ref, gl_ref, w2_ref, gb_ref, nw_ref, pw_ref, ps_ref,
                o_ref, pbuf, state, kvbuf, sall):
    ts = p_ref.shape[0]
    s_idx = pl.program_id(1)
    halo = POOL_WINDOWS[-1]

    @pl.when(s_idx == 0)
    def _():
        pbuf[pl.ds(0, halo), :] = jnp.zeros((halo, POOL_WIDTH), F32)
        state[...] = jnp.zeros(state.shape, F32)

    p = p_ref[...]
    pbuf[pl.ds(halo, ts), :] = p
    pos = s_idx * ts + lax.broadcasted_iota(I32, (ts, 1), 0)
    for g, w in enumerate(POOL_WINDOWS):
        c0 = g * POOL_GROUP_DIM
        acc = pbuf[pl.ds(halo, ts), pl.ds(c0, POOL_GROUP_DIM)]
        for j in range(1, w):
            acc = acc + pbuf[pl.ds(halo - j, ts), pl.ds(c0, POOL_GROUP_DIM)]
        cnt = jnp.minimum(pos + 1, w).astype(F32)
        mixed = acc / cnt - p[:, c0:c0 + POOL_GROUP_DIM]
        og = jnp.dot(mixed.astype(BF16), pw_ref[g], preferred_element_type=F32)
        o_ref[:, pl.ds(c0, POOL_GROUP_DIM)] = (og * ps_ref[:, pl.ds(c0, POOL_GROUP_DIM)]).astype(o_ref.dtype)
    pbuf[pl.ds(0, halo), :] = pbuf[pl.ds(ts, halo), :]

    nchunk = ts // GLA_CHUNK
    glog = jnp.dot(gl_ref[...].astype(BF16), w2_ref[...], preferred_element_type=F32) + gb_ref[...]
    gk = (jnp.minimum(glog, 0.0) - jnp.log(1.0 + jnp.exp(-jnp.abs(glog)))) * (1.0 / GLA_GATE_NORMALIZER)
    row = lax.broadcasted_iota(I32, (ts, 1), 0)
    rin = row % GLA_CHUNK
    b = gk
    sh = 1
    while sh < GLA_CHUNK:
        b = b + jnp.where(rin >= sh, pltpu.roll(b, sh, axis=0), 0.0)
        sh *= 2
    b3 = b.reshape(nchunk, GLA_CHUNK, GLA_DK_TOTAL)
    bmid = b3[:, GLA_CHUNK // 2 - 1:GLA_CHUNK // 2, :]
    blast = b3[:, GLA_CHUNK - 1:GLA_CHUNK, :]
    q3 = (q_ref[...] * (GLA_DK ** -0.5)).reshape(nchunk, GLA_CHUNK, GLA_DK_TOTAL)
    k3 = k_ref[...].reshape(nchunk, GLA_CHUNK, GLA_DK_TOTAL)
    qs = (q3 * jnp.exp(b3 - bmid)).reshape(ts, GLA_DK_TOTAL)
    ks = (k3 * jnp.exp(bmid - b3)).reshape(ts, GLA_DK_TOTAL)
    qd = (q3 * jnp.exp(b3)).reshape(ts, GLA_DK_TOTAL)
    kd = (k3 * jnp.exp(blast - b3)).reshape(ts, GLA_DK_TOTAL)
    cdec = jnp.exp(blast).reshape(nchunk, GLA_DK_TOTAL)

    v = v_ref[...]
    vb = v.astype(BF16)
    lane = lax.broadcasted_iota(I32, (1, LANES), 1)
    head_lane = [lane < GLA_DK, lane >= GLA_DK]

    blk = LANES
    ri = lax.broadcasted_iota(I32, (blk, blk), 0)
    ci = lax.broadcasted_iota(I32, (blk, blk), 1)
    causal = (ri // GLA_CHUNK == ci // GLA_CHUNK) & (ri >= ci)
    o_intra = [[None] * (ts // blk) for _ in range(GLA_HEADS)]
    for rb in range(ts // blk):
        rs = slice(rb * blk, (rb + 1) * blk)
        for pair in range(GLA_HEADS // 2):
            ls = slice(pair * LANES, (pair + 1) * LANES)
            ks_p = ks[rs, ls].astype(BF16)
            for sub in range(2):
                h = pair * 2 + sub
                q_m = jnp.where(head_lane[sub], qs[rs, ls], 0.0).astype(BF16)
                sc = lax.dot_general(q_m, ks_p, (((1,), (1,)), ((), ())), preferred_element_type=F32)
                sc = jnp.where(causal, sc, 0.0).astype(BF16)
                o_intra[h][rb] = jnp.dot(sc, vb[rs, h * GLA_DV:(h + 1) * GLA_DV], preferred_element_type=F32)

    cpb = blk // GLA_CHUNK
    chunk_of_col = lax.broadcasted_iota(I32, (1, blk), 1) // GLA_CHUNK
    stack_head_row = (lax.broadcasted_iota(I32, (cpb * LANES, 1), 0) % LANES) < GLA_DK
    for rb in range(ts // blk):
        rs = slice(rb * blk, (rb + 1) * blk)
        for pair in range(GLA_HEADS // 2):
            ls = slice(pair * LANES, (pair + 1) * LANES)
            kd_t = kd[rs, ls].T
            lhs = jnp.concatenate([jnp.where(chunk_of_col == c, kd_t, 0.0) for c in range(cpb)],
                                  axis=0).astype(BF16)
            inc = [jnp.dot(lhs, vb[rs, (pair * 2 + sub) * GLA_DV:(pair * 2 + sub + 1) * GLA_DV],
                           preferred_element_type=F32) for sub in range(2)]
            stacked = jnp.where(stack_head_row, inc[0], inc[1])
            for c in range(cpb):
                kvbuf[pair, rb * cpb + c] = stacked[c * LANES:(c + 1) * LANES]
    for pair in range(GLA_HEADS // 2):
        ls = slice(pair * LANES, (pair + 1) * LANES)
        dec_cols = cdec[:, ls].T
        st = state[pair]
        for c in range(nchunk):
            sall[pair, c] = st.astype(BF16)
            st = st * dec_cols[:, c:c + 1] + kvbuf[pair, c]
        state[pair] = st
    o_inter = [[None] * nchunk for _ in range(GLA_HEADS)]
    for pair in range(GLA_HEADS // 2):
        ls = slice(pair * LANES, (pair + 1) * LANES)
        for c in range(nchunk):
            rs = slice(c * GLA_CHUNK, (c + 1) * GLA_CHUNK)
            for sub in range(2):
                q_m = jnp.where(head_lane[sub], qd[rs, ls], 0.0).astype(BF16)
                o_inter[pair * 2 + sub][c] = jnp.dot(q_m, sall[pair, c], preferred_element_type=F32)

    nw = nw_ref[...]
    r = r_ref[...]
    for h in range(GLA_HEADS):
        o = jnp.concatenate(o_intra[h], axis=0) + jnp.concatenate(o_inter[h], axis=0)
        o = o * lax.rsqrt(jnp.mean(o * o, axis=-1, keepdims=True) + RMS_EPS) * nw
        o = o * _silu(r[:, h * GLA_DV:(h + 1) * GLA_DV])
        o_ref[:, pl.ds(POOL_WIDTH + h * GLA_DV, GLA_DV)] = o.astype(o_ref.dtype)


def _mixer(proj, batch, seq, w2p, gate_b, norm_w, pool_w, pool_scale):
    t = proj.shape[0]
    ts = MIX_TS
    nseq = seq // ts

    def rows(width, col_block):
        return pl.BlockSpec((ts, width), lambda bi, si: (bi * nseq + si, col_block))

    def full(shape):
        return pl.BlockSpec(shape, lambda bi, si: (0,) * len(shape))

    return pl.pallas_call(
        _mixer_body,
        out_shape=jax.ShapeDtypeStruct((t, POOL_WIDTH + GLA_WIDTH), BF16),
        grid=(batch, nseq),
        in_specs=[rows(POOL_WIDTH, 0),
                  rows(GLA_DK_TOTAL, 2),
                  rows(GLA_DK_TOTAL, 3),
                  rows(GLA_WIDTH, 2),
                  rows(GLA_WIDTH, 3),
                  rows(LANES, GATE_COL_BLOCK),
                  full(w2p.shape), full(gate_b.shape), full(norm_w.shape),
                  full(pool_w.shape), full(pool_scale.shape)],
        out_specs=pl.BlockSpec((ts, POOL_WIDTH + GLA_WIDTH), lambda bi, si: (bi * nseq + si, 0)),
        scratch_shapes=[pltpu.VMEM((ts + POOL_WINDOWS[-1], POOL_WIDTH), F32),
                        pltpu.VMEM((GLA_HEADS // 2, LANES, GLA_DV), F32),
                        pltpu.VMEM((GLA_HEADS // 2, ts // GLA_CHUNK, LANES, GLA_DV), F32),
                        pltpu.VMEM((GLA_HEADS // 2, ts // GLA_CHUNK, LANES, GLA_DV), BF16)],
        compiler_params=_cparams(("arbitrary", "arbitrary")),
        name="mixer",
    )(proj, proj, proj, proj, proj, proj, w2p, gate_b, norm_w, pool_w, pool_scale)


def _pack_row_words(x):
    half = x.shape[1] // 2
    u = pltpu.bitcast(x.astype(BF16).astype(F32), U32)
    hi_mask = jnp.uint32(HIGH_HALF)
    return [(u[:, half + s * LANES:half + (s + 1) * LANES] & hi_mask) | (u[:, s * LANES:(s + 1) * LANES] >> 16)
            for s in range(ROW_WORDS)]


def _unpack_row_words(w):
    return pltpu.bitcast(w << 16, F32), pltpu.bitcast(w & jnp.uint32(HIGH_HALF), F32)


def _rows(ref, first, n, align=1):
    if isinstance(first, int):
        start = first * ROW_WORDS
    else:
        start = pl.multiple_of(first * ROW_WORDS, ROW_WORDS * align)
    return ref.at[pl.ds(start, n * ROW_WORDS), :]


def _word_plane(first, m, s):
    return pl.ds(first * ROW_WORDS + s, m, stride=ROW_WORDS)


def _outproj_body(m_ref, w_ref, x_ref, g_ref, b_ref, h_ref, h4_ref):
    y = DEEPNORM_ALPHA * x_ref[...] + jnp.dot(m_ref[...], w_ref[...], preferred_element_type=F32)
    h = _layernorm(y, g_ref[...], b_ref[...])
    h_ref[...] = h
    for s, w in enumerate(_pack_row_words(h)):
        h4_ref[_word_plane(0, h.shape[0], s), :] = w


def _outproj_ln(mixed, w_bf, x2d, g, b):
    t, d = x2d.shape
    tm = PROJ_TM
    return pl.pallas_call(
        _outproj_body,
        out_shape=(jax.ShapeDtypeStruct((t, d), F32), jax.ShapeDtypeStruct((t * ROW_WORDS, LANES), U32)),
        grid=(t // tm,),
        in_specs=[pl.BlockSpec((tm, d), lambda i: (i, 0)),
                  pl.BlockSpec((d, d), lambda i: (0, 0)),
                  pl.BlockSpec((tm, d), lambda i: (i, 0)),
                  pl.BlockSpec((1, d), lambda i: (0, 0)),
                  pl.BlockSpec((1, d), lambda i: (0, 0))],
        out_specs=(pl.BlockSpec((tm, d), lambda i: (i, 0)),
                   pl.BlockSpec((tm * ROW_WORDS, LANES), lambda i: (i, 0))),
        compiler_params=_cparams(("arbitrary",)),
        name="outproj_ln",
    )(mixed, w_bf, x2d, g, b)


def _first_argmax_rows(val, rowf, nrows):
    m = jnp.max(val, axis=0, keepdims=True)
    first = jnp.min(jnp.where(val == m, rowf, float(nrows)), axis=0, keepdims=True)
    return m, first, rowf == first


def _router_body(h_ref, whi_ref, wlo_ref, bias_ref, idx_ref, gate_ref, rank_ref, cnt_ref, carry):
    tm = h_ref.shape[0]
    i = pl.program_id(0)

    @pl.when(i == 0)
    def _():
        carry[...] = jnp.zeros(carry.shape, F32)

    h = h_ref[...]
    h_hi = h.astype(BF16)
    h_lo = (h - h_hi.astype(F32)).astype(BF16)
    nt = (((1,), (1,)), ((), ()))
    logits = (lax.dot_general(whi_ref[...], h_hi, nt, preferred_element_type=F32)
              + lax.dot_general(whi_ref[...], h_lo, nt, preferred_element_type=F32)
              + lax.dot_general(wlo_ref[...], h_hi, nt, preferred_element_type=F32))
    scores = 1.0 / (1.0 + jnp.exp(-logits))
    biased = scores + bias_ref[...]
    neg = -jnp.inf

    grp = biased.reshape(N_GROUPS, GROUP_SIZE, tm)
    gi = lax.broadcasted_iota(I32, (N_GROUPS, GROUP_SIZE, tm), 1).astype(F32)
    g1 = jnp.max(grp, axis=1, keepdims=True)
    f1 = jnp.min(jnp.where(grp == g1, gi, float(GROUP_SIZE)), axis=1, keepdims=True)
    g2 = jnp.max(jnp.where(gi == f1, neg, grp), axis=1, keepdims=True)
    gscore = (g1 + g2).reshape(N_GROUPS, tm)

    growf = lax.broadcasted_iota(I32, (N_GROUPS, tm), 0).astype(F32)
    gsel = jnp.zeros((N_GROUPS, tm), F32)
    gval = gscore
    for _ in range(TOPK_GROUPS):
        _, _, pick = _first_argmax_rows(gval, growf, N_GROUPS)
        gsel = jnp.where(pick, 1.0, gsel)
        gval = jnp.where(pick, neg, gval)
    emask = jnp.broadcast_to(gsel.reshape(N_GROUPS, 1, tm), (N_GROUPS, GROUP_SIZE, tm)).reshape(N_EXPERTS, tm)

    rowf = lax.broadcasted_iota(I32, (N_EXPERTS, tm), 0).astype(F32)
    val = jnp.where(emask > 0.0, biased, neg)
    onehot = jnp.zeros((N_EXPERTS, tm), F32)
    picks, idxs, ws = [], [], []
    for _ in range(TOP_K):
        _, first, pick = _first_argmax_rows(val, rowf, N_EXPERTS)
        picks.append(pick)
        idxs.append(first)
        ws.append(jnp.sum(jnp.where(pick, scores, 0.0), axis=0, keepdims=True))
        onehot = jnp.where(pick, 1.0, onehot)
        val = jnp.where(pick, neg, val)
    w = jnp.concatenate(ws, axis=0)
    gate_ref[...] = w / jnp.sum(w, axis=0, keepdims=True) * ROUTED_SCALE
    idx_ref[...] = jnp.concatenate(idxs, axis=0).astype(I32)

    ti = lax.broadcasted_iota(I32, (tm, tm), 0)
    tj = lax.broadcasted_iota(I32, (tm, tm), 1)
    upper = jnp.where(ti < tj, 1.0, 0.0).astype(BF16)
    prefix = jnp.dot(onehot.astype(BF16), upper, preferred_element_type=F32) + carry[...]
    ranks = [jnp.sum(jnp.where(pk, prefix, 0.0), axis=0, keepdims=True) for pk in picks]
    rank_ref[...] = jnp.concatenate(ranks, axis=0).astype(I32)
    carry[...] = carry[...] + jnp.sum(onehot, axis=1, keepdims=True)
    cnt_ref[...] = carry[...]


def _router(h, wt, bias_col):
    t, d = h.shape
    tm = ROUTE_TM
    wt_hi = wt.astype(BF16)
    wt_lo = (wt - wt_hi.astype(F32)).astype(BF16)
    return pl.pallas_call(
        _router_body,
        out_shape=(jax.ShapeDtypeStruct((TOP_K, t), I32), jax.ShapeDtypeStruct((TOP_K, t), F32),
                   jax.ShapeDtypeStruct((TOP_K, t), I32), jax.ShapeDtypeStruct((N_EXPERTS, 1), F32)),
        grid=(t // tm,),
        in_specs=[pl.BlockSpec((tm, d), lambda i: (i, 0)),
                  pl.BlockSpec((N_EXPERTS, d), lambda i: (0, 0)),
                  pl.BlockSpec((N_EXPERTS, d), lambda i: (0, 0)),
                  pl.BlockSpec((N_EXPERTS, 1), lambda i: (0, 0))],
        out_specs=(pl.BlockSpec((TOP_K, tm), lambda i: (0, i)),
                   pl.BlockSpec((TOP_K, tm), lambda i: (0, i)),
                   pl.BlockSpec((TOP_K, tm), lambda i: (0, i)),
                   pl.BlockSpec((N_EXPERTS, 1), lambda i: (0, 0))),
        scratch_shapes=[pltpu.VMEM((N_EXPERTS, 1), F32)],
        compiler_params=_cparams(("arbitrary",)),
        name="router",
    )(h, wt_hi, wt_lo, bias_col)


def _positions_body(idx_ref, rank_ref, start_ref, pos_ref):
    tm = idx_ref.shape[1]
    rowi = lax.broadcasted_iota(I32, (N_EXPERTS, tm), 0)
    start = start_ref[...]
    idx = idx_ref[...]
    rows = [jnp.sum(jnp.where(rowi == idx[k:k + 1, :], start, 0.0), axis=0, keepdims=True) for k in range(TOP_K)]
    pos_ref[...] = jnp.concatenate(rows, axis=0).astype(I32) + rank_ref[...]


def _positions(idx_t, rank_t, start_col):
    t = idx_t.shape[1]
    tm = POS_TM
    return pl.pallas_call(
        _positions_body,
        out_shape=jax.ShapeDtypeStruct((TOP_K, t), I32),
        grid=(t // tm,),
        in_specs=[pl.BlockSpec((TOP_K, tm), lambda i: (0, i)),
                  pl.BlockSpec((TOP_K, tm), lambda i: (0, i)),
                  pl.BlockSpec((N_EXPERTS, 1), lambda i: (0, 0))],
        out_specs=pl.BlockSpec((TOP_K, tm), lambda i: (0, i)),
        compiler_params=_cparams(("arbitrary",)),
        name="positions",
    )(idx_t, rank_t, start_col)


def _pad_fill_copy(zeros, xs_ref, sem, row, nrows):
    return pltpu.make_async_copy(_rows(zeros, 0, nrows), _rows(xs_ref, row, nrows), sem)


def _dispatch_body(pad_row_ref, pad_n_ref, nu_ref, pos_ref, h4_ref, xs_ref, zeros, sem, pad_sem):
    i = pl.program_id(0)
    tm = h4_ref.shape[0] // ROW_WORDS
    half = MOE_BM // 2
    pad_bits = [1 << j for j in range(MOE_BM.bit_length() - 1)]
    n_half_blocks = xs_ref.shape[0] // (half * ROW_WORDS)

    def pad_pass(wait):
        def go(cp):
            if wait:
                cp.wait()
            else:
                cp.start()

        def body(e, carry):
            row = pad_row_ref[e]
            n = pad_n_ref[e]
            for bit in pad_bits:
                @pl.when((n & bit) != 0)
                def _():
                    go(_pad_fill_copy(zeros, xs_ref, pad_sem, row + (n & (bit - 1)), bit))
            return carry
        lax.fori_loop(0, N_EXPERTS, body, 0)

        def tail(hb, carry):
            go(_pad_fill_copy(zeros, xs_ref, pad_sem, hb * half, half))
            return carry
        lax.fori_loop(nu_ref[0] * 2, n_half_blocks, tail, 0)

    @pl.when(i == 0)
    def _():
        zeros[...] = jnp.zeros(zeros.shape, U32)
        pad_pass(False)

    def body(t, carry):
        for k in range(TOP_K):
            p = pos_ref[0, 0, t * TOP_K + k]
            pltpu.make_async_copy(_rows(h4_ref, t, 1), _rows(xs_ref, p, 1), sem).start(priority=k % 2)
        return carry
    lax.fori_loop(0, tm, body, 0)

    for k in range(TOP_K):
        pltpu.make_async_copy(h4_ref, _rows(xs_ref, 0, tm), sem).wait()

    @pl.when(i == pl.num_programs(0) - 1)
    def _():
        pad_pass(True)


def _dispatch(pad_row, pad_n, n_used, pos3, h4, n_rows):
    nt, _, per_step = pos3.shape
    tm = per_step // TOP_K
    grid_spec = pltpu.PrefetchScalarGridSpec(
        num_scalar_prefetch=3,
        grid=(nt,),
        in_specs=[pl.BlockSpec((1, 1, per_step), lambda i, a, b, c: (i, 0, 0), memory_space=pltpu.SMEM),
                  pl.BlockSpec((tm * ROW_WORDS, LANES), lambda i, a, b, c: (i, 0))],
        out_specs=pl.BlockSpec(memory_space=pl.ANY),
        scratch_shapes=[pltpu.VMEM((MOE_BM // 2 * ROW_WORDS, LANES), U32),
                        pltpu.SemaphoreType.DMA, pltpu.SemaphoreType.DMA],
    )
    return pl.pallas_call(
        _dispatch_body,
        out_shape=jax.ShapeDtypeStruct((n_rows * ROW_WORDS, LANES), U32),
        grid_spec=grid_spec,
        compiler_params=_cparams(("arbitrary",)),
        name="dispatch",
    )(pad_row, pad_n, n_used, pos3, h4)


def _weight_copies(e, w_hbm, stage, sem):
    return [pltpu.make_async_copy(w.at[e], st, sem.at[j]) for j, (w, st) in enumerate(zip(w_hbm, stage))]


def _row_block_copy(xs_hbm, xbuf, xsem, j):
    slot = j % X_RING
    return pltpu.make_async_copy(_rows(xs_hbm, j * MOE_BM, MOE_BM, align=MOE_BM), xbuf.at[slot], xsem.at[slot])


def _experts_body(be_ref, nu_ref, nx_ref, xs_hbm, wg_hbm, wu_hbm, wd_hbm, y4_ref,
                  xbuf, xsem, sg, su, sd, wsem, wg_b, wu_b, wd_b):
    i = pl.program_id(0)
    w_hbm = (wg_hbm, wu_hbm, wd_hbm)
    stage = (sg, su, sd)
    n_used = nu_ref[0]

    @pl.when(i < n_used)
    def _():
        e = be_ref[i]
        prev_e = be_ref[jnp.maximum(i - 1, 0)]

        @pl.when(i == 0)
        def _():
            for j in range(X_RING - 1):
                @pl.when(j < n_used)
                def _():
                    _row_block_copy(xs_hbm, xbuf, xsem, j).start()

        @pl.when(i + X_RING - 1 < n_used)
        def _():
            _row_block_copy(xs_hbm, xbuf, xsem, i + X_RING - 1).start()

        @pl.when((i == 0) | (e != prev_e))
        def _():
            @pl.when(i == 0)
            def _():
                for cp in _weight_copies(e, w_hbm, stage, wsem):
                    cp.start()
            for cp in _weight_copies(e, w_hbm, stage, wsem):
                cp.wait()
            wg_b[...] = sg[...].astype(BF16)
            wu_b[...] = su[...].astype(BF16)
            wd_b[...] = sd[...].astype(BF16)
            nxt = nx_ref[e]

            @pl.when(nxt >= 0)
            def _():
                for cp in _weight_copies(nxt, w_hbm, stage, wsem):
                    cp.start()

        _row_block_copy(xs_hbm, xbuf, xsem, i).wait()
        slot = i % X_RING
        parts = [_unpack_row_words(xbuf[slot, _word_plane(0, MOE_BM, s), :]) for s in range(ROW_WORDS)]
        x = jnp.concatenate([p[0] for p in parts] + [p[1] for p in parts], axis=1).astype(BF16)
        g = jnp.dot(x, wg_b[...], preferred_element_type=F32)
        u = jnp.dot(x, wu_b[...], preferred_element_type=F32)
        a = (_silu(g) * u).astype(BF16)
        y = jnp.dot(a, wd_b[...], preferred_element_type=F32)
        for s, w in enumerate(_pack_row_words(y)):
            y4_ref[_word_plane(0, MOE_BM, s), :] = w

    @pl.when(i >= nu_ref[0])
    def _():
        y4_ref[...] = jnp.zeros(y4_ref.shape, U32)


def _experts(block_e, n_used, next_e, xs, wg, wu, wd):
    nblk = block_e.shape[0]
    bm = MOE_BM
    e, d, de = wg.shape

    grid_spec = pltpu.PrefetchScalarGridSpec(
        num_scalar_prefetch=3,
        grid=(nblk,),
        in_specs=[pl.BlockSpec(memory_space=pl.ANY),
                  pl.BlockSpec(memory_space=pl.ANY),
                  pl.BlockSpec(memory_space=pl.ANY),
                  pl.BlockSpec(memory_space=pl.ANY)],
        out_specs=pl.BlockSpec((bm * ROW_WORDS, LANES), lambda i, be, nu, nx: (i, 0)),
        scratch_shapes=[pltpu.VMEM((X_RING, bm * ROW_WORDS, LANES), U32), pltpu.SemaphoreType.DMA((X_RING,)),
                        pltpu.VMEM((d, de), F32), pltpu.VMEM((d, de), F32), pltpu.VMEM((de, d), F32),
                        pltpu.SemaphoreType.DMA((3,)),
                        pltpu.VMEM((d, de), BF16), pltpu.VMEM((d, de), BF16), pltpu.VMEM((de, d), BF16)],
    )
    return pl.pallas_call(
        _experts_body,
        out_shape=jax.ShapeDtypeStruct((nblk * bm * ROW_WORDS, LANES), U32),
        grid_spec=grid_spec,
        compiler_params=_cparams(("arbitrary",)),
        name="experts",
    )(block_e, n_used, next_e, xs, wg, wu, wd)


def _combine_body(pos_cur, pos_nxt, y4_ref, h_ref, gate_ref, wsg_ref, wsu_ref, wsd_ref, g_ref, b_ref,
                  o_ref, buf, sem):
    i = pl.program_id(0)
    nb = pl.num_programs(0)
    tm, d = h_ref.shape
    slot = i % 2

    def issue(pos_ref, sl):
        def body(t, carry):
            for k in range(TOP_K):
                p = pos_ref[0, 0, t * TOP_K + k]
                pltpu.make_async_copy(_rows(y4_ref, p, 1), _rows(buf.at[sl], k * tm + t, 1),
                                      sem.at[sl]).start(priority=k % 2)
            return carry
        lax.fori_loop(0, tm, body, 0)

    @pl.when(i == 0)
    def _():
        issue(pos_cur, 0)

    @pl.when(i + 1 < nb)
    def _():
        issue(pos_nxt, 1 - slot)

    h = h_ref[...]
    hb = h.astype(BF16)
    sg = jnp.dot(hb, wsg_ref[...], preferred_element_type=F32)
    su = jnp.dot(hb, wsu_ref[...], preferred_element_type=F32)
    shared = jnp.dot((_silu(sg) * su).astype(BF16), wsd_ref[...], preferred_element_type=F32)

    pltpu.make_async_copy(_rows(y4_ref, 0, tm * TOP_K), buf.at[slot], sem.at[slot]).wait()
    gates = gate_ref[...]
    lo_cols, hi_cols = [], []
    for s in range(ROW_WORDS):
        lo_acc = hi_acc = None
        for k in range(TOP_K):
            lo, hi = _unpack_row_words(buf[slot, _word_plane(k * tm, tm, s), :])
            gk = gates[:, k:k + 1]
            lo_acc = gk * lo if lo_acc is None else lo_acc + gk * lo
            hi_acc = gk * hi if hi_acc is None else hi_acc + gk * hi
        lo_cols.append(lo_acc)
        hi_cols.append(hi_acc)
    routed = jnp.concatenate(lo_cols + hi_cols, axis=1)
    o_ref[...] = _layernorm(DEEPNORM_ALPHA * h + (routed + shared), g_ref[...], b_ref[...])


def _combine(pos3, y4, h, gates, wsg, wsu, wsd, g, b):
    t, d = h.shape
    tm = COMB_TM
    nt = t // tm
    ds_ = wsg.shape[1]
    return pl.pallas_call(
        _combine_body,
        out_shape=jax.ShapeDtypeStruct((t, d), F32),
        grid=(nt,),
        in_specs=[pl.BlockSpec((1, 1, tm * TOP_K), lambda i: (i, 0, 0), memory_space=pltpu.SMEM),
                  pl.BlockSpec((1, 1, tm * TOP_K), lambda i: (jnp.minimum(i + 1, nt - 1), 0, 0),
                               memory_space=pltpu.SMEM),
                  pl.BlockSpec(memory_space=pl.ANY),
                  pl.BlockSpec((tm, d), lambda i: (i, 0)),
                  pl.BlockSpec((tm, TOP_K), lambda i: (i, 0)),
                  pl.BlockSpec((d, ds_), lambda i: (0, 0)),
                  pl.BlockSpec((d, ds_), lambda i: (0, 0)),
                  pl.BlockSpec((ds_, d), lambda i: (0, 0)),
                  pl.BlockSpec((1, d), lambda i: (0, 0)),
                  pl.BlockSpec((1, d), lambda i: (0, 0))],
        out_specs=pl.BlockSpec((tm, d), lambda i: (i, 0)),
        scratch_shapes=[pltpu.VMEM((2, tm * TOP_K * ROW_WORDS, LANES), U32),
                        pltpu.SemaphoreType.DMA((2,))],
        compiler_params=_cparams(("arbitrary",)),
        name="combine",
    )(pos3, pos3, y4, h, gates, wsg, wsu, wsd, g, b)


def _expert_tables(counts, nblk):
    bm = MOE_BM
    cnt = counts.reshape(N_EXPERTS).astype(I32)
    padded = (cnt + bm - 1) // bm * bm
    padded_end = jnp.cumsum(padded)
    padded_start = padded_end - padded
    block_rows = jnp.arange(nblk, dtype=I32) * bm
    block_e = jnp.sum((padded_end[None, :] <= block_rows[:, None]).astype(I32), axis=1)
    block_e = jnp.minimum(block_e, N_EXPERTS - 1)
    n_used = (padded_end[-1:] // bm).astype(I32)
    ids = jnp.where(cnt > 0, jnp.arange(N_EXPERTS, dtype=I32), N_EXPERTS)
    after = jnp.concatenate([lax.cummin(ids, reverse=True)[1:], jnp.full((1,), N_EXPERTS, I32)])
    next_e = jnp.where(after < N_EXPERTS, after, -1).astype(I32)
    return padded_start, padded_start + cnt, padded - cnt, block_e, n_used, next_e


def kernel(x, w_in, gla_gate_w2, gla_gate_b, gla_norm_w, pool_w_group, pool_scale, w_out, ln1_g, ln1_b,
           router_w, router_bias, w_exp_gate, w_exp_up, w_exp_down, w_sh_gate, w_sh_up, w_sh_down, ln2_g, ln2_b):
    batch, seq, d = x.shape
    t = batch * seq
    h2d = x.reshape(t, d)
    for l in range(DEPTH):
        d_in = w_in.shape[2]
        w_in_b = jnp.pad(w_in[l], ((0, 0), (0, D_IN_PAD - d_in))).astype(BF16)
        w2p = jnp.pad(gla_gate_w2[l], ((0, LANES - GLA_GATE_RANK), (0, 0))).astype(BF16)
        proj = _inproj(h2d, w_in_b)
        mixed = _mixer(proj, batch, seq, w2p, gla_gate_b[l].reshape(1, -1), gla_norm_w[l].reshape(1, -1),
                       pool_w_group[l].astype(BF16), pool_scale[l].reshape(1, -1))
        h, h4 = _outproj_ln(mixed, w_out[l].astype(BF16), h2d, ln1_g[l].reshape(1, -1), ln1_b[l].reshape(1, -1))
        idx_t, gate_t, rank_t, counts = _router(h, router_w[l].T, router_bias[l].reshape(-1, 1))
        nblk = (t * TOP_K + N_EXPERTS * (MOE_BM - 1)) // MOE_BM
        start, pad_row, pad_n, block_e, n_used, next_e = _expert_tables(counts, nblk)
        pos_t = _positions(idx_t, rank_t, start.astype(F32).reshape(-1, 1))
        pos_tok = pos_t.T
        xs = _dispatch(pad_row, pad_n, n_used, pos_tok.reshape(t // DISP_TM, 1, DISP_TM * TOP_K), h4, nblk * MOE_BM)
        y4 = _experts(block_e, n_used, next_e, xs, w_exp_gate[l], w_exp_up[l], w_exp_down[l])
        h2d = _combine(pos_tok.reshape(t // COMB_TM, 1, COMB_TM * TOP_K), y4, h, gate_t.T,
                       w_sh_gate[l].astype(BF16), w_sh_up[l].astype(BF16), w_sh_down[l].astype(BF16),
                       ln2_g[l].reshape(1, -1), ln2_b[l].reshape(1, -1))
    return h2d.reshape(batch, seq, d)
```

```python
import functools

import jax
import jax.numpy as jnp
from jax import lax
from jax.experimental import pallas as pl
from jax.experimental.pallas import tpu as pltpu

F32 = jnp.float32
BF16 = jnp.bfloat16
I32 = jnp.int32
U32 = jnp.uint32
HIGH_HALF = 0xFFFF0000

POOL_WINDOWS = (2, 4, 8, 16)
POOL_GROUP_DIM = 128
POOL_WIDTH = 512
GLA_HEADS = 4
GLA_DK = 64
GLA_DV = 128
GLA_DK_TOTAL = 256
GLA_WIDTH = 512
GLA_GATE_RANK = 16
GLA_GATE_NORMALIZER = 16.0
GLA_CHUNK = 16
N_EXPERTS = 256
TOP_K = 8
N_GROUPS = 8
GROUP_SIZE = N_EXPERTS // N_GROUPS
TOPK_GROUPS = 4
ROUTED_SCALE = 2.5
DEPTH = 1
DEEPNORM_ALPHA = (2.0 * DEPTH) ** 0.25
LN_EPS = 1e-5
RMS_EPS = 1e-5

LANES = 128
SUBLANES = 8
VMEM_LIMIT = 56 * 1024 * 1024

PROJ_TM = 512
MIX_TS = 256
ROUTE_TM = 256
MOE_BM = 256
COMB_TM = 256
POS_TM = 512
DISP_TM = 256
ROW_WORDS = 4
X_RING = 3
EXP_BLOCKS_PER_STEP = 2
D_IN_PAD = 2944
GATE_COL_BLOCK = 16


def _cparams(sem):
    return pltpu.CompilerParams(dimension_semantics=sem, vmem_limit_bytes=VMEM_LIMIT)


def _silu(x):
    return x * (1.0 / (1.0 + jnp.exp(-x)))


def _layernorm(y, g, b):
    mu = jnp.mean(y, axis=-1, keepdims=True)
    yc = y - mu
    var = jnp.mean(yc * yc, axis=-1, keepdims=True)
    return yc * lax.rsqrt(var + LN_EPS) * g + b


def _inproj_body(x_ref, w_ref, o_ref):
    o_ref[...] = jnp.dot(x_ref[...].astype(BF16), w_ref[...], preferred_element_type=F32)


def _inproj(x2d, w_bf):
    t, d = x2d.shape
    n = w_bf.shape[1]
    return pl.pallas_call(
        _inproj_body,
        out_shape=jax.ShapeDtypeStruct((t, n), F32),
        grid=(t // PROJ_TM,),
        in_specs=[pl.BlockSpec((PROJ_TM, d), lambda i: (i, 0)),
                  pl.BlockSpec((d, n), lambda i: (0, 0))],
        out_specs=pl.BlockSpec((PROJ_TM, n), lambda i: (i, 0)),
        compiler_params=_cparams(("arbitrary",)),
        name="inproj",
    )(x2d, w_bf)


def _mixer_body(p_ref, q_ref, k_ref, v_ref, r_ref, gl_ref, w2_ref, gb_ref, nw_ref, pw_ref, ps_ref,
                o_ref, pbuf, state, kvbuf, sall):
    ts = p_ref.shape[0]
    s_idx = pl.program_id(1)
    halo = POOL_WINDOWS[-1]

    @pl.when(s_idx == 0)
    def _():
        pbuf[pl.ds(0, halo), :] = jnp.zeros((halo, POOL_WIDTH), F32)
        state[...] = jnp.zeros(state.shape, F32)

    p = p_ref[...]
    pbuf[pl.ds(halo, ts), :] = p
    pos = s_idx * ts + lax.broadcasted_iota(I32, (ts, 1), 0)
    for g, w in enumerate(POOL_WINDOWS):
        c0 = g * POOL_GROUP_DIM
        acc = pbuf[pl.ds(halo, ts), pl.ds(c0, POOL_GROUP_DIM)]
        for j in range(1, w):
            acc = acc + pbuf[pl.ds(halo - j, ts), pl.ds(c0, POOL_GROUP_DIM)]
        cnt = jnp.minimum(pos + 1, w).astype(F32)
        mixed = acc / cnt - p[:, c0:c0 + POOL_GROUP_DIM]
        og = jnp.dot(mixed.astype(BF16), pw_ref[g], preferred_element_type=F32)
        o_ref[:, pl.ds(c0, POOL_GROUP_DIM)] = (og * ps_ref[:, pl.ds(c0, POOL_GROUP_DIM)]).astype(o_ref.dtype)
    pbuf[pl.ds(0, halo), :] = pbuf[pl.ds(ts, halo), :]

    nchunk = ts // GLA_CHUNK
    glog = jnp.dot(gl_ref[...].astype(BF16), w2_ref[...], preferred_element_type=F32) + gb_ref[...]
    gk = (jnp.minimum(glog, 0.0) - jnp.log(1.0 + jnp.exp(-jnp.abs(glog)))) * (1.0 / GLA_GATE_NORMALIZER)
    row = lax.broadcasted_iota(I32, (ts, 1), 0)
    rin = row % GLA_CHUNK
    b = gk
    sh = 1
    while sh < GLA_CHUNK:
        b = b + jnp.where(rin >= sh, pltpu.roll(b, sh, axis=0), 0.0)
        sh *= 2
    b3 = b.reshape(nchunk, GLA_CHUNK, GLA_DK_TOTAL)
    bmid = b3[:, GLA_CHUNK // 2 - 1:GLA_CHUNK // 2, :]
    blast = b3[:, GLA_CHUNK - 1:GLA_CHUNK, :]
    q3 = (q_ref[...] * (GLA_DK ** -0.5)).reshape(nchunk, GLA_CHUNK, GLA_DK_TOTAL)
    k3 = k_ref[...].reshape(nchunk, GLA_CHUNK, GLA_DK_TOTAL)
    qs = (q3 * jnp.exp(b3 - bmid)).reshape(ts, GLA_DK_TOTAL)
    ks = (k3 * jnp.exp(bmid - b3)).reshape(ts, GLA_DK_TOTAL)
    qd = (q3 * jnp.exp(b3)).reshape(ts, GLA_DK_TOTAL)
    kd = (k3 * jnp.exp(blast - b3)).reshape(ts, GLA_DK_TOTAL)
    cdec = jnp.exp(blast).reshape(nchunk, GLA_DK_TOTAL)

    v = v_ref[...]
    vb = v.astype(BF16)
    lane = lax.broadcasted_iota(I32, (1, LANES), 1)
    head_lane = [lane < GLA_DK, lane >= GLA_DK]

    blk = LANES
    ri = lax.broadcasted_iota(I32, (blk, blk), 0)
    ci = lax.broadcasted_iota(I32, (blk, blk), 1)
    causal = (ri // GLA_CHUNK == ci // GLA_CHUNK) & (ri >= ci)
    o_intra = [[None] * (ts // blk) for _ in range(GLA_HEADS)]
    for rb in range(ts // blk):
        rs = slice(rb * blk, (rb + 1) * blk)
        for pair in range(GLA_HEADS // 2):
            ls = slice(pair * LANES, (pair + 1) * LANES)
            ks_p = ks[rs, ls].astype(BF16)
            for sub in range(2):
                h = pair * 2 + sub
                q_m = jnp.where(head_lane[sub], qs[rs, ls], 0.0).astype(BF16)
                sc = lax.dot_general(q_m, ks_p, (((1,), (1,)), ((), ())), preferred_element_type=F32)
                sc = jnp.where(causal, sc, 0.0).astype(BF16)
                o_intra[h][rb] = jnp.dot(sc, vb[rs, h * GLA_DV:(h + 1) * GLA_DV], preferred_element_type=F32)

    cpb = blk // GLA_CHUNK
    chunk_of_col = lax.broadcasted_iota(I32, (1, blk), 1) // GLA_CHUNK
    stack_head_row = (lax.broadcasted_iota(I32, (cpb * LANES, 1), 0) % LANES) < GLA_DK
    for rb in range(ts // blk):
        rs = slice(rb * blk, (rb + 1) * blk)
        for pair in range(GLA_HEADS // 2):
            ls = slice(pair * LANES, (pair + 1) * LANES)
            kd_t = kd[rs, ls].T
            lhs = jnp.concatenate([jnp.where(chunk_of_col == c, kd_t, 0.0) for c in range(cpb)],
                                  axis=0).astype(BF16)
            inc = [jnp.dot(lhs, vb[rs, (pair * 2 + sub) * GLA_DV:(pair * 2 + sub + 1) * GLA_DV],
                           preferred_element_type=F32) for sub in range(2)]
            stacked = jnp.where(stack_head_row, inc[0], inc[1])
            for c in range(cpb):
                kvbuf[pair, rb * cpb + c] = stacked[c * LANES:(c + 1) * LANES]
    for pair in range(GLA_HEADS // 2):
        ls = slice(pair * LANES, (pair + 1) * LANES)
        dec_cols = cdec[:, ls].T
        st = state[pair]
        for c in range(nchunk):
            sall[pair, c] = st.astype(BF16)
            st = st * dec_cols[:, c:c + 1] + kvbuf[pair, c]
        state[pair] = st
    o_inter = [[None] * nchunk for _ in range(GLA_HEADS)]
    for pair in range(GLA_HEADS // 2):
        ls = slice(pair * LANES, (pair + 1) * LANES)
        for c in range(nchunk):
            rs = slice(c * GLA_CHUNK, (c + 1) * GLA_CHUNK)
            for sub in range(2):
                q_m = jnp.where(head_lane[sub], qd[rs, ls], 0.0).astype(BF16)
                o_inter[pair * 2 + sub][c] = jnp.dot(q_m, sall[pair, c], preferred_element_type=F32)

    nw = nw_ref[...]
    r = r_ref[...]
    for h in range(GLA_HEADS):
        o = jnp.concatenate(o_intra[h], axis=0) + jnp.concatenate(o_inter[h], axis=0)
        o = o * lax.rsqrt(jnp.mean(o * o, axis=-1, keepdims=True) + RMS_EPS) * nw
        o = o * _silu(r[:, h * GLA_DV:(h + 1) * GLA_DV])
        o_ref[:, pl.ds(POOL_WIDTH + h * GLA_DV, GLA_DV)] = o.astype(o_ref.dtype)


def _mixer(proj, batch, seq, w2p, gate_b, norm_w, pool_w, pool_scale):
    t = proj.shape[0]
    ts = MIX_TS
    nseq = seq // ts

    def rows(width, col_block):
        return pl.BlockSpec((ts, width), lambda bi, si: (bi * nseq + si, col_block))

    def full(shape):
        return pl.BlockSpec(shape, lambda bi, si: (0,) * len(shape))

    return pl.pallas_call(
        _mixer_body,
        out_shape=jax.ShapeDtypeStruct((t, POOL_WIDTH + GLA_WIDTH), BF16),
        grid=(batch, nseq),
        in_specs=[rows(POOL_WIDTH, 0),
                  rows(GLA_DK_TOTAL, 2),
                  rows(GLA_DK_TOTAL, 3),
                  rows(GLA_WIDTH, 2),
                  rows(GLA_WIDTH, 3),
                  rows(LANES, GATE_COL_BLOCK),
                  full(w2p.shape), full(gate_b.shape), full(norm_w.shape),
                  full(pool_w.shape), full(pool_scale.shape)],
        out_specs=pl.BlockSpec((ts, POOL_WIDTH + GLA_WIDTH), lambda bi, si: (bi * nseq + si, 0)),
        scratch_shapes=[pltpu.VMEM((ts + POOL_WINDOWS[-1], POOL_WIDTH), F32),
                        pltpu.VMEM((GLA_HEADS // 2, LANES, GLA_DV), F32),
                        pltpu.VMEM((GLA_HEADS // 2, ts // GLA_CHUNK, LANES, GLA_DV), F32),
                        pltpu.VMEM((GLA_HEADS // 2, ts // GLA_CHUNK, LANES, GLA_DV), BF16)],
        compiler_params=_cparams(("arbitrary", "arbitrary")),
        name="mixer",
    )(proj, proj, proj, proj, proj, proj, w2p, gate_b, norm_w, pool_w, pool_scale)


def _pack_row_words(x):
    half = x.shape[1] // 2
    u = pltpu.bitcast(x.astype(BF16).astype(F32), U32)
    hi_mask = jnp.uint32(HIGH_HALF)
    return [(u[:, half + s * LANES:half + (s + 1) * LANES] & hi_mask) | (u[:, s * LANES:(s + 1) * LANES] >> 16)
            for s in range(ROW_WORDS)]


def _unpack_row_words(w):
    return pltpu.bitcast(w << 16, F32), pltpu.bitcast(w & jnp.uint32(HIGH_HALF), F32)


def _rows(ref, first, n, align=1):
    if isinstance(first, int):
        start = first * ROW_WORDS
    else:
        start = pl.multiple_of(first * ROW_WORDS, ROW_WORDS * align)
    return ref.at[pl.ds(start, n * ROW_WORDS), :]


def _word_plane(first, m, s):
    return pl.ds(first * ROW_WORDS + s, m, stride=ROW_WORDS)


def _outproj_body(m_ref, w_ref, x_ref, g_ref, b_ref, h_ref, h4_ref):
    y = DEEPNORM_ALPHA * x_ref[...] + jnp.dot(m_ref[...], w_ref[...], preferred_element_type=F32)
    h = _layernorm(y, g_ref[...], b_ref[...])
    h_ref[...] = h
    for s, w in enumerate(_pack_row_words(h)):
        h4_ref[_word_plane(0, h.shape[0], s), :] = w


def _outproj_ln(mixed, w_bf, x2d, g, b):
    t, d = x2d.shape
    tm = PROJ_TM
    return pl.pallas_call(
        _outproj_body,
        out_shape=(jax.ShapeDtypeStruct((t, d), F32), jax.ShapeDtypeStruct((t * ROW_WORDS, LANES), U32)),
        grid=(t // tm,),
        in_specs=[pl.BlockSpec((tm, d), lambda i: (i, 0)),
                  pl.BlockSpec((d, d), lambda i: (0, 0)),
                  pl.BlockSpec((tm, d), lambda i: (i, 0)),
                  pl.BlockSpec((1, d), lambda i: (0, 0)),
                  pl.BlockSpec((1, d), lambda i: (0, 0))],
        out_specs=(pl.BlockSpec((tm, d), lambda i: (i, 0)),
                   pl.BlockSpec((tm * ROW_WORDS, LANES), lambda i: (i, 0))),
        compiler_params=_cparams(("arbitrary",)),
        name="outproj_ln",
    )(mixed, w_bf, x2d, g, b)


def _first_argmax_rows(val, rowf, nrows):
    m = jnp.max(val, axis=0, keepdims=True)
    first = jnp.min(jnp.where(val == m, rowf, float(nrows)), axis=0, keepdims=True)
    return m, first, rowf == first


def _router_body(h_ref, whi_ref, wlo_ref, bias_ref, idx_ref, gate_ref, rank_ref, cnt_ref, carry):
    tm = h_ref.shape[0]
    i = pl.program_id(0)

    @pl.when(i == 0)
    def _():
        carry[...] = jnp.zeros(carry.shape, F32)

    h = h_ref[...]
    h_hi = h.astype(BF16)
    h_lo = (h - h_hi.astype(F32)).astype(BF16)
    nt = (((1,), (1,)), ((), ()))
    logits = (lax.dot_general(whi_ref[...], h_hi, nt, preferred_element_type=F32)
              + lax.dot_general(whi_ref[...], h_lo, nt, preferred_element_type=F32)
              + lax.dot_general(wlo_ref[...], h_hi, nt, preferred_element_type=F32))
    scores = 1.0 / (1.0 + jnp.exp(-logits))
    biased = scores + bias_ref[...]
    neg = -jnp.inf

    grp = biased.reshape(N_GROUPS, GROUP_SIZE, tm)
    gi = lax.broadcasted_iota(I32, (N_GROUPS, GROUP_SIZE, tm), 1).astype(F32)
    g1 = jnp.max(grp, axis=1, keepdims=True)
    f1 = jnp.min(jnp.where(grp == g1, gi, float(GROUP_SIZE)), axis=1, keepdims=True)
    g2 = jnp.max(jnp.where(gi == f1, neg, grp), axis=1, keepdims=True)
    gscore = (g1 + g2).reshape(N_GROUPS, tm)

    growf = lax.broadcasted_iota(I32, (N_GROUPS, tm), 0).astype(F32)
    gsel = jnp.zeros((N_GROUPS, tm), F32)
    gval = gscore
    for _ in range(TOPK_GROUPS):
        _, _, pick = _first_argmax_rows(gval, growf, N_GROUPS)
        gsel = jnp.where(pick, 1.0, gsel)
        gval = jnp.where(pick, neg, gval)
    emask = jnp.broadcast_to(gsel.reshape(N_GROUPS, 1, tm), (N_GROUPS, GROUP_SIZE, tm)).reshape(N_EXPERTS, tm)

    rowf = lax.broadcasted_iota(I32, (N_EXPERTS, tm), 0).astype(F32)
    val = jnp.where(emask > 0.0, biased, neg)
    onehot = jnp.zeros((N_EXPERTS, tm), F32)
    picks, idxs, ws = [], [], []
    for _ in range(TOP_K):
        _, first, pick = _first_argmax_rows(val, rowf, N_EXPERTS)
        picks.append(pick)
        idxs.append(first)
        ws.append(jnp.sum(jnp.where(pick, scores, 0.0), axis=0, keepdims=True))
        onehot = jnp.where(pick, 1.0, onehot)
        val = jnp.where(pick, neg, val)
    w = jnp.concatenate(ws, axis=0)
    gate_ref[...] = w / jnp.sum(w, axis=0, keepdims=True) * ROUTED_SCALE
    idx_ref[...] = jnp.concatenate(idxs, axis=0).astype(I32)

    ti = lax.broadcasted_iota(I32, (tm, tm), 0)
    tj = lax.broadcasted_iota(I32, (tm, tm), 1)
    upper = jnp.where(ti < tj, 1.0, 0.0).astype(BF16)
    prefix = jnp.dot(onehot.astype(BF16), upper, preferred_element_type=F32) + carry[...]
    ranks = [jnp.sum(jnp.where(pk, prefix, 0.0), axis=0, keepdims=True) for pk in picks]
    rank_ref[...] = jnp.concatenate(ranks, axis=0).astype(I32)
    carry[...] = carry[...] + jnp.sum(onehot, axis=1, keepdims=True)
    cnt_ref[...] = carry[...]


def _router(h, wt, bias_col):
    t, d = h.shape
    tm = ROUTE_TM
    wt_hi = wt.astype(BF16)
    wt_lo = (wt - wt_hi.astype(F32)).astype(BF16)
    return pl.pallas_call(
        _router_body,
        out_shape=(jax.ShapeDtypeStruct((TOP_K, t), I32), jax.ShapeDtypeStruct((TOP_K, t), F32),
                   jax.ShapeDtypeStruct((TOP_K, t), I32), jax.ShapeDtypeStruct((N_EXPERTS, 1), F32)),
        grid=(t // tm,),
        in_specs=[pl.BlockSpec((tm, d), lambda i: (i, 0)),
                  pl.BlockSpec((N_EXPERTS, d), lambda i: (0, 0)),
                  pl.BlockSpec((N_EXPERTS, d), lambda i: (0, 0)),
                  pl.BlockSpec((N_EXPERTS, 1), lambda i: (0, 0))],
        out_specs=(pl.BlockSpec((TOP_K, tm), lambda i: (0, i)),
                   pl.BlockSpec((TOP_K, tm), lambda i: (0, i)),
                   pl.BlockSpec((TOP_K, tm), lambda i: (0, i)),
                   pl.BlockSpec((N_EXPERTS, 1), lambda i: (0, 0))),
        scratch_shapes=[pltpu.VMEM((N_EXPERTS, 1), F32)],
        compiler_params=_cparams(("arbitrary",)),
        name="router",
    )(h, wt_hi, wt_lo, bias_col)


def _positions_body(idx_ref, rank_ref, start_ref, pos_ref):
    tm = idx_ref.shape[1]
    rowi = lax.broadcasted_iota(I32, (N_EXPERTS, tm), 0)
    start = start_ref[...]
    idx = idx_ref[...]
    rows = [jnp.sum(jnp.where(rowi == idx[k:k + 1, :], start, 0.0), axis=0, keepdims=True) for k in range(TOP_K)]
    pos_ref[...] = jnp.concatenate(rows, axis=0).astype(I32) + rank_ref[...]


def _positions(idx_t, rank_t, start_col):
    t = idx_t.shape[1]
    tm = POS_TM
    return pl.pallas_call(
        _positions_body,
        out_shape=jax.ShapeDtypeStruct((TOP_K, t), I32),
        grid=(t // tm,),
        in_specs=[pl.BlockSpec((TOP_K, tm), lambda i: (0, i)),
                  pl.BlockSpec((TOP_K, tm), lambda i: (0, i)),
                  pl.BlockSpec((N_EXPERTS, 1), lambda i: (0, 0))],
        out_specs=pl.BlockSpec((TOP_K, tm), lambda i: (0, i)),
        compiler_params=_cparams(("arbitrary",)),
        name="positions",
    )(idx_t, rank_t, start_col)


def _pad_fill_copy(zeros, xs_ref, sem, row, nrows):
    return pltpu.make_async_copy(_rows(zeros, 0, nrows), _rows(xs_ref, row, nrows), sem)


def _dispatch_body(pad_row_ref, pad_n_ref, nu_ref, pos_ref, h4_ref, xs_ref, zeros, sem, pad_sem):
    i = pl.program_id(0)
    tm = h4_ref.shape[0] // ROW_WORDS
    half = MOE_BM // 2
    pad_bits = [1 << j for j in range(MOE_BM.bit_length() - 1)]
    n_half_blocks = xs_ref.shape[0] // (half * ROW_WORDS)

    def pad_pass(wait):
        def go(cp):
            if wait:
                cp.wait()
            else:
                cp.start()

        def body(e, carry):
            row = pad_row_ref[e]
            n = pad_n_ref[e]
            for bit in pad_bits:
                @pl.when((n & bit) != 0)
                def _():
                    go(_pad_fill_copy(zeros, xs_ref, pad_sem, row + (n & (bit - 1)), bit))
            return carry
        lax.fori_loop(0, N_EXPERTS, body, 0)

        def tail(hb, carry):
            go(_pad_fill_copy(zeros, xs_ref, pad_sem, hb * half, half))
            return carry
        lax.fori_loop(nu_ref[0] * 2, n_half_blocks, tail, 0)

    @pl.when(i == 0)
    def _():
        zeros[...] = jnp.zeros(zeros.shape, U32)
        pad_pass(False)

    def body(t, carry):
        for k in range(TOP_K):
            p = pos_ref[0, 0, t * TOP_K + k]
            pltpu.make_async_copy(_rows(h4_ref, t, 1), _rows(xs_ref, p, 1), sem).start(priority=k % 2)
        return carry
    lax.fori_loop(0, tm, body, 0)

    for k in range(TOP_K):
        pltpu.make_async_copy(h4_ref, _rows(xs_ref, 0, tm), sem).wait()

    @pl.when(i == pl.num_programs(0) - 1)
    def _():
        pad_pass(True)


def _dispatch(pad_row, pad_n, n_used, pos3, h4, n_rows):
    nt, _, per_step = pos3.shape
    tm = per_step // TOP_K
    grid_spec = pltpu.PrefetchScalarGridSpec(
        num_scalar_prefetch=3,
        grid=(nt,),
        in_specs=[pl.BlockSpec((1, 1, per_step), lambda i, a, b, c: (i, 0, 0), memory_space=pltpu.SMEM),
                  pl.BlockSpec((tm * ROW_WORDS, LANES), lambda i, a, b, c: (i, 0))],
        out_specs=pl.BlockSpec(memory_space=pl.ANY),
        scratch_shapes=[pltpu.VMEM((MOE_BM // 2 * ROW_WORDS, LANES), U32),
                        pltpu.SemaphoreType.DMA, pltpu.SemaphoreType.DMA],
    )
    return pl.pallas_call(
        _dispatch_body,
        out_shape=jax.ShapeDtypeStruct((n_rows * ROW_WORDS, LANES), U32),
        grid_spec=grid_spec,
        compiler_params=_cparams(("arbitrary",)),
        name="dispatch",
    )(pad_row, pad_n, n_used, pos3, h4)


def _weight_copies(e, w_hbm, stage, sem):
    return [pltpu.make_async_copy(w.at[e], st, sem.at[j]) for j, (w, st) in enumerate(zip(w_hbm, stage))]


def _row_block_copy(xs_hbm, xbuf, xsem, j):
    slot = j % X_RING
    return pltpu.make_async_copy(_rows(xs_hbm, j * MOE_BM, MOE_BM, align=MOE_BM), xbuf.at[slot], xsem.at[slot])


def _experts_body(be_ref, nu_ref, nx_ref, xs_hbm, wg_hbm, wu_hbm, wd_hbm, y4_ref,
                  xbuf, xsem, sg, su, sd, wsem, wg_b, wu_b, wd_b):
    w_hbm = (wg_hbm, wu_hbm, wd_hbm)
    stage = (sg, su, sd)
    n_used = nu_ref[0]

    def do_block(i, out_row):
        @pl.when(i < n_used)
        def _():
            e = be_ref[i]
            prev_e = be_ref[jnp.maximum(i - 1, 0)]

            @pl.when(i == 0)
            def _():
                for j in range(X_RING - 1):
                    @pl.when(j < n_used)
                    def _():
                        _row_block_copy(xs_hbm, xbuf, xsem, j).start()

            @pl.when(i + X_RING - 1 < n_used)
            def _():
                _row_block_copy(xs_hbm, xbuf, xsem, i + X_RING - 1).start()

            @pl.when((i == 0) | (e != prev_e))
            def _():
                @pl.when(i == 0)
                def _():
                    for cp in _weight_copies(e, w_hbm, stage, wsem):
                        cp.start()
                for cp in _weight_copies(e, w_hbm, stage, wsem):
                    cp.wait()
                wg_b[...] = sg[...].astype(BF16)
                wu_b[...] = su[...].astype(BF16)
                wd_b[...] = sd[...].astype(BF16)
                nxt = nx_ref[e]

                @pl.when(nxt >= 0)
                def _():
                    for cp in _weight_copies(nxt, w_hbm, stage, wsem):
                        cp.start()

            _row_block_copy(xs_hbm, xbuf, xsem, i).wait()
            slot = i % X_RING
            parts = [_unpack_row_words(xbuf[slot, _word_plane(0, MOE_BM, s), :]) for s in range(ROW_WORDS)]
            x = jnp.concatenate([p[0] for p in parts] + [p[1] for p in parts], axis=1).astype(BF16)
            g = jnp.dot(x, wg_b[...], preferred_element_type=F32)
            u = jnp.dot(x, wu_b[...], preferred_element_type=F32)
            a = (_silu(g) * u).astype(BF16)
            y = jnp.dot(a, wd_b[...], preferred_element_type=F32)
            for s, w in enumerate(_pack_row_words(y)):
                y4_ref[_word_plane(out_row, MOE_BM, s), :] = w

        @pl.when(i >= n_used)
        def _():
            y4_ref[pl.ds(out_row * ROW_WORDS, MOE_BM * ROW_WORDS), :] = jnp.zeros((MOE_BM * ROW_WORDS, LANES), U32)

    for sub in range(EXP_BLOCKS_PER_STEP):
        do_block(pl.program_id(0) * EXP_BLOCKS_PER_STEP + sub, sub * MOE_BM)


def _experts(block_e, n_used, next_e, xs, wg, wu, wd):
    nblk = block_e.shape[0]
    bm = MOE_BM * EXP_BLOCKS_PER_STEP
    e, d, de = wg.shape

    grid_spec = pltpu.PrefetchScalarGridSpec(
        num_scalar_prefetch=3,
        grid=(nblk // EXP_BLOCKS_PER_STEP,),
        in_specs=[pl.BlockSpec(memory_space=pl.ANY),
                  pl.BlockSpec(memory_space=pl.ANY),
                  pl.BlockSpec(memory_space=pl.ANY),
                  pl.BlockSpec(memory_space=pl.ANY)],
        out_specs=pl.BlockSpec((bm * ROW_WORDS, LANES), lambda i, be, nu, nx: (i, 0)),
        scratch_shapes=[pltpu.VMEM((X_RING, MOE_BM * ROW_WORDS, LANES), U32), pltpu.SemaphoreType.DMA((X_RING,)),
                        pltpu.VMEM((d, de), F32), pltpu.VMEM((d, de), F32), pltpu.VMEM((de, d), F32),
                        pltpu.SemaphoreType.DMA((3,)),
                        pltpu.VMEM((d, de), BF16), pltpu.VMEM((d, de), BF16), pltpu.VMEM((de, d), BF16)],
    )
    return pl.pallas_call(
        _experts_body,
        out_shape=jax.ShapeDtypeStruct((nblk * MOE_BM * ROW_WORDS, LANES), U32),
        grid_spec=grid_spec,
        compiler_params=_cparams(("arbitrary",)),
        name="experts",
    )(block_e, n_used, next_e, xs, wg, wu, wd)


def _combine_body(pos_cur, pos_nxt, y4_ref, h_ref, gate_ref, wsg_ref, wsu_ref, wsd_ref, g_ref, b_ref,
                  o_ref, buf, sem):
    i = pl.program_id(0)
    nb = pl.num_programs(0)
    tm, d = h_ref.shape
    slot = i % 2

    def issue(pos_ref, sl):
        def body(t, carry):
            for k in range(TOP_K):
                p = pos_ref[0, 0, t * TOP_K + k]
                pltpu.make_async_copy(_rows(y4_ref, p, 1), _rows(buf.at[sl], k * tm + t, 1),
                                      sem.at[sl]).start(priority=k % 2)
            return carry
        lax.fori_loop(0, tm, body, 0)

    @pl.when(i == 0)
    def _():
        issue(pos_cur, 0)

    @pl.when(i + 1 < nb)
    def _():
        issue(pos_nxt, 1 - slot)

    h = h_ref[...]
    hb = h.astype(BF16)
    sg = jnp.dot(hb, wsg_ref[...], preferred_element_type=F32)
    su = jnp.dot(hb, wsu_ref[...], preferred_element_type=F32)
    shared = jnp.dot((_silu(sg) * su).astype(BF16), wsd_ref[...], preferred_element_type=F32)

    pltpu.make_async_copy(_rows(y4_ref, 0, tm * TOP_K), buf.at[slot], sem.at[slot]).wait()
    gates = gate_ref[...]
    lo_cols, hi_cols = [], []
    for s in range(ROW_WORDS):
        lo_acc = hi_acc = None
        for k in range(TOP_K):
            lo, hi = _unpack_row_words(buf[slot, _word_plane(k * tm, tm, s), :])
            gk = gates[:, k:k + 1]
            lo_acc = gk * lo if lo_acc is None else lo_acc + gk * lo
            hi_acc = gk * hi if hi_acc is None else hi_acc + gk * hi
        lo_cols.append(lo_acc)
        hi_cols.append(hi_acc)
    routed = jnp.concatenate(lo_cols + hi_cols, axis=1)
    o_ref[...] = _layernorm(DEEPNORM_ALPHA * h + (routed + shared), g_ref[...], b_ref[...])


def _combine(pos3, y4, h, gates, wsg, wsu, wsd, g, b):
    t, d = h.shape
    tm = COMB_TM
    nt = t // tm
    ds_ = wsg.shape[1]
    return pl.pallas_call(
        _combine_body,
        out_shape=jax.ShapeDtypeStruct((t, d), F32),
        grid=(nt,),
        in_specs=[pl.BlockSpec((1, 1, tm * TOP_K), lambda i: (i, 0, 0), memory_space=pltpu.SMEM),
                  pl.BlockSpec((1, 1, tm * TOP_K), lambda i: (jnp.minimum(i + 1, nt - 1), 0, 0),
                               memory_space=pltpu.SMEM),
                  pl.BlockSpec(memory_space=pl.ANY),
                  pl.BlockSpec((tm, d), lambda i: (i, 0)),
                  pl.BlockSpec((tm, TOP_K), lambda i: (i, 0)),
                  pl.BlockSpec((d, ds_), lambda i: (0, 0)),
                  pl.BlockSpec((d, ds_), lambda i: (0, 0)),
                  pl.BlockSpec((ds_, d), lambda i: (0, 0)),
                  pl.BlockSpec((1, d), lambda i: (0, 0)),
                  pl.BlockSpec((1, d), lambda i: (0, 0))],
        out_specs=pl.BlockSpec((tm, d), lambda i: (i, 0)),
        scratch_shapes=[pltpu.VMEM((2, tm * TOP_K * ROW_WORDS, LANES), U32),
                        pltpu.SemaphoreType.DMA((2,))],
        compiler_params=_cparams(("arbitrary",)),
        name="combine",
    )(pos3, pos3, y4, h, gates, wsg, wsu, wsd, g, b)


def _expert_tables(counts, nblk):
    bm = MOE_BM
    cnt = counts.reshape(N_EXPERTS).astype(I32)
    padded = (cnt + bm - 1) // bm * bm
    padded_end = jnp.cumsum(padded)
    padded_start = padded_end - padded
    block_rows = jnp.arange(nblk, dtype=I32) * bm
    block_e = jnp.sum((padded_end[None, :] <= block_rows[:, None]).astype(I32), axis=1)
    block_e = jnp.minimum(block_e, N_EXPERTS - 1)
    n_used = (padded_end[-1:] // bm).astype(I32)
    ids = jnp.where(cnt > 0, jnp.arange(N_EXPERTS, dtype=I32), N_EXPERTS)
    after = jnp.concatenate([lax.cummin(ids, reverse=True)[1:], jnp.full((1,), N_EXPERTS, I32)])
    next_e = jnp.where(after < N_EXPERTS, after, -1).astype(I32)
    return padded_start, padded_start + cnt, padded - cnt, block_e, n_used, next_e


def kernel(x, w_in, gla_gate_w2, gla_gate_b, gla_norm_w, pool_w_group, pool_scale, w_out, ln1_g, ln1_b,
           router_w, router_bias, w_exp_gate, w_exp_up, w_exp_down, w_sh_gate, w_sh_up, w_sh_down, ln2_g, ln2_b):
    batch, seq, d = x.shape
    t = batch * seq
    h2d = x.reshape(t, d)
    for l in range(DEPTH):
        d_in = w_in.shape[2]
        w_in_b = jnp.pad(w_in[l], ((0, 0), (0, D_IN_PAD - d_in))).astype(BF16)
        w2p = jnp.pad(gla_gate_w2[l], ((0, LANES - GLA_GATE_RANK), (0, 0))).astype(BF16)
        proj = _inproj(h2d, w_in_b)
        mixed = _mixer(proj, batch, seq, w2p, gla_gate_b[l].reshape(1, -1), gla_norm_w[l].reshape(1, -1),
                       pool_w_group[l].astype(BF16), pool_scale[l].reshape(1, -1))
        h, h4 = _outproj_ln(mixed, w_out[l].astype(BF16), h2d, ln1_g[l].reshape(1, -1), ln1_b[l].reshape(1, -1))
        idx_t, gate_t, rank_t, counts = _router(h, router_w[l].T, router_bias[l].reshape(-1, 1))
        nblk = (t * TOP_K + N_EXPERTS * (MOE_BM - 1)) // MOE_BM
        nblk = -(-nblk // EXP_BLOCKS_PER_STEP) * EXP_BLOCKS_PER_STEP
        start, pad_row, pad_n, block_e, n_used, next_e = _expert_tables(counts, nblk)
        pos_t = _positions(idx_t, rank_t, start.astype(F32).reshape(-1, 1))
        pos_tok = pos_t.T
        xs = _dispatch(pad_row, pad_n, n_used, pos_tok.reshape(t // DISP_TM, 1, DISP_TM * TOP_K), h4, nblk * MOE_BM)
        y4 = _experts(block_e, n_used, next_e, xs, w_exp_gate[l], w_exp_up[l], w_exp_down[l])
        h2d = _combine(pos_tok.reshape(t // COMB_TM, 1, COMB_TM * TOP_K), y4, h, gate_t.T,
                       w_sh_gate[l].astype(BF16), w_sh_up[l].astype(BF16), w_sh_down[l].astype(BF16),
                       ln2_g[l].reshape(1, -1), ln2_b[l].reshape(1, -1))
    return h2d.reshape(batch, seq, d)
```

```python
import functools

import jax
import jax.numpy as jnp
from jax import lax
from jax.experimental import pallas as pl
from jax.experimental.pallas import tpu as pltpu

F32 = jnp.float32
BF16 = jnp.bfloat16
I32 = jnp.int32
U32 = jnp.uint32
HIGH_HALF = 0xFFFF0000

POOL_WINDOWS = (2, 4, 8, 16)
POOL_GROUP_DIM = 128
POOL_WIDTH = 512
GLA_HEADS = 4
GLA_DK = 64
GLA_DV = 128
GLA_DK_TOTAL = 256
GLA_WIDTH = 512
GLA_GATE_RANK = 16
GLA_GATE_NORMALIZER = 16.0
GLA_CHUNK = 16
N_EXPERTS = 256
TOP_K = 8
N_GROUPS = 8
GROUP_SIZE = N_EXPERTS // N_GROUPS
TOPK_GROUPS = 4
ROUTED_SCALE = 2.5
DEPTH = 1
DEEPNORM_ALPHA = (2.0 * DEPTH) ** 0.25
LN_EPS = 1e-5
RMS_EPS = 1e-5

LANES = 128
SUBLANES = 8
VMEM_LIMIT = 56 * 1024 * 1024

PROJ_TM = 512
MIX_TS = 256
ROUTE_TM = 512
MOE_BM = 256
COMB_TM = 256
POS_TM = 512
DISP_TM = 256
DISP_UNROLL = 4
ROW_WORDS = 4
X_RING = 3
EXP_BLOCKS_PER_STEP = 2
D_IN_PAD = 2944
GATE_COL_BLOCK = 16


def _cparams(sem):
    return pltpu.CompilerParams(dimension_semantics=sem, vmem_limit_bytes=VMEM_LIMIT)


def _silu(x):
    return x * (1.0 / (1.0 + jnp.exp(-x)))


def _layernorm(y, g, b):
    mu = jnp.mean(y, axis=-1, keepdims=True)
    yc = y - mu
    var = jnp.mean(yc * yc, axis=-1, keepdims=True)
    return yc * lax.rsqrt(var + LN_EPS) * g + b


def _inproj_body(x_ref, w_ref, o_ref):
    o_ref[...] = jnp.dot(x_ref[...].astype(BF16), w_ref[...], preferred_element_type=F32)


def _inproj(x2d, w_bf):
    t, d = x2d.shape
    n = w_bf.shape[1]
    return pl.pallas_call(
        _inproj_body,
        out_shape=jax.ShapeDtypeStruct((t, n), F32),
        grid=(t // PROJ_TM,),
        in_specs=[pl.BlockSpec((PROJ_TM, d), lambda i: (i, 0)),
                  pl.BlockSpec((d, n), lambda i: (0, 0))],
        out_specs=pl.BlockSpec((PROJ_TM, n), lambda i: (i, 0)),
        compiler_params=_cparams(("arbitrary",)),
        name="inproj",
    )(x2d, w_bf)


def _mixer_body(p_ref, q_ref, k_ref, v_ref, r_ref, gl_ref, w2_ref, gb_ref, nw_ref, pw_ref, ps_ref,
                o_ref, pbuf, state, kvbuf, sall):
    ts = p_ref.shape[0]
    s_idx = pl.program_id(1)
    halo = POOL_WINDOWS[-1]

    @pl.when(s_idx == 0)
    def _():
        pbuf[pl.ds(0, halo), :] = jnp.zeros((halo, POOL_WIDTH), F32)
        state[...] = jnp.zeros(state.shape, F32)

    p = p_ref[...]
    pbuf[pl.ds(halo, ts), :] = p
    pos = s_idx * ts + lax.broadcasted_iota(I32, (ts, 1), 0)
    for g, w in enumerate(POOL_WINDOWS):
        c0 = g * POOL_GROUP_DIM
        acc = pbuf[pl.ds(halo, ts), pl.ds(c0, POOL_GROUP_DIM)]
        for j in range(1, w):
            acc = acc + pbuf[pl.ds(halo - j, ts), pl.ds(c0, POOL_GROUP_DIM)]
        cnt = jnp.minimum(pos + 1, w).astype(F32)
        mixed = acc / cnt - p[:, c0:c0 + POOL_GROUP_DIM]
        og = jnp.dot(mixed.astype(BF16), pw_ref[g], preferred_element_type=F32)
        o_ref[:, pl.ds(c0, POOL_GROUP_DIM)] = (og * ps_ref[:, pl.ds(c0, POOL_GROUP_DIM)]).astype(o_ref.dtype)
    pbuf[pl.ds(0, halo), :] = pbuf[pl.ds(ts, halo), :]

    nchunk = ts // GLA_CHUNK
    glog = jnp.dot(gl_ref[...].astype(BF16), w2_ref[...], preferred_element_type=F32) + gb_ref[...]
    gk = (jnp.minimum(glog, 0.0) - jnp.log(1.0 + jnp.exp(-jnp.abs(glog)))) * (1.0 / GLA_GATE_NORMALIZER)
    row = lax.broadcasted_iota(I32, (ts, 1), 0)
    rin = row % GLA_CHUNK
    b = gk
    sh = 1
    while sh < GLA_CHUNK:
        b = b + jnp.where(rin >= sh, pltpu.roll(b, sh, axis=0), 0.0)
        sh *= 2
    b3 = b.reshape(nchunk, GLA_CHUNK, GLA_DK_TOTAL)
    bmid = b3[:, GLA_CHUNK // 2 - 1:GLA_CHUNK // 2, :]
    blast = b3[:, GLA_CHUNK - 1:GLA_CHUNK, :]
    q3 = (q_ref[...] * (GLA_DK ** -0.5)).reshape(nchunk, GLA_CHUNK, GLA_DK_TOTAL)
    k3 = k_ref[...].reshape(nchunk, GLA_CHUNK, GLA_DK_TOTAL)
    qs = (q3 * jnp.exp(b3 - bmid)).reshape(ts, GLA_DK_TOTAL)
    ks = (k3 * jnp.exp(bmid - b3)).reshape(ts, GLA_DK_TOTAL)
    qd = (q3 * jnp.exp(b3)).reshape(ts, GLA_DK_TOTAL)
    kd = (k3 * jnp.exp(blast - b3)).reshape(ts, GLA_DK_TOTAL)
    cdec = jnp.exp(blast).reshape(nchunk, GLA_DK_TOTAL)

    v = v_ref[...]
    vb = v.astype(BF16)
    lane = lax.broadcasted_iota(I32, (1, LANES), 1)
    head_lane = [lane < GLA_DK, lane >= GLA_DK]

    blk = LANES
    ri = lax.broadcasted_iota(I32, (blk, blk), 0)
    ci = lax.broadcasted_iota(I32, (blk, blk), 1)
    causal = (ri // GLA_CHUNK == ci // GLA_CHUNK) & (ri >= ci)
    o_intra = [[None] * (ts // blk) for _ in range(GLA_HEADS)]
    for rb in range(ts // blk):
        rs = slice(rb * blk, (rb + 1) * blk)
        for pair in range(GLA_HEADS // 2):
            ls = slice(pair * LANES, (pair + 1) * LANES)
            ks_p = ks[rs, ls].astype(BF16)
            for sub in range(2):
                h = pair * 2 + sub
                q_m = jnp.where(head_lane[sub], qs[rs, ls], 0.0).astype(BF16)
                sc = lax.dot_general(q_m, ks_p, (((1,), (1,)), ((), ())), preferred_element_type=F32)
                sc = jnp.where(causal, sc, 0.0).astype(BF16)
                o_intra[h][rb] = jnp.dot(sc, vb[rs, h * GLA_DV:(h + 1) * GLA_DV], preferred_element_type=F32)

    cpb = blk // GLA_CHUNK
    chunk_of_col = lax.broadcasted_iota(I32, (1, blk), 1) // GLA_CHUNK
    stack_head_row = (lax.broadcasted_iota(I32, (cpb * LANES, 1), 0) % LANES) < GLA_DK
    for rb in range(ts // blk):
        rs = slice(rb * blk, (rb + 1) * blk)
        for pair in range(GLA_HEADS // 2):
            ls = slice(pair * LANES, (pair + 1) * LANES)
            kd_t = kd[rs, ls].T
            lhs = jnp.concatenate([jnp.where(chunk_of_col == c, kd_t, 0.0) for c in range(cpb)],
                                  axis=0).astype(BF16)
            inc = [jnp.dot(lhs, vb[rs, (pair * 2 + sub) * GLA_DV:(pair * 2 + sub + 1) * GLA_DV],
                           preferred_element_type=F32) for sub in range(2)]
            stacked = jnp.where(stack_head_row, inc[0], inc[1])
            for c in range(cpb):
                kvbuf[pair, rb * cpb + c] = stacked[c * LANES:(c + 1) * LANES]
    for pair in range(GLA_HEADS // 2):
        ls = slice(pair * LANES, (pair + 1) * LANES)
        dec_cols = cdec[:, ls].T
        st = state[pair]
        for c in range(nchunk):
            sall[pair, c] = st.astype(BF16)
            st = st * dec_cols[:, c:c + 1] + kvbuf[pair, c]
        state[pair] = st
    o_inter = [[None] * nchunk for _ in range(GLA_HEADS)]
    for pair in range(GLA_HEADS // 2):
        ls = slice(pair * LANES, (pair + 1) * LANES)
        for c in range(nchunk):
            rs = slice(c * GLA_CHUNK, (c + 1) * GLA_CHUNK)
            for sub in range(2):
                q_m = jnp.where(head_lane[sub], qd[rs, ls], 0.0).astype(BF16)
                o_inter[pair * 2 + sub][c] = jnp.dot(q_m, sall[pair, c], preferred_element_type=F32)

    nw = nw_ref[...]
    r = r_ref[...]
    for h in range(GLA_HEADS):
        o = jnp.concatenate(o_intra[h], axis=0) + jnp.concatenate(o_inter[h], axis=0)
        o = o * lax.rsqrt(jnp.mean(o * o, axis=-1, keepdims=True) + RMS_EPS) * nw
        o = o * _silu(r[:, h * GLA_DV:(h + 1) * GLA_DV])
        o_ref[:, pl.ds(POOL_WIDTH + h * GLA_DV, GLA_DV)] = o.astype(o_ref.dtype)


def _mixer(proj, batch, seq, w2p, gate_b, norm_w, pool_w, pool_scale):
    t = proj.shape[0]
    ts = MIX_TS
    nseq = seq // ts

    def rows(width, col_block):
        return pl.BlockSpec((ts, width), lambda bi, si: (bi * nseq + si, col_block))

    def full(shape):
        return pl.BlockSpec(shape, lambda bi, si: (0,) * len(shape))

    return pl.pallas_call(
        _mixer_body,
        out_shape=jax.ShapeDtypeStruct((t, POOL_WIDTH + GLA_WIDTH), BF16),
        grid=(batch, nseq),
        in_specs=[rows(POOL_WIDTH, 0),
                  rows(GLA_DK_TOTAL, 2),
                  rows(GLA_DK_TOTAL, 3),
                  rows(GLA_WIDTH, 2),
                  rows(GLA_WIDTH, 3),
                  rows(LANES, GATE_COL_BLOCK),
                  full(w2p.shape), full(gate_b.shape), full(norm_w.shape),
                  full(pool_w.shape), full(pool_scale.shape)],
        out_specs=pl.BlockSpec((ts, POOL_WIDTH + GLA_WIDTH), lambda bi, si: (bi * nseq + si, 0)),
        scratch_shapes=[pltpu.VMEM((ts + POOL_WINDOWS[-1], POOL_WIDTH), F32),
                        pltpu.VMEM((GLA_HEADS // 2, LANES, GLA_DV), F32),
                        pltpu.VMEM((GLA_HEADS // 2, ts // GLA_CHUNK, LANES, GLA_DV), F32),
                        pltpu.VMEM((GLA_HEADS // 2, ts // GLA_CHUNK, LANES, GLA_DV), BF16)],
        compiler_params=_cparams(("arbitrary", "arbitrary")),
        name="mixer",
    )(proj, proj, proj, proj, proj, proj, w2p, gate_b, norm_w, pool_w, pool_scale)


def _pack_row_words(x):
    half = x.shape[1] // 2
    u = pltpu.bitcast(x.astype(BF16).astype(F32), U32)
    hi_mask = jnp.uint32(HIGH_HALF)
    return [(u[:, half + s * LANES:half + (s + 1) * LANES] & hi_mask) | (u[:, s * LANES:(s + 1) * LANES] >> 16)
            for s in range(ROW_WORDS)]


def _unpack_row_words(w):
    return pltpu.bitcast(w << 16, F32), pltpu.bitcast(w & jnp.uint32(HIGH_HALF), F32)


def _rows(ref, first, n, align=1):
    if isinstance(first, int):
        start = first * ROW_WORDS
    else:
        start = pl.multiple_of(first * ROW_WORDS, ROW_WORDS * align)
    return ref.at[pl.ds(start, n * ROW_WORDS), :]


def _word_plane(first, m, s):
    return pl.ds(first * ROW_WORDS + s, m, stride=ROW_WORDS)


def _outproj_body(m_ref, w_ref, x_ref, g_ref, b_ref, h_ref, h4_ref):
    y = DEEPNORM_ALPHA * x_ref[...] + jnp.dot(m_ref[...], w_ref[...], preferred_element_type=F32)
    h = _layernorm(y, g_ref[...], b_ref[...])
    h_ref[...] = h
    for s, w in enumerate(_pack_row_words(h)):
        h4_ref[_word_plane(0, h.shape[0], s), :] = w


def _outproj_ln(mixed, w_bf, x2d, g, b):
    t, d = x2d.shape
    tm = PROJ_TM
    return pl.pallas_call(
        _outproj_body,
        out_shape=(jax.ShapeDtypeStruct((t, d), F32), jax.ShapeDtypeStruct((t * ROW_WORDS, LANES), U32)),
        grid=(t // tm,),
        in_specs=[pl.BlockSpec((tm, d), lambda i: (i, 0)),
                  pl.BlockSpec((d, d), lambda i: (0, 0)),
                  pl.BlockSpec((tm, d), lambda i: (i, 0)),
                  pl.BlockSpec((1, d), lambda i: (0, 0)),
                  pl.BlockSpec((1, d), lambda i: (0, 0))],
        out_specs=(pl.BlockSpec((tm, d), lambda i: (i, 0)),
                   pl.BlockSpec((tm * ROW_WORDS, LANES), lambda i: (i, 0))),
        compiler_params=_cparams(("arbitrary",)),
        name="outproj_ln",
    )(mixed, w_bf, x2d, g, b)


def _first_argmax_rows(val, rowf, nrows):
    m = jnp.max(val, axis=0, keepdims=True)
    first = jnp.min(jnp.where(val == m, rowf, float(nrows)), axis=0, keepdims=True)
    return m, first, rowf == first


def _router_body(h_ref, whi_ref, wlo_ref, bias_ref, idx_ref, gate_ref, rank_ref, cnt_ref, carry):
    tm = h_ref.shape[0]
    i = pl.program_id(0)

    @pl.when(i == 0)
    def _():
        carry[...] = jnp.zeros(carry.shape, F32)

    h = h_ref[...]
    h_hi = h.astype(BF16)
    h_lo = (h - h_hi.astype(F32)).astype(BF16)
    nt = (((1,), (1,)), ((), ()))
    logits = (lax.dot_general(whi_ref[...], h_hi, nt, preferred_element_type=F32)
              + lax.dot_general(whi_ref[...], h_lo, nt, preferred_element_type=F32)
              + lax.dot_general(wlo_ref[...], h_hi, nt, preferred_element_type=F32))
    scores = 1.0 / (1.0 + jnp.exp(-logits))
    biased = scores + bias_ref[...]
    neg = -jnp.inf

    grp = biased.reshape(N_GROUPS, GROUP_SIZE, tm)
    gi = lax.broadcasted_iota(I32, (N_GROUPS, GROUP_SIZE, tm), 1).astype(F32)
    g1 = jnp.max(grp, axis=1, keepdims=True)
    f1 = jnp.min(jnp.where(grp == g1, gi, float(GROUP_SIZE)), axis=1, keepdims=True)
    g2 = jnp.max(jnp.where(gi == f1, neg, grp), axis=1, keepdims=True)
    gscore = (g1 + g2).reshape(N_GROUPS, tm)

    growf = lax.broadcasted_iota(I32, (N_GROUPS, tm), 0).astype(F32)
    gsel = jnp.zeros((N_GROUPS, tm), F32)
    gval = gscore
    for _ in range(TOPK_GROUPS):
        _, _, pick = _first_argmax_rows(gval, growf, N_GROUPS)
        gsel = jnp.where(pick, 1.0, gsel)
        gval = jnp.where(pick, neg, gval)
    emask = jnp.broadcast_to(gsel.reshape(N_GROUPS, 1, tm), (N_GROUPS, GROUP_SIZE, tm)).reshape(N_EXPERTS, tm)

    rowf = lax.broadcasted_iota(I32, (N_EXPERTS, tm), 0).astype(F32)
    val = jnp.where(emask > 0.0, biased, neg)
    onehot = jnp.zeros((N_EXPERTS, tm), F32)
    picks, idxs, ws = [], [], []
    for _ in range(TOP_K):
        _, first, pick = _first_argmax_rows(val, rowf, N_EXPERTS)
        picks.append(pick)
        idxs.append(first)
        ws.append(jnp.sum(jnp.where(pick, scores, 0.0), axis=0, keepdims=True))
        onehot = jnp.where(pick, 1.0, onehot)
        val = jnp.where(pick, neg, val)
    w = jnp.concatenate(ws, axis=0)
    gate_ref[...] = w / jnp.sum(w, axis=0, keepdims=True) * ROUTED_SCALE
    idx_ref[...] = jnp.concatenate(idxs, axis=0).astype(I32)

    ti = lax.broadcasted_iota(I32, (tm, tm), 0)
    tj = lax.broadcasted_iota(I32, (tm, tm), 1)
    upper = jnp.where(ti < tj, 1.0, 0.0).astype(BF16)
    prefix = jnp.dot(onehot.astype(BF16), upper, preferred_element_type=F32) + carry[...]
    ranks = [jnp.sum(jnp.where(pk, prefix, 0.0), axis=0, keepdims=True) for pk in picks]
    rank_ref[...] = jnp.concatenate(ranks, axis=0).astype(I32)
    carry[...] = carry[...] + jnp.sum(onehot, axis=1, keepdims=True)
    cnt_ref[...] = carry[...]


def _router(h, wt, bias_col):
    t, d = h.shape
    tm = ROUTE_TM
    wt_hi = wt.astype(BF16)
    wt_lo = (wt - wt_hi.astype(F32)).astype(BF16)
    return pl.pallas_call(
        _router_body,
        out_shape=(jax.ShapeDtypeStruct((TOP_K, t), I32), jax.ShapeDtypeStruct((TOP_K, t), F32),
                   jax.ShapeDtypeStruct((TOP_K, t), I32), jax.ShapeDtypeStruct((N_EXPERTS, 1), F32)),
        grid=(t // tm,),
        in_specs=[pl.BlockSpec((tm, d), lambda i: (i, 0)),
                  pl.BlockSpec((N_EXPERTS, d), lambda i: (0, 0)),
                  pl.BlockSpec((N_EXPERTS, d), lambda i: (0, 0)),
                  pl.BlockSpec((N_EXPERTS, 1), lambda i: (0, 0))],
        out_specs=(pl.BlockSpec((TOP_K, tm), lambda i: (0, i)),
                   pl.BlockSpec((TOP_K, tm), lambda i: (0, i)),
                   pl.BlockSpec((TOP_K, tm), lambda i: (0, i)),
                   pl.BlockSpec((N_EXPERTS, 1), lambda i: (0, 0))),
        scratch_shapes=[pltpu.VMEM((N_EXPERTS, 1), F32)],
        compiler_params=_cparams(("arbitrary",)),
        name="router",
    )(h, wt_hi, wt_lo, bias_col)


def _positions_body(idx_ref, rank_ref, start_ref, pos_ref):
    tm = idx_ref.shape[1]
    rowi = lax.broadcasted_iota(I32, (N_EXPERTS, tm), 0)
    start = start_ref[...]
    idx = idx_ref[...]
    rows = [jnp.sum(jnp.where(rowi == idx[k:k + 1, :], start, 0.0), axis=0, keepdims=True) for k in range(TOP_K)]
    pos_ref[...] = jnp.concatenate(rows, axis=0).astype(I32) + rank_ref[...]


def _positions(idx_t, rank_t, start_col):
    t = idx_t.shape[1]
    tm = POS_TM
    return pl.pallas_call(
        _positions_body,
        out_shape=jax.ShapeDtypeStruct((TOP_K, t), I32),
        grid=(t // tm,),
        in_specs=[pl.BlockSpec((TOP_K, tm), lambda i: (0, i)),
                  pl.BlockSpec((TOP_K, tm), lambda i: (0, i)),
                  pl.BlockSpec((N_EXPERTS, 1), lambda i: (0, 0))],
        out_specs=pl.BlockSpec((TOP_K, tm), lambda i: (0, i)),
        compiler_params=_cparams(("arbitrary",)),
        name="positions",
    )(idx_t, rank_t, start_col)


def _pad_fill_copy(zeros, xs_ref, sem, row, nrows):
    return pltpu.make_async_copy(_rows(zeros, 0, nrows), _rows(xs_ref, row, nrows), sem)


def _dispatch_body(pad_row_ref, pad_n_ref, nu_ref, pos_ref, h4_ref, xs_ref, zeros, sem, pad_sem):
    i = pl.program_id(0)
    tm = h4_ref.shape[0] // ROW_WORDS
    half = MOE_BM // 2
    pad_bits = [1 << j for j in range(MOE_BM.bit_length() - 1)]
    n_half_blocks = xs_ref.shape[0] // (half * ROW_WORDS)

    def pad_pass(wait):
        def go(cp):
            if wait:
                cp.wait()
            else:
                cp.start()

        def body(e, carry):
            row = pad_row_ref[e]
            n = pad_n_ref[e]
            for bit in pad_bits:
                @pl.when((n & bit) != 0)
                def _():
                    go(_pad_fill_copy(zeros, xs_ref, pad_sem, row + (n & (bit - 1)), bit))
            return carry
        lax.fori_loop(0, N_EXPERTS, body, 0)

        def tail(hb, carry):
            go(_pad_fill_copy(zeros, xs_ref, pad_sem, hb * half, half))
            return carry
        lax.fori_loop(nu_ref[0] * 2, n_half_blocks, tail, 0)

    @pl.when(i == 0)
    def _():
        zeros[...] = jnp.zeros(zeros.shape, U32)
        pad_pass(False)

    def body(tt, carry):
        for dt in range(DISP_UNROLL):
            t = tt * DISP_UNROLL + dt
            for k in range(TOP_K):
                p = pos_ref[0, 0, t * TOP_K + k]
                pltpu.make_async_copy(_rows(h4_ref, t, 1), _rows(xs_ref, p, 1), sem).start(priority=k % 2)
        return carry
    lax.fori_loop(0, tm // DISP_UNROLL, body, 0)

    for k in range(TOP_K):
        pltpu.make_async_copy(h4_ref, _rows(xs_ref, 0, tm), sem).wait()

    @pl.when(i == pl.num_programs(0) - 1)
    def _():
        pad_pass(True)


def _dispatch(pad_row, pad_n, n_used, pos3, h4, n_rows):
    nt, _, per_step = pos3.shape
    tm = per_step // TOP_K
    grid_spec = pltpu.PrefetchScalarGridSpec(
        num_scalar_prefetch=3,
        grid=(nt,),
        in_specs=[pl.BlockSpec((1, 1, per_step), lambda i, a, b, c: (i, 0, 0), memory_space=pltpu.SMEM),
                  pl.BlockSpec((tm * ROW_WORDS, LANES), lambda i, a, b, c: (i, 0))],
        out_specs=pl.BlockSpec(memory_space=pl.ANY),
        scratch_shapes=[pltpu.VMEM((MOE_BM // 2 * ROW_WORDS, LANES), U32),
                        pltpu.SemaphoreType.DMA, pltpu.SemaphoreType.DMA],
    )
    return pl.pallas_call(
        _dispatch_body,
        out_shape=jax.ShapeDtypeStruct((n_rows * ROW_WORDS, LANES), U32),
        grid_spec=grid_spec,
        compiler_params=_cparams(("arbitrary",)),
        name="dispatch",
    )(pad_row, pad_n, n_used, pos3, h4)


def _weight_copies(e, w_hbm, stage, sem, slot):
    return [pltpu.make_async_copy(w.at[e], st.at[slot], sem.at[slot, j])
            for j, (w, st) in enumerate(zip(w_hbm, stage))]


def _row_block_copy(xs_hbm, xbuf, xsem, j):
    slot = j % X_RING
    return pltpu.make_async_copy(_rows(xs_hbm, j * MOE_BM, MOE_BM, align=MOE_BM), xbuf.at[slot], xsem.at[slot])


def _experts_body(be_ref, nu_ref, nx_ref, ord_ref, xs_hbm, wg_hbm, wu_hbm, wd_hbm, y4_ref,
                  xbuf, xsem, sg, su, sd, wsem, wg_b, wu_b, wd_b):
    w_hbm = (wg_hbm, wu_hbm, wd_hbm)
    stage = (sg, su, sd)
    n_used = nu_ref[0]

    def do_block(i, out_row):
        @pl.when(i < n_used)
        def _():
            e = be_ref[i]
            prev_e = be_ref[jnp.maximum(i - 1, 0)]

            @pl.when(i == 0)
            def _():
                for j in range(X_RING - 1):
                    @pl.when(j < n_used)
                    def _():
                        _row_block_copy(xs_hbm, xbuf, xsem, j).start()

            @pl.when(i + X_RING - 1 < n_used)
            def _():
                _row_block_copy(xs_hbm, xbuf, xsem, i + X_RING - 1).start()

            @pl.when((i == 0) | (e != prev_e))
            def _():
                wslot = ord_ref[e] % 2
                n1 = nx_ref[e]
                n2 = jnp.where(n1 >= 0, nx_ref[jnp.maximum(n1, 0)], -1)

                @pl.when(i == 0)
                def _():
                    for cp in _weight_copies(e, w_hbm, stage, wsem, wslot):
                        cp.start()

                    @pl.when(n1 >= 0)
                    def _():
                        for cp in _weight_copies(n1, w_hbm, stage, wsem, 1 - wslot):
                            cp.start()
                for cp in _weight_copies(e, w_hbm, stage, wsem, wslot):
                    cp.wait()
                wg_b[...] = sg[wslot].astype(BF16)
                wu_b[...] = su[wslot].astype(BF16)
                wd_b[...] = sd[wslot].astype(BF16)

                @pl.when(n2 >= 0)
                def _():
                    for cp in _weight_copies(n2, w_hbm, stage, wsem, wslot):
                        cp.start()

            _row_block_copy(xs_hbm, xbuf, xsem, i).wait()
            slot = i % X_RING
            parts = [_unpack_row_words(xbuf[slot, _word_plane(0, MOE_BM, s), :]) for s in range(ROW_WORDS)]
            x = jnp.concatenate([p[0] for p in parts] + [p[1] for p in parts], axis=1).astype(BF16)
            g = jnp.dot(x, wg_b[...], preferred_element_type=F32)
            u = jnp.dot(x, wu_b[...], preferred_element_type=F32)
            a = (_silu(g) * u).astype(BF16)
            y = jnp.dot(a, wd_b[...], preferred_element_type=F32)
            for s, w in enumerate(_pack_row_words(y)):
                y4_ref[_word_plane(out_row, MOE_BM, s), :] = w

        @pl.when(i >= n_used)
        def _():
            y4_ref[pl.ds(out_row * ROW_WORDS, MOE_BM * ROW_WORDS), :] = jnp.zeros((MOE_BM * ROW_WORDS, LANES), U32)

    for sub in range(EXP_BLOCKS_PER_STEP):
        do_block(pl.program_id(0) * EXP_BLOCKS_PER_STEP + sub, sub * MOE_BM)


def _experts(block_e, n_used, next_e, ord_e, xs, wg, wu, wd):
    nblk = block_e.shape[0]
    bm = MOE_BM * EXP_BLOCKS_PER_STEP
    e, d, de = wg.shape

    grid_spec = pltpu.PrefetchScalarGridSpec(
        num_scalar_prefetch=4,
        grid=(nblk // EXP_BLOCKS_PER_STEP,),
        in_specs=[pl.BlockSpec(memory_space=pl.ANY),
                  pl.BlockSpec(memory_space=pl.ANY),
                  pl.BlockSpec(memory_space=pl.ANY),
                  pl.BlockSpec(memory_space=pl.ANY)],
        out_specs=pl.BlockSpec((bm * ROW_WORDS, LANES), lambda i, be, nu, nx, od: (i, 0)),
        scratch_shapes=[pltpu.VMEM((X_RING, MOE_BM * ROW_WORDS, LANES), U32), pltpu.SemaphoreType.DMA((X_RING,)),
                        pltpu.VMEM((2, d, de), F32), pltpu.VMEM((2, d, de), F32), pltpu.VMEM((2, de, d), F32),
                        pltpu.SemaphoreType.DMA((2, 3)),
                        pltpu.VMEM((d, de), BF16), pltpu.VMEM((d, de), BF16), pltpu.VMEM((de, d), BF16)],
    )
    return pl.pallas_call(
        _experts_body,
        out_shape=jax.ShapeDtypeStruct((nblk * MOE_BM * ROW_WORDS, LANES), U32),
        grid_spec=grid_spec,
        compiler_params=_cparams(("arbitrary",)),
        name="experts",
    )(block_e, n_used, next_e, ord_e, xs, wg, wu, wd)


def _combine_body(pos_cur, pos_nxt, y4_ref, h_ref, gate_ref, wsg_ref, wsu_ref, wsd_ref, g_ref, b_ref,
                  o_ref, buf, sem):
    i = pl.program_id(0)
    nb = pl.num_programs(0)
    tm, d = h_ref.shape
    slot = i % 2

    def issue(pos_ref, sl):
        def body(t, carry):
            for k in range(TOP_K):
                p = pos_ref[0, 0, t * TOP_K + k]
                pltpu.make_async_copy(_rows(y4_ref, p, 1), _rows(buf.at[sl], k * tm + t, 1),
                                      sem.at[sl]).start(priority=k % 2)
            return carry
        lax.fori_loop(0, tm, body, 0)

    @pl.when(i == 0)
    def _():
        issue(pos_cur, 0)

    @pl.when(i + 1 < nb)
    def _():
        issue(pos_nxt, 1 - slot)

    h = h_ref[...]
    hb = h.astype(BF16)
    sg = jnp.dot(hb, wsg_ref[...], preferred_element_type=F32)
    su = jnp.dot(hb, wsu_ref[...], preferred_element_type=F32)
    shared = jnp.dot((_silu(sg) * su).astype(BF16), wsd_ref[...], preferred_element_type=F32)

    pltpu.make_async_copy(_rows(y4_ref, 0, tm * TOP_K), buf.at[slot], sem.at[slot]).wait()
    gates = gate_ref[...]
    lo_cols, hi_cols = [], []
    for s in range(ROW_WORDS):
        lo_acc = hi_acc = None
        for k in range(TOP_K):
            lo, hi = _unpack_row_words(buf[slot, _word_plane(k * tm, tm, s), :])
            gk = gates[:, k:k + 1]
            lo_acc = gk * lo if lo_acc is None else lo_acc + gk * lo
            hi_acc = gk * hi if hi_acc is None else hi_acc + gk * hi
        lo_cols.append(lo_acc)
        hi_cols.append(hi_acc)
    routed = jnp.concatenate(lo_cols + hi_cols, axis=1)
    o_ref[...] = _layernorm(DEEPNORM_ALPHA * h + (routed + shared), g_ref[...], b_ref[...])


def _combine(pos3, y4, h, gates, wsg, wsu, wsd, g, b):
    t, d = h.shape
    tm = COMB_TM
    nt = t // tm
    ds_ = wsg.shape[1]
    return pl.pallas_call(
        _combine_body,
        out_shape=jax.ShapeDtypeStruct((t, d), F32),
        grid=(nt,),
        in_specs=[pl.BlockSpec((1, 1, tm * TOP_K), lambda i: (i, 0, 0), memory_space=pltpu.SMEM),
                  pl.BlockSpec((1, 1, tm * TOP_K), lambda i: (jnp.minimum(i + 1, nt - 1), 0, 0),
                               memory_space=pltpu.SMEM),
                  pl.BlockSpec(memory_space=pl.ANY),
                  pl.BlockSpec((tm, d), lambda i: (i, 0)),
                  pl.BlockSpec((tm, TOP_K), lambda i: (i, 0)),
                  pl.BlockSpec((d, ds_), lambda i: (0, 0)),
                  pl.BlockSpec((d, ds_), lambda i: (0, 0)),
                  pl.BlockSpec((ds_, d), lambda i: (0, 0)),
                  pl.BlockSpec((1, d), lambda i: (0, 0)),
                  pl.BlockSpec((1, d), lambda i: (0, 0))],
        out_specs=pl.BlockSpec((tm, d), lambda i: (i, 0)),
        scratch_shapes=[pltpu.VMEM((2, tm * TOP_K * ROW_WORDS, LANES), U32),
                        pltpu.SemaphoreType.DMA((2,))],
        compiler_params=_cparams(("arbitrary",)),
        name="combine",
    )(pos3, pos3, y4, h, gates, wsg, wsu, wsd, g, b)


def _expert_tables(counts, nblk):
    bm = MOE_BM
    cnt = counts.reshape(N_EXPERTS).astype(I32)
    padded = (cnt + bm - 1) // bm * bm
    padded_end = jnp.cumsum(padded)
    padded_start = padded_end - padded
    block_rows = jnp.arange(nblk, dtype=I32) * bm
    block_e = jnp.sum((padded_end[None, :] <= block_rows[:, None]).astype(I32), axis=1)
    block_e = jnp.minimum(block_e, N_EXPERTS - 1)
    n_used = (padded_end[-1:] // bm).astype(I32)
    ids = jnp.where(cnt > 0, jnp.arange(N_EXPERTS, dtype=I32), N_EXPERTS)
    after = jnp.concatenate([lax.cummin(ids, reverse=True)[1:], jnp.full((1,), N_EXPERTS, I32)])
    next_e = jnp.where(after < N_EXPERTS, after, -1).astype(I32)
    ord_e = (jnp.cumsum((cnt > 0).astype(I32)) - 1).astype(I32)
    return padded_start, padded_start + cnt, padded - cnt, block_e, n_used, next_e, ord_e


def kernel(x, w_in, gla_gate_w2, gla_gate_b, gla_norm_w, pool_w_group, pool_scale, w_out, ln1_g, ln1_b,
           router_w, router_bias, w_exp_gate, w_exp_up, w_exp_down, w_sh_gate, w_sh_up, w_sh_down, ln2_g, ln2_b):
    batch, seq, d = x.shape
    t = batch * seq
    h2d = x.reshape(t, d)
    for l in range(DEPTH):
        d_in = w_in.shape[2]
        w_in_b = jnp.pad(w_in[l], ((0, 0), (0, D_IN_PAD - d_in))).astype(BF16)
        w2p = jnp.pad(gla_gate_w2[l], ((0, LANES - GLA_GATE_RANK), (0, 0))).astype(BF16)
        proj = _inproj(h2d, w_in_b)
        mixed = _mixer(proj, batch, seq, w2p, gla_gate_b[l].reshape(1, -1), gla_norm_w[l].reshape(1, -1),
                       pool_w_group[l].astype(BF16), pool_scale[l].reshape(1, -1))
        h, h4 = _outproj_ln(mixed, w_out[l].astype(BF16), h2d, ln1_g[l].reshape(1, -1), ln1_b[l].reshape(1, -1))
        idx_t, gate_t, rank_t, counts = _router(h, router_w[l].T, router_bias[l].reshape(-1, 1))
        nblk = (t * TOP_K + N_EXPERTS * (MOE_BM - 1)) // MOE_BM
        nblk = -(-nblk // EXP_BLOCKS_PER_STEP) * EXP_BLOCKS_PER_STEP
        start, pad_row, pad_n, block_e, n_used, next_e, ord_e = _expert_tables(counts, nblk)
        pos_t = _positions(idx_t, rank_t, start.astype(F32).reshape(-1, 1))
        pos_tok = pos_t.T
        xs = _dispatch(pad_row, pad_n, n_used, pos_tok.reshape(t // DISP_TM, 1, DISP_TM * TOP_K), h4, nblk * MOE_BM)
        y4 = _experts(block_e, n_used, next_e, ord_e, xs, w_exp_gate[l], w_exp_up[l], w_exp_down[l])
        h2d = _combine(pos_tok.reshape(t // COMB_TM, 1, COMB_TM * TOP_K), y4, h, gate_t.T,
                       w_sh_gate[l].astype(BF16), w_sh_up[l].astype(BF16), w_sh_down[l].astype(BF16),
                       ln2_g[l].reshape(1, -1), ln2_b[l].reshape(1, -1))
    return h2d.reshape(batch, seq, d)
```

```python
import functools

import jax
import jax.numpy as jnp
from jax import lax
from jax.experimental import pallas as pl
from jax.experimental.pallas import tpu as pltpu

F32 = jnp.float32
BF16 = jnp.bfloat16
I32 = jnp.int32
U32 = jnp.uint32
HIGH_HALF = 0xFFFF0000

POOL_WINDOWS = (2, 4, 8, 16)
POOL_GROUP_DIM = 128
POOL_WIDTH = 512
GLA_HEADS = 4
GLA_DK = 64
GLA_DV = 128
GLA_DK_TOTAL = 256
GLA_WIDTH = 512
GLA_GATE_RANK = 16
GLA_GATE_NORMALIZER = 16.0
GLA_CHUNK = 16
N_EXPERTS = 256
TOP_K = 8
N_GROUPS = 8
GROUP_SIZE = N_EXPERTS // N_GROUPS
TOPK_GROUPS = 4
ROUTED_SCALE = 2.5
DEPTH = 1
DEEPNORM_ALPHA = (2.0 * DEPTH) ** 0.25
LN_EPS = 1e-5
RMS_EPS = 1e-5

LANES = 128
SUBLANES = 8
VMEM_LIMIT = 56 * 1024 * 1024

PROJ_TM = 512
MIX_TS = 512
ROUTE_TM = 512
MOE_BM = 256
COMB_TM = 256
POS_TM = 512
DISP_TM = 256
DISP_UNROLL = 4
ROW_WORDS = 4
X_RING = 3
EXP_BLOCKS_PER_STEP = 2
D_IN_PAD = 2944
GATE_COL_BLOCK = 16


def _cparams(sem):
    return pltpu.CompilerParams(dimension_semantics=sem, vmem_limit_bytes=VMEM_LIMIT)


def _silu(x):
    return x * (1.0 / (1.0 + jnp.exp(-x)))


def _layernorm(y, g, b):
    mu = jnp.mean(y, axis=-1, keepdims=True)
    yc = y - mu
    var = jnp.mean(yc * yc, axis=-1, keepdims=True)
    return yc * lax.rsqrt(var + LN_EPS) * g + b


def _inproj_body(x_ref, w_ref, o_ref):
    o_ref[...] = jnp.dot(x_ref[...].astype(BF16), w_ref[...], preferred_element_type=F32)


def _inproj(x2d, w_bf):
    t, d = x2d.shape
    n = w_bf.shape[1]
    return pl.pallas_call(
        _inproj_body,
        out_shape=jax.ShapeDtypeStruct((t, n), F32),
        grid=(t // PROJ_TM,),
        in_specs=[pl.BlockSpec((PROJ_TM, d), lambda i: (i, 0)),
                  pl.BlockSpec((d, n), lambda i: (0, 0))],
        out_specs=pl.BlockSpec((PROJ_TM, n), lambda i: (i, 0)),
        compiler_params=_cparams(("arbitrary",)),
        name="inproj",
    )(x2d, w_bf)


def _mixer_body(p_ref, q_ref, k_ref, v_ref, r_ref, gl_ref, w2_ref, gb_ref, nw_ref, pw_ref, ps_ref,
                o_ref, pbuf, state, kvbuf, sall):
    ts = p_ref.shape[0]
    s_idx = pl.program_id(1)
    halo = POOL_WINDOWS[-1]

    @pl.when(s_idx == 0)
    def _():
        pbuf[pl.ds(0, halo), :] = jnp.zeros((halo, POOL_WIDTH), F32)
        state[...] = jnp.zeros(state.shape, F32)

    p = p_ref[...]
    pbuf[pl.ds(halo, ts), :] = p
    pos = s_idx * ts + lax.broadcasted_iota(I32, (ts, 1), 0)
    for g, w in enumerate(POOL_WINDOWS):
        c0 = g * POOL_GROUP_DIM
        ext = pbuf[:, pl.ds(c0, POOL_GROUP_DIM)]
        sh = 1
        while sh < w:
            ext = ext + pltpu.roll(ext, sh, axis=0)
            sh *= 2
        acc = ext[halo:, :]
        cnt = jnp.minimum(pos + 1, w).astype(F32)
        mixed = acc / cnt - p[:, c0:c0 + POOL_GROUP_DIM]
        og = jnp.dot(mixed.astype(BF16), pw_ref[g], preferred_element_type=F32)
        o_ref[:, pl.ds(c0, POOL_GROUP_DIM)] = (og * ps_ref[:, pl.ds(c0, POOL_GROUP_DIM)]).astype(o_ref.dtype)
    pbuf[pl.ds(0, halo), :] = pbuf[pl.ds(ts, halo), :]

    nchunk = ts // GLA_CHUNK
    glog = jnp.dot(gl_ref[...].astype(BF16), w2_ref[...], preferred_element_type=F32) + gb_ref[...]
    gk = (jnp.minimum(glog, 0.0) - jnp.log(1.0 + jnp.exp(-jnp.abs(glog)))) * (1.0 / GLA_GATE_NORMALIZER)
    row = lax.broadcasted_iota(I32, (ts, 1), 0)
    rin = row % GLA_CHUNK
    b = gk
    sh = 1
    while sh < GLA_CHUNK:
        b = b + jnp.where(rin >= sh, pltpu.roll(b, sh, axis=0), 0.0)
        sh *= 2
    b3 = b.reshape(nchunk, GLA_CHUNK, GLA_DK_TOTAL)
    bmid = b3[:, GLA_CHUNK // 2 - 1:GLA_CHUNK // 2, :]
    blast = b3[:, GLA_CHUNK - 1:GLA_CHUNK, :]
    q3 = (q_ref[...] * (GLA_DK ** -0.5)).reshape(nchunk, GLA_CHUNK, GLA_DK_TOTAL)
    k3 = k_ref[...].reshape(nchunk, GLA_CHUNK, GLA_DK_TOTAL)
    qs = (q3 * jnp.exp(b3 - bmid)).reshape(ts, GLA_DK_TOTAL)
    ks = (k3 * jnp.exp(bmid - b3)).reshape(ts, GLA_DK_TOTAL)
    qd = (q3 * jnp.exp(b3)).reshape(ts, GLA_DK_TOTAL)
    kd = (k3 * jnp.exp(blast - b3)).reshape(ts, GLA_DK_TOTAL)
    cdec = jnp.exp(blast).reshape(nchunk, GLA_DK_TOTAL)

    v = v_ref[...]
    vb = v.astype(BF16)
    lane = lax.broadcasted_iota(I32, (1, LANES), 1)
    head_lane = [lane < GLA_DK, lane >= GLA_DK]

    blk = LANES
    ri = lax.broadcasted_iota(I32, (blk, blk), 0)
    ci = lax.broadcasted_iota(I32, (blk, blk), 1)
    causal = (ri // GLA_CHUNK == ci // GLA_CHUNK) & (ri >= ci)
    o_intra = [[None] * (ts // blk) for _ in range(GLA_HEADS)]
    for rb in range(ts // blk):
        rs = slice(rb * blk, (rb + 1) * blk)
        for pair in range(GLA_HEADS // 2):
            ls = slice(pair * LANES, (pair + 1) * LANES)
            ks_p = ks[rs, ls].astype(BF16)
            for sub in range(2):
                h = pair * 2 + sub
                q_m = jnp.where(head_lane[sub], qs[rs, ls], 0.0).astype(BF16)
                sc = lax.dot_general(q_m, ks_p, (((1,), (1,)), ((), ())), preferred_element_type=F32)
                sc = jnp.where(causal, sc, 0.0).astype(BF16)
                o_intra[h][rb] = jnp.dot(sc, vb[rs, h * GLA_DV:(h + 1) * GLA_DV], preferred_element_type=F32)

    cpb = blk // GLA_CHUNK
    chunk_of_col = lax.broadcasted_iota(I32, (1, blk), 1) // GLA_CHUNK
    kdb = kd.astype(BF16)
    for rb in range(ts // blk):
        rs = slice(rb * blk, (rb + 1) * blk)
        for pair in range(GLA_HEADS // 2):
            ls = slice(pair * LANES, (pair + 1) * LANES)
            inc = []
            for sub in range(2):
                h = pair * 2 + sub
                v_t = v[rs, h * GLA_DV:(h + 1) * GLA_DV].T
                lhs = jnp.concatenate([jnp.where(chunk_of_col == c, v_t, 0.0) for c in range(cpb)],
                                      axis=0).astype(BF16)
                inc.append(jnp.dot(lhs, kdb[rs, ls], preferred_element_type=F32))
            stacked = jnp.where(head_lane[0], inc[0], inc[1])
            for c in range(cpb):
                kvbuf[pair, rb * cpb + c] = stacked[c * LANES:(c + 1) * LANES]
    for pair in range(GLA_HEADS // 2):
        ls = slice(pair * LANES, (pair + 1) * LANES)
        st = state[pair]
        for c in range(nchunk):
            sall[pair, c] = st.astype(BF16)
            st = st * cdec[c:c + 1, ls] + kvbuf[pair, c]
        state[pair] = st
    o_inter = [[None] * nchunk for _ in range(GLA_HEADS)]
    for pair in range(GLA_HEADS // 2):
        ls = slice(pair * LANES, (pair + 1) * LANES)
        for c in range(nchunk):
            rs = slice(c * GLA_CHUNK, (c + 1) * GLA_CHUNK)
            q_c = qd[rs, ls]
            q2 = jnp.concatenate([jnp.where(head_lane[sub], q_c, 0.0) for sub in range(2)], axis=0).astype(BF16)
            res = lax.dot_general(q2, sall[pair, c], (((1,), (1,)), ((), ())), preferred_element_type=F32)
            for sub in range(2):
                o_inter[pair * 2 + sub][c] = res[sub * GLA_CHUNK:(sub + 1) * GLA_CHUNK]

    nw = nw_ref[...]
    r = r_ref[...]
    for h in range(GLA_HEADS):
        o = jnp.concatenate(o_intra[h], axis=0) + jnp.concatenate(o_inter[h], axis=0)
        o = o * lax.rsqrt(jnp.mean(o * o, axis=-1, keepdims=True) + RMS_EPS) * nw
        o = o * _silu(r[:, h * GLA_DV:(h + 1) * GLA_DV])
        o_ref[:, pl.ds(POOL_WIDTH + h * GLA_DV, GLA_DV)] = o.astype(o_ref.dtype)


def _mixer(proj, batch, seq, w2p, gate_b, norm_w, pool_w, pool_scale):
    t = proj.shape[0]
    ts = MIX_TS
    nseq = seq // ts

    def rows(width, col_block):
        return pl.BlockSpec((ts, width), lambda bi, si: (bi * nseq + si, col_block))

    def full(shape):
        return pl.BlockSpec(shape, lambda bi, si: (0,) * len(shape))

    return pl.pallas_call(
        _mixer_body,
        out_shape=jax.ShapeDtypeStruct((t, POOL_WIDTH + GLA_WIDTH), BF16),
        grid=(batch, nseq),
        in_specs=[rows(POOL_WIDTH, 0),
                  rows(GLA_DK_TOTAL, 2),
                  rows(GLA_DK_TOTAL, 3),
                  rows(GLA_WIDTH, 2),
                  rows(GLA_WIDTH, 3),
                  rows(LANES, GATE_COL_BLOCK),
                  full(w2p.shape), full(gate_b.shape), full(norm_w.shape),
                  full(pool_w.shape), full(pool_scale.shape)],
        out_specs=pl.BlockSpec((ts, POOL_WIDTH + GLA_WIDTH), lambda bi, si: (bi * nseq + si, 0)),
        scratch_shapes=[pltpu.VMEM((ts + POOL_WINDOWS[-1], POOL_WIDTH), F32),
                        pltpu.VMEM((GLA_HEADS // 2, LANES, GLA_DV), F32),
                        pltpu.VMEM((GLA_HEADS // 2, ts // GLA_CHUNK, LANES, GLA_DV), F32),
                        pltpu.VMEM((GLA_HEADS // 2, ts // GLA_CHUNK, LANES, GLA_DV), BF16)],
        compiler_params=_cparams(("arbitrary", "arbitrary")),
        name="mixer",
    )(proj, proj, proj, proj, proj, proj, w2p, gate_b, norm_w, pool_w, pool_scale)


def _pack_row_words(x):
    half = x.shape[1] // 2
    u = pltpu.bitcast(x.astype(BF16).astype(F32), U32)
    hi_mask = jnp.uint32(HIGH_HALF)
    return [(u[:, half + s * LANES:half + (s + 1) * LANES] & hi_mask) | (u[:, s * LANES:(s + 1) * LANES] >> 16)
            for s in range(ROW_WORDS)]


def _unpack_row_words(w):
    return pltpu.bitcast(w << 16, F32), pltpu.bitcast(w & jnp.uint32(HIGH_HALF), F32)


def _rows(ref, first, n, align=1):
    if isinstance(first, int):
        start = first * ROW_WORDS
    else:
        start = pl.multiple_of(first * ROW_WORDS, ROW_WORDS * align)
    return ref.at[pl.ds(start, n * ROW_WORDS), :]


def _word_plane(first, m, s):
    return pl.ds(first * ROW_WORDS + s, m, stride=ROW_WORDS)


def _outproj_body(m_ref, w_ref, x_ref, g_ref, b_ref, h_ref, h4_ref):
    y = DEEPNORM_ALPHA * x_ref[...] + jnp.dot(m_ref[...], w_ref[...], preferred_element_type=F32)
    h = _layernorm(y, g_ref[...], b_ref[...])
    h_ref[...] = h
    for s, w in enumerate(_pack_row_words(h)):
        h4_ref[_word_plane(0, h.shape[0], s), :] = w


def _outproj_ln(mixed, w_bf, x2d, g, b):
    t, d = x2d.shape
    tm = PROJ_TM
    return pl.pallas_call(
        _outproj_body,
        out_shape=(jax.ShapeDtypeStruct((t, d), F32), jax.ShapeDtypeStruct((t * ROW_WORDS, LANES), U32)),
        grid=(t // tm,),
        in_specs=[pl.BlockSpec((tm, d), lambda i: (i, 0)),
                  pl.BlockSpec((d, d), lambda i: (0, 0)),
                  pl.BlockSpec((tm, d), lambda i: (i, 0)),
                  pl.BlockSpec((1, d), lambda i: (0, 0)),
                  pl.BlockSpec((1, d), lambda i: (0, 0))],
        out_specs=(pl.BlockSpec((tm, d), lambda i: (i, 0)),
                   pl.BlockSpec((tm * ROW_WORDS, LANES), lambda i: (i, 0))),
        compiler_params=_cparams(("arbitrary",)),
        name="outproj_ln",
    )(mixed, w_bf, x2d, g, b)


def _first_argmax_rows(val, rowf, nrows):
    m = jnp.max(val, axis=0, keepdims=True)
    first = jnp.min(jnp.where(val == m, rowf, float(nrows)), axis=0, keepdims=True)
    return m, first, rowf == first


def _router_body(h_ref, whi_ref, wlo_ref, bias_ref, idx_ref, gate_ref, rank_ref, cnt_ref, carry):
    tm = h_ref.shape[0]
    i = pl.program_id(0)

    @pl.when(i == 0)
    def _():
        carry[...] = jnp.zeros(carry.shape, F32)

    h = h_ref[...]
    h_hi = h.astype(BF16)
    h_lo = (h - h_hi.astype(F32)).astype(BF16)
    nt = (((1,), (1,)), ((), ()))
    logits = (lax.dot_general(whi_ref[...], h_hi, nt, preferred_element_type=F32)
              + lax.dot_general(whi_ref[...], h_lo, nt, preferred_element_type=F32)
              + lax.dot_general(wlo_ref[...], h_hi, nt, preferred_element_type=F32))
    scores = 1.0 / (1.0 + jnp.exp(-logits))
    biased = scores + bias_ref[...]
    neg = -jnp.inf

    grp = biased.reshape(N_GROUPS, GROUP_SIZE, tm)
    gi = lax.broadcasted_iota(I32, (N_GROUPS, GROUP_SIZE, tm), 1).astype(F32)
    g1 = jnp.max(grp, axis=1, keepdims=True)
    f1 = jnp.min(jnp.where(grp == g1, gi, float(GROUP_SIZE)), axis=1, keepdims=True)
    g2 = jnp.max(jnp.where(gi == f1, neg, grp), axis=1, keepdims=True)
    gscore = (g1 + g2).reshape(N_GROUPS, tm)

    growf = lax.broadcasted_iota(I32, (N_GROUPS, tm), 0).astype(F32)
    gsel = jnp.zeros((N_GROUPS, tm), F32)
    gval = gscore
    for _ in range(TOPK_GROUPS):
        _, _, pick = _first_argmax_rows(gval, growf, N_GROUPS)
        gsel = jnp.where(pick, 1.0, gsel)
        gval = jnp.where(pick, neg, gval)
    emask = jnp.broadcast_to(gsel.reshape(N_GROUPS, 1, tm), (N_GROUPS, GROUP_SIZE, tm)).reshape(N_EXPERTS, tm)

    rowf = lax.broadcasted_iota(I32, (N_EXPERTS, tm), 0).astype(F32)
    val = jnp.where(emask > 0.0, biased, neg)
    onehot = jnp.zeros((N_EXPERTS, tm), F32)
    picks, idxs, ws = [], [], []
    for _ in range(TOP_K):
        _, first, pick = _first_argmax_rows(val, rowf, N_EXPERTS)
        picks.append(pick)
        idxs.append(first)
        ws.append(jnp.sum(jnp.where(pick, scores, 0.0), axis=0, keepdims=True))
        onehot = jnp.where(pick, 1.0, onehot)
        val = jnp.where(pick, neg, val)
    w = jnp.concatenate(ws, axis=0)
    gate_ref[...] = w / jnp.sum(w, axis=0, keepdims=True) * ROUTED_SCALE
    idx_ref[...] = jnp.concatenate(idxs, axis=0).astype(I32)

    ti = lax.broadcasted_iota(I32, (tm, tm), 0)
    tj = lax.broadcasted_iota(I32, (tm, tm), 1)
    upper = jnp.where(ti < tj, 1.0, 0.0).astype(BF16)
    prefix = jnp.dot(onehot.astype(BF16), upper, preferred_element_type=F32) + carry[...]
    ranks = [jnp.sum(jnp.where(pk, prefix, 0.0), axis=0, keepdims=True) for pk in picks]
    rank_ref[...] = jnp.concatenate(ranks, axis=0).astype(I32)
    carry[...] = carry[...] + jnp.sum(onehot, axis=1, keepdims=True)
    cnt_ref[...] = carry[...]


def _router(h, wt, bias_col):
    t, d = h.shape
    tm = ROUTE_TM
    wt_hi = wt.astype(BF16)
    wt_lo = (wt - wt_hi.astype(F32)).astype(BF16)
    return pl.pallas_call(
        _router_body,
        out_shape=(jax.ShapeDtypeStruct((TOP_K, t), I32), jax.ShapeDtypeStruct((TOP_K, t), F32),
                   jax.ShapeDtypeStruct((TOP_K, t), I32), jax.ShapeDtypeStruct((N_EXPERTS, 1), F32)),
        grid=(t // tm,),
        in_specs=[pl.BlockSpec((tm, d), lambda i: (i, 0)),
                  pl.BlockSpec((N_EXPERTS, d), lambda i: (0, 0)),
                  pl.BlockSpec((N_EXPERTS, d), lambda i: (0, 0)),
                  pl.BlockSpec((N_EXPERTS, 1), lambda i: (0, 0))],
        out_specs=(pl.BlockSpec((TOP_K, tm), lambda i: (0, i)),
                   pl.BlockSpec((TOP_K, tm), lambda i: (0, i)),
                   pl.BlockSpec((TOP_K, tm), lambda i: (0, i)),
                   pl.BlockSpec((N_EXPERTS, 1), lambda i: (0, 0))),
        scratch_shapes=[pltpu.VMEM((N_EXPERTS, 1), F32)],
        compiler_params=_cparams(("arbitrary",)),
        name="router",
    )(h, wt_hi, wt_lo, bias_col)


def _positions_body(idx_ref, rank_ref, start_ref, pos_ref):
    tm = idx_ref.shape[1]
    rowi = lax.broadcasted_iota(I32, (N_EXPERTS, tm), 0)
    start = start_ref[...]
    idx = idx_ref[...]
    rows = [jnp.sum(jnp.where(rowi == idx[k:k + 1, :], start, 0.0), axis=0, keepdims=True) for k in range(TOP_K)]
    pos_ref[...] = jnp.concatenate(rows, axis=0).astype(I32) + rank_ref[...]


def _positions(idx_t, rank_t, start_col):
    t = idx_t.shape[1]
    tm = POS_TM
    return pl.pallas_call(
        _positions_body,
        out_shape=jax.ShapeDtypeStruct((TOP_K, t), I32),
        grid=(t // tm,),
        in_specs=[pl.BlockSpec((TOP_K, tm), lambda i: (0, i)),
                  pl.BlockSpec((TOP_K, tm), lambda i: (0, i)),
                  pl.BlockSpec((N_EXPERTS, 1), lambda i: (0, 0))],
        out_specs=pl.BlockSpec((TOP_K, tm), lambda i: (0, i)),
        compiler_params=_cparams(("arbitrary",)),
        name="positions",
    )(idx_t, rank_t, start_col)


def _pad_fill_copy(zeros, xs_ref, sem, row, nrows):
    return pltpu.make_async_copy(_rows(zeros, 0, nrows), _rows(xs_ref, row, nrows), sem)


def _dispatch_body(pad_row_ref, pad_n_ref, nu_ref, pos_ref, h4_ref, xs_ref, zeros, sem, pad_sem):
    i = pl.program_id(0)
    tm = h4_ref.shape[0] // ROW_WORDS
    half = MOE_BM // 2
    pad_bits = [1 << j for j in range(MOE_BM.bit_length() - 1)]
    n_half_blocks = xs_ref.shape[0] // (half * ROW_WORDS)

    def pad_pass(wait):
        def go(cp):
            if wait:
                cp.wait()
            else:
                cp.start()

        def body(e, carry):
            row = pad_row_ref[e]
            n = pad_n_ref[e]
            for bit in pad_bits:
                @pl.when((n & bit) != 0)
                def _():
                    go(_pad_fill_copy(zeros, xs_ref, pad_sem, row + (n & (bit - 1)), bit))
            return carry
        lax.fori_loop(0, N_EXPERTS, body, 0)

        def tail(hb, carry):
            go(_pad_fill_copy(zeros, xs_ref, pad_sem, hb * half, half))
            return carry
        lax.fori_loop(nu_ref[0] * 2, n_half_blocks, tail, 0)

    @pl.when(i == 0)
    def _():
        zeros[...] = jnp.zeros(zeros.shape, U32)
        pad_pass(False)

    def body(tt, carry):
        for dt in range(DISP_UNROLL):
            t = tt * DISP_UNROLL + dt
            for k in range(TOP_K):
                p = pos_ref[0, 0, t * TOP_K + k]
                pltpu.make_async_copy(_rows(h4_ref, t, 1), _rows(xs_ref, p, 1), sem).start(priority=k % 2)
        return carry
    lax.fori_loop(0, tm // DISP_UNROLL, body, 0)

    for k in range(TOP_K):
        pltpu.make_async_copy(h4_ref, _rows(xs_ref, 0, tm), sem).wait()

    @pl.when(i == pl.num_programs(0) - 1)
    def _():
        pad_pass(True)


def _dispatch(pad_row, pad_n, n_used, pos3, h4, n_rows):
    nt, _, per_step = pos3.shape
    tm = per_step // TOP_K
    grid_spec = pltpu.PrefetchScalarGridSpec(
        num_scalar_prefetch=3,
        grid=(nt,),
        in_specs=[pl.BlockSpec((1, 1, per_step), lambda i, a, b, c: (i, 0, 0), memory_space=pltpu.SMEM),
                  pl.BlockSpec((tm * ROW_WORDS, LANES), lambda i, a, b, c: (i, 0))],
        out_specs=pl.BlockSpec(memory_space=pl.ANY),
        scratch_shapes=[pltpu.VMEM((MOE_BM // 2 * ROW_WORDS, LANES), U32),
                        pltpu.SemaphoreType.DMA, pltpu.SemaphoreType.DMA],
    )
    return pl.pallas_call(
        _dispatch_body,
        out_shape=jax.ShapeDtypeStruct((n_rows * ROW_WORDS, LANES), U32),
        grid_spec=grid_spec,
        compiler_params=_cparams(("arbitrary",)),
        name="dispatch",
    )(pad_row, pad_n, n_used, pos3, h4)


def _weight_copies(e, w_hbm, stage, sem, slot):
    return [pltpu.make_async_copy(w.at[e], st.at[slot], sem.at[slot, j])
            for j, (w, st) in enumerate(zip(w_hbm, stage))]


def _row_block_copy(xs_hbm, xbuf, xsem, j):
    slot = j % X_RING
    return pltpu.make_async_copy(_rows(xs_hbm, j * MOE_BM, MOE_BM, align=MOE_BM), xbuf.at[slot], xsem.at[slot])


def _experts_body(be_ref, nu_ref, nx_ref, ord_ref, xs_hbm, wg_hbm, wu_hbm, wd_hbm, y4_ref,
                  xbuf, xsem, sg, su, sd, wsem, wg_b, wu_b, wd_b):
    w_hbm = (wg_hbm, wu_hbm, wd_hbm)
    stage = (sg, su, sd)
    n_used = nu_ref[0]

    def do_block(i, out_row):
        @pl.when(i < n_used)
        def _():
            e = be_ref[i]
            prev_e = be_ref[jnp.maximum(i - 1, 0)]

            @pl.when(i == 0)
            def _():
                for j in range(X_RING - 1):
                    @pl.when(j < n_used)
                    def _():
                        _row_block_copy(xs_hbm, xbuf, xsem, j).start()

            @pl.when(i + X_RING - 1 < n_used)
            def _():
                _row_block_copy(xs_hbm, xbuf, xsem, i + X_RING - 1).start()

            @pl.when((i == 0) | (e != prev_e))
            def _():
                wslot = ord_ref[e] % 2
                n1 = nx_ref[e]
                n2 = jnp.where(n1 >= 0, nx_ref[jnp.maximum(n1, 0)], -1)

                @pl.when(i == 0)
                def _():
                    for cp in _weight_copies(e, w_hbm, stage, wsem, wslot):
                        cp.start()

                    @pl.when(n1 >= 0)
                    def _():
                        for cp in _weight_copies(n1, w_hbm, stage, wsem, 1 - wslot):
                            cp.start()
                for cp in _weight_copies(e, w_hbm, stage, wsem, wslot):
                    cp.wait()
                wg_b[...] = sg[wslot].astype(BF16)
                wu_b[...] = su[wslot].astype(BF16)
                wd_b[...] = sd[wslot].astype(BF16)

                @pl.when(n2 >= 0)
                def _():
                    for cp in _weight_copies(n2, w_hbm, stage, wsem, wslot):
                        cp.start()

            _row_block_copy(xs_hbm, xbuf, xsem, i).wait()
            slot = i % X_RING
            parts = [_unpack_row_words(xbuf[slot, _word_plane(0, MOE_BM, s), :]) for s in range(ROW_WORDS)]
            x = jnp.concatenate([p[0] for p in parts] + [p[1] for p in parts], axis=1).astype(BF16)
            g = jnp.dot(x, wg_b[...], preferred_element_type=F32)
            u = jnp.dot(x, wu_b[...], preferred_element_type=F32)
            a = (_silu(g) * u).astype(BF16)
            y = jnp.dot(a, wd_b[...], preferred_element_type=F32)
            for s, w in enumerate(_pack_row_words(y)):
                y4_ref[_word_plane(out_row, MOE_BM, s), :] = w

        @pl.when(i >= n_used)
        def _():
            y4_ref[pl.ds(out_row * ROW_WORDS, MOE_BM * ROW_WORDS), :] = jnp.zeros((MOE_BM * ROW_WORDS, LANES), U32)

    for sub in range(EXP_BLOCKS_PER_STEP):
        do_block(pl.program_id(0) * EXP_BLOCKS_PER_STEP + sub, sub * MOE_BM)


def _experts(block_e, n_used, next_e, ord_e, xs, wg, wu, wd):
    nblk = block_e.shape[0]
    bm = MOE_BM * EXP_BLOCKS_PER_STEP
    e, d, de = wg.shape

    grid_spec = pltpu.PrefetchScalarGridSpec(
        num_scalar_prefetch=4,
        grid=(nblk // EXP_BLOCKS_PER_STEP,),
        in_specs=[pl.BlockSpec(memory_space=pl.ANY),
                  pl.BlockSpec(memory_space=pl.ANY),
                  pl.BlockSpec(memory_space=pl.ANY),
                  pl.BlockSpec(memory_space=pl.ANY)],
        out_specs=pl.BlockSpec((bm * ROW_WORDS, LANES), lambda i, be, nu, nx, od: (i, 0)),
        scratch_shapes=[pltpu.VMEM((X_RING, MOE_BM * ROW_WORDS, LANES), U32), pltpu.SemaphoreType.DMA((X_RING,)),
                        pltpu.VMEM((2, d, de), F32), pltpu.VMEM((2, d, de), F32), pltpu.VMEM((2, de, d), F32),
                        pltpu.SemaphoreType.DMA((2, 3)),
                        pltpu.VMEM((d, de), BF16), pltpu.VMEM((d, de), BF16), pltpu.VMEM((de, d), BF16)],
    )
    return pl.pallas_call(
        _experts_body,
        out_shape=jax.ShapeDtypeStruct((nblk * MOE_BM * ROW_WORDS, LANES), U32),
        grid_spec=grid_spec,
        compiler_params=_cparams(("arbitrary",)),
        name="experts",
    )(block_e, n_used, next_e, ord_e, xs, wg, wu, wd)


def _combine_body(pos_cur, pos_nxt, y4_ref, h_ref, gate_ref, wsg_ref, wsu_ref, wsd_ref, g_ref, b_ref,
                  o_ref, buf, sem):
    i = pl.program_id(0)
    nb = pl.num_programs(0)
    tm, d = h_ref.shape
    slot = i % 2

    def issue(pos_ref, sl):
        def body(t, carry):
            for k in range(TOP_K):
                p = pos_ref[0, 0, t * TOP_K + k]
                pltpu.make_async_copy(_rows(y4_ref, p, 1), _rows(buf.at[sl], k * tm + t, 1),
                                      sem.at[sl]).start(priority=k % 2)
            return carry
        lax.fori_loop(0, tm, body, 0)

    @pl.when(i == 0)
    def _():
        issue(pos_cur, 0)

    @pl.when(i + 1 < nb)
    def _():
        issue(pos_nxt, 1 - slot)

    h = h_ref[...]
    hb = h.astype(BF16)
    sg = jnp.dot(hb, wsg_ref[...], preferred_element_type=F32)
    su = jnp.dot(hb, wsu_ref[...], preferred_element_type=F32)
    shared = jnp.dot((_silu(sg) * su).astype(BF16), wsd_ref[...], preferred_element_type=F32)

    pltpu.make_async_copy(_rows(y4_ref, 0, tm * TOP_K), buf.at[slot], sem.at[slot]).wait()
    gates = gate_ref[...]
    lo_cols, hi_cols = [], []
    for s in range(ROW_WORDS):
        lo_acc = hi_acc = None
        for k in range(TOP_K):
            lo, hi = _unpack_row_words(buf[slot, _word_plane(k * tm, tm, s), :])
            gk = gates[:, k:k + 1]
            lo_acc = gk * lo if lo_acc is None else lo_acc + gk * lo
            hi_acc = gk * hi if hi_acc is None else hi_acc + gk * hi
        lo_cols.append(lo_acc)
        hi_cols.append(hi_acc)
    routed = jnp.concatenate(lo_cols + hi_cols, axis=1)
    o_ref[...] = _layernorm(DEEPNORM_ALPHA * h + (routed + shared), g_ref[...], b_ref[...])


def _combine(pos3, y4, h, gates, wsg, wsu, wsd, g, b):
    t, d = h.shape
    tm = COMB_TM
    nt = t // tm
    ds_ = wsg.shape[1]
    return pl.pallas_call(
        _combine_body,
        out_shape=jax.ShapeDtypeStruct((t, d), F32),
        grid=(nt,),
        in_specs=[pl.BlockSpec((1, 1, tm * TOP_K), lambda i: (i, 0, 0), memory_space=pltpu.SMEM),
                  pl.BlockSpec((1, 1, tm * TOP_K), lambda i: (jnp.minimum(i + 1, nt - 1), 0, 0),
                               memory_space=pltpu.SMEM),
                  pl.BlockSpec(memory_space=pl.ANY),
                  pl.BlockSpec((tm, d), lambda i: (i, 0)),
                  pl.BlockSpec((tm, TOP_K), lambda i: (i, 0)),
                  pl.BlockSpec((d, ds_), lambda i: (0, 0)),
                  pl.BlockSpec((d, ds_), lambda i: (0, 0)),
                  pl.BlockSpec((ds_, d), lambda i: (0, 0)),
                  pl.BlockSpec((1, d), lambda i: (0, 0)),
                  pl.BlockSpec((1, d), lambda i: (0, 0))],
        out_specs=pl.BlockSpec((tm, d), lambda i: (i, 0)),
        scratch_shapes=[pltpu.VMEM((2, tm * TOP_K * ROW_WORDS, LANES), U32),
                        pltpu.SemaphoreType.DMA((2,))],
        compiler_params=_cparams(("arbitrary",)),
        name="combine",
    )(pos3, pos3, y4, h, gates, wsg, wsu, wsd, g, b)


def _expert_tables(counts, nblk):
    bm = MOE_BM
    cnt = counts.reshape(N_EXPERTS).astype(I32)
    padded = (cnt + bm - 1) // bm * bm
    padded_end = jnp.cumsum(padded)
    padded_start = padded_end - padded
    block_rows = jnp.arange(nblk, dtype=I32) * bm
    block_e = jnp.sum((padded_end[None, :] <= block_rows[:, None]).astype(I32), axis=1)
    block_e = jnp.minimum(block_e, N_EXPERTS - 1)
    n_used = (padded_end[-1:] // bm).astype(I32)
    ids = jnp.where(cnt > 0, jnp.arange(N_EXPERTS, dtype=I32), N_EXPERTS)
    after = jnp.concatenate([lax.cummin(ids, reverse=True)[1:], jnp.full((1,), N_EXPERTS, I32)])
    next_e = jnp.where(after < N_EXPERTS, after, -1).astype(I32)
    ord_e = (jnp.cumsum((cnt > 0).astype(I32)) - 1).astype(I32)
    return padded_start, padded_start + cnt, padded - cnt, block_e, n_used, next_e, ord_e


def kernel(x, w_in, gla_gate_w2, gla_gate_b, gla_norm_w, pool_w_group, pool_scale, w_out, ln1_g, ln1_b,
           router_w, router_bias, w_exp_gate, w_exp_up, w_exp_down, w_sh_gate, w_sh_up, w_sh_down, ln2_g, ln2_b):
    batch, seq, d = x.shape
    t = batch * seq
    h2d = x.reshape(t, d)
    for l in range(DEPTH):
        d_in = w_in.shape[2]
        w_in_b = jnp.pad(w_in[l], ((0, 0), (0, D_IN_PAD - d_in))).astype(BF16)
        w2p = jnp.pad(gla_gate_w2[l], ((0, LANES - GLA_GATE_RANK), (0, 0))).astype(BF16)
        proj = _inproj(h2d, w_in_b)
        mixed = _mixer(proj, batch, seq, w2p, gla_gate_b[l].reshape(1, -1), gla_norm_w[l].reshape(1, -1),
                       pool_w_group[l].astype(BF16), pool_scale[l].reshape(1, -1))
        h, h4 = _outproj_ln(mixed, w_out[l].astype(BF16), h2d, ln1_g[l].reshape(1, -1), ln1_b[l].reshape(1, -1))
        idx_t, gate_t, rank_t, counts = _router(h, router_w[l].T, router_bias[l].reshape(-1, 1))
        nblk = (t * TOP_K + N_EXPERTS * (MOE_BM - 1)) // MOE_BM
        nblk = -(-nblk // EXP_BLOCKS_PER_STEP) * EXP_BLOCKS_PER_STEP
        start, pad_row, pad_n, block_e, n_used, next_e, ord_e = _expert_tables(counts, nblk)
        pos_t = _positions(idx_t, rank_t, start.astype(F32).reshape(-1, 1))
        pos_tok = pos_t.T
        xs = _dispatch(pad_row, pad_n, n_used, pos_tok.reshape(t // DISP_TM, 1, DISP_TM * TOP_K), h4, nblk * MOE_BM)
        y4 = _experts(block_e, n_used, next_e, ord_e, xs, w_exp_gate[l], w_exp_up[l], w_exp_down[l])
        h2d = _combine(pos_tok.reshape(t // COMB_TM, 1, COMB_TM * TOP_K), y4, h, gate_t.T,
                       w_sh_gate[l].astype(BF16), w_sh_up[l].astype(BF16), w_sh_down[l].astype(BF16),
                       ln2_g[l].reshape(1, -1), ln2_b[l].reshape(1, -1))
    return h2d.reshape(batch, seq, d)
```

```python
import functools

import jax
import jax.numpy as jnp
from jax import lax
from jax.experimental import pallas as pl
from jax.experimental.pallas import tpu as pltpu

F32 = jnp.float32
BF16 = jnp.bfloat16
I32 = jnp.int32
U32 = jnp.uint32
HIGH_HALF = 0xFFFF0000

POOL_WINDOWS = (2, 4, 8, 16)
POOL_GROUP_DIM = 128
POOL_WIDTH = 512
GLA_HEADS = 4
GLA_DK = 64
GLA_DV = 128
GLA_DK_TOTAL = 256
GLA_WIDTH = 512
GLA_GATE_RANK = 16
GLA_GATE_NORMALIZER = 16.0
GLA_CHUNK = 16
GLA_SAFE_EXP = 60.0
N_EXPERTS = 256
TOP_K = 8
N_GROUPS = 8
GROUP_SIZE = N_EXPERTS // N_GROUPS
TOPK_GROUPS = 4
ROUTED_SCALE = 2.5
DEPTH = 1
DEEPNORM_ALPHA = (2.0 * DEPTH) ** 0.25
LN_EPS = 1e-5
RMS_EPS = 1e-5

LANES = 128
SUBLANES = 8
VMEM_LIMIT = 56 * 1024 * 1024

PROJ_TM = 512
MIX_TS = 512
ROUTE_TM = 512
MOE_BM = 256
COMB_TM = 256
POS_TM = 512
DISP_TM = 256
DISP_UNROLL = 4
ROW_WORDS = 4
X_RING = 3
EXP_BLOCKS_PER_STEP = 2
D_IN_PAD = 2944
GATE_COL_BLOCK = 16


def _cparams(sem):
    return pltpu.CompilerParams(dimension_semantics=sem, vmem_limit_bytes=VMEM_LIMIT)


def _silu(x):
    return x * (1.0 / (1.0 + jnp.exp(-x)))


def _layernorm(y, g, b):
    mu = jnp.mean(y, axis=-1, keepdims=True)
    yc = y - mu
    var = jnp.mean(yc * yc, axis=-1, keepdims=True)
    return yc * lax.rsqrt(var + LN_EPS) * g + b


def _inproj_body(x_ref, w_ref, o_ref):
    o_ref[...] = jnp.dot(x_ref[...].astype(BF16), w_ref[...], preferred_element_type=F32)


def _inproj(x2d, w_bf):
    t, d = x2d.shape
    n = w_bf.shape[1]
    return pl.pallas_call(
        _inproj_body,
        out_shape=jax.ShapeDtypeStruct((t, n), F32),
        grid=(t // PROJ_TM,),
        in_specs=[pl.BlockSpec((PROJ_TM, d), lambda i: (i, 0)),
                  pl.BlockSpec((d, n), lambda i: (0, 0))],
        out_specs=pl.BlockSpec((PROJ_TM, n), lambda i: (i, 0)),
        compiler_params=_cparams(("arbitrary",)),
        name="inproj",
    )(x2d, w_bf)


def _mixer_body(p_ref, q_ref, k_ref, v_ref, r_ref, gl_ref, w2_ref, gb_ref, nw_ref, pw_ref, ps_ref,
                o_ref, pbuf, state, kvbuf, sall, obuf, gk_s):
    ts = p_ref.shape[0]
    s_idx = pl.program_id(1)
    halo = POOL_WINDOWS[-1]

    @pl.when(s_idx == 0)
    def _():
        pbuf[pl.ds(0, halo), :] = jnp.zeros((halo, POOL_WIDTH), F32)
        state[...] = jnp.zeros(state.shape, F32)

    p = p_ref[...]
    pbuf[pl.ds(halo, ts), :] = p
    pos = s_idx * ts + lax.broadcasted_iota(I32, (ts, 1), 0)
    for g, w in enumerate(POOL_WINDOWS):
        c0 = g * POOL_GROUP_DIM
        ext = pbuf[:, pl.ds(c0, POOL_GROUP_DIM)]
        sh = 1
        while sh < w:
            ext = ext + pltpu.roll(ext, sh, axis=0)
            sh *= 2
        acc = ext[halo:, :]
        cnt = jnp.minimum(pos + 1, w).astype(F32)
        mixed = acc / cnt - p[:, c0:c0 + POOL_GROUP_DIM]
        og = jnp.dot(mixed.astype(BF16), pw_ref[g], preferred_element_type=F32)
        o_ref[:, pl.ds(c0, POOL_GROUP_DIM)] = (og * ps_ref[:, pl.ds(c0, POOL_GROUP_DIM)]).astype(o_ref.dtype)
    pbuf[pl.ds(0, halo), :] = pbuf[pl.ds(ts, halo), :]

    nchunk = ts // GLA_CHUNK
    glog = jnp.dot(gl_ref[...].astype(BF16), w2_ref[...], preferred_element_type=F32) + gb_ref[...]
    gk = (jnp.minimum(glog, 0.0) - jnp.log(1.0 + jnp.exp(-jnp.abs(glog)))) * (1.0 / GLA_GATE_NORMALIZER)
    row = lax.broadcasted_iota(I32, (ts, 1), 0)
    rin = row % GLA_CHUNK
    b = gk
    sh = 1
    while sh < GLA_CHUNK:
        b = b + jnp.where(rin >= sh, pltpu.roll(b, sh, axis=0), 0.0)
        sh *= 2
    b3 = b.reshape(nchunk, GLA_CHUNK, GLA_DK_TOTAL)
    bmid = b3[:, GLA_CHUNK // 2 - 1:GLA_CHUNK // 2, :]
    blast = b3[:, GLA_CHUNK - 1:GLA_CHUNK, :]
    lane = lax.broadcasted_iota(I32, (1, LANES), 1)
    head_lane = [lane < GLA_DK, lane >= GLA_DK]
    q_scale = GLA_DK ** -0.5
    safe = jnp.max(jnp.abs(b3 - bmid)) <= GLA_SAFE_EXP

    @pl.when(safe)
    def _chunked():
        v = v_ref[...]
        vb = v.astype(BF16)
        q3 = (q_ref[...] * q_scale).reshape(nchunk, GLA_CHUNK, GLA_DK_TOTAL)
        k3 = k_ref[...].reshape(nchunk, GLA_CHUNK, GLA_DK_TOTAL)
        qs = (q3 * jnp.exp(b3 - bmid)).reshape(ts, GLA_DK_TOTAL)
        ks = (k3 * jnp.exp(bmid - b3)).reshape(ts, GLA_DK_TOTAL)
        qd = (q3 * jnp.exp(b3)).reshape(ts, GLA_DK_TOTAL)
        kd = (k3 * jnp.exp(blast - b3)).reshape(ts, GLA_DK_TOTAL)
        cdec = jnp.exp(blast).reshape(nchunk, GLA_DK_TOTAL)

        blk = LANES
        ri = lax.broadcasted_iota(I32, (blk, blk), 0)
        ci = lax.broadcasted_iota(I32, (blk, blk), 1)
        causal = (ri // GLA_CHUNK == ci // GLA_CHUNK) & (ri >= ci)
        o_intra = [[None] * (ts // blk) for _ in range(GLA_HEADS)]
        for rb in range(ts // blk):
            rs = slice(rb * blk, (rb + 1) * blk)
            for pair in range(GLA_HEADS // 2):
                ls = slice(pair * LANES, (pair + 1) * LANES)
                ks_p = ks[rs, ls].astype(BF16)
                for sub in range(2):
                    h = pair * 2 + sub
                    q_m = jnp.where(head_lane[sub], qs[rs, ls], 0.0).astype(BF16)
                    sc = lax.dot_general(q_m, ks_p, (((1,), (1,)), ((), ())), preferred_element_type=F32)
                    sc = jnp.where(causal, sc, 0.0).astype(BF16)
                    o_intra[h][rb] = jnp.dot(sc, vb[rs, h * GLA_DV:(h + 1) * GLA_DV], preferred_element_type=F32)

        cpb = blk // GLA_CHUNK
        chunk_of_col = lax.broadcasted_iota(I32, (1, blk), 1) // GLA_CHUNK
        kdb = kd.astype(BF16)
        for rb in range(ts // blk):
            rs = slice(rb * blk, (rb + 1) * blk)
            for pair in range(GLA_HEADS // 2):
                ls = slice(pair * LANES, (pair + 1) * LANES)
                inc = []
                for sub in range(2):
                    h = pair * 2 + sub
                    v_t = v[rs, h * GLA_DV:(h + 1) * GLA_DV].T
                    lhs = jnp.concatenate([jnp.where(chunk_of_col == c, v_t, 0.0) for c in range(cpb)],
                                          axis=0).astype(BF16)
                    inc.append(jnp.dot(lhs, kdb[rs, ls], preferred_element_type=F32))
                stacked = jnp.where(head_lane[0], inc[0], inc[1])
                for c in range(cpb):
                    kvbuf[pair, rb * cpb + c] = stacked[c * LANES:(c + 1) * LANES]
        for pair in range(GLA_HEADS // 2):
            ls = slice(pair * LANES, (pair + 1) * LANES)
            st = state[pair]
            for c in range(nchunk):
                sall[pair, c] = st.astype(BF16)
                st = st * cdec[c:c + 1, ls] + kvbuf[pair, c]
            state[pair] = st
        o_inter = [[None] * nchunk for _ in range(GLA_HEADS)]
        for pair in range(GLA_HEADS // 2):
            ls = slice(pair * LANES, (pair + 1) * LANES)
            for c in range(nchunk):
                rs = slice(c * GLA_CHUNK, (c + 1) * GLA_CHUNK)
                q_c = qd[rs, ls]
                q2 = jnp.concatenate([jnp.where(head_lane[sub], q_c, 0.0) for sub in range(2)], axis=0).astype(BF16)
                res = lax.dot_general(q2, sall[pair, c], (((1,), (1,)), ((), ())), preferred_element_type=F32)
                for sub in range(2):
                    o_inter[pair * 2 + sub][c] = res[sub * GLA_CHUNK:(sub + 1) * GLA_CHUNK]
        for h in range(GLA_HEADS):
            obuf[:, pl.ds(h * GLA_DV, GLA_DV)] = (jnp.concatenate(o_intra[h], axis=0)
                                                  + jnp.concatenate(o_inter[h], axis=0))

    @pl.when(jnp.logical_not(safe))
    def _row_by_row():
        gk_s[...] = gk
        row8 = lax.broadcasted_iota(I32, (SUBLANES, 1), 0)

        def slab(i8, carry):
            r0 = pl.multiple_of(i8 * SUBLANES, SUBLANES)
            q8 = q_ref[pl.ds(r0, SUBLANES), :] * q_scale
            k8 = k_ref[pl.ds(r0, SUBLANES), :]
            v8 = v_ref[pl.ds(r0, SUBLANES), :]
            g8 = jnp.exp(gk_s[pl.ds(r0, SUBLANES), :])
            outs = [jnp.zeros((SUBLANES, GLA_DV), F32) for _ in range(GLA_HEADS)]
            for pair in range(GLA_HEADS // 2):
                ls = slice(pair * LANES, (pair + 1) * LANES)
                st = state[pair]
                for r in range(SUBLANES):
                    sel = row8 == r
                    k_r = jnp.where(sel, k8[:, ls], 0.0).astype(BF16)
                    inc = []
                    for sub in range(2):
                        h = pair * 2 + sub
                        v_r = jnp.where(sel, v8[:, h * GLA_DV:(h + 1) * GLA_DV], 0.0).astype(BF16)
                        inc.append(lax.dot_general(v_r, k_r, (((0,), (0,)), ((), ())), preferred_element_type=F32))
                    st = st * g8[r:r + 1, ls] + jnp.where(head_lane[0], inc[0], inc[1])
                    q2 = jnp.concatenate([jnp.where(sel & head_lane[sub], q8[:, ls], 0.0) for sub in range(2)],
                                         axis=0).astype(BF16)
                    res = lax.dot_general(q2, st.astype(BF16), (((1,), (1,)), ((), ())),
                                          preferred_element_type=F32)
                    for sub in range(2):
                        outs[pair * 2 + sub] = outs[pair * 2 + sub] + res[sub * SUBLANES:(sub + 1) * SUBLANES]
                state[pair] = st
            obuf[pl.ds(r0, SUBLANES), :] = jnp.concatenate(outs, axis=1)
            return carry
        lax.fori_loop(0, ts // SUBLANES, slab, 0)

    nw = nw_ref[...]
    r = r_ref[...]
    for h in range(GLA_HEADS):
        o = obuf[:, pl.ds(h * GLA_DV, GLA_DV)]
        o = o * lax.rsqrt(jnp.mean(o * o, axis=-1, keepdims=True) + RMS_EPS) * nw
        o = o * _silu(r[:, h * GLA_DV:(h + 1) * GLA_DV])
        o_ref[:, pl.ds(POOL_WIDTH + h * GLA_DV, GLA_DV)] = o.astype(o_ref.dtype)


def _mixer(proj, batch, seq, w2p, gate_b, norm_w, pool_w, pool_scale):
    t = proj.shape[0]
    ts = MIX_TS
    nseq = seq // ts

    def rows(width, col_block):
        return pl.BlockSpec((ts, width), lambda bi, si: (bi * nseq + si, col_block))

    def full(shape):
        return pl.BlockSpec(shape, lambda bi, si: (0,) * len(shape))

    return pl.pallas_call(
        _mixer_body,
        out_shape=jax.ShapeDtypeStruct((t, POOL_WIDTH + GLA_WIDTH), BF16),
        grid=(batch, nseq),
        in_specs=[rows(POOL_WIDTH, 0),
                  rows(GLA_DK_TOTAL, 2),
                  rows(GLA_DK_TOTAL, 3),
                  rows(GLA_WIDTH, 2),
                  rows(GLA_WIDTH, 3),
                  rows(LANES, GATE_COL_BLOCK),
                  full(w2p.shape), full(gate_b.shape), full(norm_w.shape),
                  full(pool_w.shape), full(pool_scale.shape)],
        out_specs=pl.BlockSpec((ts, POOL_WIDTH + GLA_WIDTH), lambda bi, si: (bi * nseq + si, 0)),
        scratch_shapes=[pltpu.VMEM((ts + POOL_WINDOWS[-1], POOL_WIDTH), F32),
                        pltpu.VMEM((GLA_HEADS // 2, LANES, GLA_DV), F32),
                        pltpu.VMEM((GLA_HEADS // 2, ts // GLA_CHUNK, LANES, GLA_DV), F32),
                        pltpu.VMEM((GLA_HEADS // 2, ts // GLA_CHUNK, LANES, GLA_DV), BF16),
                        pltpu.VMEM((ts, GLA_WIDTH), F32), pltpu.VMEM((ts, GLA_DK_TOTAL), F32)],
        compiler_params=_cparams(("arbitrary", "arbitrary")),
        name="mixer",
    )(proj, proj, proj, proj, proj, proj, w2p, gate_b, norm_w, pool_w, pool_scale)


def _pack_row_words(x):
    half = x.shape[1] // 2
    u = pltpu.bitcast(x.astype(BF16).astype(F32), U32)
    hi_mask = jnp.uint32(HIGH_HALF)
    return [(u[:, half + s * LANES:half + (s + 1) * LANES] & hi_mask) | (u[:, s * LANES:(s + 1) * LANES] >> 16)
            for s in range(ROW_WORDS)]


def _unpack_row_words(w):
    return pltpu.bitcast(w << 16, F32), pltpu.bitcast(w & jnp.uint32(HIGH_HALF), F32)


def _rows(ref, first, n, align=1):
    if isinstance(first, int):
        start = first * ROW_WORDS
    else:
        start = pl.multiple_of(first * ROW_WORDS, ROW_WORDS * align)
    return ref.at[pl.ds(start, n * ROW_WORDS), :]


def _word_plane(first, m, s):
    return pl.ds(first * ROW_WORDS + s, m, stride=ROW_WORDS)


def _outproj_body(m_ref, w_ref, x_ref, g_ref, b_ref, h_ref, h4_ref):
    y = DEEPNORM_ALPHA * x_ref[...] + jnp.dot(m_ref[...], w_ref[...], preferred_element_type=F32)
    h = _layernorm(y, g_ref[...], b_ref[...])
    h_ref[...] = h
    for s, w in enumerate(_pack_row_words(h)):
        h4_ref[_word_plane(0, h.shape[0], s), :] = w


def _outproj_ln(mixed, w_bf, x2d, g, b):
    t, d = x2d.shape
    tm = PROJ_TM
    return pl.pallas_call(
        _outproj_body,
        out_shape=(jax.ShapeDtypeStruct((t, d), F32), jax.ShapeDtypeStruct((t * ROW_WORDS, LANES), U32)),
        grid=(t // tm,),
        in_specs=[pl.BlockSpec((tm, d), lambda i: (i, 0)),
                  pl.BlockSpec((d, d), lambda i: (0, 0)),
                  pl.BlockSpec((tm, d), lambda i: (i, 0)),
                  pl.BlockSpec((1, d), lambda i: (0, 0)),
                  pl.BlockSpec((1, d), lambda i: (0, 0))],
        out_specs=(pl.BlockSpec((tm, d), lambda i: (i, 0)),
                   pl.BlockSpec((tm * ROW_WORDS, LANES), lambda i: (i, 0))),
        compiler_params=_cparams(("arbitrary",)),
        name="outproj_ln",
    )(mixed, w_bf, x2d, g, b)


def _first_argmax_rows(val, rowf, nrows):
    m = jnp.max(val, axis=0, keepdims=True)
    first = jnp.min(jnp.where(val == m, rowf, float(nrows)), axis=0, keepdims=True)
    return m, first, rowf == first


def _router_body(h_ref, whi_ref, wlo_ref, bias_ref, idx_ref, gate_ref, rank_ref, cnt_ref, carry):
    tm = h_ref.shape[0]
    i = pl.program_id(0)

    @pl.when(i == 0)
    def _():
        carry[...] = jnp.zeros(carry.shape, F32)

    h = h_ref[...]
    h_hi = h.astype(BF16)
    h_lo = (h - h_hi.astype(F32)).astype(BF16)
    nt = (((1,), (1,)), ((), ()))
    logits = (lax.dot_general(whi_ref[...], h_hi, nt, preferred_element_type=F32)
              + lax.dot_general(whi_ref[...], h_lo, nt, preferred_element_type=F32)
              + lax.dot_general(wlo_ref[...], h_hi, nt, preferred_element_type=F32))
    scores = 1.0 / (1.0 + jnp.exp(-logits))
    biased = scores + bias_ref[...]
    neg = -jnp.inf

    grp = biased.reshape(N_GROUPS, GROUP_SIZE, tm)
    gi = lax.broadcasted_iota(I32, (N_GROUPS, GROUP_SIZE, tm), 1).astype(F32)
    g1 = jnp.max(grp, axis=1, keepdims=True)
    f1 = jnp.min(jnp.where(grp == g1, gi, float(GROUP_SIZE)), axis=1, keepdims=True)
    g2 = jnp.max(jnp.where(gi == f1, neg, grp), axis=1, keepdims=True)
    gscore = (g1 + g2).reshape(N_GROUPS, tm)

    growf = lax.broadcasted_iota(I32, (N_GROUPS, tm), 0).astype(F32)
    gsel = jnp.zeros((N_GROUPS, tm), F32)
    gval = gscore
    for _ in range(TOPK_GROUPS):
        _, _, pick = _first_argmax_rows(gval, growf, N_GROUPS)
        gsel = jnp.where(pick, 1.0, gsel)
        gval = jnp.where(pick, neg, gval)
    emask = jnp.broadcast_to(gsel.reshape(N_GROUPS, 1, tm), (N_GROUPS, GROUP_SIZE, tm)).reshape(N_EXPERTS, tm)

    rowf = lax.broadcasted_iota(I32, (N_EXPERTS, tm), 0).astype(F32)
    val = jnp.where(emask > 0.0, biased, neg)
    onehot = jnp.zeros((N_EXPERTS, tm), F32)
    picks, idxs, ws = [], [], []
    for _ in range(TOP_K):
        _, first, pick = _first_argmax_rows(val, rowf, N_EXPERTS)
        picks.append(pick)
        idxs.append(first)
        ws.append(jnp.sum(jnp.where(pick, scores, 0.0), axis=0, keepdims=True))
        onehot = jnp.where(pick, 1.0, onehot)
        val = jnp.where(pick, neg, val)
    w = jnp.concatenate(ws, axis=0)
    gate_ref[...] = w / jnp.sum(w, axis=0, keepdims=True) * ROUTED_SCALE
    idx_ref[...] = jnp.concatenate(idxs, axis=0).astype(I32)

    ti = lax.broadcasted_iota(I32, (tm, tm), 0)
    tj = lax.broadcasted_iota(I32, (tm, tm), 1)
    upper = jnp.where(ti < tj, 1.0, 0.0).astype(BF16)
    prefix = jnp.dot(onehot.astype(BF16), upper, preferred_element_type=F32) + carry[...]
    ranks = [jnp.sum(jnp.where(pk, prefix, 0.0), axis=0, keepdims=True) for pk in picks]
    rank_ref[...] = jnp.concatenate(ranks, axis=0).astype(I32)
    carry[...] = carry[...] + jnp.sum(onehot, axis=1, keepdims=True)
    cnt_ref[...] = carry[...]


def _router(h, wt, bias_col):
    t, d = h.shape
    tm = ROUTE_TM
    wt_hi = wt.astype(BF16)
    wt_lo = (wt - wt_hi.astype(F32)).astype(BF16)
    return pl.pallas_call(
        _router_body,
        out_shape=(jax.ShapeDtypeStruct((TOP_K, t), I32), jax.ShapeDtypeStruct((TOP_K, t), F32),
                   jax.ShapeDtypeStruct((TOP_K, t), I32), jax.ShapeDtypeStruct((N_EXPERTS, 1), F32)),
        grid=(t // tm,),
        in_specs=[pl.BlockSpec((tm, d), lambda i: (i, 0)),
                  pl.BlockSpec((N_EXPERTS, d), lambda i: (0, 0)),
                  pl.BlockSpec((N_EXPERTS, d), lambda i: (0, 0)),
                  pl.BlockSpec((N_EXPERTS, 1), lambda i: (0, 0))],
        out_specs=(pl.BlockSpec((TOP_K, tm), lambda i: (0, i)),
                   pl.BlockSpec((TOP_K, tm), lambda i: (0, i)),
                   pl.BlockSpec((TOP_K, tm), lambda i: (0, i)),
                   pl.BlockSpec((N_EXPERTS, 1), lambda i: (0, 0))),
        scratch_shapes=[pltpu.VMEM((N_EXPERTS, 1), F32)],
        compiler_params=_cparams(("arbitrary",)),
        name="router",
    )(h, wt_hi, wt_lo, bias_col)


def _positions_body(idx_ref, rank_ref, start_ref, pos_ref):
    tm = idx_ref.shape[1]
    rowi = lax.broadcasted_iota(I32, (N_EXPERTS, tm), 0)
    start = start_ref[...]
    idx = idx_ref[...]
    rows = [jnp.sum(jnp.where(rowi == idx[k:k + 1, :], start, 0.0), axis=0, keepdims=True) for k in range(TOP_K)]
    pos_ref[...] = jnp.concatenate(rows, axis=0).astype(I32) + rank_ref[...]


def _positions(idx_t, rank_t, start_col):
    t = idx_t.shape[1]
    tm = POS_TM
    return pl.pallas_call(
        _positions_body,
        out_shape=jax.ShapeDtypeStruct((TOP_K, t), I32),
        grid=(t // tm,),
        in_specs=[pl.BlockSpec((TOP_K, tm), lambda i: (0, i)),
                  pl.BlockSpec((TOP_K, tm), lambda i: (0, i)),
                  pl.BlockSpec((N_EXPERTS, 1), lambda i: (0, 0))],
        out_specs=pl.BlockSpec((TOP_K, tm), lambda i: (0, i)),
        compiler_params=_cparams(("arbitrary",)),
        name="positions",
    )(idx_t, rank_t, start_col)


def _pad_fill_copy(zeros, xs_ref, sem, row, nrows):
    return pltpu.make_async_copy(_rows(zeros, 0, nrows), _rows(xs_ref, row, nrows), sem)


def _dispatch_body(pad_row_ref, pad_n_ref, nu_ref, pos_ref, h4_ref, xs_ref, zeros, sem, pad_sem):
    i = pl.program_id(0)
    tm = h4_ref.shape[0] // ROW_WORDS
    half = MOE_BM // 2
    pad_bits = [1 << j for j in range(MOE_BM.bit_length() - 1)]
    n_half_blocks = xs_ref.shape[0] // (half * ROW_WORDS)

    def pad_pass(wait):
        def go(cp):
            if wait:
                cp.wait()
            else:
                cp.start()

        def body(e, carry):
            row = pad_row_ref[e]
            n = pad_n_ref[e]
            for bit in pad_bits:
                @pl.when((n & bit) != 0)
                def _():
                    go(_pad_fill_copy(zeros, xs_ref, pad_sem, row + (n & (bit - 1)), bit))
            return carry
        lax.fori_loop(0, N_EXPERTS, body, 0)

        def tail(hb, carry):
            go(_pad_fill_copy(zeros, xs_ref, pad_sem, hb * half, half))
            return carry
        lax.fori_loop(nu_ref[0] * 2, n_half_blocks, tail, 0)

    @pl.when(i == 0)
    def _():
        zeros[...] = jnp.zeros(zeros.shape, U32)
        pad_pass(False)

    def body(tt, carry):
        for dt in range(DISP_UNROLL):
            t = tt * DISP_UNROLL + dt
            for k in range(TOP_K):
                p = pos_ref[0, 0, t * TOP_K + k]
                pltpu.make_async_copy(_rows(h4_ref, t, 1), _rows(xs_ref, p, 1), sem).start(priority=k % 2)
        return carry
    lax.fori_loop(0, tm // DISP_UNROLL, body, 0)

    for k in range(TOP_K):
        pltpu.make_async_copy(h4_ref, _rows(xs_ref, 0, tm), sem).wait()

    @pl.when(i == pl.num_programs(0) - 1)
    def _():
        pad_pass(True)


def _dispatch(pad_row, pad_n, n_used, pos3, h4, n_rows):
    nt, _, per_step = pos3.shape
    tm = per_step // TOP_K
    grid_spec = pltpu.PrefetchScalarGridSpec(
        num_scalar_prefetch=3,
        grid=(nt,),
        in_specs=[pl.BlockSpec((1, 1, per_step), lambda i, a, b, c: (i, 0, 0), memory_space=pltpu.SMEM),
                  pl.BlockSpec((tm * ROW_WORDS, LANES), lambda i, a, b, c: (i, 0))],
        out_specs=pl.BlockSpec(memory_space=pl.ANY),
        scratch_shapes=[pltpu.VMEM((MOE_BM // 2 * ROW_WORDS, LANES), U32),
                        pltpu.SemaphoreType.DMA, pltpu.SemaphoreType.DMA],
    )
    return pl.pallas_call(
        _dispatch_body,
        out_shape=jax.ShapeDtypeStruct((n_rows * ROW_WORDS, LANES), U32),
        grid_spec=grid_spec,
        compiler_params=_cparams(("arbitrary",)),
        name="dispatch",
    )(pad_row, pad_n, n_used, pos3, h4)


def _weight_copies(e, w_hbm, stage, sem, slot):
    return [pltpu.make_async_copy(w.at[e], st.at[slot], sem.at[slot, j])
            for j, (w, st) in enumerate(zip(w_hbm, stage))]


def _row_block_copy(xs_hbm, xbuf, xsem, j):
    slot = j % X_RING
    return pltpu.make_async_copy(_rows(xs_hbm, j * MOE_BM, MOE_BM, align=MOE_BM), xbuf.at[slot], xsem.at[slot])


def _experts_body(be_ref, nu_ref, nx_ref, ord_ref, xs_hbm, wg_hbm, wu_hbm, wd_hbm, y4_ref,
                  xbuf, xsem, sg, su, sd, wsem, wg_b, wu_b, wd_b):
    w_hbm = (wg_hbm, wu_hbm, wd_hbm)
    stage = (sg, su, sd)
    n_used = nu_ref[0]

    def do_block(i, out_row):
        @pl.when(i < n_used)
        def _():
            e = be_ref[i]
            prev_e = be_ref[jnp.maximum(i - 1, 0)]

            @pl.when(i == 0)
            def _():
                for j in range(X_RING - 1):
                    @pl.when(j < n_used)
                    def _():
                        _row_block_copy(xs_hbm, xbuf, xsem, j).start()

            @pl.when(i + X_RING - 1 < n_used)
            def _():
                _row_block_copy(xs_hbm, xbuf, xsem, i + X_RING - 1).start()

            @pl.when((i == 0) | (e != prev_e))
            def _():
                wslot = ord_ref[e] % 2
                n1 = nx_ref[e]
                n2 = jnp.where(n1 >= 0, nx_ref[jnp.maximum(n1, 0)], -1)

                @pl.when(i == 0)
                def _():
                    for cp in _weight_copies(e, w_hbm, stage, wsem, wslot):
                        cp.start()

                    @pl.when(n1 >= 0)
                    def _():
                        for cp in _weight_copies(n1, w_hbm, stage, wsem, 1 - wslot):
                            cp.start()
                for cp in _weight_copies(e, w_hbm, stage, wsem, wslot):
                    cp.wait()
                wg_b[...] = sg[wslot].astype(BF16)
                wu_b[...] = su[wslot].astype(BF16)
                wd_b[...] = sd[wslot].astype(BF16)

                @pl.when(n2 >= 0)
                def _():
                    for cp in _weight_copies(n2, w_hbm, stage, wsem, wslot):
                        cp.start()

            _row_block_copy(xs_hbm, xbuf, xsem, i).wait()
            slot = i % X_RING
            parts = [_unpack_row_words(xbuf[slot, _word_plane(0, MOE_BM, s), :]) for s in range(ROW_WORDS)]
            x = jnp.concatenate([p[0] for p in parts] + [p[1] for p in parts], axis=1).astype(BF16)
            g = jnp.dot(x, wg_b[...], preferred_element_type=F32)
            u = jnp.dot(x, wu_b[...], preferred_element_type=F32)
            a = (_silu(g) * u).astype(BF16)
            y = jnp.dot(a, wd_b[...], preferred_element_type=F32)
            for s, w in enumerate(_pack_row_words(y)):
                y4_ref[_word_plane(out_row, MOE_BM, s), :] = w

        @pl.when(i >= n_used)
        def _():
            y4_ref[pl.ds(out_row * ROW_WORDS, MOE_BM * ROW_WORDS), :] = jnp.zeros((MOE_BM * ROW_WORDS, LANES), U32)

    for sub in range(EXP_BLOCKS_PER_STEP):
        do_block(pl.program_id(0) * EXP_BLOCKS_PER_STEP + sub, sub * MOE_BM)


def _experts(block_e, n_used, next_e, ord_e, xs, wg, wu, wd):
    nblk = block_e.shape[0]
    bm = MOE_BM * EXP_BLOCKS_PER_STEP
    e, d, de = wg.shape

    grid_spec = pltpu.PrefetchScalarGridSpec(
        num_scalar_prefetch=4,
        grid=(nblk // EXP_BLOCKS_PER_STEP,),
        in_specs=[pl.BlockSpec(memory_space=pl.ANY),
                  pl.BlockSpec(memory_space=pl.ANY),
                  pl.BlockSpec(memory_space=pl.ANY),
                  pl.BlockSpec(memory_space=pl.ANY)],
        out_specs=pl.BlockSpec((bm * ROW_WORDS, LANES), lambda i, be, nu, nx, od: (i, 0)),
        scratch_shapes=[pltpu.VMEM((X_RING, MOE_BM * ROW_WORDS, LANES), U32), pltpu.SemaphoreType.DMA((X_RING,)),
                        pltpu.VMEM((2, d, de), F32), pltpu.VMEM((2, d, de), F32), pltpu.VMEM((2, de, d), F32),
                        pltpu.SemaphoreType.DMA((2, 3)),
                        pltpu.VMEM((d, de), BF16), pltpu.VMEM((d, de), BF16), pltpu.VMEM((de, d), BF16)],
    )
    return pl.pallas_call(
        _experts_body,
        out_shape=jax.ShapeDtypeStruct((nblk * MOE_BM * ROW_WORDS, LANES), U32),
        grid_spec=grid_spec,
        compiler_params=_cparams(("arbitrary",)),
        name="experts",
    )(block_e, n_used, next_e, ord_e, xs, wg, wu, wd)


def _combine_body(pos_cur, pos_nxt, y4_ref, h_ref, gate_ref, wsg_ref, wsu_ref, wsd_ref, g_ref, b_ref,
                  o_ref, buf, sem):
    i = pl.program_id(0)
    nb = pl.num_programs(0)
    tm, d = h_ref.shape
    slot = i % 2

    def issue(pos_ref, sl):
        def body(t, carry):
            for k in range(TOP_K):
                p = pos_ref[0, 0, t * TOP_K + k]
                pltpu.make_async_copy(_rows(y4_ref, p, 1), _rows(buf.at[sl], k * tm + t, 1),
                                      sem.at[sl]).start(priority=k % 2)
            return carry
        lax.fori_loop(0, tm, body, 0)

    @pl.when(i == 0)
    def _():
        issue(pos_cur, 0)

    @pl.when(i + 1 < nb)
    def _():
        issue(pos_nxt, 1 - slot)

    h = h_ref[...]
    hb = h.astype(BF16)
    sg = jnp.dot(hb, wsg_ref[...], preferred_element_type=F32)
    su = jnp.dot(hb, wsu_ref[...], preferred_element_type=F32)
    shared = jnp.dot((_silu(sg) * su).astype(BF16), wsd_ref[...], preferred_element_type=F32)

    pltpu.make_async_copy(_rows(y4_ref, 0, tm * TOP_K), buf.at[slot], sem.at[slot]).wait()
    gates = gate_ref[...]
    lo_cols, hi_cols = [], []
    for s in range(ROW_WORDS):
        lo_acc = hi_acc = None
        for k in range(TOP_K):
            lo, hi = _unpack_row_words(buf[slot, _word_plane(k * tm, tm, s), :])
            gk = gates[:, k:k + 1]
            lo_acc = gk * lo if lo_acc is None else lo_acc + gk * lo
            hi_acc = gk * hi if hi_acc is None else hi_acc + gk * hi
        lo_cols.append(lo_acc)
        hi_cols.append(hi_acc)
    routed = jnp.concatenate(lo_cols + hi_cols, axis=1)
    o_ref[...] = _layernorm(DEEPNORM_ALPHA * h + (routed + shared), g_ref[...], b_ref[...])


def _combine(pos3, y4, h, gates, wsg, wsu, wsd, g, b):
    t, d = h.shape
    tm = COMB_TM
    nt = t // tm
    ds_ = wsg.shape[1]
    return pl.pallas_call(
        _combine_body,
        out_shape=jax.ShapeDtypeStruct((t, d), F32),
        grid=(nt,),
        in_specs=[pl.BlockSpec((1, 1, tm * TOP_K), lambda i: (i, 0, 0), memory_space=pltpu.SMEM),
                  pl.BlockSpec((1, 1, tm * TOP_K), lambda i: (jnp.minimum(i + 1, nt - 1), 0, 0),
                               memory_space=pltpu.SMEM),
                  pl.BlockSpec(memory_space=pl.ANY),
                  pl.BlockSpec((tm, d), lambda i: (i, 0)),
                  pl.BlockSpec((tm, TOP_K), lambda i: (i, 0)),
                  pl.BlockSpec((d, ds_), lambda i: (0, 0)),
                  pl.BlockSpec((d, ds_), lambda i: (0, 0)),
                  pl.BlockSpec((ds_, d), lambda i: (0, 0)),
                  pl.BlockSpec((1, d), lambda i: (0, 0)),
                  pl.BlockSpec((1, d), lambda i: (0, 0))],
        out_specs=pl.BlockSpec((tm, d), lambda i: (i, 0)),
        scratch_shapes=[pltpu.VMEM((2, tm * TOP_K * ROW_WORDS, LANES), U32),
                        pltpu.SemaphoreType.DMA((2,))],
        compiler_params=_cparams(("arbitrary",)),
        name="combine",
    )(pos3, pos3, y4, h, gates, wsg, wsu, wsd, g, b)


def _expert_tables(counts, nblk):
    bm = MOE_BM
    cnt = counts.reshape(N_EXPERTS).astype(I32)
    padded = (cnt + bm - 1) // bm * bm
    padded_end = jnp.cumsum(padded)
    padded_start = padded_end - padded
    block_rows = jnp.arange(nblk, dtype=I32) * bm
    block_e = jnp.sum((padded_end[None, :] <= block_rows[:, None]).astype(I32), axis=1)
    block_e = jnp.minimum(block_e, N_EXPERTS - 1)
    n_used = (padded_end[-1:] // bm).astype(I32)
    ids = jnp.where(cnt > 0, jnp.arange(N_EXPERTS, dtype=I32), N_EXPERTS)
    after = jnp.concatenate([lax.cummin(ids, reverse=True)[1:], jnp.full((1,), N_EXPERTS, I32)])
    next_e = jnp.where(after < N_EXPERTS, after, -1).astype(I32)
    ord_e = (jnp.cumsum((cnt > 0).astype(I32)) - 1).astype(I32)
    return padded_start, padded_start + cnt, padded - cnt, block_e, n_used, next_e, ord_e


def kernel(x, w_in, gla_gate_w2, gla_gate_b, gla_norm_w, pool_w_group, pool_scale, w_out, ln1_g, ln1_b,
           router_w, router_bias, w_exp_gate, w_exp_up, w_exp_down, w_sh_gate, w_sh_up, w_sh_down, ln2_g, ln2_b):
    batch, seq, d = x.shape
    t = batch * seq
    h2d = x.reshape(t, d)
    for l in range(DEPTH):
        d_in = w_in.shape[2]
        w_in_b = jnp.pad(w_in[l], ((0, 0), (0, D_IN_PAD - d_in))).astype(BF16)
        w2p = jnp.pad(gla_gate_w2[l], ((0, LANES - GLA_GATE_RANK), (0, 0))).astype(BF16)
        proj = _inproj(h2d, w_in_b)
        mixed = _mixer(proj, batch, seq, w2p, gla_gate_b[l].reshape(1, -1), gla_norm_w[l].reshape(1, -1),
                       pool_w_group[l].astype(BF16), pool_scale[l].reshape(1, -1))
        h, h4 = _outproj_ln(mixed, w_out[l].astype(BF16), h2d, ln1_g[l].reshape(1, -1), ln1_b[l].reshape(1, -1))
        idx_t, gate_t, rank_t, counts = _router(h, router_w[l].T, router_bias[l].reshape(-1, 1))
        nblk = (t * TOP_K + N_EXPERTS * (MOE_BM - 1)) // MOE_BM
        nblk = -(-nblk // EXP_BLOCKS_PER_STEP) * EXP_BLOCKS_PER_STEP
        start, pad_row, pad_n, block_e, n_used, next_e, ord_e = _expert_tables(counts, nblk)
        pos_t = _positions(idx_t, rank_t, start.astype(F32).reshape(-1, 1))
        pos_tok = pos_t.T
        xs = _dispatch(pad_row, pad_n, n_used, pos_tok.reshape(t // DISP_TM, 1, DISP_TM * TOP_K), h4, nblk * MOE_BM)
        y4 = _experts(block_e, n_used, next_e, ord_e, xs, w_exp_gate[l], w_exp_up[l], w_exp_down[l])
        h2d = _combine(pos_tok.reshape(t // COMB_TM, 1, COMB_TM * TOP_K), y4, h, gate_t.T,
                       w_sh_gate[l].astype(BF16), w_sh_up[l].astype(BF16), w_sh_down[l].astype(BF16),
                       ln2_g[l].reshape(1, -1), ln2_b[l].reshape(1, -1))
    return h2d.reshape(batch, seq, d)
```

```python
import jax
import jax.numpy as jnp
from jax import lax
from jax.experimental import pallas as pl
from jax.experimental.pallas import tpu as pltpu

F32 = jnp.float32
BF16 = jnp.bfloat16
I32 = jnp.int32
U32 = jnp.uint32
HIGH_HALF = 0xFFFF0000

POOL_WINDOWS = (2, 4, 8, 16)
POOL_GROUP_DIM = 128
POOL_WIDTH = 512
GLA_HEADS = 4
GLA_DK = 64
GLA_DV = 128
GLA_DK_TOTAL = 256
GLA_WIDTH = 512
GLA_GATE_RANK = 16
GLA_GATE_NORMALIZER = 16.0
GLA_CHUNK = 16
GLA_SAFE_EXP = 60.0
N_EXPERTS = 256
TOP_K = 8
N_GROUPS = 8
GROUP_SIZE = N_EXPERTS // N_GROUPS
TOPK_GROUPS = 4
ROUTED_SCALE = 2.5
DEPTH = 1
DEEPNORM_ALPHA = (2.0 * DEPTH) ** 0.25
LN_EPS = 1e-5
RMS_EPS = 1e-5

LANES = 128
SUBLANES = 8
VMEM_LIMIT = 56 * 1024 * 1024

MIX_TS = 512
ROUTE_TM = 512
MOE_BM = 256
COMB_TM = 256
POS_TM = 512
DISP_TM = 256
DISP_UNROLL = 4
ROW_WORDS = 4
X_RING = 3
EXP_BLOCKS_PER_STEP = 2
D_IN_PAD = 2944


def _cparams(sem):
    return pltpu.CompilerParams(dimension_semantics=sem, vmem_limit_bytes=VMEM_LIMIT)


def _silu(x):
    return x * (1.0 / (1.0 + jnp.exp(-x)))


def _layernorm(y, g, b):
    mu = jnp.mean(y, axis=-1, keepdims=True)
    yc = y - mu
    var = jnp.mean(yc * yc, axis=-1, keepdims=True)
    return yc * lax.rsqrt(var + LN_EPS) * g + b


def _mixer_body(p_ref, q_ref, k_ref, v_ref, r_ref, gl_ref, w2_ref, gb_ref, nw_ref, pw_ref, ps_ref,
                o_ref, pbuf, state, kvbuf, sall, obuf, gk_s):
    ts = p_ref.shape[0]
    s_idx = pl.program_id(1)
    halo = POOL_WINDOWS[-1]

    @pl.when(s_idx == 0)
    def _():
        pbuf[pl.ds(0, halo), :] = jnp.zeros((halo, POOL_WIDTH), F32)
        state[...] = jnp.zeros(state.shape, F32)

    p = p_ref[...]
    pbuf[pl.ds(halo, ts), :] = p
    pos = s_idx * ts + lax.broadcasted_iota(I32, (ts, 1), 0)
    for g, w in enumerate(POOL_WINDOWS):
        c0 = g * POOL_GROUP_DIM
        ext = pbuf[:, pl.ds(c0, POOL_GROUP_DIM)]
        sh = 1
        while sh < w:
            ext = ext + pltpu.roll(ext, sh, axis=0)
            sh *= 2
        acc = ext[halo:, :]
        cnt = jnp.minimum(pos + 1, w).astype(F32)
        mixed = acc / cnt - p[:, c0:c0 + POOL_GROUP_DIM]
        og = jnp.dot(mixed.astype(BF16), pw_ref[g], preferred_element_type=F32)
        o_ref[:, pl.ds(c0, POOL_GROUP_DIM)] = (og * ps_ref[:, pl.ds(c0, POOL_GROUP_DIM)]).astype(o_ref.dtype)
    pbuf[pl.ds(0, halo), :] = pbuf[pl.ds(ts, halo), :]

    nchunk = ts // GLA_CHUNK
    glog = jnp.dot(gl_ref[...].astype(BF16), w2_ref[...], preferred_element_type=F32) + gb_ref[...]
    gk = (jnp.minimum(glog, 0.0) - jnp.log(1.0 + jnp.exp(-jnp.abs(glog)))) * (1.0 / GLA_GATE_NORMALIZER)
    row = lax.broadcasted_iota(I32, (ts, 1), 0)
    rin = row % GLA_CHUNK
    b = gk
    sh = 1
    while sh < GLA_CHUNK:
        b = b + jnp.where(rin >= sh, pltpu.roll(b, sh, axis=0), 0.0)
        sh *= 2
    b3 = b.reshape(nchunk, GLA_CHUNK, GLA_DK_TOTAL)
    bmid = b3[:, GLA_CHUNK // 2 - 1:GLA_CHUNK // 2, :]
    blast = b3[:, GLA_CHUNK - 1:GLA_CHUNK, :]
    lane = lax.broadcasted_iota(I32, (1, LANES), 1)
    head_lane = [lane < GLA_DK, lane >= GLA_DK]
    q_scale = GLA_DK ** -0.5
    safe = jnp.max(jnp.abs(b3 - bmid)) <= GLA_SAFE_EXP

    @pl.when(safe)
    def _chunked():
        v = v_ref[...]
        vb = v.astype(BF16)
        q3 = (q_ref[...] * q_scale).reshape(nchunk, GLA_CHUNK, GLA_DK_TOTAL)
        k3 = k_ref[...].reshape(nchunk, GLA_CHUNK, GLA_DK_TOTAL)
        qs = (q3 * jnp.exp(b3 - bmid)).reshape(ts, GLA_DK_TOTAL)
        ks = (k3 * jnp.exp(bmid - b3)).reshape(ts, GLA_DK_TOTAL)
        qd = (q3 * jnp.exp(b3)).reshape(ts, GLA_DK_TOTAL)
        kd = (k3 * jnp.exp(blast - b3)).reshape(ts, GLA_DK_TOTAL)
        cdec = jnp.exp(blast).reshape(nchunk, GLA_DK_TOTAL)

        blk = LANES
        ri = lax.broadcasted_iota(I32, (blk, blk), 0)
        ci = lax.broadcasted_iota(I32, (blk, blk), 1)
        causal = (ri // GLA_CHUNK == ci // GLA_CHUNK) & (ri >= ci)
        o_intra = [[None] * (ts // blk) for _ in range(GLA_HEADS)]
        for rb in range(ts // blk):
            rs = slice(rb * blk, (rb + 1) * blk)
            for pair in range(GLA_HEADS // 2):
                ls = slice(pair * LANES, (pair + 1) * LANES)
                ks_p = ks[rs, ls].astype(BF16)
                for sub in range(2):
                    h = pair * 2 + sub
                    q_m = jnp.where(head_lane[sub], qs[rs, ls], 0.0).astype(BF16)
                    sc = lax.dot_general(q_m, ks_p, (((1,), (1,)), ((), ())), preferred_element_type=F32)
                    sc = jnp.where(causal, sc, 0.0).astype(BF16)
                    o_intra[h][rb] = jnp.dot(sc, vb[rs, h * GLA_DV:(h + 1) * GLA_DV], preferred_element_type=F32)

        cpb = blk // GLA_CHUNK
        chunk_of_col = lax.broadcasted_iota(I32, (1, blk), 1) // GLA_CHUNK
        kdb = kd.astype(BF16)
        for rb in range(ts // blk):
            rs = slice(rb * blk, (rb + 1) * blk)
            for pair in range(GLA_HEADS // 2):
                ls = slice(pair * LANES, (pair + 1) * LANES)
                inc = []
                for sub in range(2):
                    h = pair * 2 + sub
                    v_t = v[rs, h * GLA_DV:(h + 1) * GLA_DV].T
                    lhs = jnp.concatenate([jnp.where(chunk_of_col == c, v_t, 0.0) for c in range(cpb)],
                                          axis=0).astype(BF16)
                    inc.append(jnp.dot(lhs, kdb[rs, ls], preferred_element_type=F32))
                stacked = jnp.where(head_lane[0], inc[0], inc[1])
                for c in range(cpb):
                    kvbuf[pair, rb * cpb + c] = stacked[c * LANES:(c + 1) * LANES]
        for pair in range(GLA_HEADS // 2):
            ls = slice(pair * LANES, (pair + 1) * LANES)
            st = state[pair]
            for c in range(nchunk):
                sall[pair, c] = st.astype(BF16)
                st = st * cdec[c:c + 1, ls] + kvbuf[pair, c]
            state[pair] = st
        o_inter = [[None] * nchunk for _ in range(GLA_HEADS)]
        for pair in range(GLA_HEADS // 2):
            ls = slice(pair * LANES, (pair + 1) * LANES)
            for c in range(nchunk):
                rs = slice(c * GLA_CHUNK, (c + 1) * GLA_CHUNK)
                q_c = qd[rs, ls]
                q2 = jnp.concatenate([jnp.where(head_lane[sub], q_c, 0.0) for sub in range(2)], axis=0).astype(BF16)
                res = lax.dot_general(q2, sall[pair, c], (((1,), (1,)), ((), ())), preferred_element_type=F32)
                for sub in range(2):
                    o_inter[pair * 2 + sub][c] = res[sub * GLA_CHUNK:(sub + 1) * GLA_CHUNK]
        for h in range(GLA_HEADS):
            obuf[:, pl.ds(h * GLA_DV, GLA_DV)] = (jnp.concatenate(o_intra[h], axis=0)
                                                  + jnp.concatenate(o_inter[h], axis=0))

    @pl.when(jnp.logical_not(safe))
    def _row_by_row():
        gk_s[...] = gk
        row8 = lax.broadcasted_iota(I32, (SUBLANES, 1), 0)

        def slab(i8, carry):
            r0 = pl.multiple_of(i8 * SUBLANES, SUBLANES)
            q8 = q_ref[pl.ds(r0, SUBLANES), :] * q_scale
            k8 = k_ref[pl.ds(r0, SUBLANES), :]
            v8 = v_ref[pl.ds(r0, SUBLANES), :]
            g8 = jnp.exp(gk_s[pl.ds(r0, SUBLANES), :])
            outs = [jnp.zeros((SUBLANES, GLA_DV), F32) for _ in range(GLA_HEADS)]
            for pair in range(GLA_HEADS // 2):
                ls = slice(pair * LANES, (pair + 1) * LANES)
                st = state[pair]
                for r in range(SUBLANES):
                    sel = row8 == r
                    k_r = jnp.where(sel, k8[:, ls], 0.0).astype(BF16)
                    inc = []
                    for sub in range(2):
                        h = pair * 2 + sub
                        v_r = jnp.where(sel, v8[:, h * GLA_DV:(h + 1) * GLA_DV], 0.0).astype(BF16)
                        inc.append(lax.dot_general(v_r, k_r, (((0,), (0,)), ((), ())), preferred_element_type=F32))
                    st = st * g8[r:r + 1, ls] + jnp.where(head_lane[0], inc[0], inc[1])
                    q2 = jnp.concatenate([jnp.where(sel & head_lane[sub], q8[:, ls], 0.0) for sub in range(2)],
                                         axis=0).astype(BF16)
                    res = lax.dot_general(q2, st.astype(BF16), (((1,), (1,)), ((), ())),
                                          preferred_element_type=F32)
                    for sub in range(2):
                        outs[pair * 2 + sub] = outs[pair * 2 + sub] + res[sub * SUBLANES:(sub + 1) * SUBLANES]
                state[pair] = st
            obuf[pl.ds(r0, SUBLANES), :] = jnp.concatenate(outs, axis=1)
            return carry
        lax.fori_loop(0, ts // SUBLANES, slab, 0)

    nw = nw_ref[...]
    r = r_ref[...]
    for h in range(GLA_HEADS):
        o = obuf[:, pl.ds(h * GLA_DV, GLA_DV)]
        o = o * lax.rsqrt(jnp.mean(o * o, axis=-1, keepdims=True) + RMS_EPS) * nw
        o = o * _silu(r[:, h * GLA_DV:(h + 1) * GLA_DV])
        o_ref[:, pl.ds(POOL_WIDTH + h * GLA_DV, GLA_DV)] = o.astype(o_ref.dtype)


def _front_body(x_ref, win_ref, w2_ref, gb_ref, nw_ref, pw_ref, ps_ref, wout_ref, g_ref, b_ref, h_ref, h4_ref,
                proj_s, mix_s, pbuf, state, kvbuf, sall, obuf, gk_s):
    proj_s[...] = jnp.dot(x_ref[...].astype(BF16), win_ref[...], preferred_element_type=F32)
    col = 0
    views = []
    for width in (POOL_WIDTH, GLA_DK_TOTAL, GLA_DK_TOTAL, GLA_WIDTH, GLA_WIDTH, LANES):
        views.append(proj_s.at[:, pl.ds(col, width)])
        col += width
    _mixer_body(*views, w2_ref, gb_ref, nw_ref, pw_ref, ps_ref, mix_s, pbuf, state, kvbuf, sall, obuf, gk_s)
    y = DEEPNORM_ALPHA * x_ref[...] + jnp.dot(mix_s[...], wout_ref[...], preferred_element_type=F32)
    h = _layernorm(y, g_ref[...], b_ref[...])
    h_ref[...] = h
    for s, w in enumerate(_pack_row_words(h)):
        h4_ref[_word_plane(0, h.shape[0], s), :] = w


def _front(x2d, batch, seq, w_in_b, w2p, gate_b, norm_w, pool_w, pool_scale, w_out_b, ln_g, ln_b):
    t, d = x2d.shape
    ts = MIX_TS
    nseq = seq // ts

    def full(shape):
        return pl.BlockSpec(shape, lambda bi, si: (0,) * len(shape))

    return pl.pallas_call(
        _front_body,
        out_shape=(jax.ShapeDtypeStruct((t, d), F32), jax.ShapeDtypeStruct((t * ROW_WORDS, LANES), U32)),
        grid=(batch, nseq),
        in_specs=[pl.BlockSpec((ts, d), lambda bi, si: (bi * nseq + si, 0)),
                  full(w_in_b.shape), full(w2p.shape), full(gate_b.shape), full(norm_w.shape),
                  full(pool_w.shape), full(pool_scale.shape), full(w_out_b.shape), full(ln_g.shape), full(ln_b.shape)],
        out_specs=(pl.BlockSpec((ts, d), lambda bi, si: (bi * nseq + si, 0)),
                   pl.BlockSpec((ts * ROW_WORDS, LANES), lambda bi, si: (bi * nseq + si, 0))),
        scratch_shapes=[pltpu.VMEM((ts, D_IN_PAD), F32),
                        pltpu.VMEM((ts, POOL_WIDTH + GLA_WIDTH), BF16),
                        pltpu.VMEM((ts + POOL_WINDOWS[-1], POOL_WIDTH), F32),
                        pltpu.VMEM((GLA_HEADS // 2, LANES, GLA_DV), F32),
                        pltpu.VMEM((GLA_HEADS // 2, ts // GLA_CHUNK, LANES, GLA_DV), F32),
                        pltpu.VMEM((GLA_HEADS // 2, ts // GLA_CHUNK, LANES, GLA_DV), BF16),
                        pltpu.VMEM((ts, GLA_WIDTH), F32), pltpu.VMEM((ts, GLA_DK_TOTAL), F32)],
        compiler_params=_cparams(("arbitrary", "arbitrary")),
        name="front",
    )(x2d, w_in_b, w2p, gate_b, norm_w, pool_w, pool_scale, w_out_b, ln_g, ln_b)


def _pack_row_words(x):
    half = x.shape[1] // 2
    u = pltpu.bitcast(x.astype(BF16).astype(F32), U32)
    hi_mask = jnp.uint32(HIGH_HALF)
    return [(u[:, half + s * LANES:half + (s + 1) * LANES] & hi_mask) | (u[:, s * LANES:(s + 1) * LANES] >> 16)
            for s in range(ROW_WORDS)]


def _unpack_row_words(w):
    return pltpu.bitcast(w << 16, F32), pltpu.bitcast(w & jnp.uint32(HIGH_HALF), F32)


def _rows(ref, first, n, align=1):
    if isinstance(first, int):
        start = first * ROW_WORDS
    else:
        start = pl.multiple_of(first * ROW_WORDS, ROW_WORDS * align)
    return ref.at[pl.ds(start, n * ROW_WORDS), :]


def _word_plane(first, m, s):
    return pl.ds(first * ROW_WORDS + s, m, stride=ROW_WORDS)


def _first_argmax_rows(val, rowf, nrows):
    m = jnp.max(val, axis=0, keepdims=True)
    first = jnp.min(jnp.where(val == m, rowf, float(nrows)), axis=0, keepdims=True)
    return m, first, rowf == first


def _router_body(h_ref, whi_ref, wlo_ref, bias_ref, idx_ref, gate_ref, rank_ref, cnt_ref, carry):
    tm = h_ref.shape[0]
    i = pl.program_id(0)

    @pl.when(i == 0)
    def _():
        carry[...] = jnp.zeros(carry.shape, F32)

    h = h_ref[...]
    h_hi = h.astype(BF16)
    h_lo = (h - h_hi.astype(F32)).astype(BF16)
    nt = (((1,), (1,)), ((), ()))
    logits = (lax.dot_general(whi_ref[...], h_hi, nt, preferred_element_type=F32)
              + lax.dot_general(whi_ref[...], h_lo, nt, preferred_element_type=F32)
              + lax.dot_general(wlo_ref[...], h_hi, nt, preferred_element_type=F32))
    scores = 1.0 / (1.0 + jnp.exp(-logits))
    biased = scores + bias_ref[...]
    neg = -jnp.inf

    grp = biased.reshape(N_GROUPS, GROUP_SIZE, tm)
    gi = lax.broadcasted_iota(I32, (N_GROUPS, GROUP_SIZE, tm), 1).astype(F32)
    g1 = jnp.max(grp, axis=1, keepdims=True)
    f1 = jnp.min(jnp.where(grp == g1, gi, float(GROUP_SIZE)), axis=1, keepdims=True)
    g2 = jnp.max(jnp.where(gi == f1, neg, grp), axis=1, keepdims=True)
    gscore = (g1 + g2).reshape(N_GROUPS, tm)

    growf = lax.broadcasted_iota(I32, (N_GROUPS, tm), 0).astype(F32)
    gsel = jnp.zeros((N_GROUPS, tm), F32)
    gval = gscore
    for _ in range(TOPK_GROUPS):
        _, _, pick = _first_argmax_rows(gval, growf, N_GROUPS)
        gsel = jnp.where(pick, 1.0, gsel)
        gval = jnp.where(pick, neg, gval)
    emask = jnp.broadcast_to(gsel.reshape(N_GROUPS, 1, tm), (N_GROUPS, GROUP_SIZE, tm)).reshape(N_EXPERTS, tm)

    rowf = lax.broadcasted_iota(I32, (N_EXPERTS, tm), 0).astype(F32)
    val = jnp.where(emask > 0.0, biased, neg)
    onehot = jnp.zeros((N_EXPERTS, tm), F32)
    picks, idxs, ws = [], [], []
    for _ in range(TOP_K):
        _, first, pick = _first_argmax_rows(val, rowf, N_EXPERTS)
        picks.append(pick)
        idxs.append(first)
        ws.append(jnp.sum(jnp.where(pick, scores, 0.0), axis=0, keepdims=True))
        onehot = jnp.where(pick, 1.0, onehot)
        val = jnp.where(pick, neg, val)
    w = jnp.concatenate(ws, axis=0)
    gate_ref[...] = w / jnp.sum(w, axis=0, keepdims=True) * ROUTED_SCALE
    idx_ref[...] = jnp.concatenate(idxs, axis=0).astype(I32)

    ti = lax.broadcasted_iota(I32, (tm, tm), 0)
    tj = lax.broadcasted_iota(I32, (tm, tm), 1)
    upper = jnp.where(ti < tj, 1.0, 0.0).astype(BF16)
    prefix = jnp.dot(onehot.astype(BF16), upper, preferred_element_type=F32) + carry[...]
    ranks = [jnp.sum(jnp.where(pk, prefix, 0.0), axis=0, keepdims=True) for pk in picks]
    rank_ref[...] = jnp.concatenate(ranks, axis=0).astype(I32)
    carry[...] = carry[...] + jnp.sum(onehot, axis=1, keepdims=True)
    cnt_ref[...] = carry[...]


def _router(h, wt, bias_col):
    t, d = h.shape
    tm = ROUTE_TM
    wt_hi = wt.astype(BF16)
    wt_lo = (wt - wt_hi.astype(F32)).astype(BF16)
    return pl.pallas_call(
        _router_body,
        out_shape=(jax.ShapeDtypeStruct((TOP_K, t), I32), jax.ShapeDtypeStruct((TOP_K, t), F32),
                   jax.ShapeDtypeStruct((TOP_K, t), I32), jax.ShapeDtypeStruct((N_EXPERTS, 1), F32)),
        grid=(t // tm,),
        in_specs=[pl.BlockSpec((tm, d), lambda i: (i, 0)),
                  pl.BlockSpec((N_EXPERTS, d), lambda i: (0, 0)),
                  pl.BlockSpec((N_EXPERTS, d), lambda i: (0, 0)),
                  pl.BlockSpec((N_EXPERTS, 1), lambda i: (0, 0))],
        out_specs=(pl.BlockSpec((TOP_K, tm), lambda i: (0, i)),
                   pl.BlockSpec((TOP_K, tm), lambda i: (0, i)),
                   pl.BlockSpec((TOP_K, tm), lambda i: (0, i)),
                   pl.BlockSpec((N_EXPERTS, 1), lambda i: (0, 0))),
        scratch_shapes=[pltpu.VMEM((N_EXPERTS, 1), F32)],
        compiler_params=_cparams(("arbitrary",)),
        name="router",
    )(h, wt_hi, wt_lo, bias_col)


def _positions_body(idx_ref, rank_ref, start_ref, pos_ref):
    tm = idx_ref.shape[1]
    rowi = lax.broadcasted_iota(I32, (N_EXPERTS, tm), 0)
    start = start_ref[...]
    idx = idx_ref[...]
    rows = [jnp.sum(jnp.where(rowi == idx[k:k + 1, :], start, 0.0), axis=0, keepdims=True) for k in range(TOP_K)]
    pos_ref[...] = jnp.concatenate(rows, axis=0).astype(I32) + rank_ref[...]


def _positions(idx_t, rank_t, start_col):
    t = idx_t.shape[1]
    tm = POS_TM
    return pl.pallas_call(
        _positions_body,
        out_shape=jax.ShapeDtypeStruct((TOP_K, t), I32),
        grid=(t // tm,),
        in_specs=[pl.BlockSpec((TOP_K, tm), lambda i: (0, i)),
                  pl.BlockSpec((TOP_K, tm), lambda i: (0, i)),
                  pl.BlockSpec((N_EXPERTS, 1), lambda i: (0, 0))],
        out_specs=pl.BlockSpec((TOP_K, tm), lambda i: (0, i)),
        compiler_params=_cparams(("arbitrary",)),
        name="positions",
    )(idx_t, rank_t, start_col)


def _pad_fill_copy(zeros, xs_ref, sem, row, nrows):
    return pltpu.make_async_copy(_rows(zeros, 0, nrows), _rows(xs_ref, row, nrows), sem)


def _dispatch_body(pad_row_ref, pad_n_ref, nu_ref, pos_ref, h4_ref, xs_ref, zeros, sem, pad_sem):
    i = pl.program_id(0)
    tm = h4_ref.shape[0] // ROW_WORDS
    half = MOE_BM // 2
    pad_bits = [1 << j for j in range(MOE_BM.bit_length() - 1)]
    n_half_blocks = xs_ref.shape[0] // (half * ROW_WORDS)

    def pad_pass(wait):
        def go(cp):
            if wait:
                cp.wait()
            else:
                cp.start()

        def body(e, carry):
            row = pad_row_ref[e]
            n = pad_n_ref[e]
            for bit in pad_bits:
                @pl.when((n & bit) != 0)
                def _():
                    go(_pad_fill_copy(zeros, xs_ref, pad_sem, row + (n & (bit - 1)), bit))
            return carry
        lax.fori_loop(0, N_EXPERTS, body, 0)

        def tail(hb, carry):
            go(_pad_fill_copy(zeros, xs_ref, pad_sem, hb * half, half))
            return carry
        lax.fori_loop(nu_ref[0] * 2, n_half_blocks, tail, 0)

    @pl.when(i == 0)
    def _():
        zeros[...] = jnp.zeros(zeros.shape, U32)
        pad_pass(False)

    def body(tt, carry):
        for dt in range(DISP_UNROLL):
            t = tt * DISP_UNROLL + dt
            for k in range(TOP_K):
                p = pos_ref[0, 0, t * TOP_K + k]
                pltpu.make_async_copy(_rows(h4_ref, t, 1), _rows(xs_ref, p, 1), sem).start(priority=k % 2)
        return carry
    lax.fori_loop(0, tm // DISP_UNROLL, body, 0)

    for k in range(TOP_K):
        pltpu.make_async_copy(h4_ref, _rows(xs_ref, 0, tm), sem).wait()

    @pl.when(i == pl.num_programs(0) - 1)
    def _():
        pad_pass(True)


def _dispatch(pad_row, pad_n, n_used, pos3, h4, n_rows):
    nt, _, per_step = pos3.shape
    tm = per_step // TOP_K
    grid_spec = pltpu.PrefetchScalarGridSpec(
        num_scalar_prefetch=3,
        grid=(nt,),
        in_specs=[pl.BlockSpec((1, 1, per_step), lambda i, a, b, c: (i, 0, 0), memory_space=pltpu.SMEM),
                  pl.BlockSpec((tm * ROW_WORDS, LANES), lambda i, a, b, c: (i, 0))],
        out_specs=pl.BlockSpec(memory_space=pl.ANY),
        scratch_shapes=[pltpu.VMEM((MOE_BM // 2 * ROW_WORDS, LANES), U32),
                        pltpu.SemaphoreType.DMA, pltpu.SemaphoreType.DMA],
    )
    return pl.pallas_call(
        _dispatch_body,
        out_shape=jax.ShapeDtypeStruct((n_rows * ROW_WORDS, LANES), U32),
        grid_spec=grid_spec,
        compiler_params=_cparams(("arbitrary",)),
        name="dispatch",
    )(pad_row, pad_n, n_used, pos3, h4)


def _weight_copies(e, w_hbm, stage, sem, slot):
    return [pltpu.make_async_copy(w.at[e], st.at[slot], sem.at[slot, j])
            for j, (w, st) in enumerate(zip(w_hbm, stage))]


def _row_block_copy(xs_hbm, xbuf, xsem, j):
    slot = j % X_RING
    return pltpu.make_async_copy(_rows(xs_hbm, j * MOE_BM, MOE_BM, align=MOE_BM), xbuf.at[slot], xsem.at[slot])


def _experts_body(be_ref, nu_ref, nx_ref, ord_ref, xs_hbm, wg_hbm, wu_hbm, wd_hbm, y4_ref,
                  xbuf, xsem, sg, su, sd, wsem, wg_b, wu_b, wd_b):
    w_hbm = (wg_hbm, wu_hbm, wd_hbm)
    stage = (sg, su, sd)
    n_used = nu_ref[0]

    def do_block(i, out_row):
        @pl.when(i < n_used)
        def _():
            e = be_ref[i]
            prev_e = be_ref[jnp.maximum(i - 1, 0)]

            @pl.when(i == 0)
            def _():
                for j in range(X_RING - 1):
                    @pl.when(j < n_used)
                    def _():
                        _row_block_copy(xs_hbm, xbuf, xsem, j).start()

            @pl.when(i + X_RING - 1 < n_used)
            def _():
                _row_block_copy(xs_hbm, xbuf, xsem, i + X_RING - 1).start()

            @pl.when((i == 0) | (e != prev_e))
            def _():
                wslot = ord_ref[e] % 2
                n1 = nx_ref[e]
                n2 = jnp.where(n1 >= 0, nx_ref[jnp.maximum(n1, 0)], -1)

                @pl.when(i == 0)
                def _():
                    for cp in _weight_copies(e, w_hbm, stage, wsem, wslot):
                        cp.start()

                    @pl.when(n1 >= 0)
                    def _():
                        for cp in _weight_copies(n1, w_hbm, stage, wsem, 1 - wslot):
                            cp.start()
                for cp in _weight_copies(e, w_hbm, stage, wsem, wslot):
                    cp.wait()
                wg_b[...] = sg[wslot].astype(BF16)
                wu_b[...] = su[wslot].astype(BF16)
                wd_b[...] = sd[wslot].astype(BF16)

                @pl.when(n2 >= 0)
                def _():
                    for cp in _weight_copies(n2, w_hbm, stage, wsem, wslot):
                        cp.start()

            _row_block_copy(xs_hbm, xbuf, xsem, i).wait()
            slot = i % X_RING
            parts = [_unpack_row_words(xbuf[slot, _word_plane(0, MOE_BM, s), :]) for s in range(ROW_WORDS)]
            x = jnp.concatenate([p[0] for p in parts] + [p[1] for p in parts], axis=1).astype(BF16)
            g = jnp.dot(x, wg_b[...], preferred_element_type=F32)
            u = jnp.dot(x, wu_b[...], preferred_element_type=F32)
            a = (_silu(g) * u).astype(BF16)
            y = jnp.dot(a, wd_b[...], preferred_element_type=F32)
            for s, w in enumerate(_pack_row_words(y)):
                y4_ref[_word_plane(out_row, MOE_BM, s), :] = w

        @pl.when(i >= n_used)
        def _():
            y4_ref[pl.ds(out_row * ROW_WORDS, MOE_BM * ROW_WORDS), :] = jnp.zeros((MOE_BM * ROW_WORDS, LANES), U32)

    for sub in range(EXP_BLOCKS_PER_STEP):
        do_block(pl.program_id(0) * EXP_BLOCKS_PER_STEP + sub, sub * MOE_BM)


def _experts(block_e, n_used, next_e, ord_e, xs, wg, wu, wd):
    nblk = block_e.shape[0]
    bm = MOE_BM * EXP_BLOCKS_PER_STEP
    e, d, de = wg.shape

    grid_spec = pltpu.PrefetchScalarGridSpec(
        num_scalar_prefetch=4,
        grid=(nblk // EXP_BLOCKS_PER_STEP,),
        in_specs=[pl.BlockSpec(memory_space=pl.ANY),
                  pl.BlockSpec(memory_space=pl.ANY),
                  pl.BlockSpec(memory_space=pl.ANY),
                  pl.BlockSpec(memory_space=pl.ANY)],
        out_specs=pl.BlockSpec((bm * ROW_WORDS, LANES), lambda i, be, nu, nx, od: (i, 0)),
        scratch_shapes=[pltpu.VMEM((X_RING, MOE_BM * ROW_WORDS, LANES), U32), pltpu.SemaphoreType.DMA((X_RING,)),
                        pltpu.VMEM((2, d, de), F32), pltpu.VMEM((2, d, de), F32), pltpu.VMEM((2, de, d), F32),
                        pltpu.SemaphoreType.DMA((2, 3)),
                        pltpu.VMEM((d, de), BF16), pltpu.VMEM((d, de), BF16), pltpu.VMEM((de, d), BF16)],
    )
    return pl.pallas_call(
        _experts_body,
        out_shape=jax.ShapeDtypeStruct((nblk * MOE_BM * ROW_WORDS, LANES), U32),
        grid_spec=grid_spec,
        compiler_params=_cparams(("arbitrary",)),
        name="experts",
    )(block_e, n_used, next_e, ord_e, xs, wg, wu, wd)


def _combine_body(pos_cur, pos_nxt, y4_ref, h_ref, gate_ref, wsg_ref, wsu_ref, wsd_ref, g_ref, b_ref,
                  o_ref, buf, sem):
    i = pl.program_id(0)
    nb = pl.num_programs(0)
    tm, d = h_ref.shape
    slot = i % 2

    def issue(pos_ref, sl):
        def body(t, carry):
            for k in range(TOP_K):
                p = pos_ref[0, 0, t * TOP_K + k]
                pltpu.make_async_copy(_rows(y4_ref, p, 1), _rows(buf.at[sl], k * tm + t, 1),
                                      sem.at[sl]).start(priority=k % 2)
            return carry
        lax.fori_loop(0, tm, body, 0)

    @pl.when(i == 0)
    def _():
        issue(pos_cur, 0)

    @pl.when(i + 1 < nb)
    def _():
        issue(pos_nxt, 1 - slot)

    h = h_ref[...]
    hb = h.astype(BF16)
    sg = jnp.dot(hb, wsg_ref[...], preferred_element_type=F32)
    su = jnp.dot(hb, wsu_ref[...], preferred_element_type=F32)
    shared = jnp.dot((_silu(sg) * su).astype(BF16), wsd_ref[...], preferred_element_type=F32)

    pltpu.make_async_copy(_rows(y4_ref, 0, tm * TOP_K), buf.at[slot], sem.at[slot]).wait()
    gates = gate_ref[...]
    lo_cols, hi_cols = [], []
    for s in range(ROW_WORDS):
        lo_acc = hi_acc = None
        for k in range(TOP_K):
            lo, hi = _unpack_row_words(buf[slot, _word_plane(k * tm, tm, s), :])
            gk = gates[:, k:k + 1]
            lo_acc = gk * lo if lo_acc is None else lo_acc + gk * lo
            hi_acc = gk * hi if hi_acc is None else hi_acc + gk * hi
        lo_cols.append(lo_acc)
        hi_cols.append(hi_acc)
    routed = jnp.concatenate(lo_cols + hi_cols, axis=1)
    o_ref[...] = _layernorm(DEEPNORM_ALPHA * h + (routed + shared), g_ref[...], b_ref[...])


def _combine(pos3, y4, h, gates, wsg, wsu, wsd, g, b):
    t, d = h.shape
    tm = COMB_TM
    nt = t // tm
    ds_ = wsg.shape[1]
    return pl.pallas_call(
        _combine_body,
        out_shape=jax.ShapeDtypeStruct((t, d), F32),
        grid=(nt,),
        in_specs=[pl.BlockSpec((1, 1, tm * TOP_K), lambda i: (i, 0, 0), memory_space=pltpu.SMEM),
                  pl.BlockSpec((1, 1, tm * TOP_K), lambda i: (jnp.minimum(i + 1, nt - 1), 0, 0),
                               memory_space=pltpu.SMEM),
                  pl.BlockSpec(memory_space=pl.ANY),
                  pl.BlockSpec((tm, d), lambda i: (i, 0)),
                  pl.BlockSpec((tm, TOP_K), lambda i: (i, 0)),
                  pl.BlockSpec((d, ds_), lambda i: (0, 0)),
                  pl.BlockSpec((d, ds_), lambda i: (0, 0)),
                  pl.BlockSpec((ds_, d), lambda i: (0, 0)),
                  pl.BlockSpec((1, d), lambda i: (0, 0)),
                  pl.BlockSpec((1, d), lambda i: (0, 0))],
        out_specs=pl.BlockSpec((tm, d), lambda i: (i, 0)),
        scratch_shapes=[pltpu.VMEM((2, tm * TOP_K * ROW_WORDS, LANES), U32),
                        pltpu.SemaphoreType.DMA((2,))],
        compiler_params=_cparams(("arbitrary",)),
        name="combine",
    )(pos3, pos3, y4, h, gates, wsg, wsu, wsd, g, b)


def _expert_tables(counts, nblk):
    bm = MOE_BM
    cnt = counts.reshape(N_EXPERTS).astype(I32)
    padded = (cnt + bm - 1) // bm * bm
    padded_end = jnp.cumsum(padded)
    padded_start = padded_end - padded
    block_rows = jnp.arange(nblk, dtype=I32) * bm
    block_e = jnp.sum((padded_end[None, :] <= block_rows[:, None]).astype(I32), axis=1)
    block_e = jnp.minimum(block_e, N_EXPERTS - 1)
    n_used = (padded_end[-1:] // bm).astype(I32)
    ids = jnp.where(cnt > 0, jnp.arange(N_EXPERTS, dtype=I32), N_EXPERTS)
    after = jnp.concatenate([lax.cummin(ids, reverse=True)[1:], jnp.full((1,), N_EXPERTS, I32)])
    next_e = jnp.where(after < N_EXPERTS, after, -1).astype(I32)
    ord_e = (jnp.cumsum((cnt > 0).astype(I32)) - 1).astype(I32)
    return padded_start, padded_start + cnt, padded - cnt, block_e, n_used, next_e, ord_e


def kernel(x, w_in, gla_gate_w2, gla_gate_b, gla_norm_w, pool_w_group, pool_scale, w_out, ln1_g, ln1_b,
           router_w, router_bias, w_exp_gate, w_exp_up, w_exp_down, w_sh_gate, w_sh_up, w_sh_down, ln2_g, ln2_b):
    batch, seq, d = x.shape
    t = batch * seq
    h2d = x.reshape(t, d)
    for l in range(DEPTH):
        d_in = w_in.shape[2]
        w_in_b = jnp.pad(w_in[l], ((0, 0), (0, D_IN_PAD - d_in))).astype(BF16)
        w2p = jnp.pad(gla_gate_w2[l], ((0, LANES - GLA_GATE_RANK), (0, 0))).astype(BF16)
        h, h4 = _front(h2d, batch, seq, w_in_b, w2p, gla_gate_b[l].reshape(1, -1), gla_norm_w[l].reshape(1, -1),
                       pool_w_group[l].astype(BF16), pool_scale[l].reshape(1, -1),
                       w_out[l].astype(BF16), ln1_g[l].reshape(1, -1), ln1_b[l].reshape(1, -1))
        idx_t, gate_t, rank_t, counts = _router(h, router_w[l].T, router_bias[l].reshape(-1, 1))
        nblk = (t * TOP_K + N_EXPERTS * (MOE_BM - 1)) // MOE_BM
        nblk = -(-nblk // EXP_BLOCKS_PER_STEP) * EXP_BLOCKS_PER_STEP
        start, pad_row, pad_n, block_e, n_used, next_e, ord_e = _expert_tables(counts, nblk)
        pos_t = _positions(idx_t, rank_t, start.astype(F32).reshape(-1, 1))
        pos_tok = pos_t.T
        xs = _dispatch(pad_row, pad_n, n_used, pos_tok.reshape(t // DISP_TM, 1, DISP_TM * TOP_K), h4, nblk * MOE_BM)
        y4 = _experts(block_e, n_used, next_e, ord_e, xs, w_exp_gate[l], w_exp_up[l], w_exp_down[l])
        h2d = _combine(pos_tok.reshape(t // COMB_TM, 1, COMB_TM * TOP_K), y4, h, gate_t.T,
                       w_sh_gate[l].astype(BF16), w_sh_up[l].astype(BF16), w_sh_down[l].astype(BF16),
                       ln2_g[l].reshape(1, -1), ln2_b[l].reshape(1, -1))
    return h2d.reshape(batch, seq, d)
```

```python
import jax
import jax.numpy as jnp
from jax import lax
from jax.experimental import pallas as pl
from jax.experimental.pallas import tpu as pltpu

F32 = jnp.float32
BF16 = jnp.bfloat16
I32 = jnp.int32
U32 = jnp.uint32
HIGH_HALF = 0xFFFF0000

POOL_WINDOWS = (2, 4, 8, 16)
POOL_GROUP_DIM = 128
POOL_WIDTH = 512
GLA_HEADS = 4
GLA_DK = 64
GLA_DV = 128
GLA_DK_TOTAL = 256
GLA_WIDTH = 512
GLA_GATE_RANK = 16
GLA_GATE_NORMALIZER = 16.0
GLA_CHUNK = 16
GLA_SAFE_EXP = 60.0
N_EXPERTS = 256
TOP_K = 8
N_GROUPS = 8
GROUP_SIZE = N_EXPERTS // N_GROUPS
TOPK_GROUPS = 4
ROUTED_SCALE = 2.5
DEPTH = 1
DEEPNORM_ALPHA = (2.0 * DEPTH) ** 0.25
LN_EPS = 1e-5
RMS_EPS = 1e-5

LANES = 128
SUBLANES = 8
VMEM_LIMIT = 56 * 1024 * 1024

MIX_TS = 512
ROUTE_TM = 512
MOE_BM = 256
COMB_TM = 256
POS_TM = 512
DISP_TM = 256
DISP_UNROLL = 4
ROW_WORDS = 4
X_RING = 3
EXP_BLOCKS_PER_STEP = 2
D_IN_PAD = 2176


def _cparams(sem):
    return pltpu.CompilerParams(dimension_semantics=sem, vmem_limit_bytes=VMEM_LIMIT)


def _silu(x):
    return x * (1.0 / (1.0 + jnp.exp(-x)))


def _layernorm(y, g, b):
    mu = jnp.mean(y, axis=-1, keepdims=True)
    yc = y - mu
    var = jnp.mean(yc * yc, axis=-1, keepdims=True)
    return yc * lax.rsqrt(var + LN_EPS) * g + b


def _mixer_body(p_ref, q_ref, k_ref, v_ref, r_ref, gl_ref, w2_ref, gb_ref, nw_ref, pw_ref, ps_ref,
                o_ref, pbuf, state, kvbuf, sall, obuf, gk_s):
    ts = p_ref.shape[0]
    s_idx = pl.program_id(1)
    halo = POOL_WINDOWS[-1]

    @pl.when(s_idx == 0)
    def _():
        pbuf[pl.ds(0, halo), :] = jnp.zeros((halo, POOL_WIDTH), F32)
        state[...] = jnp.zeros(state.shape, F32)

    p = p_ref[...]
    pbuf[pl.ds(halo, ts), :] = p
    pos = s_idx * ts + lax.broadcasted_iota(I32, (ts, 1), 0)
    for g, w in enumerate(POOL_WINDOWS):
        c0 = g * POOL_GROUP_DIM
        ext = pbuf[:, pl.ds(c0, POOL_GROUP_DIM)]
        sh = 1
        while sh < w:
            ext = ext + pltpu.roll(ext, sh, axis=0)
            sh *= 2
        acc = ext[halo:, :]
        cnt = jnp.minimum(pos + 1, w).astype(F32)
        mixed = acc / cnt - p[:, c0:c0 + POOL_GROUP_DIM]
        og = jnp.dot(mixed.astype(BF16), pw_ref[g], preferred_element_type=F32)
        o_ref[:, pl.ds(c0, POOL_GROUP_DIM)] = (og * ps_ref[:, pl.ds(c0, POOL_GROUP_DIM)]).astype(o_ref.dtype)
    pbuf[pl.ds(0, halo), :] = pbuf[pl.ds(ts, halo), :]

    nchunk = ts // GLA_CHUNK
    glog = jnp.dot(gl_ref[...].astype(BF16), w2_ref[...], preferred_element_type=F32) + gb_ref[...]
    gk = (jnp.minimum(glog, 0.0) - jnp.log(1.0 + jnp.exp(-jnp.abs(glog)))) * (1.0 / GLA_GATE_NORMALIZER)
    row = lax.broadcasted_iota(I32, (ts, 1), 0)
    rin = row % GLA_CHUNK
    b = gk
    sh = 1
    while sh < GLA_CHUNK:
        b = b + jnp.where(rin >= sh, pltpu.roll(b, sh, axis=0), 0.0)
        sh *= 2
    b3 = b.reshape(nchunk, GLA_CHUNK, GLA_DK_TOTAL)
    bmid = b3[:, GLA_CHUNK // 2 - 1:GLA_CHUNK // 2, :]
    blast = b3[:, GLA_CHUNK - 1:GLA_CHUNK, :]
    lane = lax.broadcasted_iota(I32, (1, LANES), 1)
    head_lane = [lane < GLA_DK, lane >= GLA_DK]
    q_scale = GLA_DK ** -0.5
    safe = jnp.max(jnp.abs(b3 - bmid)) <= GLA_SAFE_EXP

    @pl.when(safe)
    def _chunked():
        v = v_ref[...]
        vb = v.astype(BF16)
        q3 = (q_ref[...] * q_scale).reshape(nchunk, GLA_CHUNK, GLA_DK_TOTAL)
        k3 = k_ref[...].reshape(nchunk, GLA_CHUNK, GLA_DK_TOTAL)
        qs = (q3 * jnp.exp(b3 - bmid)).reshape(ts, GLA_DK_TOTAL)
        ks = (k3 * jnp.exp(bmid - b3)).reshape(ts, GLA_DK_TOTAL)
        qd = (q3 * jnp.exp(b3)).reshape(ts, GLA_DK_TOTAL)
        kd = (k3 * jnp.exp(blast - b3)).reshape(ts, GLA_DK_TOTAL)
        cdec = jnp.exp(blast).reshape(nchunk, GLA_DK_TOTAL)

        blk = LANES
        ri = lax.broadcasted_iota(I32, (blk, blk), 0)
        ci = lax.broadcasted_iota(I32, (blk, blk), 1)
        causal = (ri // GLA_CHUNK == ci // GLA_CHUNK) & (ri >= ci)
        o_intra = [[None] * (ts // blk) for _ in range(GLA_HEADS)]
        for rb in range(ts // blk):
            rs = slice(rb * blk, (rb + 1) * blk)
            for pair in range(GLA_HEADS // 2):
                ls = slice(pair * LANES, (pair + 1) * LANES)
                ks_p = ks[rs, ls].astype(BF16)
                q_p = qs[rs, ls]
                q2 = jnp.concatenate([jnp.where(head_lane[sub], q_p, 0.0) for sub in range(2)], axis=0).astype(BF16)
                sc2 = lax.dot_general(q2, ks_p, (((1,), (1,)), ((), ())), preferred_element_type=F32)
                for sub in range(2):
                    h = pair * 2 + sub
                    sc = jnp.where(causal, sc2[sub * blk:(sub + 1) * blk], 0.0).astype(BF16)
                    o_intra[h][rb] = jnp.dot(sc, vb[rs, h * GLA_DV:(h + 1) * GLA_DV], preferred_element_type=F32)

        cpb = blk // GLA_CHUNK
        chunk_of_col = lax.broadcasted_iota(I32, (1, blk), 1) // GLA_CHUNK
        kdb = kd.astype(BF16)
        for rb in range(ts // blk):
            rs = slice(rb * blk, (rb + 1) * blk)
            for pair in range(GLA_HEADS // 2):
                ls = slice(pair * LANES, (pair + 1) * LANES)
                inc = []
                for sub in range(2):
                    h = pair * 2 + sub
                    v_t = v[rs, h * GLA_DV:(h + 1) * GLA_DV].T
                    lhs = jnp.concatenate([jnp.where(chunk_of_col == c, v_t, 0.0) for c in range(cpb)],
                                          axis=0).astype(BF16)
                    inc.append(jnp.dot(lhs, kdb[rs, ls], preferred_element_type=F32))
                stacked = jnp.where(head_lane[0], inc[0], inc[1])
                for c in range(cpb):
                    kvbuf[pair, rb * cpb + c] = stacked[c * LANES:(c + 1) * LANES]
        for pair in range(GLA_HEADS // 2):
            ls = slice(pair * LANES, (pair + 1) * LANES)
            st = state[pair]
            for c in range(nchunk):
                sall[pair, c] = st.astype(BF16)
                st = st * cdec[c:c + 1, ls] + kvbuf[pair, c]
            state[pair] = st
        o_inter = [[None] * nchunk for _ in range(GLA_HEADS)]
        for pair in range(GLA_HEADS // 2):
            ls = slice(pair * LANES, (pair + 1) * LANES)
            for c in range(nchunk):
                rs = slice(c * GLA_CHUNK, (c + 1) * GLA_CHUNK)
                q_c = qd[rs, ls]
                q2 = jnp.concatenate([jnp.where(head_lane[sub], q_c, 0.0) for sub in range(2)], axis=0).astype(BF16)
                res = lax.dot_general(q2, sall[pair, c], (((1,), (1,)), ((), ())), preferred_element_type=F32)
                for sub in range(2):
                    o_inter[pair * 2 + sub][c] = res[sub * GLA_CHUNK:(sub + 1) * GLA_CHUNK]
        for h in range(GLA_HEADS):
            obuf[:, pl.ds(h * GLA_DV, GLA_DV)] = (jnp.concatenate(o_intra[h], axis=0)
                                                  + jnp.concatenate(o_inter[h], axis=0))

    @pl.when(jnp.logical_not(safe))
    def _row_by_row():
        gk_s[...] = gk
        row8 = lax.broadcasted_iota(I32, (SUBLANES, 1), 0)

        def slab(i8, carry):
            r0 = pl.multiple_of(i8 * SUBLANES, SUBLANES)
            q8 = q_ref[pl.ds(r0, SUBLANES), :] * q_scale
            k8 = k_ref[pl.ds(r0, SUBLANES), :]
            v8 = v_ref[pl.ds(r0, SUBLANES), :]
            g8 = jnp.exp(gk_s[pl.ds(r0, SUBLANES), :])
            outs = [jnp.zeros((SUBLANES, GLA_DV), F32) for _ in range(GLA_HEADS)]
            for pair in range(GLA_HEADS // 2):
                ls = slice(pair * LANES, (pair + 1) * LANES)
                st = state[pair]
                for r in range(SUBLANES):
                    sel = row8 == r
                    k_r = jnp.where(sel, k8[:, ls], 0.0).astype(BF16)
                    inc = []
                    for sub in range(2):
                        h = pair * 2 + sub
                        v_r = jnp.where(sel, v8[:, h * GLA_DV:(h + 1) * GLA_DV], 0.0).astype(BF16)
                        inc.append(lax.dot_general(v_r, k_r, (((0,), (0,)), ((), ())), preferred_element_type=F32))
                    st = st * g8[r:r + 1, ls] + jnp.where(head_lane[0], inc[0], inc[1])
                    q2 = jnp.concatenate([jnp.where(sel & head_lane[sub], q8[:, ls], 0.0) for sub in range(2)],
                                         axis=0).astype(BF16)
                    res = lax.dot_general(q2, st.astype(BF16), (((1,), (1,)), ((), ())),
                                          preferred_element_type=F32)
                    for sub in range(2):
                        outs[pair * 2 + sub] = outs[pair * 2 + sub] + res[sub * SUBLANES:(sub + 1) * SUBLANES]
                state[pair] = st
            obuf[pl.ds(r0, SUBLANES), :] = jnp.concatenate(outs, axis=1)
            return carry
        lax.fori_loop(0, ts // SUBLANES, slab, 0)

    nw = nw_ref[...]
    r = r_ref[...]
    for h in range(GLA_HEADS):
        o = obuf[:, pl.ds(h * GLA_DV, GLA_DV)]
        o = o * lax.rsqrt(jnp.mean(o * o, axis=-1, keepdims=True) + RMS_EPS) * nw
        o = o * _silu(r[:, h * GLA_DV:(h + 1) * GLA_DV])
        o_ref[:, pl.ds(POOL_WIDTH + h * GLA_DV, GLA_DV)] = o.astype(o_ref.dtype)


def _front_body(x_ref, win_ref, w2_ref, gb_ref, nw_ref, pw_ref, ps_ref, wout_ref, g_ref, b_ref, h_ref, h4_ref,
                pp_s, pq_s, pk_s, pv_s, pr_s, pg_s, mix_s, pbuf, state, kvbuf, sall, obuf, gk_s):
    groups = (pp_s, pq_s, pk_s, pv_s, pr_s, pg_s)
    cols = [0]
    for ref in groups:
        cols.append(cols[-1] + ref.shape[1])
    xb = x_ref[...].astype(BF16)
    for gi in (0, 5, 1, 2, 3, 4):
        groups[gi][...] = jnp.dot(xb, win_ref[:, pl.ds(cols[gi], groups[gi].shape[1])],
                                  preferred_element_type=F32)
    _mixer_body(*groups, w2_ref, gb_ref, nw_ref, pw_ref, ps_ref, mix_s, pbuf, state, kvbuf, sall, obuf, gk_s)
    y = DEEPNORM_ALPHA * x_ref[...] + jnp.dot(mix_s[...], wout_ref[...], preferred_element_type=F32)
    h = _layernorm(y, g_ref[...], b_ref[...])
    h_ref[...] = h
    for s, w in enumerate(_pack_row_words(h)):
        h4_ref[_word_plane(0, h.shape[0], s), :] = w


def _front(x2d, batch, seq, w_in_b, w2p, gate_b, norm_w, pool_w, pool_scale, w_out_b, ln_g, ln_b):
    t, d = x2d.shape
    ts = MIX_TS
    nseq = seq // ts

    def full(shape):
        return pl.BlockSpec(shape, lambda bi, si: (0,) * len(shape))

    return pl.pallas_call(
        _front_body,
        out_shape=(jax.ShapeDtypeStruct((t, d), F32), jax.ShapeDtypeStruct((t * ROW_WORDS, LANES), U32)),
        grid=(batch, nseq),
        in_specs=[pl.BlockSpec((ts, d), lambda bi, si: (bi * nseq + si, 0)),
                  full(w_in_b.shape), full(w2p.shape), full(gate_b.shape), full(norm_w.shape),
                  full(pool_w.shape), full(pool_scale.shape), full(w_out_b.shape), full(ln_g.shape), full(ln_b.shape)],
        out_specs=(pl.BlockSpec((ts, d), lambda bi, si: (bi * nseq + si, 0)),
                   pl.BlockSpec((ts * ROW_WORDS, LANES), lambda bi, si: (bi * nseq + si, 0))),
        scratch_shapes=[pltpu.VMEM((ts, POOL_WIDTH), F32), pltpu.VMEM((ts, GLA_DK_TOTAL), F32),
                        pltpu.VMEM((ts, GLA_DK_TOTAL), F32), pltpu.VMEM((ts, GLA_WIDTH), F32),
                        pltpu.VMEM((ts, GLA_WIDTH), F32), pltpu.VMEM((ts, D_IN_PAD - 2 * GLA_DK_TOTAL
                                                                      - 2 * GLA_WIDTH - POOL_WIDTH), F32),
                        pltpu.VMEM((ts, POOL_WIDTH + GLA_WIDTH), BF16),
                        pltpu.VMEM((ts + POOL_WINDOWS[-1], POOL_WIDTH), F32),
                        pltpu.VMEM((GLA_HEADS // 2, LANES, GLA_DV), F32),
                        pltpu.VMEM((GLA_HEADS // 2, ts // GLA_CHUNK, LANES, GLA_DV), F32),
                        pltpu.VMEM((GLA_HEADS // 2, ts // GLA_CHUNK, LANES, GLA_DV), BF16),
                        pltpu.VMEM((ts, GLA_WIDTH), F32), pltpu.VMEM((ts, GLA_DK_TOTAL), F32)],
        compiler_params=_cparams(("arbitrary", "arbitrary")),
        name="front",
    )(x2d, w_in_b, w2p, gate_b, norm_w, pool_w, pool_scale, w_out_b, ln_g, ln_b)


def _pack_row_words(x):
    half = x.shape[1] // 2
    u = pltpu.bitcast(x.astype(BF16).astype(F32), U32)
    hi_mask = jnp.uint32(HIGH_HALF)
    return [(u[:, half + s * LANES:half + (s + 1) * LANES] & hi_mask) | (u[:, s * LANES:(s + 1) * LANES] >> 16)
            for s in range(ROW_WORDS)]


def _unpack_row_words(w):
    return pltpu.bitcast(w << 16, F32), pltpu.bitcast(w & jnp.uint32(HIGH_HALF), F32)


def _rows(ref, first, n, align=1):
    if isinstance(first, int):
        start = first * ROW_WORDS
    else:
        start = pl.multiple_of(first * ROW_WORDS, ROW_WORDS * align)
    return ref.at[pl.ds(start, n * ROW_WORDS), :]


def _word_plane(first, m, s):
    return pl.ds(first * ROW_WORDS + s, m, stride=ROW_WORDS)


def _first_argmax_rows(val, rowf, nrows):
    m = jnp.max(val, axis=0, keepdims=True)
    first = jnp.min(jnp.where(val == m, rowf, float(nrows)), axis=0, keepdims=True)
    return m, first, rowf == first


def _router_body(h_ref, whi_ref, wlo_ref, bias_ref, idx_ref, gate_ref, rank_ref, cnt_ref, carry):
    tm = h_ref.shape[0]
    i = pl.program_id(0)

    @pl.when(i == 0)
    def _():
        carry[...] = jnp.zeros(carry.shape, F32)

    h = h_ref[...]
    h_hi = h.astype(BF16)
    h_lo = (h - h_hi.astype(F32)).astype(BF16)
    nt = (((1,), (1,)), ((), ()))
    logits = (lax.dot_general(whi_ref[...], h_hi, nt, preferred_element_type=F32)
              + lax.dot_general(whi_ref[...], h_lo, nt, preferred_element_type=F32)
              + lax.dot_general(wlo_ref[...], h_hi, nt, preferred_element_type=F32))
    scores = 1.0 / (1.0 + jnp.exp(-logits))
    biased = scores + bias_ref[...]
    neg = -jnp.inf

    grp = biased.reshape(N_GROUPS, GROUP_SIZE, tm)
    gi = lax.broadcasted_iota(I32, (N_GROUPS, GROUP_SIZE, tm), 1).astype(F32)
    g1 = jnp.max(grp, axis=1, keepdims=True)
    f1 = jnp.min(jnp.where(grp == g1, gi, float(GROUP_SIZE)), axis=1, keepdims=True)
    g2 = jnp.max(jnp.where(gi == f1, neg, grp), axis=1, keepdims=True)
    gscore = (g1 + g2).reshape(N_GROUPS, tm)

    growf = lax.broadcasted_iota(I32, (N_GROUPS, tm), 0).astype(F32)
    gsel = jnp.zeros((N_GROUPS, tm), F32)
    gval = gscore
    for _ in range(TOPK_GROUPS):
        _, _, pick = _first_argmax_rows(gval, growf, N_GROUPS)
        gsel = jnp.where(pick, 1.0, gsel)
        gval = jnp.where(pick, neg, gval)
    emask = jnp.broadcast_to(gsel.reshape(N_GROUPS, 1, tm), (N_GROUPS, GROUP_SIZE, tm)).reshape(N_EXPERTS, tm)

    rowf = lax.broadcasted_iota(I32, (N_EXPERTS, tm), 0).astype(F32)
    val = jnp.where(emask > 0.0, biased, neg)
    onehot = jnp.zeros((N_EXPERTS, tm), F32)
    picks, idxs, ws = [], [], []
    for _ in range(TOP_K):
        _, first, pick = _first_argmax_rows(val, rowf, N_EXPERTS)
        picks.append(pick)
        idxs.append(first)
        ws.append(jnp.sum(jnp.where(pick, scores, 0.0), axis=0, keepdims=True))
        onehot = jnp.where(pick, 1.0, onehot)
        val = jnp.where(pick, neg, val)
    w = jnp.concatenate(ws, axis=0)
    gate_ref[...] = w / jnp.sum(w, axis=0, keepdims=True) * ROUTED_SCALE
    idx_ref[...] = jnp.concatenate(idxs, axis=0).astype(I32)

    ti = lax.broadcasted_iota(I32, (tm, tm), 0)
    tj = lax.broadcasted_iota(I32, (tm, tm), 1)
    upper = jnp.where(ti < tj, 1.0, 0.0).astype(BF16)
    prefix = jnp.dot(onehot.astype(BF16), upper, preferred_element_type=F32) + carry[...]
    ranks = [jnp.sum(jnp.where(pk, prefix, 0.0), axis=0, keepdims=True) for pk in picks]
    rank_ref[...] = jnp.concatenate(ranks, axis=0).astype(I32)
    carry[...] = carry[...] + jnp.sum(onehot, axis=1, keepdims=True)
    cnt_ref[...] = carry[...]


def _router(h, wt, bias_col):
    t, d = h.shape
    tm = ROUTE_TM
    wt_hi = wt.astype(BF16)
    wt_lo = (wt - wt_hi.astype(F32)).astype(BF16)
    return pl.pallas_call(
        _router_body,
        out_shape=(jax.ShapeDtypeStruct((TOP_K, t), I32), jax.ShapeDtypeStruct((TOP_K, t), F32),
                   jax.ShapeDtypeStruct((TOP_K, t), I32), jax.ShapeDtypeStruct((N_EXPERTS, 1), F32)),
        grid=(t // tm,),
        in_specs=[pl.BlockSpec((tm, d), lambda i: (i, 0)),
                  pl.BlockSpec((N_EXPERTS, d), lambda i: (0, 0)),
                  pl.BlockSpec((N_EXPERTS, d), lambda i: (0, 0)),
                  pl.BlockSpec((N_EXPERTS, 1), lambda i: (0, 0))],
        out_specs=(pl.BlockSpec((TOP_K, tm), lambda i: (0, i)),
                   pl.BlockSpec((TOP_K, tm), lambda i: (0, i)),
                   pl.BlockSpec((TOP_K, tm), lambda i: (0, i)),
                   pl.BlockSpec((N_EXPERTS, 1), lambda i: (0, 0))),
        scratch_shapes=[pltpu.VMEM((N_EXPERTS, 1), F32)],
        compiler_params=_cparams(("arbitrary",)),
        name="router",
    )(h, wt_hi, wt_lo, bias_col)


def _positions_body(idx_ref, rank_ref, start_ref, pos_ref):
    tm = idx_ref.shape[1]
    rowi = lax.broadcasted_iota(I32, (N_EXPERTS, tm), 0)
    start = start_ref[...]
    idx = idx_ref[...]
    rows = [jnp.sum(jnp.where(rowi == idx[k:k + 1, :], start, 0.0), axis=0, keepdims=True) for k in range(TOP_K)]
    pos_ref[...] = jnp.concatenate(rows, axis=0).astype(I32) + rank_ref[...]


def _positions(idx_t, rank_t, start_col):
    t = idx_t.shape[1]
    tm = POS_TM
    return pl.pallas_call(
        _positions_body,
        out_shape=jax.ShapeDtypeStruct((TOP_K, t), I32),
        grid=(t // tm,),
        in_specs=[pl.BlockSpec((TOP_K, tm), lambda i: (0, i)),
                  pl.BlockSpec((TOP_K, tm), lambda i: (0, i)),
                  pl.BlockSpec((N_EXPERTS, 1), lambda i: (0, 0))],
        out_specs=pl.BlockSpec((TOP_K, tm), lambda i: (0, i)),
        compiler_params=_cparams(("arbitrary",)),
        name="positions",
    )(idx_t, rank_t, start_col)


def _pad_fill_copy(zeros, xs_ref, sem, row, nrows):
    return pltpu.make_async_copy(_rows(zeros, 0, nrows), _rows(xs_ref, row, nrows), sem)


def _dispatch_body(pad_row_ref, pad_n_ref, nu_ref, pos_ref, h4_ref, xs_ref, zeros, sem, pad_sem):
    i = pl.program_id(0)
    tm = h4_ref.shape[0] // ROW_WORDS
    half = MOE_BM // 2
    pad_bits = [1 << j for j in range(MOE_BM.bit_length() - 1)]
    n_half_blocks = xs_ref.shape[0] // (half * ROW_WORDS)

    def pad_pass(wait):
        def go(cp):
            if wait:
                cp.wait()
            else:
                cp.start()

        def body(e, carry):
            row = pad_row_ref[e]
            n = pad_n_ref[e]
            for bit in pad_bits:
                @pl.when((n & bit) != 0)
                def _():
                    go(_pad_fill_copy(zeros, xs_ref, pad_sem, row + (n & (bit - 1)), bit))
            return carry
        lax.fori_loop(0, N_EXPERTS, body, 0)

        def tail(hb, carry):
            go(_pad_fill_copy(zeros, xs_ref, pad_sem, hb * half, half))
            return carry
        lax.fori_loop(nu_ref[0] * 2, n_half_blocks, tail, 0)

    @pl.when(i == 0)
    def _():
        zeros[...] = jnp.zeros(zeros.shape, U32)
        pad_pass(False)

    def body(tt, carry):
        for dt in range(DISP_UNROLL):
            t = tt * DISP_UNROLL + dt
            for k in range(TOP_K):
                p = pos_ref[0, 0, t * TOP_K + k]
                pltpu.make_async_copy(_rows(h4_ref, t, 1), _rows(xs_ref, p, 1), sem).start(priority=k % 2)
        return carry
    lax.fori_loop(0, tm // DISP_UNROLL, body, 0)

    for k in range(TOP_K):
        pltpu.make_async_copy(h4_ref, _rows(xs_ref, 0, tm), sem).wait()

    @pl.when(i == pl.num_programs(0) - 1)
    def _():
        pad_pass(True)


def _dispatch(pad_row, pad_n, n_used, pos3, h4, n_rows):
    nt, _, per_step = pos3.shape
    tm = per_step // TOP_K
    grid_spec = pltpu.PrefetchScalarGridSpec(
        num_scalar_prefetch=3,
        grid=(nt,),
        in_specs=[pl.BlockSpec((1, 1, per_step), lambda i, a, b, c: (i, 0, 0), memory_space=pltpu.SMEM),
                  pl.BlockSpec((tm * ROW_WORDS, LANES), lambda i, a, b, c: (i, 0))],
        out_specs=pl.BlockSpec(memory_space=pl.ANY),
        scratch_shapes=[pltpu.VMEM((MOE_BM // 2 * ROW_WORDS, LANES), U32),
                        pltpu.SemaphoreType.DMA, pltpu.SemaphoreType.DMA],
    )
    return pl.pallas_call(
        _dispatch_body,
        out_shape=jax.ShapeDtypeStruct((n_rows * ROW_WORDS, LANES), U32),
        grid_spec=grid_spec,
        compiler_params=_cparams(("arbitrary",)),
        name="dispatch",
    )(pad_row, pad_n, n_used, pos3, h4)


def _weight_copies(e, w_hbm, stage, sem, slot):
    return [pltpu.make_async_copy(w.at[e], st.at[slot], sem.at[slot, j])
            for j, (w, st) in enumerate(zip(w_hbm, stage))]


def _row_block_copy(xs_hbm, xbuf, xsem, j):
    slot = j % X_RING
    return pltpu.make_async_copy(_rows(xs_hbm, j * MOE_BM, MOE_BM, align=MOE_BM), xbuf.at[slot], xsem.at[slot])


def _experts_body(be_ref, nu_ref, nx_ref, ord_ref, xs_hbm, wg_hbm, wu_hbm, wd_hbm, y4_ref,
                  xbuf, xsem, sg, su, sd, wsem, wg_b, wu_b, wd_b):
    w_hbm = (wg_hbm, wu_hbm, wd_hbm)
    stage = (sg, su, sd)
    n_used = nu_ref[0]

    def do_block(i, out_row):
        @pl.when(i < n_used)
        def _():
            e = be_ref[i]
            prev_e = be_ref[jnp.maximum(i - 1, 0)]

            @pl.when(i == 0)
            def _():
                for j in range(X_RING - 1):
                    @pl.when(j < n_used)
                    def _():
                        _row_block_copy(xs_hbm, xbuf, xsem, j).start()

            @pl.when(i + X_RING - 1 < n_used)
            def _():
                _row_block_copy(xs_hbm, xbuf, xsem, i + X_RING - 1).start()

            @pl.when((i == 0) | (e != prev_e))
            def _():
                wslot = ord_ref[e] % 2
                n1 = nx_ref[e]
                n2 = jnp.where(n1 >= 0, nx_ref[jnp.maximum(n1, 0)], -1)

                @pl.when(i == 0)
                def _():
                    for cp in _weight_copies(e, w_hbm, stage, wsem, wslot):
                        cp.start()

                    @pl.when(n1 >= 0)
                    def _():
                        for cp in _weight_copies(n1, w_hbm, stage, wsem, 1 - wslot):
                            cp.start()
                for cp in _weight_copies(e, w_hbm, stage, wsem, wslot):
                    cp.wait()
                wg_b[...] = sg[wslot].astype(BF16)
                wu_b[...] = su[wslot].astype(BF16)
                wd_b[...] = sd[wslot].astype(BF16)

                @pl.when(n2 >= 0)
                def _():
                    for cp in _weight_copies(n2, w_hbm, stage, wsem, wslot):
                        cp.start()

            _row_block_copy(xs_hbm, xbuf, xsem, i).wait()
            slot = i % X_RING
            parts = [_unpack_row_words(xbuf[slot, _word_plane(0, MOE_BM, s), :]) for s in range(ROW_WORDS)]
            x = jnp.concatenate([p[0] for p in parts] + [p[1] for p in parts], axis=1).astype(BF16)
            g = jnp.dot(x, wg_b[...], preferred_element_type=F32)
            u = jnp.dot(x, wu_b[...], preferred_element_type=F32)
            a = (_silu(g) * u).astype(BF16)
            y = jnp.dot(a, wd_b[...], preferred_element_type=F32)
            for s, w in enumerate(_pack_row_words(y)):
                y4_ref[_word_plane(out_row, MOE_BM, s), :] = w

        @pl.when(i >= n_used)
        def _():
            y4_ref[pl.ds(out_row * ROW_WORDS, MOE_BM * ROW_WORDS), :] = jnp.zeros((MOE_BM * ROW_WORDS, LANES), U32)

    for sub in range(EXP_BLOCKS_PER_STEP):
        do_block(pl.program_id(0) * EXP_BLOCKS_PER_STEP + sub, sub * MOE_BM)


def _experts(block_e, n_used, next_e, ord_e, xs, wg, wu, wd):
    nblk = block_e.shape[0]
    bm = MOE_BM * EXP_BLOCKS_PER_STEP
    e, d, de = wg.shape

    grid_spec = pltpu.PrefetchScalarGridSpec(
        num_scalar_prefetch=4,
        grid=(nblk // EXP_BLOCKS_PER_STEP,),
        in_specs=[pl.BlockSpec(memory_space=pl.ANY),
                  pl.BlockSpec(memory_space=pl.ANY),
                  pl.BlockSpec(memory_space=pl.ANY),
                  pl.BlockSpec(memory_space=pl.ANY)],
        out_specs=pl.BlockSpec((bm * ROW_WORDS, LANES), lambda i, be, nu, nx, od: (i, 0)),
        scratch_shapes=[pltpu.VMEM((X_RING, MOE_BM * ROW_WORDS, LANES), U32), pltpu.SemaphoreType.DMA((X_RING,)),
                        pltpu.VMEM((2, d, de), F32), pltpu.VMEM((2, d, de), F32), pltpu.VMEM((2, de, d), F32),
                        pltpu.SemaphoreType.DMA((2, 3)),
                        pltpu.VMEM((d, de), BF16), pltpu.VMEM((d, de), BF16), pltpu.VMEM((de, d), BF16)],
    )
    return pl.pallas_call(
        _experts_body,
        out_shape=jax.ShapeDtypeStruct((nblk * MOE_BM * ROW_WORDS, LANES), U32),
        grid_spec=grid_spec,
        compiler_params=_cparams(("arbitrary",)),
        name="experts",
    )(block_e, n_used, next_e, ord_e, xs, wg, wu, wd)


def _combine_body(pos_cur, pos_nxt, y4_ref, h_ref, gate_ref, wsg_ref, wsu_ref, wsd_ref, g_ref, b_ref,
                  o_ref, buf, sem):
    i = pl.program_id(0)
    nb = pl.num_programs(0)
    tm, d = h_ref.shape
    slot = i % 2

    def issue(pos_ref, sl):
        def body(t, carry):
            for k in range(TOP_K):
                p = pos_ref[0, 0, t * TOP_K + k]
                pltpu.make_async_copy(_rows(y4_ref, p, 1), _rows(buf.at[sl], k * tm + t, 1),
                                      sem.at[sl]).start(priority=k % 2)
            return carry
        lax.fori_loop(0, tm, body, 0)

    @pl.when(i == 0)
    def _():
        issue(pos_cur, 0)

    @pl.when(i + 1 < nb)
    def _():
        issue(pos_nxt, 1 - slot)

    h = h_ref[...]
    hb = h.astype(BF16)
    sg = jnp.dot(hb, wsg_ref[...], preferred_element_type=F32)
    su = jnp.dot(hb, wsu_ref[...], preferred_element_type=F32)
    shared = jnp.dot((_silu(sg) * su).astype(BF16), wsd_ref[...], preferred_element_type=F32)

    pltpu.make_async_copy(_rows(y4_ref, 0, tm * TOP_K), buf.at[slot], sem.at[slot]).wait()
    gates = gate_ref[...]
    lo_cols, hi_cols = [], []
    for s in range(ROW_WORDS):
        lo_acc = hi_acc = None
        for k in range(TOP_K):
            lo, hi = _unpack_row_words(buf[slot, _word_plane(k * tm, tm, s), :])
            gk = gates[:, k:k + 1]
            lo_acc = gk * lo if lo_acc is None else lo_acc + gk * lo
            hi_acc = gk * hi if hi_acc is None else hi_acc + gk * hi
        lo_cols.append(lo_acc)
        hi_cols.append(hi_acc)
    routed = jnp.concatenate(lo_cols + hi_cols, axis=1)
    o_ref[...] = _layernorm(DEEPNORM_ALPHA * h + (routed + shared), g_ref[...], b_ref[...])


def _combine(pos3, y4, h, gates, wsg, wsu, wsd, g, b):
    t, d = h.shape
    tm = COMB_TM
    nt = t // tm
    ds_ = wsg.shape[1]
    return pl.pallas_call(
        _combine_body,
        out_shape=jax.ShapeDtypeStruct((t, d), F32),
        grid=(nt,),
        in_specs=[pl.BlockSpec((1, 1, tm * TOP_K), lambda i: (i, 0, 0), memory_space=pltpu.SMEM),
                  pl.BlockSpec((1, 1, tm * TOP_K), lambda i: (jnp.minimum(i + 1, nt - 1), 0, 0),
                               memory_space=pltpu.SMEM),
                  pl.BlockSpec(memory_space=pl.ANY),
                  pl.BlockSpec((tm, d), lambda i: (i, 0)),
                  pl.BlockSpec((tm, TOP_K), lambda i: (i, 0)),
                  pl.BlockSpec((d, ds_), lambda i: (0, 0)),
                  pl.BlockSpec((d, ds_), lambda i: (0, 0)),
                  pl.BlockSpec((ds_, d), lambda i: (0, 0)),
                  pl.BlockSpec((1, d), lambda i: (0, 0)),
                  pl.BlockSpec((1, d), lambda i: (0, 0))],
        out_specs=pl.BlockSpec((tm, d), lambda i: (i, 0)),
        scratch_shapes=[pltpu.VMEM((2, tm * TOP_K * ROW_WORDS, LANES), U32),
                        pltpu.SemaphoreType.DMA((2,))],
        compiler_params=_cparams(("arbitrary",)),
        name="combine",
    )(pos3, pos3, y4, h, gates, wsg, wsu, wsd, g, b)


def _expert_tables(counts, nblk):
    bm = MOE_BM
    cnt = counts.reshape(N_EXPERTS).astype(I32)
    padded = (cnt + bm - 1) // bm * bm
    padded_end = jnp.cumsum(padded)
    padded_start = padded_end - padded
    block_rows = jnp.arange(nblk, dtype=I32) * bm
    block_e = jnp.sum((padded_end[None, :] <= block_rows[:, None]).astype(I32), axis=1)
    block_e = jnp.minimum(block_e, N_EXPERTS - 1)
    n_used = (padded_end[-1:] // bm).astype(I32)
    ids = jnp.where(cnt > 0, jnp.arange(N_EXPERTS, dtype=I32), N_EXPERTS)
    after = jnp.concatenate([lax.cummin(ids, reverse=True)[1:], jnp.full((1,), N_EXPERTS, I32)])
    next_e = jnp.where(after < N_EXPERTS, after, -1).astype(I32)
    ord_e = (jnp.cumsum((cnt > 0).astype(I32)) - 1).astype(I32)
    return padded_start, padded_start + cnt, padded - cnt, block_e, n_used, next_e, ord_e


def kernel(x, w_in, gla_gate_w2, gla_gate_b, gla_norm_w, pool_w_group, pool_scale, w_out, ln1_g, ln1_b,
           router_w, router_bias, w_exp_gate, w_exp_up, w_exp_down, w_sh_gate, w_sh_up, w_sh_down, ln2_g, ln2_b):
    batch, seq, d = x.shape
    t = batch * seq
    h2d = x.reshape(t, d)
    for l in range(DEPTH):
        d_in = w_in.shape[2]
        w_in_b = jnp.pad(w_in[l], ((0, 0), (0, D_IN_PAD - d_in))).astype(BF16)
        w2p = jnp.pad(gla_gate_w2[l], ((0, LANES - GLA_GATE_RANK), (0, 0))).astype(BF16)
        h, h4 = _front(h2d, batch, seq, w_in_b, w2p, gla_gate_b[l].reshape(1, -1), gla_norm_w[l].reshape(1, -1),
                       pool_w_group[l].astype(BF16), pool_scale[l].reshape(1, -1),
                       w_out[l].astype(BF16), ln1_g[l].reshape(1, -1), ln1_b[l].reshape(1, -1))
        idx_t, gate_t, rank_t, counts = _router(h, router_w[l].T, router_bias[l].reshape(-1, 1))
        nblk = (t * TOP_K + N_EXPERTS * (MOE_BM - 1)) // MOE_BM
        nblk = -(-nblk // EXP_BLOCKS_PER_STEP) * EXP_BLOCKS_PER_STEP
        start, pad_row, pad_n, block_e, n_used, next_e, ord_e = _expert_tables(counts, nblk)
        pos_t = _positions(idx_t, rank_t, start.astype(F32).reshape(-1, 1))
        pos_tok = pos_t.T
        xs = _dispatch(pad_row, pad_n, n_used, pos_tok.reshape(t // DISP_TM, 1, DISP_TM * TOP_K), h4, nblk * MOE_BM)
        y4 = _experts(block_e, n_used, next_e, ord_e, xs, w_exp_gate[l], w_exp_up[l], w_exp_down[l])
        h2d = _combine(pos_tok.reshape(t // COMB_TM, 1, COMB_TM * TOP_K), y4, h, gate_t.T,
                       w_sh_gate[l].astype(BF16), w_sh_up[l].astype(BF16), w_sh_down[l].astype(BF16),
                       ln2_g[l].reshape(1, -1), ln2_b[l].reshape(1, -1))
    return h2d.reshape(batch, seq, d)
```

```python
import jax
import jax.numpy as jnp
from jax import lax
from jax.experimental import pallas as pl
from jax.experimental.pallas import tpu as pltpu

F32 = jnp.float32
BF16 = jnp.bfloat16
I32 = jnp.int32
U32 = jnp.uint32
HIGH_HALF = 0xFFFF0000

POOL_WINDOWS = (2, 4, 8, 16)
POOL_GROUP_DIM = 128
POOL_WIDTH = 512
GLA_HEADS = 4
GLA_DK = 64
GLA_DV = 128
GLA_DK_TOTAL = 256
GLA_WIDTH = 512
GLA_GATE_RANK = 16
GLA_GATE_NORMALIZER = 16.0
GLA_CHUNK = 16
GLA_SAFE_EXP = 60.0
N_EXPERTS = 256
TOP_K = 8
N_GROUPS = 8
GROUP_SIZE = N_EXPERTS // N_GROUPS
TOPK_GROUPS = 4
ROUTED_SCALE = 2.5
DEPTH = 1
DEEPNORM_ALPHA = (2.0 * DEPTH) ** 0.25
LN_EPS = 1e-5
RMS_EPS = 1e-5

LANES = 128
SUBLANES = 8
VMEM_LIMIT = 56 * 1024 * 1024

MIX_TS = 512
ROUTE_TM = 512
MOE_BM = 256
COMB_TM = 512
POS_TM = 512
DISP_TM = 256
DISP_UNROLL = 4
ROW_WORDS = 4
X_RING = 4
EXP_BLOCKS_PER_STEP = 4
D_IN_PAD = 2176


def _cparams(sem):
    return pltpu.CompilerParams(dimension_semantics=sem, vmem_limit_bytes=VMEM_LIMIT)


def _silu(x):
    return x * (1.0 / (1.0 + jnp.exp(-x)))


def _layernorm(y, g, b):
    mu = jnp.mean(y, axis=-1, keepdims=True)
    yc = y - mu
    var = jnp.mean(yc * yc, axis=-1, keepdims=True)
    return yc * lax.rsqrt(var + LN_EPS) * g + b


def _mixer_body(p_ref, q_ref, k_ref, v_ref, r_ref, gl_ref, w2_ref, gb_ref, nw_ref, pw_ref, ps_ref,
                o_ref, pbuf, state, kvbuf, sall, obuf, gk_s):
    ts = p_ref.shape[0]
    s_idx = pl.program_id(1)
    halo = POOL_WINDOWS[-1]

    @pl.when(s_idx == 0)
    def _():
        pbuf[pl.ds(0, halo), :] = jnp.zeros((halo, POOL_WIDTH), F32)
        state[...] = jnp.zeros(state.shape, F32)

    p = p_ref[...]
    pbuf[pl.ds(halo, ts), :] = p
    pos = s_idx * ts + lax.broadcasted_iota(I32, (ts, 1), 0)
    for g, w in enumerate(POOL_WINDOWS):
        c0 = g * POOL_GROUP_DIM
        ext = pbuf[:, pl.ds(c0, POOL_GROUP_DIM)]
        sh = 1
        while sh < w:
            ext = ext + pltpu.roll(ext, sh, axis=0)
            sh *= 2
        acc = ext[halo:, :]
        cnt = jnp.minimum(pos + 1, w).astype(F32)
        mixed = acc / cnt - p[:, c0:c0 + POOL_GROUP_DIM]
        og = jnp.dot(mixed.astype(BF16), pw_ref[g], preferred_element_type=F32)
        o_ref[:, pl.ds(c0, POOL_GROUP_DIM)] = (og * ps_ref[:, pl.ds(c0, POOL_GROUP_DIM)]).astype(o_ref.dtype)
    pbuf[pl.ds(0, halo), :] = pbuf[pl.ds(ts, halo), :]

    nchunk = ts // GLA_CHUNK
    glog = jnp.dot(gl_ref[...].astype(BF16), w2_ref[...], preferred_element_type=F32) + gb_ref[...]
    gk = (jnp.minimum(glog, 0.0) - jnp.log(1.0 + jnp.exp(-jnp.abs(glog)))) * (1.0 / GLA_GATE_NORMALIZER)
    row = lax.broadcasted_iota(I32, (ts, 1), 0)
    rin = row % GLA_CHUNK
    b = gk
    sh = 1
    while sh < GLA_CHUNK:
        b = b + jnp.where(rin >= sh, pltpu.roll(b, sh, axis=0), 0.0)
        sh *= 2
    b3 = b.reshape(nchunk, GLA_CHUNK, GLA_DK_TOTAL)
    bmid = b3[:, GLA_CHUNK // 2 - 1:GLA_CHUNK // 2, :]
    blast = b3[:, GLA_CHUNK - 1:GLA_CHUNK, :]
    lane = lax.broadcasted_iota(I32, (1, LANES), 1)
    head_lane = [lane < GLA_DK, lane >= GLA_DK]
    q_scale = GLA_DK ** -0.5
    safe = jnp.max(jnp.abs(b3 - bmid)) <= GLA_SAFE_EXP

    @pl.when(safe)
    def _chunked():
        v = v_ref[...]
        vb = v.astype(BF16)
        q3 = (q_ref[...] * q_scale).reshape(nchunk, GLA_CHUNK, GLA_DK_TOTAL)
        k3 = k_ref[...].reshape(nchunk, GLA_CHUNK, GLA_DK_TOTAL)
        qs = (q3 * jnp.exp(b3 - bmid)).reshape(ts, GLA_DK_TOTAL)
        ks = (k3 * jnp.exp(bmid - b3)).reshape(ts, GLA_DK_TOTAL)
        qd = (q3 * jnp.exp(b3)).reshape(ts, GLA_DK_TOTAL)
        kd = (k3 * jnp.exp(blast - b3)).reshape(ts, GLA_DK_TOTAL)
        cdec = jnp.exp(blast).reshape(nchunk, GLA_DK_TOTAL)

        blk = LANES
        ri = lax.broadcasted_iota(I32, (blk, blk), 0)
        ci = lax.broadcasted_iota(I32, (blk, blk), 1)
        causal = (ri // GLA_CHUNK == ci // GLA_CHUNK) & (ri >= ci)
        o_intra = [[None] * (ts // blk) for _ in range(GLA_HEADS)]
        for rb in range(ts // blk):
            rs = slice(rb * blk, (rb + 1) * blk)
            for pair in range(GLA_HEADS // 2):
                ls = slice(pair * LANES, (pair + 1) * LANES)
                ks_p = ks[rs, ls].astype(BF16)
                q_p = qs[rs, ls]
                q2 = jnp.concatenate([jnp.where(head_lane[sub], q_p, 0.0) for sub in range(2)], axis=0).astype(BF16)
                sc2 = lax.dot_general(q2, ks_p, (((1,), (1,)), ((), ())), preferred_element_type=F32)
                for sub in range(2):
                    h = pair * 2 + sub
                    sc = jnp.where(causal, sc2[sub * blk:(sub + 1) * blk], 0.0).astype(BF16)
                    o_intra[h][rb] = jnp.dot(sc, vb[rs, h * GLA_DV:(h + 1) * GLA_DV], preferred_element_type=F32)

        cpb = blk // GLA_CHUNK
        chunk_of_col = lax.broadcasted_iota(I32, (1, blk), 1) // GLA_CHUNK
        kdb = kd.astype(BF16)
        for rb in range(ts // blk):
            rs = slice(rb * blk, (rb + 1) * blk)
            for pair in range(GLA_HEADS // 2):
                ls = slice(pair * LANES, (pair + 1) * LANES)
                inc = []
                for sub in range(2):
                    h = pair * 2 + sub
                    v_t = v[rs, h * GLA_DV:(h + 1) * GLA_DV].T
                    lhs = jnp.concatenate([jnp.where(chunk_of_col == c, v_t, 0.0) for c in range(cpb)],
                                          axis=0).astype(BF16)
                    inc.append(jnp.dot(lhs, kdb[rs, ls], preferred_element_type=F32))
                stacked = jnp.where(head_lane[0], inc[0], inc[1])
                for c in range(cpb):
                    kvbuf[pair, rb * cpb + c] = stacked[c * LANES:(c + 1) * LANES]
        for pair in range(GLA_HEADS // 2):
            ls = slice(pair * LANES, (pair + 1) * LANES)
            st = state[pair]
            for c in range(nchunk):
                sall[pair, c] = st.astype(BF16)
                st = st * cdec[c:c + 1, ls] + kvbuf[pair, c]
            state[pair] = st
        o_inter = [[None] * nchunk for _ in range(GLA_HEADS)]
        for pair in range(GLA_HEADS // 2):
            ls = slice(pair * LANES, (pair + 1) * LANES)
            for c in range(nchunk):
                rs = slice(c * GLA_CHUNK, (c + 1) * GLA_CHUNK)
                q_c = qd[rs, ls]
                q2 = jnp.concatenate([jnp.where(head_lane[sub], q_c, 0.0) for sub in range(2)], axis=0).astype(BF16)
                res = lax.dot_general(q2, sall[pair, c], (((1,), (1,)), ((), ())), preferred_element_type=F32)
                for sub in range(2):
                    o_inter[pair * 2 + sub][c] = res[sub * GLA_CHUNK:(sub + 1) * GLA_CHUNK]
        for h in range(GLA_HEADS):
            obuf[:, pl.ds(h * GLA_DV, GLA_DV)] = (jnp.concatenate(o_intra[h], axis=0)
                                                  + jnp.concatenate(o_inter[h], axis=0))

    @pl.when(jnp.logical_not(safe))
    def _row_by_row():
        gk_s[...] = gk
        row8 = lax.broadcasted_iota(I32, (SUBLANES, 1), 0)

        def slab(i8, carry):
            r0 = pl.multiple_of(i8 * SUBLANES, SUBLANES)
            q8 = q_ref[pl.ds(r0, SUBLANES), :] * q_scale
            k8 = k_ref[pl.ds(r0, SUBLANES), :]
            v8 = v_ref[pl.ds(r0, SUBLANES), :]
            g8 = jnp.exp(gk_s[pl.ds(r0, SUBLANES), :])
            outs = [jnp.zeros((SUBLANES, GLA_DV), F32) for _ in range(GLA_HEADS)]
            for pair in range(GLA_HEADS // 2):
                ls = slice(pair * LANES, (pair + 1) * LANES)
                st = state[pair]
                for r in range(SUBLANES):
                    sel = row8 == r
                    k_r = jnp.where(sel, k8[:, ls], 0.0).astype(BF16)
                    inc = []
                    for sub in range(2):
                        h = pair * 2 + sub
                        v_r = jnp.where(sel, v8[:, h * GLA_DV:(h + 1) * GLA_DV], 0.0).astype(BF16)
                        inc.append(lax.dot_general(v_r, k_r, (((0,), (0,)), ((), ())), preferred_element_type=F32))
                    st = st * g8[r:r + 1, ls] + jnp.where(head_lane[0], inc[0], inc[1])
                    q2 = jnp.concatenate([jnp.where(sel & head_lane[sub], q8[:, ls], 0.0) for sub in range(2)],
                                         axis=0).astype(BF16)
                    res = lax.dot_general(q2, st.astype(BF16), (((1,), (1,)), ((), ())),
                                          preferred_element_type=F32)
                    for sub in range(2):
                        outs[pair * 2 + sub] = outs[pair * 2 + sub] + res[sub * SUBLANES:(sub + 1) * SUBLANES]
                state[pair] = st
            obuf[pl.ds(r0, SUBLANES), :] = jnp.concatenate(outs, axis=1)
            return carry
        lax.fori_loop(0, ts // SUBLANES, slab, 0)

    nw = nw_ref[...]
    r = r_ref[...]
    for h in range(GLA_HEADS):
        o = obuf[:, pl.ds(h * GLA_DV, GLA_DV)]
        o = o * lax.rsqrt(jnp.mean(o * o, axis=-1, keepdims=True) + RMS_EPS) * nw
        o = o * _silu(r[:, h * GLA_DV:(h + 1) * GLA_DV])
        o_ref[:, pl.ds(POOL_WIDTH + h * GLA_DV, GLA_DV)] = o.astype(o_ref.dtype)


def _front_body(x_ref, win_ref, w2_ref, gb_ref, nw_ref, pw_ref, ps_ref, wout_ref, g_ref, b_ref, h_ref, h4_ref,
                pp_s, pq_s, pk_s, pv_s, pr_s, pg_s, mix_s, pbuf, state, kvbuf, sall, obuf, gk_s):
    groups = (pp_s, pq_s, pk_s, pv_s, pr_s, pg_s)
    cols = [0]
    for ref in groups:
        cols.append(cols[-1] + ref.shape[1])
    xb = x_ref[...].astype(BF16)
    for gi in (0, 5, 1, 2, 3, 4):
        groups[gi][...] = jnp.dot(xb, win_ref[:, pl.ds(cols[gi], groups[gi].shape[1])],
                                  preferred_element_type=F32)
    _mixer_body(*groups, w2_ref, gb_ref, nw_ref, pw_ref, ps_ref, mix_s, pbuf, state, kvbuf, sall, obuf, gk_s)
    y = DEEPNORM_ALPHA * x_ref[...] + jnp.dot(mix_s[...], wout_ref[...], preferred_element_type=F32)
    h = _layernorm(y, g_ref[...], b_ref[...])
    h_ref[...] = h
    for s, w in enumerate(_pack_row_words(h)):
        h4_ref[_word_plane(0, h.shape[0], s), :] = w


def _front(x2d, batch, seq, w_in_b, w2p, gate_b, norm_w, pool_w, pool_scale, w_out_b, ln_g, ln_b):
    t, d = x2d.shape
    ts = MIX_TS
    nseq = seq // ts

    def full(shape):
        return pl.BlockSpec(shape, lambda bi, si: (0,) * len(shape))

    return pl.pallas_call(
        _front_body,
        out_shape=(jax.ShapeDtypeStruct((t, d), F32), jax.ShapeDtypeStruct((t * ROW_WORDS, LANES), U32)),
        grid=(batch, nseq),
        in_specs=[pl.BlockSpec((ts, d), lambda bi, si: (bi * nseq + si, 0)),
                  full(w_in_b.shape), full(w2p.shape), full(gate_b.shape), full(norm_w.shape),
                  full(pool_w.shape), full(pool_scale.shape), full(w_out_b.shape), full(ln_g.shape), full(ln_b.shape)],
        out_specs=(pl.BlockSpec((ts, d), lambda bi, si: (bi * nseq + si, 0)),
                   pl.BlockSpec((ts * ROW_WORDS, LANES), lambda bi, si: (bi * nseq + si, 0))),
        scratch_shapes=[pltpu.VMEM((ts, POOL_WIDTH), F32), pltpu.VMEM((ts, GLA_DK_TOTAL), F32),
                        pltpu.VMEM((ts, GLA_DK_TOTAL), F32), pltpu.VMEM((ts, GLA_WIDTH), F32),
                        pltpu.VMEM((ts, GLA_WIDTH), F32), pltpu.VMEM((ts, D_IN_PAD - 2 * GLA_DK_TOTAL
                                                                      - 2 * GLA_WIDTH - POOL_WIDTH), F32),
                        pltpu.VMEM((ts, POOL_WIDTH + GLA_WIDTH), BF16),
                        pltpu.VMEM((ts + POOL_WINDOWS[-1], POOL_WIDTH), F32),
                        pltpu.VMEM((GLA_HEADS // 2, LANES, GLA_DV), F32),
                        pltpu.VMEM((GLA_HEADS // 2, ts // GLA_CHUNK, LANES, GLA_DV), F32),
                        pltpu.VMEM((GLA_HEADS // 2, ts // GLA_CHUNK, LANES, GLA_DV), BF16),
                        pltpu.VMEM((ts, GLA_WIDTH), F32), pltpu.VMEM((ts, GLA_DK_TOTAL), F32)],
        compiler_params=_cparams(("arbitrary", "arbitrary")),
        name="front",
    )(x2d, w_in_b, w2p, gate_b, norm_w, pool_w, pool_scale, w_out_b, ln_g, ln_b)


def _pack_row_words(x):
    half = x.shape[1] // 2
    u = pltpu.bitcast(x.astype(BF16).astype(F32), U32)
    hi_mask = jnp.uint32(HIGH_HALF)
    return [(u[:, half + s * LANES:half + (s + 1) * LANES] & hi_mask) | (u[:, s * LANES:(s + 1) * LANES] >> 16)
            for s in range(ROW_WORDS)]


def _unpack_row_words(w):
    return pltpu.bitcast(w << 16, F32), pltpu.bitcast(w & jnp.uint32(HIGH_HALF), F32)


def _rows(ref, first, n, align=1):
    if isinstance(first, int):
        start = first * ROW_WORDS
    else:
        start = pl.multiple_of(first * ROW_WORDS, ROW_WORDS * align)
    return ref.at[pl.ds(start, n * ROW_WORDS), :]


def _word_plane(first, m, s):
    return pl.ds(first * ROW_WORDS + s, m, stride=ROW_WORDS)


def _first_argmax_rows(val, rowf, nrows):
    m = jnp.max(val, axis=0, keepdims=True)
    first = jnp.min(jnp.where(val == m, rowf, float(nrows)), axis=0, keepdims=True)
    return m, first, rowf == first


def _router_body(h_ref, whi_ref, wlo_ref, bias_ref, idx_ref, gate_ref, rank_ref, cnt_ref, carry):
    tm = h_ref.shape[0]
    i = pl.program_id(0)

    @pl.when(i == 0)
    def _():
        carry[...] = jnp.zeros(carry.shape, F32)

    h = h_ref[...]
    h_hi = h.astype(BF16)
    h_lo = (h - h_hi.astype(F32)).astype(BF16)
    nt = (((1,), (1,)), ((), ()))
    logits = (lax.dot_general(whi_ref[...], h_hi, nt, preferred_element_type=F32)
              + lax.dot_general(whi_ref[...], h_lo, nt, preferred_element_type=F32)
              + lax.dot_general(wlo_ref[...], h_hi, nt, preferred_element_type=F32))
    scores = 1.0 / (1.0 + jnp.exp(-logits))
    biased = scores + bias_ref[...]
    neg = -jnp.inf

    grp = biased.reshape(N_GROUPS, GROUP_SIZE, tm)
    gi = lax.broadcasted_iota(I32, (N_GROUPS, GROUP_SIZE, tm), 1).astype(F32)
    g1 = jnp.max(grp, axis=1, keepdims=True)
    f1 = jnp.min(jnp.where(grp == g1, gi, float(GROUP_SIZE)), axis=1, keepdims=True)
    g2 = jnp.max(jnp.where(gi == f1, neg, grp), axis=1, keepdims=True)
    gscore = (g1 + g2).reshape(N_GROUPS, tm)

    growf = lax.broadcasted_iota(I32, (N_GROUPS, tm), 0).astype(F32)
    gsel = jnp.zeros((N_GROUPS, tm), F32)
    gval = gscore
    for _ in range(TOPK_GROUPS):
        _, _, pick = _first_argmax_rows(gval, growf, N_GROUPS)
        gsel = jnp.where(pick, 1.0, gsel)
        gval = jnp.where(pick, neg, gval)
    emask = jnp.broadcast_to(gsel.reshape(N_GROUPS, 1, tm), (N_GROUPS, GROUP_SIZE, tm)).reshape(N_EXPERTS, tm)

    rowf = lax.broadcasted_iota(I32, (N_EXPERTS, tm), 0).astype(F32)
    val = jnp.where(emask > 0.0, biased, neg)
    onehot = jnp.zeros((N_EXPERTS, tm), F32)
    picks, idxs, ws = [], [], []
    for _ in range(TOP_K):
        _, first, pick = _first_argmax_rows(val, rowf, N_EXPERTS)
        picks.append(pick)
        idxs.append(first)
        ws.append(jnp.sum(jnp.where(pick, scores, 0.0), axis=0, keepdims=True))
        onehot = jnp.where(pick, 1.0, onehot)
        val = jnp.where(pick, neg, val)
    w = jnp.concatenate(ws, axis=0)
    gate_ref[...] = w / jnp.sum(w, axis=0, keepdims=True) * ROUTED_SCALE
    idx_ref[...] = jnp.concatenate(idxs, axis=0).astype(I32)

    ti = lax.broadcasted_iota(I32, (tm, tm), 0)
    tj = lax.broadcasted_iota(I32, (tm, tm), 1)
    upper = jnp.where(ti < tj, 1.0, 0.0).astype(BF16)
    prefix = jnp.dot(onehot.astype(BF16), upper, preferred_element_type=F32) + carry[...]
    ranks = [jnp.sum(jnp.where(pk, prefix, 0.0), axis=0, keepdims=True) for pk in picks]
    rank_ref[...] = jnp.concatenate(ranks, axis=0).astype(I32)
    carry[...] = carry[...] + jnp.sum(onehot, axis=1, keepdims=True)
    cnt_ref[...] = carry[...]


def _router(h, wt, bias_col):
    t, d = h.shape
    tm = ROUTE_TM
    wt_hi = wt.astype(BF16)
    wt_lo = (wt - wt_hi.astype(F32)).astype(BF16)
    return pl.pallas_call(
        _router_body,
        out_shape=(jax.ShapeDtypeStruct((TOP_K, t), I32), jax.ShapeDtypeStruct((TOP_K, t), F32),
                   jax.ShapeDtypeStruct((TOP_K, t), I32), jax.ShapeDtypeStruct((N_EXPERTS, 1), F32)),
        grid=(t // tm,),
        in_specs=[pl.BlockSpec((tm, d), lambda i: (i, 0)),
                  pl.BlockSpec((N_EXPERTS, d), lambda i: (0, 0)),
                  pl.BlockSpec((N_EXPERTS, d), lambda i: (0, 0)),
                  pl.BlockSpec((N_EXPERTS, 1), lambda i: (0, 0))],
        out_specs=(pl.BlockSpec((TOP_K, tm), lambda i: (0, i)),
                   pl.BlockSpec((TOP_K, tm), lambda i: (0, i)),
                   pl.BlockSpec((TOP_K, tm), lambda i: (0, i)),
                   pl.BlockSpec((N_EXPERTS, 1), lambda i: (0, 0))),
        scratch_shapes=[pltpu.VMEM((N_EXPERTS, 1), F32)],
        compiler_params=_cparams(("arbitrary",)),
        name="router",
    )(h, wt_hi, wt_lo, bias_col)


def _positions_body(idx_ref, rank_ref, start_ref, pos_ref):
    tm = idx_ref.shape[1]
    rowi = lax.broadcasted_iota(I32, (N_EXPERTS, tm), 0)
    start = start_ref[...]
    idx = idx_ref[...]
    rows = [jnp.sum(jnp.where(rowi == idx[k:k + 1, :], start, 0.0), axis=0, keepdims=True) for k in range(TOP_K)]
    pos_ref[...] = jnp.concatenate(rows, axis=0).astype(I32) + rank_ref[...]


def _positions(idx_t, rank_t, start_col):
    t = idx_t.shape[1]
    tm = POS_TM
    return pl.pallas_call(
        _positions_body,
        out_shape=jax.ShapeDtypeStruct((TOP_K, t), I32),
        grid=(t // tm,),
        in_specs=[pl.BlockSpec((TOP_K, tm), lambda i: (0, i)),
                  pl.BlockSpec((TOP_K, tm), lambda i: (0, i)),
                  pl.BlockSpec((N_EXPERTS, 1), lambda i: (0, 0))],
        out_specs=pl.BlockSpec((TOP_K, tm), lambda i: (0, i)),
        compiler_params=_cparams(("arbitrary",)),
        name="positions",
    )(idx_t, rank_t, start_col)


def _pad_fill_copy(zeros, xs_ref, sem, row, nrows):
    return pltpu.make_async_copy(_rows(zeros, 0, nrows), _rows(xs_ref, row, nrows), sem)


def _dispatch_body(pad_row_ref, pad_n_ref, nu_ref, pos_ref, h4_ref, xs_ref, zeros, sem, pad_sem):
    i = pl.program_id(0)
    tm = h4_ref.shape[0] // ROW_WORDS
    half = MOE_BM // 2
    pad_bits = [1 << j for j in range(MOE_BM.bit_length() - 1)]
    n_half_blocks = xs_ref.shape[0] // (half * ROW_WORDS)

    def pad_pass(wait):
        def go(cp):
            if wait:
                cp.wait()
            else:
                cp.start()

        def body(e, carry):
            row = pad_row_ref[e]
            n = pad_n_ref[e]
            for bit in pad_bits:
                @pl.when((n & bit) != 0)
                def _():
                    go(_pad_fill_copy(zeros, xs_ref, pad_sem, row + (n & (bit - 1)), bit))
            return carry
        lax.fori_loop(0, N_EXPERTS, body, 0)

        def tail(hb, carry):
            go(_pad_fill_copy(zeros, xs_ref, pad_sem, hb * half, half))
            return carry
        lax.fori_loop(nu_ref[0] * 2, n_half_blocks, tail, 0)

    @pl.when(i == 0)
    def _():
        zeros[...] = jnp.zeros(zeros.shape, U32)
        pad_pass(False)

    def body(tt, carry):
        for dt in range(DISP_UNROLL):
            t = tt * DISP_UNROLL + dt
            for k in range(TOP_K):
                p = pos_ref[0, 0, t * TOP_K + k]
                pltpu.make_async_copy(_rows(h4_ref, t, 1), _rows(xs_ref, p, 1), sem).start(priority=k % 2)
        return carry
    lax.fori_loop(0, tm // DISP_UNROLL, body, 0)

    for k in range(TOP_K):
        pltpu.make_async_copy(h4_ref, _rows(xs_ref, 0, tm), sem).wait()

    @pl.when(i == pl.num_programs(0) - 1)
    def _():
        pad_pass(True)


def _dispatch(pad_row, pad_n, n_used, pos3, h4, n_rows):
    nt, _, per_step = pos3.shape
    tm = per_step // TOP_K
    grid_spec = pltpu.PrefetchScalarGridSpec(
        num_scalar_prefetch=3,
        grid=(nt,),
        in_specs=[pl.BlockSpec((1, 1, per_step), lambda i, a, b, c: (i, 0, 0), memory_space=pltpu.SMEM),
                  pl.BlockSpec((tm * ROW_WORDS, LANES), lambda i, a, b, c: (i, 0))],
        out_specs=pl.BlockSpec(memory_space=pl.ANY),
        scratch_shapes=[pltpu.VMEM((MOE_BM // 2 * ROW_WORDS, LANES), U32),
                        pltpu.SemaphoreType.DMA, pltpu.SemaphoreType.DMA],
    )
    return pl.pallas_call(
        _dispatch_body,
        out_shape=jax.ShapeDtypeStruct((n_rows * ROW_WORDS, LANES), U32),
        grid_spec=grid_spec,
        compiler_params=_cparams(("arbitrary",)),
        name="dispatch",
    )(pad_row, pad_n, n_used, pos3, h4)


def _weight_copies(e, w_hbm, stage, sem, slot):
    return [pltpu.make_async_copy(w.at[e], st.at[slot], sem.at[slot, j])
            for j, (w, st) in enumerate(zip(w_hbm, stage))]


def _row_block_copy(xs_hbm, xbuf, xsem, j):
    slot = j % X_RING
    return pltpu.make_async_copy(_rows(xs_hbm, j * MOE_BM, MOE_BM, align=MOE_BM), xbuf.at[slot], xsem.at[slot])


def _experts_body(be_ref, nu_ref, nx_ref, ord_ref, xs_hbm, wg_hbm, wu_hbm, wd_hbm, y4_ref,
                  xbuf, xsem, sg, su, sd, wsem, wg_b, wu_b, wd_b):
    w_hbm = (wg_hbm, wu_hbm, wd_hbm)
    stage = (sg, su, sd)
    n_used = nu_ref[0]

    def do_block(i, out_row):
        @pl.when(i < n_used)
        def _():
            e = be_ref[i]
            prev_e = be_ref[jnp.maximum(i - 1, 0)]

            @pl.when(i == 0)
            def _():
                for j in range(X_RING - 1):
                    @pl.when(j < n_used)
                    def _():
                        _row_block_copy(xs_hbm, xbuf, xsem, j).start()

            @pl.when(i + X_RING - 1 < n_used)
            def _():
                _row_block_copy(xs_hbm, xbuf, xsem, i + X_RING - 1).start()

            @pl.when((i == 0) | (e != prev_e))
            def _():
                wslot = ord_ref[e] % 2
                n1 = nx_ref[e]
                n2 = jnp.where(n1 >= 0, nx_ref[jnp.maximum(n1, 0)], -1)

                @pl.when(i == 0)
                def _():
                    for cp in _weight_copies(e, w_hbm, stage, wsem, wslot):
                        cp.start()

                    @pl.when(n1 >= 0)
                    def _():
                        for cp in _weight_copies(n1, w_hbm, stage, wsem, 1 - wslot):
                            cp.start()
                for cp in _weight_copies(e, w_hbm, stage, wsem, wslot):
                    cp.wait()
                wg_b[...] = sg[wslot].astype(BF16)
                wu_b[...] = su[wslot].astype(BF16)
                wd_b[...] = sd[wslot].astype(BF16)

                @pl.when(n2 >= 0)
                def _():
                    for cp in _weight_copies(n2, w_hbm, stage, wsem, wslot):
                        cp.start()

            _row_block_copy(xs_hbm, xbuf, xsem, i).wait()
            slot = i % X_RING
            parts = [_unpack_row_words(xbuf[slot, _word_plane(0, MOE_BM, s), :]) for s in range(ROW_WORDS)]
            x = jnp.concatenate([p[0] for p in parts] + [p[1] for p in parts], axis=1).astype(BF16)
            g = jnp.dot(x, wg_b[...], preferred_element_type=F32)
            u = jnp.dot(x, wu_b[...], preferred_element_type=F32)
            a = (_silu(g) * u).astype(BF16)
            y = jnp.dot(a, wd_b[...], preferred_element_type=F32)
            for s, w in enumerate(_pack_row_words(y)):
                y4_ref[_word_plane(out_row, MOE_BM, s), :] = w

        @pl.when(i >= n_used)
        def _():
            y4_ref[pl.ds(out_row * ROW_WORDS, MOE_BM * ROW_WORDS), :] = jnp.zeros((MOE_BM * ROW_WORDS, LANES), U32)

    for sub in range(EXP_BLOCKS_PER_STEP):
        do_block(pl.program_id(0) * EXP_BLOCKS_PER_STEP + sub, sub * MOE_BM)


def _experts(block_e, n_used, next_e, ord_e, xs, wg, wu, wd):
    nblk = block_e.shape[0]
    bm = MOE_BM * EXP_BLOCKS_PER_STEP
    e, d, de = wg.shape

    grid_spec = pltpu.PrefetchScalarGridSpec(
        num_scalar_prefetch=4,
        grid=(nblk // EXP_BLOCKS_PER_STEP,),
        in_specs=[pl.BlockSpec(memory_space=pl.ANY),
                  pl.BlockSpec(memory_space=pl.ANY),
                  pl.BlockSpec(memory_space=pl.ANY),
                  pl.BlockSpec(memory_space=pl.ANY)],
        out_specs=pl.BlockSpec((bm * ROW_WORDS, LANES), lambda i, be, nu, nx, od: (i, 0)),
        scratch_shapes=[pltpu.VMEM((X_RING, MOE_BM * ROW_WORDS, LANES), U32), pltpu.SemaphoreType.DMA((X_RING,)),
                        pltpu.VMEM((2, d, de), F32), pltpu.VMEM((2, d, de), F32), pltpu.VMEM((2, de, d), F32),
                        pltpu.SemaphoreType.DMA((2, 3)),
                        pltpu.VMEM((d, de), BF16), pltpu.VMEM((d, de), BF16), pltpu.VMEM((de, d), BF16)],
    )
    return pl.pallas_call(
        _experts_body,
        out_shape=jax.ShapeDtypeStruct((nblk * MOE_BM * ROW_WORDS, LANES), U32),
        grid_spec=grid_spec,
        compiler_params=_cparams(("arbitrary",)),
        name="experts",
    )(block_e, n_used, next_e, ord_e, xs, wg, wu, wd)


def _combine_body(pos_cur, pos_nxt, y4_ref, h_ref, gate_ref, wsg_ref, wsu_ref, wsd_ref, g_ref, b_ref,
                  o_ref, buf, sem):
    i = pl.program_id(0)
    nb = pl.num_programs(0)
    tm, d = h_ref.shape
    slot = i % 2

    def issue(pos_ref, sl):
        def body(t, carry):
            for k in range(TOP_K):
                p = pos_ref[0, 0, t * TOP_K + k]
                pltpu.make_async_copy(_rows(y4_ref, p, 1), _rows(buf.at[sl], k * tm + t, 1),
                                      sem.at[sl]).start(priority=k % 2)
            return carry
        lax.fori_loop(0, tm, body, 0)

    @pl.when(i == 0)
    def _():
        issue(pos_cur, 0)

    @pl.when(i + 1 < nb)
    def _():
        issue(pos_nxt, 1 - slot)

    h = h_ref[...]
    hb = h.astype(BF16)
    sg = jnp.dot(hb, wsg_ref[...], preferred_element_type=F32)
    su = jnp.dot(hb, wsu_ref[...], preferred_element_type=F32)
    shared = jnp.dot((_silu(sg) * su).astype(BF16), wsd_ref[...], preferred_element_type=F32)

    pltpu.make_async_copy(_rows(y4_ref, 0, tm * TOP_K), buf.at[slot], sem.at[slot]).wait()
    gates = gate_ref[...]
    lo_cols, hi_cols = [], []
    for s in range(ROW_WORDS):
        lo_acc = hi_acc = None
        for k in range(TOP_K):
            lo, hi = _unpack_row_words(buf[slot, _word_plane(k * tm, tm, s), :])
            gk = gates[:, k:k + 1]
            lo_acc = gk * lo if lo_acc is None else lo_acc + gk * lo
            hi_acc = gk * hi if hi_acc is None else hi_acc + gk * hi
        lo_cols.append(lo_acc)
        hi_cols.append(hi_acc)
    routed = jnp.concatenate(lo_cols + hi_cols, axis=1)
    o_ref[...] = _layernorm(DEEPNORM_ALPHA * h + (routed + shared), g_ref[...], b_ref[...])


def _combine(pos3, y4, h, gates, wsg, wsu, wsd, g, b):
    t, d = h.shape
    tm = COMB_TM
    nt = t // tm
    ds_ = wsg.shape[1]
    return pl.pallas_call(
        _combine_body,
        out_shape=jax.ShapeDtypeStruct((t, d), F32),
        grid=(nt,),
        in_specs=[pl.BlockSpec((1, 1, tm * TOP_K), lambda i: (i, 0, 0), memory_space=pltpu.SMEM),
                  pl.BlockSpec((1, 1, tm * TOP_K), lambda i: (jnp.minimum(i + 1, nt - 1), 0, 0),
                               memory_space=pltpu.SMEM),
                  pl.BlockSpec(memory_space=pl.ANY),
                  pl.BlockSpec((tm, d), lambda i: (i, 0)),
                  pl.BlockSpec((tm, TOP_K), lambda i: (i, 0)),
                  pl.BlockSpec((d, ds_), lambda i: (0, 0)),
                  pl.BlockSpec((d, ds_), lambda i: (0, 0)),
                  pl.BlockSpec((ds_, d), lambda i: (0, 0)),
                  pl.BlockSpec((1, d), lambda i: (0, 0)),
                  pl.BlockSpec((1, d), lambda i: (0, 0))],
        out_specs=pl.BlockSpec((tm, d), lambda i: (i, 0)),
        scratch_shapes=[pltpu.VMEM((2, tm * TOP_K * ROW_WORDS, LANES), U32),
                        pltpu.SemaphoreType.DMA((2,))],
        compiler_params=_cparams(("arbitrary",)),
        name="combine",
    )(pos3, pos3, y4, h, gates, wsg, wsu, wsd, g, b)


def _expert_tables(counts, nblk):
    bm = MOE_BM
    cnt = counts.reshape(N_EXPERTS).astype(I32)
    padded = (cnt + bm - 1) // bm * bm
    padded_end = jnp.cumsum(padded)
    padded_start = padded_end - padded
    block_rows = jnp.arange(nblk, dtype=I32) * bm
    block_e = jnp.sum((padded_end[None, :] <= block_rows[:, None]).astype(I32), axis=1)
    block_e = jnp.minimum(block_e, N_EXPERTS - 1)
    n_used = (padded_end[-1:] // bm).astype(I32)
    ids = jnp.where(cnt > 0, jnp.arange(N_EXPERTS, dtype=I32), N_EXPERTS)
    after = jnp.concatenate([lax.cummin(ids, reverse=True)[1:], jnp.full((1,), N_EXPERTS, I32)])
    next_e = jnp.where(after < N_EXPERTS, after, -1).astype(I32)
    ord_e = (jnp.cumsum((cnt > 0).astype(I32)) - 1).astype(I32)
    return padded_start, padded_start + cnt, padded - cnt, block_e, n_used, next_e, ord_e


def kernel(x, w_in, gla_gate_w2, gla_gate_b, gla_norm_w, pool_w_group, pool_scale, w_out, ln1_g, ln1_b,
           router_w, router_bias, w_exp_gate, w_exp_up, w_exp_down, w_sh_gate, w_sh_up, w_sh_down, ln2_g, ln2_b):
    batch, seq, d = x.shape
    t = batch * seq
    h2d = x.reshape(t, d)
    for l in range(DEPTH):
        d_in = w_in.shape[2]
        w_in_b = jnp.pad(w_in[l], ((0, 0), (0, D_IN_PAD - d_in))).astype(BF16)
        w2p = jnp.pad(gla_gate_w2[l], ((0, LANES - GLA_GATE_RANK), (0, 0))).astype(BF16)
        h, h4 = _front(h2d, batch, seq, w_in_b, w2p, gla_gate_b[l].reshape(1, -1), gla_norm_w[l].reshape(1, -1),
                       pool_w_group[l].astype(BF16), pool_scale[l].reshape(1, -1),
                       w_out[l].astype(BF16), ln1_g[l].reshape(1, -1), ln1_b[l].reshape(1, -1))
        idx_t, gate_t, rank_t, counts = _router(h, router_w[l].T, router_bias[l].reshape(-1, 1))
        nblk = (t * TOP_K + N_EXPERTS * (MOE_BM - 1)) // MOE_BM
        nblk = -(-nblk // EXP_BLOCKS_PER_STEP) * EXP_BLOCKS_PER_STEP
        start, pad_row, pad_n, block_e, n_used, next_e, ord_e = _expert_tables(counts, nblk)
        pos_t = _positions(idx_t, rank_t, start.astype(F32).reshape(-1, 1))
        pos_tok = pos_t.T
        xs = _dispatch(pad_row, pad_n, n_used, pos_tok.reshape(t // DISP_TM, 1, DISP_TM * TOP_K), h4, nblk * MOE_BM)
        y4 = _experts(block_e, n_used, next_e, ord_e, xs, w_exp_gate[l], w_exp_up[l], w_exp_down[l])
        h2d = _combine(pos_tok.reshape(t // COMB_TM, 1, COMB_TM * TOP_K), y4, h, gate_t.T,
                       w_sh_gate[l].astype(BF16), w_sh_up[l].astype(BF16), w_sh_down[l].astype(BF16),
                       ln2_g[l].reshape(1, -1), ln2_b[l].reshape(1, -1))
    return h2d.reshape(batch, seq, d)
```

```python
import jax
import jax.numpy as jnp
from jax import lax
from jax.experimental import pallas as pl
from jax.experimental.pallas import tpu as pltpu

F32 = jnp.float32
BF16 = jnp.bfloat16
I32 = jnp.int32
U32 = jnp.uint32
HIGH_HALF = 0xFFFF0000

POOL_WINDOWS = (2, 4, 8, 16)
POOL_GROUP_DIM = 128
POOL_WIDTH = 512
GLA_HEADS = 4
GLA_DK = 64
GLA_DV = 128
GLA_DK_TOTAL = 256
GLA_WIDTH = 512
GLA_GATE_RANK = 16
GLA_GATE_NORMALIZER = 16.0
GLA_CHUNK = 16
GLA_SAFE_EXP = 60.0
N_EXPERTS = 256
TOP_K = 8
N_GROUPS = 8
GROUP_SIZE = N_EXPERTS // N_GROUPS
TOPK_GROUPS = 4
ROUTED_SCALE = 2.5
DEPTH = 1
DEEPNORM_ALPHA = (2.0 * DEPTH) ** 0.25
LN_EPS = 1e-5
RMS_EPS = 1e-5

LANES = 128
SUBLANES = 8
VMEM_LIMIT = 56 * 1024 * 1024

MIX_TS = 512
ROUTE_TM = 512
MOE_BM = 256
COMB_TM = 512
POS_TM = 512
DISP_TM = 512
DISP_UNROLL = 4
ROW_WORDS = 4
X_RING = 6
EXP_BLOCKS_PER_STEP = 8
D_IN_PAD = 2176


def _cparams(sem):
    return pltpu.CompilerParams(dimension_semantics=sem, vmem_limit_bytes=VMEM_LIMIT)


def _silu(x):
    return x * (1.0 / (1.0 + jnp.exp(-x)))


def _layernorm(y, g, b):
    mu = jnp.mean(y, axis=-1, keepdims=True)
    yc = y - mu
    var = jnp.mean(yc * yc, axis=-1, keepdims=True)
    return yc * lax.rsqrt(var + LN_EPS) * g + b


def _mixer_body(p_ref, q_ref, k_ref, v_ref, r_ref, gl_ref, w2_ref, gb_ref, nw_ref, pw_ref, ps_ref,
                o_ref, pbuf, state, kvbuf, sall, obuf, gk_s):
    ts = p_ref.shape[0]
    s_idx = pl.program_id(1)
    halo = POOL_WINDOWS[-1]

    @pl.when(s_idx == 0)
    def _():
        pbuf[pl.ds(0, halo), :] = jnp.zeros((halo, POOL_WIDTH), F32)
        state[...] = jnp.zeros(state.shape, F32)

    p = p_ref[...]
    pbuf[pl.ds(halo, ts), :] = p
    pos = s_idx * ts + lax.broadcasted_iota(I32, (ts, 1), 0)
    for g, w in enumerate(POOL_WINDOWS):
        c0 = g * POOL_GROUP_DIM
        ext = pbuf[:, pl.ds(c0, POOL_GROUP_DIM)]
        sh = 1
        while sh < w:
            ext = ext + pltpu.roll(ext, sh, axis=0)
            sh *= 2
        acc = ext[halo:, :]
        cnt = jnp.minimum(pos + 1, w).astype(F32)
        mixed = acc / cnt - p[:, c0:c0 + POOL_GROUP_DIM]
        og = jnp.dot(mixed.astype(BF16), pw_ref[g], preferred_element_type=F32)
        o_ref[:, pl.ds(c0, POOL_GROUP_DIM)] = (og * ps_ref[:, pl.ds(c0, POOL_GROUP_DIM)]).astype(o_ref.dtype)
    pbuf[pl.ds(0, halo), :] = pbuf[pl.ds(ts, halo), :]

    nchunk = ts // GLA_CHUNK
    glog = jnp.dot(gl_ref[...].astype(BF16), w2_ref[...], preferred_element_type=F32) + gb_ref[...]
    gk = (jnp.minimum(glog, 0.0) - jnp.log(1.0 + jnp.exp(-jnp.abs(glog)))) * (1.0 / GLA_GATE_NORMALIZER)
    row = lax.broadcasted_iota(I32, (ts, 1), 0)
    rin = row % GLA_CHUNK
    b = gk
    sh = 1
    while sh < GLA_CHUNK:
        b = b + jnp.where(rin >= sh, pltpu.roll(b, sh, axis=0), 0.0)
        sh *= 2
    b3 = b.reshape(nchunk, GLA_CHUNK, GLA_DK_TOTAL)
    bmid = b3[:, GLA_CHUNK // 2 - 1:GLA_CHUNK // 2, :]
    blast = b3[:, GLA_CHUNK - 1:GLA_CHUNK, :]
    lane = lax.broadcasted_iota(I32, (1, LANES), 1)
    head_lane = [lane < GLA_DK, lane >= GLA_DK]
    q_scale = GLA_DK ** -0.5
    safe = jnp.max(jnp.abs(b3 - bmid)) <= GLA_SAFE_EXP

    @pl.when(safe)
    def _chunked():
        v = v_ref[...]
        vb = v.astype(BF16)
        q3 = (q_ref[...] * q_scale).reshape(nchunk, GLA_CHUNK, GLA_DK_TOTAL)
        k3 = k_ref[...].reshape(nchunk, GLA_CHUNK, GLA_DK_TOTAL)
        qs = (q3 * jnp.exp(b3 - bmid)).reshape(ts, GLA_DK_TOTAL)
        ks = (k3 * jnp.exp(bmid - b3)).reshape(ts, GLA_DK_TOTAL)
        qd = (q3 * jnp.exp(b3)).reshape(ts, GLA_DK_TOTAL)
        kd = (k3 * jnp.exp(blast - b3)).reshape(ts, GLA_DK_TOTAL)
        cdec = jnp.exp(blast).reshape(nchunk, GLA_DK_TOTAL)

        blk = LANES
        ri = lax.broadcasted_iota(I32, (blk, blk), 0)
        ci = lax.broadcasted_iota(I32, (blk, blk), 1)
        causal = (ri // GLA_CHUNK == ci // GLA_CHUNK) & (ri >= ci)
        o_intra = [[None] * (ts // blk) for _ in range(GLA_HEADS)]
        for rb in range(ts // blk):
            rs = slice(rb * blk, (rb + 1) * blk)
            for pair in range(GLA_HEADS // 2):
                ls = slice(pair * LANES, (pair + 1) * LANES)
                ks_p = ks[rs, ls].astype(BF16)
                q_p = qs[rs, ls]
                q2 = jnp.concatenate([jnp.where(head_lane[sub], q_p, 0.0) for sub in range(2)], axis=0).astype(BF16)
                sc2 = lax.dot_general(q2, ks_p, (((1,), (1,)), ((), ())), preferred_element_type=F32)
                for sub in range(2):
                    h = pair * 2 + sub
                    sc = jnp.where(causal, sc2[sub * blk:(sub + 1) * blk], 0.0).astype(BF16)
                    o_intra[h][rb] = jnp.dot(sc, vb[rs, h * GLA_DV:(h + 1) * GLA_DV], preferred_element_type=F32)

        cpb = blk // GLA_CHUNK
        chunk_of_col = lax.broadcasted_iota(I32, (1, blk), 1) // GLA_CHUNK
        kdb = kd.astype(BF16)
        for rb in range(ts // blk):
            rs = slice(rb * blk, (rb + 1) * blk)
            for pair in range(GLA_HEADS // 2):
                ls = slice(pair * LANES, (pair + 1) * LANES)
                inc = []
                for sub in range(2):
                    h = pair * 2 + sub
                    v_t = v[rs, h * GLA_DV:(h + 1) * GLA_DV].T
                    lhs = jnp.concatenate([jnp.where(chunk_of_col == c, v_t, 0.0) for c in range(cpb)],
                                          axis=0).astype(BF16)
                    inc.append(jnp.dot(lhs, kdb[rs, ls], preferred_element_type=F32))
                stacked = jnp.where(head_lane[0], inc[0], inc[1])
                for c in range(cpb):
                    kvbuf[pair, rb * cpb + c] = stacked[c * LANES:(c + 1) * LANES]
        for pair in range(GLA_HEADS // 2):
            ls = slice(pair * LANES, (pair + 1) * LANES)
            st = state[pair]
            for c in range(nchunk):
                sall[pair, c] = st.astype(BF16)
                st = st * cdec[c:c + 1, ls] + kvbuf[pair, c]
            state[pair] = st
        o_inter = [[None] * nchunk for _ in range(GLA_HEADS)]
        for pair in range(GLA_HEADS // 2):
            ls = slice(pair * LANES, (pair + 1) * LANES)
            for c in range(nchunk):
                rs = slice(c * GLA_CHUNK, (c + 1) * GLA_CHUNK)
                q_c = qd[rs, ls]
                q2 = jnp.concatenate([jnp.where(head_lane[sub], q_c, 0.0) for sub in range(2)], axis=0).astype(BF16)
                res = lax.dot_general(q2, sall[pair, c], (((1,), (1,)), ((), ())), preferred_element_type=F32)
                for sub in range(2):
                    o_inter[pair * 2 + sub][c] = res[sub * GLA_CHUNK:(sub + 1) * GLA_CHUNK]
        for h in range(GLA_HEADS):
            obuf[:, pl.ds(h * GLA_DV, GLA_DV)] = (jnp.concatenate(o_intra[h], axis=0)
                                                  + jnp.concatenate(o_inter[h], axis=0))

    @pl.when(jnp.logical_not(safe))
    def _row_by_row():
        gk_s[...] = gk
        row8 = lax.broadcasted_iota(I32, (SUBLANES, 1), 0)

        def slab(i8, carry):
            r0 = pl.multiple_of(i8 * SUBLANES, SUBLANES)
            q8 = q_ref[pl.ds(r0, SUBLANES), :] * q_scale
            k8 = k_ref[pl.ds(r0, SUBLANES), :]
            v8 = v_ref[pl.ds(r0, SUBLANES), :]
            g8 = jnp.exp(gk_s[pl.ds(r0, SUBLANES), :])
            outs = [jnp.zeros((SUBLANES, GLA_DV), F32) for _ in range(GLA_HEADS)]
            for pair in range(GLA_HEADS // 2):
                ls = slice(pair * LANES, (pair + 1) * LANES)
                st = state[pair]
                for r in range(SUBLANES):
                    sel = row8 == r
                    k_r = jnp.where(sel, k8[:, ls], 0.0).astype(BF16)
                    inc = []
                    for sub in range(2):
                        h = pair * 2 + sub
                        v_r = jnp.where(sel, v8[:, h * GLA_DV:(h + 1) * GLA_DV], 0.0).astype(BF16)
                        inc.append(lax.dot_general(v_r, k_r, (((0,), (0,)), ((), ())), preferred_element_type=F32))
                    st = st * g8[r:r + 1, ls] + jnp.where(head_lane[0], inc[0], inc[1])
                    q2 = jnp.concatenate([jnp.where(sel & head_lane[sub], q8[:, ls], 0.0) for sub in range(2)],
                                         axis=0).astype(BF16)
                    res = lax.dot_general(q2, st.astype(BF16), (((1,), (1,)), ((), ())),
                                          preferred_element_type=F32)
                    for sub in range(2):
                        outs[pair * 2 + sub] = outs[pair * 2 + sub] + res[sub * SUBLANES:(sub + 1) * SUBLANES]
                state[pair] = st
            obuf[pl.ds(r0, SUBLANES), :] = jnp.concatenate(outs, axis=1)
            return carry
        lax.fori_loop(0, ts // SUBLANES, slab, 0)

    nw = nw_ref[...]
    r = r_ref[...]
    for h in range(GLA_HEADS):
        o = obuf[:, pl.ds(h * GLA_DV, GLA_DV)]
        o = o * lax.rsqrt(jnp.mean(o * o, axis=-1, keepdims=True) + RMS_EPS) * nw
        o = o * _silu(r[:, h * GLA_DV:(h + 1) * GLA_DV])
        o_ref[:, pl.ds(POOL_WIDTH + h * GLA_DV, GLA_DV)] = o.astype(o_ref.dtype)


def _front_body(x_ref, win_ref, w2_ref, gb_ref, nw_ref, pw_ref, ps_ref, wout_ref, g_ref, b_ref, h_ref, h4_ref,
                pp_s, pq_s, pk_s, pv_s, pr_s, pg_s, mix_s, pbuf, state, kvbuf, sall, obuf, gk_s):
    groups = (pp_s, pq_s, pk_s, pv_s, pr_s, pg_s)
    cols = [0]
    for ref in groups:
        cols.append(cols[-1] + ref.shape[1])
    xb = x_ref[...].astype(BF16)
    for gi in (0, 5, 1, 2, 3, 4):
        groups[gi][...] = jnp.dot(xb, win_ref[:, pl.ds(cols[gi], groups[gi].shape[1])],
                                  preferred_element_type=F32)
    _mixer_body(*groups, w2_ref, gb_ref, nw_ref, pw_ref, ps_ref, mix_s, pbuf, state, kvbuf, sall, obuf, gk_s)
    y = DEEPNORM_ALPHA * x_ref[...] + jnp.dot(mix_s[...], wout_ref[...], preferred_element_type=F32)
    h = _layernorm(y, g_ref[...], b_ref[...])
    h_ref[...] = h
    for s, w in enumerate(_pack_row_words(h)):
        h4_ref[_word_plane(0, h.shape[0], s), :] = w


def _front(x2d, batch, seq, w_in_b, w2p, gate_b, norm_w, pool_w, pool_scale, w_out_b, ln_g, ln_b):
    t, d = x2d.shape
    ts = MIX_TS
    nseq = seq // ts

    def full(shape):
        return pl.BlockSpec(shape, lambda bi, si: (0,) * len(shape))

    return pl.pallas_call(
        _front_body,
        out_shape=(jax.ShapeDtypeStruct((t, d), F32), jax.ShapeDtypeStruct((t * ROW_WORDS, LANES), U32)),
        grid=(batch, nseq),
        in_specs=[pl.BlockSpec((ts, d), lambda bi, si: (bi * nseq + si, 0)),
                  full(w_in_b.shape), full(w2p.shape), full(gate_b.shape), full(norm_w.shape),
                  full(pool_w.shape), full(pool_scale.shape), full(w_out_b.shape), full(ln_g.shape), full(ln_b.shape)],
        out_specs=(pl.BlockSpec((ts, d), lambda bi, si: (bi * nseq + si, 0)),
                   pl.BlockSpec((ts * ROW_WORDS, LANES), lambda bi, si: (bi * nseq + si, 0))),
        scratch_shapes=[pltpu.VMEM((ts, POOL_WIDTH), F32), pltpu.VMEM((ts, GLA_DK_TOTAL), F32),
                        pltpu.VMEM((ts, GLA_DK_TOTAL), F32), pltpu.VMEM((ts, GLA_WIDTH), F32),
                        pltpu.VMEM((ts, GLA_WIDTH), F32), pltpu.VMEM((ts, D_IN_PAD - 2 * GLA_DK_TOTAL
                                                                      - 2 * GLA_WIDTH - POOL_WIDTH), F32),
                        pltpu.VMEM((ts, POOL_WIDTH + GLA_WIDTH), BF16),
                        pltpu.VMEM((ts + POOL_WINDOWS[-1], POOL_WIDTH), F32),
                        pltpu.VMEM((GLA_HEADS // 2, LANES, GLA_DV), F32),
                        pltpu.VMEM((GLA_HEADS // 2, ts // GLA_CHUNK, LANES, GLA_DV), F32),
                        pltpu.VMEM((GLA_HEADS // 2, ts // GLA_CHUNK, LANES, GLA_DV), BF16),
                        pltpu.VMEM((ts, GLA_WIDTH), F32), pltpu.VMEM((ts, GLA_DK_TOTAL), F32)],
        compiler_params=_cparams(("arbitrary", "arbitrary")),
        name="front",
    )(x2d, w_in_b, w2p, gate_b, norm_w, pool_w, pool_scale, w_out_b, ln_g, ln_b)


def _pack_row_words(x):
    half = x.shape[1] // 2
    u = pltpu.bitcast(x.astype(BF16).astype(F32), U32)
    hi_mask = jnp.uint32(HIGH_HALF)
    return [(u[:, half + s * LANES:half + (s + 1) * LANES] & hi_mask) | (u[:, s * LANES:(s + 1) * LANES] >> 16)
            for s in range(ROW_WORDS)]


def _unpack_row_words(w):
    return pltpu.bitcast(w << 16, F32), pltpu.bitcast(w & jnp.uint32(HIGH_HALF), F32)


def _rows(ref, first, n, align=1):
    if isinstance(first, int):
        start = first * ROW_WORDS
    else:
        start = pl.multiple_of(first * ROW_WORDS, ROW_WORDS * align)
    return ref.at[pl.ds(start, n * ROW_WORDS), :]


def _word_plane(first, m, s):
    return pl.ds(first * ROW_WORDS + s, m, stride=ROW_WORDS)


def _first_argmax_rows(val, rowf, nrows):
    m = jnp.max(val, axis=0, keepdims=True)
    first = jnp.min(jnp.where(val == m, rowf, float(nrows)), axis=0, keepdims=True)
    return m, first, rowf == first


def _router_body(h_ref, whi_ref, wlo_ref, bias_ref, idx_ref, gate_ref, rank_ref, cnt_ref, carry):
    tm = h_ref.shape[0]
    i = pl.program_id(0)

    @pl.when(i == 0)
    def _():
        carry[...] = jnp.zeros(carry.shape, F32)

    h = h_ref[...]
    h_hi = h.astype(BF16)
    h_lo = (h - h_hi.astype(F32)).astype(BF16)
    nt = (((1,), (1,)), ((), ()))
    logits = (lax.dot_general(whi_ref[...], h_hi, nt, preferred_element_type=F32)
              + lax.dot_general(whi_ref[...], h_lo, nt, preferred_element_type=F32)
              + lax.dot_general(wlo_ref[...], h_hi, nt, preferred_element_type=F32))
    scores = 1.0 / (1.0 + jnp.exp(-logits))
    biased = scores + bias_ref[...]
    neg = -jnp.inf

    grp = biased.reshape(N_GROUPS, GROUP_SIZE, tm)
    gi = lax.broadcasted_iota(I32, (N_GROUPS, GROUP_SIZE, tm), 1).astype(F32)
    g1 = jnp.max(grp, axis=1, keepdims=True)
    f1 = jnp.min(jnp.where(grp == g1, gi, float(GROUP_SIZE)), axis=1, keepdims=True)
    g2 = jnp.max(jnp.where(gi == f1, neg, grp), axis=1, keepdims=True)
    gscore = (g1 + g2).reshape(N_GROUPS, tm)

    growf = lax.broadcasted_iota(I32, (N_GROUPS, tm), 0).astype(F32)
    gsel = jnp.zeros((N_GROUPS, tm), F32)
    gval = gscore
    for _ in range(TOPK_GROUPS):
        _, _, pick = _first_argmax_rows(gval, growf, N_GROUPS)
        gsel = jnp.where(pick, 1.0, gsel)
        gval = jnp.where(pick, neg, gval)
    emask = jnp.broadcast_to(gsel.reshape(N_GROUPS, 1, tm), (N_GROUPS, GROUP_SIZE, tm)).reshape(N_EXPERTS, tm)

    rowf = lax.broadcasted_iota(I32, (N_EXPERTS, tm), 0).astype(F32)
    val = jnp.where(emask > 0.0, biased, neg)
    onehot = jnp.zeros((N_EXPERTS, tm), F32)
    picks, idxs, ws = [], [], []
    for _ in range(TOP_K):
        _, first, pick = _first_argmax_rows(val, rowf, N_EXPERTS)
        picks.append(pick)
        idxs.append(first)
        ws.append(jnp.sum(jnp.where(pick, scores, 0.0), axis=0, keepdims=True))
        onehot = jnp.where(pick, 1.0, onehot)
        val = jnp.where(pick, neg, val)
    w = jnp.concatenate(ws, axis=0)
    gate_ref[...] = w / jnp.sum(w, axis=0, keepdims=True) * ROUTED_SCALE
    idx_ref[...] = jnp.concatenate(idxs, axis=0).astype(I32)

    ti = lax.broadcasted_iota(I32, (tm, tm), 0)
    tj = lax.broadcasted_iota(I32, (tm, tm), 1)
    upper = jnp.where(ti < tj, 1.0, 0.0).astype(BF16)
    prefix = jnp.dot(onehot.astype(BF16), upper, preferred_element_type=F32) + carry[...]
    ranks = [jnp.sum(jnp.where(pk, prefix, 0.0), axis=0, keepdims=True) for pk in picks]
    rank_ref[...] = jnp.concatenate(ranks, axis=0).astype(I32)
    carry[...] = carry[...] + jnp.sum(onehot, axis=1, keepdims=True)
    cnt_ref[...] = carry[...]


def _router(h, wt, bias_col):
    t, d = h.shape
    tm = ROUTE_TM
    wt_hi = wt.astype(BF16)
    wt_lo = (wt - wt_hi.astype(F32)).astype(BF16)
    return pl.pallas_call(
        _router_body,
        out_shape=(jax.ShapeDtypeStruct((TOP_K, t), I32), jax.ShapeDtypeStruct((TOP_K, t), F32),
                   jax.ShapeDtypeStruct((TOP_K, t), I32), jax.ShapeDtypeStruct((N_EXPERTS, 1), F32)),
        grid=(t // tm,),
        in_specs=[pl.BlockSpec((tm, d), lambda i: (i, 0)),
                  pl.BlockSpec((N_EXPERTS, d), lambda i: (0, 0)),
                  pl.BlockSpec((N_EXPERTS, d), lambda i: (0, 0)),
                  pl.BlockSpec((N_EXPERTS, 1), lambda i: (0, 0))],
        out_specs=(pl.BlockSpec((TOP_K, tm), lambda i: (0, i)),
                   pl.BlockSpec((TOP_K, tm), lambda i: (0, i)),
                   pl.BlockSpec((TOP_K, tm), lambda i: (0, i)),
                   pl.BlockSpec((N_EXPERTS, 1), lambda i: (0, 0))),
        scratch_shapes=[pltpu.VMEM((N_EXPERTS, 1), F32)],
        compiler_params=_cparams(("arbitrary",)),
        name="router",
    )(h, wt_hi, wt_lo, bias_col)


def _positions_body(idx_ref, rank_ref, start_ref, pos_ref):
    tm = idx_ref.shape[1]
    rowi = lax.broadcasted_iota(I32, (N_EXPERTS, tm), 0)
    start = start_ref[...]
    idx = idx_ref[...]
    rows = [jnp.sum(jnp.where(rowi == idx[k:k + 1, :], start, 0.0), axis=0, keepdims=True) for k in range(TOP_K)]
    pos_ref[...] = jnp.concatenate(rows, axis=0).astype(I32) + rank_ref[...]


def _positions(idx_t, rank_t, start_col):
    t = idx_t.shape[1]
    tm = POS_TM
    return pl.pallas_call(
        _positions_body,
        out_shape=jax.ShapeDtypeStruct((TOP_K, t), I32),
        grid=(t // tm,),
        in_specs=[pl.BlockSpec((TOP_K, tm), lambda i: (0, i)),
                  pl.BlockSpec((TOP_K, tm), lambda i: (0, i)),
                  pl.BlockSpec((N_EXPERTS, 1), lambda i: (0, 0))],
        out_specs=pl.BlockSpec((TOP_K, tm), lambda i: (0, i)),
        compiler_params=_cparams(("arbitrary",)),
        name="positions",
    )(idx_t, rank_t, start_col)


def _pad_fill_copy(zeros, xs_ref, sem, row, nrows):
    return pltpu.make_async_copy(_rows(zeros, 0, nrows), _rows(xs_ref, row, nrows), sem)


def _dispatch_body(pad_row_ref, pad_n_ref, nu_ref, pos_ref, h4_ref, xs_ref, zeros, sem, pad_sem):
    i = pl.program_id(0)
    tm = h4_ref.shape[0] // ROW_WORDS
    half = MOE_BM // 2
    pad_bits = [1 << j for j in range(MOE_BM.bit_length() - 1)]
    n_half_blocks = xs_ref.shape[0] // (half * ROW_WORDS)

    def pad_pass(wait):
        def go(cp):
            if wait:
                cp.wait()
            else:
                cp.start()

        def body(e, carry):
            row = pad_row_ref[e]
            n = pad_n_ref[e]
            for bit in pad_bits:
                @pl.when((n & bit) != 0)
                def _():
                    go(_pad_fill_copy(zeros, xs_ref, pad_sem, row + (n & (bit - 1)), bit))
            return carry
        lax.fori_loop(0, N_EXPERTS, body, 0)

        def tail(hb, carry):
            go(_pad_fill_copy(zeros, xs_ref, pad_sem, hb * half, half))
            return carry
        lax.fori_loop(nu_ref[0] * 2, n_half_blocks, tail, 0)

    @pl.when(i == 0)
    def _():
        zeros[...] = jnp.zeros(zeros.shape, U32)
        pad_pass(False)

    def body(tt, carry):
        for dt in range(DISP_UNROLL):
            t = tt * DISP_UNROLL + dt
            for k in range(TOP_K):
                p = pos_ref[0, 0, t * TOP_K + k]
                pltpu.make_async_copy(_rows(h4_ref, t, 1), _rows(xs_ref, p, 1), sem).start(priority=k % 2)
        return carry
    lax.fori_loop(0, tm // DISP_UNROLL, body, 0)

    for k in range(TOP_K):
        pltpu.make_async_copy(h4_ref, _rows(xs_ref, 0, tm), sem).wait()

    @pl.when(i == pl.num_programs(0) - 1)
    def _():
        pad_pass(True)


def _dispatch(pad_row, pad_n, n_used, pos3, h4, n_rows):
    nt, _, per_step = pos3.shape
    tm = per_step // TOP_K
    grid_spec = pltpu.PrefetchScalarGridSpec(
        num_scalar_prefetch=3,
        grid=(nt,),
        in_specs=[pl.BlockSpec((1, 1, per_step), lambda i, a, b, c: (i, 0, 0), memory_space=pltpu.SMEM),
                  pl.BlockSpec((tm * ROW_WORDS, LANES), lambda i, a, b, c: (i, 0))],
        out_specs=pl.BlockSpec(memory_space=pl.ANY),
        scratch_shapes=[pltpu.VMEM((MOE_BM // 2 * ROW_WORDS, LANES), U32),
                        pltpu.SemaphoreType.DMA, pltpu.SemaphoreType.DMA],
    )
    return pl.pallas_call(
        _dispatch_body,
        out_shape=jax.ShapeDtypeStruct((n_rows * ROW_WORDS, LANES), U32),
        grid_spec=grid_spec,
        compiler_params=_cparams(("arbitrary",)),
        name="dispatch",
    )(pad_row, pad_n, n_used, pos3, h4)


def _weight_copies(e, w_hbm, stage, sem, slot):
    return [pltpu.make_async_copy(w.at[e], st.at[slot], sem.at[slot, j])
            for j, (w, st) in enumerate(zip(w_hbm, stage))]


def _row_block_copy(xs_hbm, xbuf, xsem, j):
    slot = j % X_RING
    return pltpu.make_async_copy(_rows(xs_hbm, j * MOE_BM, MOE_BM, align=MOE_BM), xbuf.at[slot], xsem.at[slot])


def _experts_body(be_ref, nu_ref, nx_ref, ord_ref, xs_hbm, wg_hbm, wu_hbm, wd_hbm, y4_ref,
                  xbuf, xsem, sg, su, sd, wsem, wg_b, wu_b, wd_b):
    w_hbm = (wg_hbm, wu_hbm, wd_hbm)
    stage = (sg, su, sd)
    n_used = nu_ref[0]

    def do_block(i, out_row):
        @pl.when(i < n_used)
        def _():
            e = be_ref[i]
            prev_e = be_ref[jnp.maximum(i - 1, 0)]

            @pl.when(i == 0)
            def _():
                for j in range(X_RING - 1):
                    @pl.when(j < n_used)
                    def _():
                        _row_block_copy(xs_hbm, xbuf, xsem, j).start()

            @pl.when(i + X_RING - 1 < n_used)
            def _():
                _row_block_copy(xs_hbm, xbuf, xsem, i + X_RING - 1).start()

            @pl.when((i == 0) | (e != prev_e))
            def _():
                wslot = ord_ref[e] % 2
                n1 = nx_ref[e]
                n2 = jnp.where(n1 >= 0, nx_ref[jnp.maximum(n1, 0)], -1)

                @pl.when(i == 0)
                def _():
                    for cp in _weight_copies(e, w_hbm, stage, wsem, wslot):
                        cp.start()

                    @pl.when(n1 >= 0)
                    def _():
                        for cp in _weight_copies(n1, w_hbm, stage, wsem, 1 - wslot):
                            cp.start()
                for cp in _weight_copies(e, w_hbm, stage, wsem, wslot):
                    cp.wait()
                wg_b[...] = sg[wslot].astype(BF16)
                wu_b[...] = su[wslot].astype(BF16)
                wd_b[...] = sd[wslot].astype(BF16)

                @pl.when(n2 >= 0)
                def _():
                    for cp in _weight_copies(n2, w_hbm, stage, wsem, wslot):
                        cp.start()

            _row_block_copy(xs_hbm, xbuf, xsem, i).wait()
            slot = i % X_RING
            parts = [_unpack_row_words(xbuf[slot, _word_plane(0, MOE_BM, s), :]) for s in range(ROW_WORDS)]
            x = jnp.concatenate([p[0] for p in parts] + [p[1] for p in parts], axis=1).astype(BF16)
            g = jnp.dot(x, wg_b[...], preferred_element_type=F32)
            u = jnp.dot(x, wu_b[...], preferred_element_type=F32)
            a = (_silu(g) * u).astype(BF16)
            y = jnp.dot(a, wd_b[...], preferred_element_type=F32)
            for s, w in enumerate(_pack_row_words(y)):
                y4_ref[_word_plane(out_row, MOE_BM, s), :] = w

        @pl.when(i >= n_used)
        def _():
            y4_ref[pl.ds(out_row * ROW_WORDS, MOE_BM * ROW_WORDS), :] = jnp.zeros((MOE_BM * ROW_WORDS, LANES), U32)

    for sub in range(EXP_BLOCKS_PER_STEP):
        do_block(pl.program_id(0) * EXP_BLOCKS_PER_STEP + sub, sub * MOE_BM)


def _experts(block_e, n_used, next_e, ord_e, xs, wg, wu, wd):
    nblk = block_e.shape[0]
    bm = MOE_BM * EXP_BLOCKS_PER_STEP
    e, d, de = wg.shape

    grid_spec = pltpu.PrefetchScalarGridSpec(
        num_scalar_prefetch=4,
        grid=(nblk // EXP_BLOCKS_PER_STEP,),
        in_specs=[pl.BlockSpec(memory_space=pl.ANY),
                  pl.BlockSpec(memory_space=pl.ANY),
                  pl.BlockSpec(memory_space=pl.ANY),
                  pl.BlockSpec(memory_space=pl.ANY)],
        out_specs=pl.BlockSpec((bm * ROW_WORDS, LANES), lambda i, be, nu, nx, od: (i, 0)),
        scratch_shapes=[pltpu.VMEM((X_RING, MOE_BM * ROW_WORDS, LANES), U32), pltpu.SemaphoreType.DMA((X_RING,)),
                        pltpu.VMEM((2, d, de), F32), pltpu.VMEM((2, d, de), F32), pltpu.VMEM((2, de, d), F32),
                        pltpu.SemaphoreType.DMA((2, 3)),
                        pltpu.VMEM((d, de), BF16), pltpu.VMEM((d, de), BF16), pltpu.VMEM((de, d), BF16)],
    )
    return pl.pallas_call(
        _experts_body,
        out_shape=jax.ShapeDtypeStruct((nblk * MOE_BM * ROW_WORDS, LANES), U32),
        grid_spec=grid_spec,
        compiler_params=_cparams(("arbitrary",)),
        name="experts",
    )(block_e, n_used, next_e, ord_e, xs, wg, wu, wd)


def _combine_body(pos_cur, pos_nxt, y4_ref, h_ref, gate_ref, wsg_ref, wsu_ref, wsd_ref, g_ref, b_ref,
                  o_ref, buf, sem):
    i = pl.program_id(0)
    nb = pl.num_programs(0)
    tm, d = h_ref.shape
    slot = i % 2

    def issue(pos_ref, sl):
        def body(t, carry):
            for k in range(TOP_K):
                p = pos_ref[0, 0, t * TOP_K + k]
                pltpu.make_async_copy(_rows(y4_ref, p, 1), _rows(buf.at[sl], k * tm + t, 1),
                                      sem.at[sl]).start(priority=k % 2)
            return carry
        lax.fori_loop(0, tm, body, 0)

    @pl.when(i == 0)
    def _():
        issue(pos_cur, 0)

    @pl.when(i + 1 < nb)
    def _():
        issue(pos_nxt, 1 - slot)

    h = h_ref[...]
    hb = h.astype(BF16)
    sg = jnp.dot(hb, wsg_ref[...], preferred_element_type=F32)
    su = jnp.dot(hb, wsu_ref[...], preferred_element_type=F32)
    shared = jnp.dot((_silu(sg) * su).astype(BF16), wsd_ref[...], preferred_element_type=F32)

    pltpu.make_async_copy(_rows(y4_ref, 0, tm * TOP_K), buf.at[slot], sem.at[slot]).wait()
    gates = gate_ref[...]
    lo_cols, hi_cols = [], []
    for s in range(ROW_WORDS):
        lo_acc = hi_acc = None
        for k in range(TOP_K):
            lo, hi = _unpack_row_words(buf[slot, _word_plane(k * tm, tm, s), :])
            gk = gates[:, k:k + 1]
            lo_acc = gk * lo if lo_acc is None else lo_acc + gk * lo
            hi_acc = gk * hi if hi_acc is None else hi_acc + gk * hi
        lo_cols.append(lo_acc)
        hi_cols.append(hi_acc)
    routed = jnp.concatenate(lo_cols + hi_cols, axis=1)
    o_ref[...] = _layernorm(DEEPNORM_ALPHA * h + (routed + shared), g_ref[...], b_ref[...])


def _combine(pos3, y4, h, gates, wsg, wsu, wsd, g, b):
    t, d = h.shape
    tm = COMB_TM
    nt = t // tm
    ds_ = wsg.shape[1]
    return pl.pallas_call(
        _combine_body,
        out_shape=jax.ShapeDtypeStruct((t, d), F32),
        grid=(nt,),
        in_specs=[pl.BlockSpec((1, 1, tm * TOP_K), lambda i: (i, 0, 0), memory_space=pltpu.SMEM),
                  pl.BlockSpec((1, 1, tm * TOP_K), lambda i: (jnp.minimum(i + 1, nt - 1), 0, 0),
                               memory_space=pltpu.SMEM),
                  pl.BlockSpec(memory_space=pl.ANY),
                  pl.BlockSpec((tm, d), lambda i: (i, 0)),
                  pl.BlockSpec((tm, TOP_K), lambda i: (i, 0)),
                  pl.BlockSpec((d, ds_), lambda i: (0, 0)),
                  pl.BlockSpec((d, ds_), lambda i: (0, 0)),
                  pl.BlockSpec((ds_, d), lambda i: (0, 0)),
                  pl.BlockSpec((1, d), lambda i: (0, 0)),
                  pl.BlockSpec((1, d), lambda i: (0, 0))],
        out_specs=pl.BlockSpec((tm, d), lambda i: (i, 0)),
        scratch_shapes=[pltpu.VMEM((2, tm * TOP_K * ROW_WORDS, LANES), U32),
                        pltpu.SemaphoreType.DMA((2,))],
        compiler_params=_cparams(("arbitrary",)),
        name="combine",
    )(pos3, pos3, y4, h, gates, wsg, wsu, wsd, g, b)


def _expert_tables(counts, nblk):
    bm = MOE_BM
    cnt = counts.reshape(N_EXPERTS).astype(I32)
    padded = (cnt + bm - 1) // bm * bm
    padded_end = jnp.cumsum(padded)
    padded_start = padded_end - padded
    block_rows = jnp.arange(nblk, dtype=I32) * bm
    block_e = jnp.sum((padded_end[None, :] <= block_rows[:, None]).astype(I32), axis=1)
    block_e = jnp.minimum(block_e, N_EXPERTS - 1)
    n_used = (padded_end[-1:] // bm).astype(I32)
    ids = jnp.where(cnt > 0, jnp.arange(N_EXPERTS, dtype=I32), N_EXPERTS)
    after = jnp.concatenate([lax.cummin(ids, reverse=True)[1:], jnp.full((1,), N_EXPERTS, I32)])
    next_e = jnp.where(after < N_EXPERTS, after, -1).astype(I32)
    ord_e = (jnp.cumsum((cnt > 0).astype(I32)) - 1).astype(I32)
    return padded_start, padded_start + cnt, padded - cnt, block_e, n_used, next_e, ord_e


def kernel(x, w_in, gla_gate_w2, gla_gate_b, gla_norm_w, pool_w_group, pool_scale, w_out, ln1_g, ln1_b,
           router_w, router_bias, w_exp_gate, w_exp_up, w_exp_down, w_sh_gate, w_sh_up, w_sh_down, ln2_g, ln2_b):
    batch, seq, d = x.shape
    t = batch * seq
    h2d = x.reshape(t, d)
    for l in range(DEPTH):
        d_in = w_in.shape[2]
        w_in_b = jnp.pad(w_in[l], ((0, 0), (0, D_IN_PAD - d_in))).astype(BF16)
        w2p = jnp.pad(gla_gate_w2[l], ((0, LANES - GLA_GATE_RANK), (0, 0))).astype(BF16)
        h, h4 = _front(h2d, batch, seq, w_in_b, w2p, gla_gate_b[l].reshape(1, -1), gla_norm_w[l].reshape(1, -1),
                       pool_w_group[l].astype(BF16), pool_scale[l].reshape(1, -1),
                       w_out[l].astype(BF16), ln1_g[l].reshape(1, -1), ln1_b[l].reshape(1, -1))
        idx_t, gate_t, rank_t, counts = _router(h, router_w[l].T, router_bias[l].reshape(-1, 1))
        nblk = (t * TOP_K + N_EXPERTS * (MOE_BM - 1)) // MOE_BM
        nblk = -(-nblk // EXP_BLOCKS_PER_STEP) * EXP_BLOCKS_PER_STEP
        start, pad_row, pad_n, block_e, n_used, next_e, ord_e = _expert_tables(counts, nblk)
        pos_t = _positions(idx_t, rank_t, start.astype(F32).reshape(-1, 1))
        pos_tok = pos_t.T
        xs = _dispatch(pad_row, pad_n, n_used, pos_tok.reshape(t // DISP_TM, 1, DISP_TM * TOP_K), h4, nblk * MOE_BM)
        y4 = _experts(block_e, n_used, next_e, ord_e, xs, w_exp_gate[l], w_exp_up[l], w_exp_down[l])
        h2d = _combine(pos_tok.reshape(t // COMB_TM, 1, COMB_TM * TOP_K), y4, h, gate_t.T,
                       w_sh_gate[l].astype(BF16), w_sh_up[l].astype(BF16), w_sh_down[l].astype(BF16),
                       ln2_g[l].reshape(1, -1), ln2_b[l].reshape(1, -1))
    return h2d.reshape(batch, seq, d)
```

```python
import jax
import jax.numpy as jnp
from jax import lax
from jax.experimental import pallas as pl
from jax.experimental.pallas import tpu as pltpu

F32 = jnp.float32
BF16 = jnp.bfloat16
I32 = jnp.int32
U32 = jnp.uint32
HIGH_HALF = 0xFFFF0000

POOL_WINDOWS = (2, 4, 8, 16)
POOL_GROUP_DIM = 128
POOL_WIDTH = 512
GLA_HEADS = 4
GLA_DK = 64
GLA_DV = 128
GLA_DK_TOTAL = 256
GLA_WIDTH = 512
GLA_GATE_RANK = 16
GLA_GATE_NORMALIZER = 16.0
GLA_CHUNK = 16
GLA_SAFE_EXP = 60.0
N_EXPERTS = 256
TOP_K = 8
N_GROUPS = 8
GROUP_SIZE = N_EXPERTS // N_GROUPS
TOPK_GROUPS = 4
ROUTED_SCALE = 2.5
DEPTH = 1
DEEPNORM_ALPHA = (2.0 * DEPTH) ** 0.25
LN_EPS = 1e-5
RMS_EPS = 1e-5

LANES = 128
SUBLANES = 8
VMEM_LIMIT = 56 * 1024 * 1024

MIX_TS = 512
ROUTE_TM = 512
MOE_BM = 256
COMB_TM = 512
POS_TM = 512
DISP_TM = 512
DISP_UNROLL = 4
ROW_WORDS = 4
X_RING = 6
EXP_BLOCKS_PER_STEP = 8
D_IN_PAD = 2176


def _cparams(sem):
    return pltpu.CompilerParams(dimension_semantics=sem, vmem_limit_bytes=VMEM_LIMIT)


def _silu(x):
    return x * (1.0 / (1.0 + jnp.exp(-x)))


def _layernorm(y, g, b):
    mu = jnp.mean(y, axis=-1, keepdims=True)
    yc = y - mu
    var = jnp.mean(yc * yc, axis=-1, keepdims=True)
    return yc * lax.rsqrt(var + LN_EPS) * g + b


def _mixer_body(p_ref, q_ref, k_ref, v_ref, r_ref, gl_ref, w2_ref, gb_ref, nw_ref, pw_ref, ps_ref,
                o_ref, pbuf, state, kvbuf, sall, obuf, gk_s):
    ts = p_ref.shape[0]
    s_idx = pl.program_id(1)
    halo = POOL_WINDOWS[-1]

    @pl.when(s_idx == 0)
    def _():
        pbuf[pl.ds(0, halo), :] = jnp.zeros((halo, POOL_WIDTH), F32)
        state[...] = jnp.zeros(state.shape, F32)

    p = p_ref[...]
    pbuf[pl.ds(halo, ts), :] = p
    pos = s_idx * ts + lax.broadcasted_iota(I32, (ts, 1), 0)
    for g, w in enumerate(POOL_WINDOWS):
        c0 = g * POOL_GROUP_DIM
        ext = pbuf[:, pl.ds(c0, POOL_GROUP_DIM)]
        sh = 1
        while sh < w:
            ext = ext + pltpu.roll(ext, sh, axis=0)
            sh *= 2
        acc = ext[halo:, :]
        cnt = jnp.minimum(pos + 1, w).astype(F32)
        mixed = acc / cnt - p[:, c0:c0 + POOL_GROUP_DIM]
        og = jnp.dot(mixed.astype(BF16), pw_ref[g], preferred_element_type=F32)
        o_ref[:, pl.ds(c0, POOL_GROUP_DIM)] = (og * ps_ref[:, pl.ds(c0, POOL_GROUP_DIM)]).astype(o_ref.dtype)
    pbuf[pl.ds(0, halo), :] = pbuf[pl.ds(ts, halo), :]

    nchunk = ts // GLA_CHUNK
    glog = jnp.dot(gl_ref[...].astype(BF16), w2_ref[...], preferred_element_type=F32) + gb_ref[...]
    gk = (jnp.minimum(glog, 0.0) - jnp.log(1.0 + jnp.exp(-jnp.abs(glog)))) * (1.0 / GLA_GATE_NORMALIZER)
    row = lax.broadcasted_iota(I32, (ts, 1), 0)
    rin = row % GLA_CHUNK
    b = gk
    sh = 1
    while sh < GLA_CHUNK:
        b = b + jnp.where(rin >= sh, pltpu.roll(b, sh, axis=0), 0.0)
        sh *= 2
    b3 = b.reshape(nchunk, GLA_CHUNK, GLA_DK_TOTAL)
    bmid = b3[:, GLA_CHUNK // 2 - 1:GLA_CHUNK // 2, :]
    blast = b3[:, GLA_CHUNK - 1:GLA_CHUNK, :]
    lane = lax.broadcasted_iota(I32, (1, LANES), 1)
    head_lane = [lane < GLA_DK, lane >= GLA_DK]
    q_scale = GLA_DK ** -0.5
    safe = jnp.max(jnp.abs(b3 - bmid)) <= GLA_SAFE_EXP

    @pl.when(safe)
    def _chunked():
        v = v_ref[...]
        vb = v.astype(BF16)
        q3 = (q_ref[...] * q_scale).reshape(nchunk, GLA_CHUNK, GLA_DK_TOTAL)
        k3 = k_ref[...].reshape(nchunk, GLA_CHUNK, GLA_DK_TOTAL)
        qs = (q3 * jnp.exp(b3 - bmid)).reshape(ts, GLA_DK_TOTAL)
        ks = (k3 * jnp.exp(bmid - b3)).reshape(ts, GLA_DK_TOTAL)
        qd = (q3 * jnp.exp(b3)).reshape(ts, GLA_DK_TOTAL)
        kd = (k3 * jnp.exp(blast - b3)).reshape(ts, GLA_DK_TOTAL)
        cdec = jnp.exp(blast).reshape(nchunk, GLA_DK_TOTAL)

        blk = LANES
        ri = lax.broadcasted_iota(I32, (blk, blk), 0)
        ci = lax.broadcasted_iota(I32, (blk, blk), 1)
        causal = (ri // GLA_CHUNK == ci // GLA_CHUNK) & (ri >= ci)
        o_intra = [[None] * (ts // blk) for _ in range(GLA_HEADS)]
        for rb in range(ts // blk):
            rs = slice(rb * blk, (rb + 1) * blk)
            for pair in range(GLA_HEADS // 2):
                ls = slice(pair * LANES, (pair + 1) * LANES)
                ks_p = ks[rs, ls].astype(BF16)
                q_p = qs[rs, ls]
                q2 = jnp.concatenate([jnp.where(head_lane[sub], q_p, 0.0) for sub in range(2)], axis=0).astype(BF16)
                sc2 = lax.dot_general(q2, ks_p, (((1,), (1,)), ((), ())), preferred_element_type=F32)
                for sub in range(2):
                    h = pair * 2 + sub
                    sc = jnp.where(causal, sc2[sub * blk:(sub + 1) * blk], 0.0).astype(BF16)
                    o_intra[h][rb] = jnp.dot(sc, vb[rs, h * GLA_DV:(h + 1) * GLA_DV], preferred_element_type=F32)

        cpb = blk // GLA_CHUNK
        chunk_of_col = lax.broadcasted_iota(I32, (1, blk), 1) // GLA_CHUNK
        kdb = kd.astype(BF16)
        for rb in range(ts // blk):
            rs = slice(rb * blk, (rb + 1) * blk)
            for pair in range(GLA_HEADS // 2):
                ls = slice(pair * LANES, (pair + 1) * LANES)
                inc = []
                for sub in range(2):
                    h = pair * 2 + sub
                    v_t = v[rs, h * GLA_DV:(h + 1) * GLA_DV].T
                    lhs = jnp.concatenate([jnp.where(chunk_of_col == c, v_t, 0.0) for c in range(cpb)],
                                          axis=0).astype(BF16)
                    inc.append(jnp.dot(lhs, kdb[rs, ls], preferred_element_type=F32))
                stacked = jnp.where(head_lane[0], inc[0], inc[1])
                for c in range(cpb):
                    kvbuf[pair, rb * cpb + c] = stacked[c * LANES:(c + 1) * LANES]
        for pair in range(GLA_HEADS // 2):
            ls = slice(pair * LANES, (pair + 1) * LANES)
            st = state[pair]
            for c in range(nchunk):
                sall[pair, c] = st.astype(BF16)
                st = st * cdec[c:c + 1, ls] + kvbuf[pair, c]
            state[pair] = st
        o_inter = [[None] * nchunk for _ in range(GLA_HEADS)]
        for pair in range(GLA_HEADS // 2):
            ls = slice(pair * LANES, (pair + 1) * LANES)
            for c in range(nchunk):
                rs = slice(c * GLA_CHUNK, (c + 1) * GLA_CHUNK)
                q_c = qd[rs, ls]
                q2 = jnp.concatenate([jnp.where(head_lane[sub], q_c, 0.0) for sub in range(2)], axis=0).astype(BF16)
                res = lax.dot_general(q2, sall[pair, c], (((1,), (1,)), ((), ())), preferred_element_type=F32)
                for sub in range(2):
                    o_inter[pair * 2 + sub][c] = res[sub * GLA_CHUNK:(sub + 1) * GLA_CHUNK]
        for h in range(GLA_HEADS):
            obuf[:, pl.ds(h * GLA_DV, GLA_DV)] = (jnp.concatenate(o_intra[h], axis=0)
                                                  + jnp.concatenate(o_inter[h], axis=0))

    @pl.when(jnp.logical_not(safe))
    def _row_by_row():
        gk_s[...] = gk
        row8 = lax.broadcasted_iota(I32, (SUBLANES, 1), 0)

        def slab(i8, carry):
            r0 = pl.multiple_of(i8 * SUBLANES, SUBLANES)
            q8 = q_ref[pl.ds(r0, SUBLANES), :] * q_scale
            k8 = k_ref[pl.ds(r0, SUBLANES), :]
            v8 = v_ref[pl.ds(r0, SUBLANES), :]
            g8 = jnp.exp(gk_s[pl.ds(r0, SUBLANES), :])
            outs = [jnp.zeros((SUBLANES, GLA_DV), F32) for _ in range(GLA_HEADS)]
            for pair in range(GLA_HEADS // 2):
                ls = slice(pair * LANES, (pair + 1) * LANES)
                st = state[pair]
                for r in range(SUBLANES):
                    sel = row8 == r
                    k_r = jnp.where(sel, k8[:, ls], 0.0).astype(BF16)
                    inc = []
                    for sub in range(2):
                        h = pair * 2 + sub
                        v_r = jnp.where(sel, v8[:, h * GLA_DV:(h + 1) * GLA_DV], 0.0).astype(BF16)
                        inc.append(lax.dot_general(v_r, k_r, (((0,), (0,)), ((), ())), preferred_element_type=F32))
                    st = st * g8[r:r + 1, ls] + jnp.where(head_lane[0], inc[0], inc[1])
                    q2 = jnp.concatenate([jnp.where(sel & head_lane[sub], q8[:, ls], 0.0) for sub in range(2)],
                                         axis=0).astype(BF16)
                    res = lax.dot_general(q2, st.astype(BF16), (((1,), (1,)), ((), ())),
                                          preferred_element_type=F32)
                    for sub in range(2):
                        outs[pair * 2 + sub] = outs[pair * 2 + sub] + res[sub * SUBLANES:(sub + 1) * SUBLANES]
                state[pair] = st
            obuf[pl.ds(r0, SUBLANES), :] = jnp.concatenate(outs, axis=1)
            return carry
        lax.fori_loop(0, ts // SUBLANES, slab, 0)

    nw = nw_ref[...]
    r = r_ref[...]
    for h in range(GLA_HEADS):
        o = obuf[:, pl.ds(h * GLA_DV, GLA_DV)]
        o = o * lax.rsqrt(jnp.mean(o * o, axis=-1, keepdims=True) + RMS_EPS) * nw
        o = o * _silu(r[:, h * GLA_DV:(h + 1) * GLA_DV])
        o_ref[:, pl.ds(POOL_WIDTH + h * GLA_DV, GLA_DV)] = o.astype(o_ref.dtype)


def _front_body(x_ref, win_ref, w2_ref, gb_ref, nw_ref, pw_ref, ps_ref, wout_ref, g_ref, b_ref, h_ref, h4_ref,
                pp_s, pq_s, pk_s, pv_s, pr_s, pg_s, mix_s, pbuf, state, kvbuf, sall, obuf, gk_s):
    groups = (pp_s, pq_s, pk_s, pv_s, pr_s, pg_s)
    cols = [0]
    for ref in groups:
        cols.append(cols[-1] + ref.shape[1])
    xb = x_ref[...].astype(BF16)
    for gi in (0, 5, 1, 2, 3, 4):
        groups[gi][...] = jnp.dot(xb, win_ref[:, pl.ds(cols[gi], groups[gi].shape[1])],
                                  preferred_element_type=F32)
    _mixer_body(*groups, w2_ref, gb_ref, nw_ref, pw_ref, ps_ref, mix_s, pbuf, state, kvbuf, sall, obuf, gk_s)
    y = DEEPNORM_ALPHA * x_ref[...] + jnp.dot(mix_s[...], wout_ref[...], preferred_element_type=F32)
    h = _layernorm(y, g_ref[...], b_ref[...])
    h_ref[...] = h
    for s, w in enumerate(_pack_row_words(h)):
        h4_ref[_word_plane(0, h.shape[0], s), :] = w


def _front(x2d, batch, seq, w_in_b, w2p, gate_b, norm_w, pool_w, pool_scale, w_out_b, ln_g, ln_b):
    t, d = x2d.shape
    ts = MIX_TS
    nseq = seq // ts

    def full(shape):
        return pl.BlockSpec(shape, lambda bi, si: (0,) * len(shape))

    return pl.pallas_call(
        _front_body,
        out_shape=(jax.ShapeDtypeStruct((t, d), F32), jax.ShapeDtypeStruct((t * ROW_WORDS, LANES), U32)),
        grid=(batch, nseq),
        in_specs=[pl.BlockSpec((ts, d), lambda bi, si: (bi * nseq + si, 0)),
                  full(w_in_b.shape), full(w2p.shape), full(gate_b.shape), full(norm_w.shape),
                  full(pool_w.shape), full(pool_scale.shape), full(w_out_b.shape), full(ln_g.shape), full(ln_b.shape)],
        out_specs=(pl.BlockSpec((ts, d), lambda bi, si: (bi * nseq + si, 0)),
                   pl.BlockSpec((ts * ROW_WORDS, LANES), lambda bi, si: (bi * nseq + si, 0))),
        scratch_shapes=[pltpu.VMEM((ts, POOL_WIDTH), F32), pltpu.VMEM((ts, GLA_DK_TOTAL), F32),
                        pltpu.VMEM((ts, GLA_DK_TOTAL), F32), pltpu.VMEM((ts, GLA_WIDTH), F32),
                        pltpu.VMEM((ts, GLA_WIDTH), F32), pltpu.VMEM((ts, D_IN_PAD - 2 * GLA_DK_TOTAL
                                                                      - 2 * GLA_WIDTH - POOL_WIDTH), F32),
                        pltpu.VMEM((ts, POOL_WIDTH + GLA_WIDTH), BF16),
                        pltpu.VMEM((ts + POOL_WINDOWS[-1], POOL_WIDTH), F32),
                        pltpu.VMEM((GLA_HEADS // 2, LANES, GLA_DV), F32),
                        pltpu.VMEM((GLA_HEADS // 2, ts // GLA_CHUNK, LANES, GLA_DV), F32),
                        pltpu.VMEM((GLA_HEADS // 2, ts // GLA_CHUNK, LANES, GLA_DV), BF16),
                        pltpu.VMEM((ts, GLA_WIDTH), F32), pltpu.VMEM((ts, GLA_DK_TOTAL), F32)],
        compiler_params=_cparams(("arbitrary", "arbitrary")),
        name="front",
    )(x2d, w_in_b, w2p, gate_b, norm_w, pool_w, pool_scale, w_out_b, ln_g, ln_b)


def _pack_row_words(x):
    half = x.shape[1] // 2
    u = pltpu.bitcast(x.astype(BF16).astype(F32), U32)
    hi_mask = jnp.uint32(HIGH_HALF)
    return [(u[:, half + s * LANES:half + (s + 1) * LANES] & hi_mask) | (u[:, s * LANES:(s + 1) * LANES] >> 16)
            for s in range(ROW_WORDS)]


def _unpack_row_words(w):
    return pltpu.bitcast(w << 16, F32), pltpu.bitcast(w & jnp.uint32(HIGH_HALF), F32)


def _rows(ref, first, n, align=1):
    if isinstance(first, int):
        start = first * ROW_WORDS
    else:
        start = pl.multiple_of(first * ROW_WORDS, ROW_WORDS * align)
    return ref.at[pl.ds(start, n * ROW_WORDS), :]


def _word_plane(first, m, s):
    return pl.ds(first * ROW_WORDS + s, m, stride=ROW_WORDS)


def _first_argmax_rows(val, rowf, nrows):
    m = jnp.max(val, axis=0, keepdims=True)
    first = jnp.min(jnp.where(val == m, rowf, float(nrows)), axis=0, keepdims=True)
    return m, first, rowf == first


def _router_body(h_ref, whi_ref, wlo_ref, bias_ref, idx_ref, gate_ref, rank_ref, cnt_ref, carry):
    tm = h_ref.shape[0]
    i = pl.program_id(0)

    @pl.when(i == 0)
    def _():
        carry[...] = jnp.zeros(carry.shape, F32)

    h = h_ref[...]
    h_hi = h.astype(BF16)
    h_lo = (h - h_hi.astype(F32)).astype(BF16)
    nt = (((1,), (1,)), ((), ()))
    logits = (lax.dot_general(whi_ref[...], h_hi, nt, preferred_element_type=F32)
              + lax.dot_general(whi_ref[...], h_lo, nt, preferred_element_type=F32)
              + lax.dot_general(wlo_ref[...], h_hi, nt, preferred_element_type=F32))
    scores = 1.0 / (1.0 + jnp.exp(-logits))
    biased = scores + bias_ref[...]
    neg = -jnp.inf

    grp = biased.reshape(N_GROUPS, GROUP_SIZE, tm)
    gi = lax.broadcasted_iota(I32, (N_GROUPS, GROUP_SIZE, tm), 1).astype(F32)
    g1 = jnp.max(grp, axis=1, keepdims=True)
    f1 = jnp.min(jnp.where(grp == g1, gi, float(GROUP_SIZE)), axis=1, keepdims=True)
    g2 = jnp.max(jnp.where(gi == f1, neg, grp), axis=1, keepdims=True)
    gscore = (g1 + g2).reshape(N_GROUPS, tm)

    growf = lax.broadcasted_iota(I32, (N_GROUPS, tm), 0).astype(F32)
    gsel = jnp.zeros((N_GROUPS, tm), F32)
    gval = gscore
    for _ in range(TOPK_GROUPS):
        _, _, pick = _first_argmax_rows(gval, growf, N_GROUPS)
        gsel = jnp.where(pick, 1.0, gsel)
        gval = jnp.where(pick, neg, gval)
    emask = jnp.broadcast_to(gsel.reshape(N_GROUPS, 1, tm), (N_GROUPS, GROUP_SIZE, tm)).reshape(N_EXPERTS, tm)

    rowf = lax.broadcasted_iota(I32, (N_EXPERTS, tm), 0).astype(F32)
    val = jnp.where(emask > 0.0, biased, neg)
    onehot = jnp.zeros((N_EXPERTS, tm), F32)
    picks, idxs, ws = [], [], []
    for _ in range(TOP_K):
        _, first, pick = _first_argmax_rows(val, rowf, N_EXPERTS)
        picks.append(pick)
        idxs.append(first)
        ws.append(jnp.sum(jnp.where(pick, scores, 0.0), axis=0, keepdims=True))
        onehot = jnp.where(pick, 1.0, onehot)
        val = jnp.where(pick, neg, val)
    w = jnp.concatenate(ws, axis=0)
    gate_ref[...] = w / jnp.sum(w, axis=0, keepdims=True) * ROUTED_SCALE
    idx_ref[...] = jnp.concatenate(idxs, axis=0).astype(I32)

    ti = lax.broadcasted_iota(I32, (tm, tm), 0)
    tj = lax.broadcasted_iota(I32, (tm, tm), 1)
    upper = jnp.where(ti < tj, 1.0, 0.0).astype(BF16)
    prefix = jnp.dot(onehot.astype(BF16), upper, preferred_element_type=F32) + carry[...]
    ranks = [jnp.sum(jnp.where(pk, prefix, 0.0), axis=0, keepdims=True) for pk in picks]
    rank_ref[...] = jnp.concatenate(ranks, axis=0).astype(I32)
    carry[...] = carry[...] + jnp.sum(onehot, axis=1, keepdims=True)
    cnt_ref[...] = carry[...]


def _router(h, wt, bias_col):
    t, d = h.shape
    tm = ROUTE_TM
    wt_hi = wt.astype(BF16)
    wt_lo = (wt - wt_hi.astype(F32)).astype(BF16)
    return pl.pallas_call(
        _router_body,
        out_shape=(jax.ShapeDtypeStruct((TOP_K, t), I32), jax.ShapeDtypeStruct((TOP_K, t), F32),
                   jax.ShapeDtypeStruct((TOP_K, t), I32), jax.ShapeDtypeStruct((N_EXPERTS, 1), F32)),
        grid=(t // tm,),
        in_specs=[pl.BlockSpec((tm, d), lambda i: (i, 0)),
                  pl.BlockSpec((N_EXPERTS, d), lambda i: (0, 0)),
                  pl.BlockSpec((N_EXPERTS, d), lambda i: (0, 0)),
                  pl.BlockSpec((N_EXPERTS, 1), lambda i: (0, 0))],
        out_specs=(pl.BlockSpec((TOP_K, tm), lambda i: (0, i)),
                   pl.BlockSpec((TOP_K, tm), lambda i: (0, i)),
                   pl.BlockSpec((TOP_K, tm), lambda i: (0, i)),
                   pl.BlockSpec((N_EXPERTS, 1), lambda i: (0, 0))),
        scratch_shapes=[pltpu.VMEM((N_EXPERTS, 1), F32)],
        compiler_params=_cparams(("arbitrary",)),
        name="router",
    )(h, wt_hi, wt_lo, bias_col)


def _positions_body(idx_ref, rank_ref, start_ref, pos_ref):
    tm = idx_ref.shape[1]
    rowi = lax.broadcasted_iota(I32, (N_EXPERTS, tm), 0)
    start = start_ref[...]
    idx = idx_ref[...]
    rows = [jnp.sum(jnp.where(rowi == idx[k:k + 1, :], start, 0.0), axis=0, keepdims=True) for k in range(TOP_K)]
    pos_ref[...] = jnp.concatenate(rows, axis=0).astype(I32) + rank_ref[...]


def _positions(idx_t, rank_t, start_col):
    t = idx_t.shape[1]
    tm = POS_TM
    return pl.pallas_call(
        _positions_body,
        out_shape=jax.ShapeDtypeStruct((TOP_K, t), I32),
        grid=(t // tm,),
        in_specs=[pl.BlockSpec((TOP_K, tm), lambda i: (0, i)),
                  pl.BlockSpec((TOP_K, tm), lambda i: (0, i)),
                  pl.BlockSpec((N_EXPERTS, 1), lambda i: (0, 0))],
        out_specs=pl.BlockSpec((TOP_K, tm), lambda i: (0, i)),
        compiler_params=_cparams(("arbitrary",)),
        name="positions",
    )(idx_t, rank_t, start_col)


def _pad_fill_copy(zeros, xs_ref, sem, row, nrows):
    return pltpu.make_async_copy(_rows(zeros, 0, nrows), _rows(xs_ref, row, nrows), sem)


def _dispatch_body(pad_row_ref, pad_n_ref, nu_ref, pos_ref, h4_ref, xs_ref, zeros, sem, pad_sem):
    i = pl.program_id(0)
    tm = h4_ref.shape[0] // ROW_WORDS
    half = MOE_BM // 2
    pad_bits = [1 << j for j in range(MOE_BM.bit_length() - 1)]
    n_half_blocks = xs_ref.shape[0] // (half * ROW_WORDS)

    def pad_pass(wait):
        def go(cp):
            if wait:
                cp.wait()
            else:
                cp.start()

        def body(e, carry):
            row = pad_row_ref[e]
            n = pad_n_ref[e]
            for bit in pad_bits:
                @pl.when((n & bit) != 0)
                def _():
                    go(_pad_fill_copy(zeros, xs_ref, pad_sem, row + (n & (bit - 1)), bit))
            return carry
        lax.fori_loop(0, N_EXPERTS, body, 0)

        def tail(hb, carry):
            go(_pad_fill_copy(zeros, xs_ref, pad_sem, hb * half, half))
            return carry
        lax.fori_loop(nu_ref[0] * 2, n_half_blocks, tail, 0)

    @pl.when(i == 0)
    def _():
        zeros[...] = jnp.zeros(zeros.shape, U32)
        pad_pass(False)

    def body(tt, carry):
        for dt in range(DISP_UNROLL):
            t = tt * DISP_UNROLL + dt
            for k in range(TOP_K):
                p = pos_ref[0, 0, t * TOP_K + k]
                pltpu.make_async_copy(_rows(h4_ref, t, 1), _rows(xs_ref, p, 1), sem).start(priority=k % 2)
        return carry
    lax.fori_loop(0, tm // DISP_UNROLL, body, 0)

    for k in range(TOP_K):
        pltpu.make_async_copy(h4_ref, _rows(xs_ref, 0, tm), sem).wait()

    @pl.when(i == pl.num_programs(0) - 1)
    def _():
        pad_pass(True)


def _dispatch(pad_row, pad_n, n_used, pos3, h4, n_rows):
    nt, _, per_step = pos3.shape
    tm = per_step // TOP_K
    grid_spec = pltpu.PrefetchScalarGridSpec(
        num_scalar_prefetch=3,
        grid=(nt,),
        in_specs=[pl.BlockSpec((1, 1, per_step), lambda i, a, b, c: (i, 0, 0), memory_space=pltpu.SMEM),
                  pl.BlockSpec((tm * ROW_WORDS, LANES), lambda i, a, b, c: (i, 0))],
        out_specs=pl.BlockSpec(memory_space=pl.ANY),
        scratch_shapes=[pltpu.VMEM((MOE_BM // 2 * ROW_WORDS, LANES), U32),
                        pltpu.SemaphoreType.DMA, pltpu.SemaphoreType.DMA],
    )
    return pl.pallas_call(
        _dispatch_body,
        out_shape=jax.ShapeDtypeStruct((n_rows * ROW_WORDS, LANES), U32),
        grid_spec=grid_spec,
        compiler_params=_cparams(("arbitrary",)),
        name="dispatch",
    )(pad_row, pad_n, n_used, pos3, h4)


def _weight_copies(e, w_hbm, stage, sem, slot):
    return [pltpu.make_async_copy(w.at[e], st.at[slot], sem.at[slot, j])
            for j, (w, st) in enumerate(zip(w_hbm, stage))]


def _row_block_copy(xs_hbm, xbuf, xsem, j):
    slot = j % X_RING
    return pltpu.make_async_copy(_rows(xs_hbm, j * MOE_BM, MOE_BM, align=MOE_BM), xbuf.at[slot], xsem.at[slot])


def _experts_body(be_ref, nu_ref, nx_ref, ord_ref, xs_hbm, wg_hbm, wu_hbm, wd_hbm, y4_ref,
                  xbuf, xsem, sg, su, sd, wsem, wg_b, wu_b, wd_b):
    w_hbm = (wg_hbm, wu_hbm, wd_hbm)
    stage = (sg, su, sd)
    n_used = nu_ref[0]

    def do_block(i, out_row):
        @pl.when(i < n_used)
        def _():
            e = be_ref[i]
            prev_e = be_ref[jnp.maximum(i - 1, 0)]

            @pl.when(i == 0)
            def _():
                for j in range(X_RING - 1):
                    @pl.when(j < n_used)
                    def _():
                        _row_block_copy(xs_hbm, xbuf, xsem, j).start()

            @pl.when(i + X_RING - 1 < n_used)
            def _():
                _row_block_copy(xs_hbm, xbuf, xsem, i + X_RING - 1).start()

            @pl.when((i == 0) | (e != prev_e))
            def _():
                wslot = ord_ref[e] % 2
                n1 = nx_ref[e]
                n2 = jnp.where(n1 >= 0, nx_ref[jnp.maximum(n1, 0)], -1)

                @pl.when(i == 0)
                def _():
                    for cp in _weight_copies(e, w_hbm, stage, wsem, wslot):
                        cp.start()

                    @pl.when(n1 >= 0)
                    def _():
                        for cp in _weight_copies(n1, w_hbm, stage, wsem, 1 - wslot):
                            cp.start()
                for cp in _weight_copies(e, w_hbm, stage, wsem, wslot):
                    cp.wait()
                wg_b[...] = sg[wslot].astype(BF16)
                wu_b[...] = su[wslot].astype(BF16)
                wd_b[...] = sd[wslot].astype(BF16)

                @pl.when(n2 >= 0)
                def _():
                    for cp in _weight_copies(n2, w_hbm, stage, wsem, wslot):
                        cp.start()

            _row_block_copy(xs_hbm, xbuf, xsem, i).wait()
            slot = i % X_RING
            parts = [_unpack_row_words(xbuf[slot, _word_plane(0, MOE_BM, s), :]) for s in range(ROW_WORDS)]
            x = jnp.concatenate([p[0] for p in parts] + [p[1] for p in parts], axis=1).astype(BF16)
            g = jnp.dot(x, wg_b[...], preferred_element_type=F32)
            u = jnp.dot(x, wu_b[...], preferred_element_type=F32)
            a = (_silu(g) * u).astype(BF16)
            y = jnp.dot(a, wd_b[...], preferred_element_type=F32)
            for s, w in enumerate(_pack_row_words(y)):
                y4_ref[_word_plane(out_row, MOE_BM, s), :] = w

        @pl.when(i >= n_used)
        def _():
            y4_ref[pl.ds(out_row * ROW_WORDS, MOE_BM * ROW_WORDS), :] = jnp.zeros((MOE_BM * ROW_WORDS, LANES), U32)

    for sub in range(EXP_BLOCKS_PER_STEP):
        do_block(pl.program_id(0) * EXP_BLOCKS_PER_STEP + sub, sub * MOE_BM)


def _experts(block_e, n_used, next_e, ord_e, xs, wg, wu, wd):
    nblk = block_e.shape[0]
    bm = MOE_BM * EXP_BLOCKS_PER_STEP
    e, d, de = wg.shape

    grid_spec = pltpu.PrefetchScalarGridSpec(
        num_scalar_prefetch=4,
        grid=(nblk // EXP_BLOCKS_PER_STEP,),
        in_specs=[pl.BlockSpec(memory_space=pl.ANY),
                  pl.BlockSpec(memory_space=pl.ANY),
                  pl.BlockSpec(memory_space=pl.ANY),
                  pl.BlockSpec(memory_space=pl.ANY)],
        out_specs=pl.BlockSpec((bm * ROW_WORDS, LANES), lambda i, be, nu, nx, od: (i, 0)),
        scratch_shapes=[pltpu.VMEM((X_RING, MOE_BM * ROW_WORDS, LANES), U32), pltpu.SemaphoreType.DMA((X_RING,)),
                        pltpu.VMEM((2, d, de), F32), pltpu.VMEM((2, d, de), F32), pltpu.VMEM((2, de, d), F32),
                        pltpu.SemaphoreType.DMA((2, 3)),
                        pltpu.VMEM((d, de), BF16), pltpu.VMEM((d, de), BF16), pltpu.VMEM((de, d), BF16)],
    )
    return pl.pallas_call(
        _experts_body,
        out_shape=jax.ShapeDtypeStruct((nblk * MOE_BM * ROW_WORDS, LANES), U32),
        grid_spec=grid_spec,
        compiler_params=_cparams(("arbitrary",)),
        name="experts",
    )(block_e, n_used, next_e, ord_e, xs, wg, wu, wd)


def _combine_body(pos_cur, pos_nxt, y4_ref, h_ref, gate_ref, wsg_ref, wsu_ref, wsd_ref, g_ref, b_ref,
                  o_ref, buf, sem, racc):
    i = pl.program_id(0)
    nb = pl.num_programs(0)
    tm, d = h_ref.shape
    half = d // 2
    slot = i % 2

    def issue_rows(pos_ref, sl, t):
        for k in range(TOP_K):
            p = pos_ref[0, 0, t * TOP_K + k]
            pltpu.make_async_copy(_rows(y4_ref, p, 1), _rows(buf.at[sl], k * tm + t, 1),
                                  sem.at[sl]).start(priority=k % 2)

    @pl.when(i == 0)
    def _():
        def body(t, carry):
            issue_rows(pos_cur, 0, t)
            return carry
        lax.fori_loop(0, tm, body, 0)

    pltpu.make_async_copy(_rows(y4_ref, 0, tm * TOP_K), buf.at[slot], sem.at[slot]).wait()

    def combine_rows(t8):
        r0 = pl.multiple_of(t8 * SUBLANES, SUBLANES)
        gates = gate_ref[pl.ds(r0, SUBLANES), :]
        for s in range(ROW_WORDS):
            lo_acc = hi_acc = None
            for k in range(TOP_K):
                w = buf[slot, pl.ds((k * tm + r0) * ROW_WORDS + s, SUBLANES, stride=ROW_WORDS), :]
                lo, hi = _unpack_row_words(w)
                gk = gates[:, k:k + 1]
                lo_acc = gk * lo if lo_acc is None else lo_acc + gk * lo
                hi_acc = gk * hi if hi_acc is None else hi_acc + gk * hi
            racc[pl.ds(r0, SUBLANES), pl.ds(s * LANES, LANES)] = lo_acc
            racc[pl.ds(r0, SUBLANES), pl.ds(half + s * LANES, LANES)] = hi_acc

    @pl.when(i + 1 < nb)
    def _():
        def body(t8, carry):
            for dt in range(SUBLANES):
                issue_rows(pos_nxt, 1 - slot, t8 * SUBLANES + dt)
            combine_rows(t8)
            return carry
        lax.fori_loop(0, tm // SUBLANES, body, 0)

    @pl.when(i + 1 >= nb)
    def _():
        def body(t8, carry):
            combine_rows(t8)
            return carry
        lax.fori_loop(0, tm // SUBLANES, body, 0)

    h = h_ref[...]
    hb = h.astype(BF16)
    sg = jnp.dot(hb, wsg_ref[...], preferred_element_type=F32)
    su = jnp.dot(hb, wsu_ref[...], preferred_element_type=F32)
    shared = jnp.dot((_silu(sg) * su).astype(BF16), wsd_ref[...], preferred_element_type=F32)
    o_ref[...] = _layernorm(DEEPNORM_ALPHA * h + (racc[...] + shared), g_ref[...], b_ref[...])


def _combine(pos3, y4, h, gates, wsg, wsu, wsd, g, b):
    t, d = h.shape
    tm = COMB_TM
    nt = t // tm
    ds_ = wsg.shape[1]
    return pl.pallas_call(
        _combine_body,
        out_shape=jax.ShapeDtypeStruct((t, d), F32),
        grid=(nt,),
        in_specs=[pl.BlockSpec((1, 1, tm * TOP_K), lambda i: (i, 0, 0), memory_space=pltpu.SMEM),
                  pl.BlockSpec((1, 1, tm * TOP_K), lambda i: (jnp.minimum(i + 1, nt - 1), 0, 0),
                               memory_space=pltpu.SMEM),
                  pl.BlockSpec(memory_space=pl.ANY),
                  pl.BlockSpec((tm, d), lambda i: (i, 0)),
                  pl.BlockSpec((tm, TOP_K), lambda i: (i, 0)),
                  pl.BlockSpec((d, ds_), lambda i: (0, 0)),
                  pl.BlockSpec((d, ds_), lambda i: (0, 0)),
                  pl.BlockSpec((ds_, d), lambda i: (0, 0)),
                  pl.BlockSpec((1, d), lambda i: (0, 0)),
                  pl.BlockSpec((1, d), lambda i: (0, 0))],
        out_specs=pl.BlockSpec((tm, d), lambda i: (i, 0)),
        scratch_shapes=[pltpu.VMEM((2, tm * TOP_K * ROW_WORDS, LANES), U32),
                        pltpu.SemaphoreType.DMA((2,)),
                        pltpu.VMEM((tm, d), F32)],
        compiler_params=_cparams(("arbitrary",)),
        name="combine",
    )(pos3, pos3, y4, h, gates, wsg, wsu, wsd, g, b)


def _expert_tables(counts, nblk):
    bm = MOE_BM
    cnt = counts.reshape(N_EXPERTS).astype(I32)
    padded = (cnt + bm - 1) // bm * bm
    padded_end = jnp.cumsum(padded)
    padded_start = padded_end - padded
    block_rows = jnp.arange(nblk, dtype=I32) * bm
    block_e = jnp.sum((padded_end[None, :] <= block_rows[:, None]).astype(I32), axis=1)
    block_e = jnp.minimum(block_e, N_EXPERTS - 1)
    n_used = (padded_end[-1:] // bm).astype(I32)
    ids = jnp.where(cnt > 0, jnp.arange(N_EXPERTS, dtype=I32), N_EXPERTS)
    after = jnp.concatenate([lax.cummin(ids, reverse=True)[1:], jnp.full((1,), N_EXPERTS, I32)])
    next_e = jnp.where(after < N_EXPERTS, after, -1).astype(I32)
    ord_e = (jnp.cumsum((cnt > 0).astype(I32)) - 1).astype(I32)
    return padded_start, padded_start + cnt, padded - cnt, block_e, n_used, next_e, ord_e


def kernel(x, w_in, gla_gate_w2, gla_gate_b, gla_norm_w, pool_w_group, pool_scale, w_out, ln1_g, ln1_b,
           router_w, router_bias, w_exp_gate, w_exp_up, w_exp_down, w_sh_gate, w_sh_up, w_sh_down, ln2_g, ln2_b):
    batch, seq, d = x.shape
    t = batch * seq
    h2d = x.reshape(t, d)
    for l in range(DEPTH):
        d_in = w_in.shape[2]
        w_in_b = jnp.pad(w_in[l], ((0, 0), (0, D_IN_PAD - d_in))).astype(BF16)
        w2p = jnp.pad(gla_gate_w2[l], ((0, LANES - GLA_GATE_RANK), (0, 0))).astype(BF16)
        h, h4 = _front(h2d, batch, seq, w_in_b, w2p, gla_gate_b[l].reshape(1, -1), gla_norm_w[l].reshape(1, -1),
                       pool_w_group[l].astype(BF16), pool_scale[l].reshape(1, -1),
                       w_out[l].astype(BF16), ln1_g[l].reshape(1, -1), ln1_b[l].reshape(1, -1))
        idx_t, gate_t, rank_t, counts = _router(h, router_w[l].T, router_bias[l].reshape(-1, 1))
        nblk = (t * TOP_K + N_EXPERTS * (MOE_BM - 1)) // MOE_BM
        nblk = -(-nblk // EXP_BLOCKS_PER_STEP) * EXP_BLOCKS_PER_STEP
        start, pad_row, pad_n, block_e, n_used, next_e, ord_e = _expert_tables(counts, nblk)
        pos_t = _positions(idx_t, rank_t, start.astype(F32).reshape(-1, 1))
        pos_tok = pos_t.T
        xs = _dispatch(pad_row, pad_n, n_used, pos_tok.reshape(t // DISP_TM, 1, DISP_TM * TOP_K), h4, nblk * MOE_BM)
        y4 = _experts(block_e, n_used, next_e, ord_e, xs, w_exp_gate[l], w_exp_up[l], w_exp_down[l])
        h2d = _combine(pos_tok.reshape(t // COMB_TM, 1, COMB_TM * TOP_K), y4, h, gate_t.T,
                       w_sh_gate[l].astype(BF16), w_sh_up[l].astype(BF16), w_sh_down[l].astype(BF16),
                       ln2_g[l].reshape(1, -1), ln2_b[l].reshape(1, -1))
    return h2d.reshape(batch, seq, d)
```

```python
import jax
import jax.numpy as jnp
from jax import lax
from jax.experimental import pallas as pl
from jax.experimental.pallas import tpu as pltpu

F32 = jnp.float32
BF16 = jnp.bfloat16
I32 = jnp.int32
U32 = jnp.uint32
HIGH_HALF = 0xFFFF0000

POOL_WINDOWS = (2, 4, 8, 16)
POOL_GROUP_DIM = 128
POOL_WIDTH = 512
GLA_HEADS = 4
GLA_DK = 64
GLA_DV = 128
GLA_DK_TOTAL = 256
GLA_WIDTH = 512
GLA_GATE_RANK = 16
GLA_GATE_NORMALIZER = 16.0
GLA_CHUNK = 16
GLA_SAFE_EXP = 60.0
N_EXPERTS = 256
TOP_K = 8
N_GROUPS = 8
GROUP_SIZE = N_EXPERTS // N_GROUPS
TOPK_GROUPS = 4
ROUTED_SCALE = 2.5
DEPTH = 1
DEEPNORM_ALPHA = (2.0 * DEPTH) ** 0.25
LN_EPS = 1e-5
RMS_EPS = 1e-5

LANES = 128
SUBLANES = 8
VMEM_LIMIT = 56 * 1024 * 1024

MIX_TS = 512
ROUTE_TM = 512
MOE_BM = 256
COMB_TM = 512
COMB_GROUP = 64
POS_TM = 512
DISP_TM = 512
DISP_UNROLL = 4
ROW_WORDS = 4
X_RING = 6
EXP_BLOCKS_PER_STEP = 8
D_IN_PAD = 2176


def _cparams(sem):
    return pltpu.CompilerParams(dimension_semantics=sem, vmem_limit_bytes=VMEM_LIMIT)


def _silu(x):
    return x * (1.0 / (1.0 + jnp.exp(-x)))


def _layernorm(y, g, b):
    mu = jnp.mean(y, axis=-1, keepdims=True)
    yc = y - mu
    var = jnp.mean(yc * yc, axis=-1, keepdims=True)
    return yc * lax.rsqrt(var + LN_EPS) * g + b


def _mixer_body(p_ref, q_ref, k_ref, v_ref, r_ref, gl_ref, w2_ref, gb_ref, nw_ref, pw_ref, ps_ref,
                o_ref, pbuf, state, kvbuf, sall, obuf, gk_s):
    ts = p_ref.shape[0]
    s_idx = pl.program_id(1)
    halo = POOL_WINDOWS[-1]

    @pl.when(s_idx == 0)
    def _():
        pbuf[pl.ds(0, halo), :] = jnp.zeros((halo, POOL_WIDTH), F32)
        state[...] = jnp.zeros(state.shape, F32)

    p = p_ref[...]
    pbuf[pl.ds(halo, ts), :] = p
    pos = s_idx * ts + lax.broadcasted_iota(I32, (ts, 1), 0)
    for g, w in enumerate(POOL_WINDOWS):
        c0 = g * POOL_GROUP_DIM
        ext = pbuf[:, pl.ds(c0, POOL_GROUP_DIM)]
        sh = 1
        while sh < w:
            ext = ext + pltpu.roll(ext, sh, axis=0)
            sh *= 2
        acc = ext[halo:, :]
        cnt = jnp.minimum(pos + 1, w).astype(F32)
        mixed = acc / cnt - p[:, c0:c0 + POOL_GROUP_DIM]
        og = jnp.dot(mixed.astype(BF16), pw_ref[g], preferred_element_type=F32)
        o_ref[:, pl.ds(c0, POOL_GROUP_DIM)] = (og * ps_ref[:, pl.ds(c0, POOL_GROUP_DIM)]).astype(o_ref.dtype)
    pbuf[pl.ds(0, halo), :] = pbuf[pl.ds(ts, halo), :]

    nchunk = ts // GLA_CHUNK
    glog = jnp.dot(gl_ref[...].astype(BF16), w2_ref[...], preferred_element_type=F32) + gb_ref[...]
    gk = (jnp.minimum(glog, 0.0) - jnp.log(1.0 + jnp.exp(-jnp.abs(glog)))) * (1.0 / GLA_GATE_NORMALIZER)
    row = lax.broadcasted_iota(I32, (ts, 1), 0)
    rin = row % GLA_CHUNK
    b = gk
    sh = 1
    while sh < GLA_CHUNK:
        b = b + jnp.where(rin >= sh, pltpu.roll(b, sh, axis=0), 0.0)
        sh *= 2
    b3 = b.reshape(nchunk, GLA_CHUNK, GLA_DK_TOTAL)
    bmid = b3[:, GLA_CHUNK // 2 - 1:GLA_CHUNK // 2, :]
    blast = b3[:, GLA_CHUNK - 1:GLA_CHUNK, :]
    lane = lax.broadcasted_iota(I32, (1, LANES), 1)
    head_lane = [lane < GLA_DK, lane >= GLA_DK]
    q_scale = GLA_DK ** -0.5
    safe = jnp.max(jnp.abs(b3 - bmid)) <= GLA_SAFE_EXP

    @pl.when(safe)
    def _chunked():
        v = v_ref[...]
        vb = v.astype(BF16)
        q3 = (q_ref[...] * q_scale).reshape(nchunk, GLA_CHUNK, GLA_DK_TOTAL)
        k3 = k_ref[...].reshape(nchunk, GLA_CHUNK, GLA_DK_TOTAL)
        qs = (q3 * jnp.exp(b3 - bmid)).reshape(ts, GLA_DK_TOTAL)
        ks = (k3 * jnp.exp(bmid - b3)).reshape(ts, GLA_DK_TOTAL)
        qd = (q3 * jnp.exp(b3)).reshape(ts, GLA_DK_TOTAL)
        kd = (k3 * jnp.exp(blast - b3)).reshape(ts, GLA_DK_TOTAL)
        cdec = jnp.exp(blast).reshape(nchunk, GLA_DK_TOTAL)

        blk = LANES
        ri = lax.broadcasted_iota(I32, (blk, blk), 0)
        ci = lax.broadcasted_iota(I32, (blk, blk), 1)
        causal = (ri // GLA_CHUNK == ci // GLA_CHUNK) & (ri >= ci)
        o_intra = [[None] * (ts // blk) for _ in range(GLA_HEADS)]
        for rb in range(ts // blk):
            rs = slice(rb * blk, (rb + 1) * blk)
            for pair in range(GLA_HEADS // 2):
                ls = slice(pair * LANES, (pair + 1) * LANES)
                ks_p = ks[rs, ls].astype(BF16)
                q_p = qs[rs, ls]
                q2 = jnp.concatenate([jnp.where(head_lane[sub], q_p, 0.0) for sub in range(2)], axis=0).astype(BF16)
                sc2 = lax.dot_general(q2, ks_p, (((1,), (1,)), ((), ())), preferred_element_type=F32)
                for sub in range(2):
                    h = pair * 2 + sub
                    sc = jnp.where(causal, sc2[sub * blk:(sub + 1) * blk], 0.0).astype(BF16)
                    o_intra[h][rb] = jnp.dot(sc, vb[rs, h * GLA_DV:(h + 1) * GLA_DV], preferred_element_type=F32)

        cpb = blk // GLA_CHUNK
        chunk_of_col = lax.broadcasted_iota(I32, (1, blk), 1) // GLA_CHUNK
        kdb = kd.astype(BF16)
        for rb in range(ts // blk):
            rs = slice(rb * blk, (rb + 1) * blk)
            for pair in range(GLA_HEADS // 2):
                ls = slice(pair * LANES, (pair + 1) * LANES)
                inc = []
                for sub in range(2):
                    h = pair * 2 + sub
                    v_t = v[rs, h * GLA_DV:(h + 1) * GLA_DV].T
                    lhs = jnp.concatenate([jnp.where(chunk_of_col == c, v_t, 0.0) for c in range(cpb)],
                                          axis=0).astype(BF16)
                    inc.append(jnp.dot(lhs, kdb[rs, ls], preferred_element_type=F32))
                stacked = jnp.where(head_lane[0], inc[0], inc[1])
                for c in range(cpb):
                    kvbuf[pair, rb * cpb + c] = stacked[c * LANES:(c + 1) * LANES]
        for pair in range(GLA_HEADS // 2):
            ls = slice(pair * LANES, (pair + 1) * LANES)
            st = state[pair]
            for c in range(nchunk):
                sall[pair, c] = st.astype(BF16)
                st = st * cdec[c:c + 1, ls] + kvbuf[pair, c]
            state[pair] = st
        o_inter = [[None] * nchunk for _ in range(GLA_HEADS)]
        for pair in range(GLA_HEADS // 2):
            ls = slice(pair * LANES, (pair + 1) * LANES)
            for c in range(nchunk):
                rs = slice(c * GLA_CHUNK, (c + 1) * GLA_CHUNK)
                q_c = qd[rs, ls]
                q2 = jnp.concatenate([jnp.where(head_lane[sub], q_c, 0.0) for sub in range(2)], axis=0).astype(BF16)
                res = lax.dot_general(q2, sall[pair, c], (((1,), (1,)), ((), ())), preferred_element_type=F32)
                for sub in range(2):
                    o_inter[pair * 2 + sub][c] = res[sub * GLA_CHUNK:(sub + 1) * GLA_CHUNK]
        for h in range(GLA_HEADS):
            obuf[:, pl.ds(h * GLA_DV, GLA_DV)] = (jnp.concatenate(o_intra[h], axis=0)
                                                  + jnp.concatenate(o_inter[h], axis=0))

    @pl.when(jnp.logical_not(safe))
    def _row_by_row():
        gk_s[...] = gk
        row8 = lax.broadcasted_iota(I32, (SUBLANES, 1), 0)

        def slab(i8, carry):
            r0 = pl.multiple_of(i8 * SUBLANES, SUBLANES)
            q8 = q_ref[pl.ds(r0, SUBLANES), :] * q_scale
            k8 = k_ref[pl.ds(r0, SUBLANES), :]
            v8 = v_ref[pl.ds(r0, SUBLANES), :]
            g8 = jnp.exp(gk_s[pl.ds(r0, SUBLANES), :])
            outs = [jnp.zeros((SUBLANES, GLA_DV), F32) for _ in range(GLA_HEADS)]
            for pair in range(GLA_HEADS // 2):
                ls = slice(pair * LANES, (pair + 1) * LANES)
                st = state[pair]
                for r in range(SUBLANES):
                    sel = row8 == r
                    k_r = jnp.where(sel, k8[:, ls], 0.0).astype(BF16)
                    inc = []
                    for sub in range(2):
                        h = pair * 2 + sub
                        v_r = jnp.where(sel, v8[:, h * GLA_DV:(h + 1) * GLA_DV], 0.0).astype(BF16)
                        inc.append(lax.dot_general(v_r, k_r, (((0,), (0,)), ((), ())), preferred_element_type=F32))
                    st = st * g8[r:r + 1, ls] + jnp.where(head_lane[0], inc[0], inc[1])
                    q2 = jnp.concatenate([jnp.where(sel & head_lane[sub], q8[:, ls], 0.0) for sub in range(2)],
                                         axis=0).astype(BF16)
                    res = lax.dot_general(q2, st.astype(BF16), (((1,), (1,)), ((), ())),
                                          preferred_element_type=F32)
                    for sub in range(2):
                        outs[pair * 2 + sub] = outs[pair * 2 + sub] + res[sub * SUBLANES:(sub + 1) * SUBLANES]
                state[pair] = st
            obuf[pl.ds(r0, SUBLANES), :] = jnp.concatenate(outs, axis=1)
            return carry
        lax.fori_loop(0, ts // SUBLANES, slab, 0)

    nw = nw_ref[...]
    r = r_ref[...]
    for h in range(GLA_HEADS):
        o = obuf[:, pl.ds(h * GLA_DV, GLA_DV)]
        o = o * lax.rsqrt(jnp.mean(o * o, axis=-1, keepdims=True) + RMS_EPS) * nw
        o = o * _silu(r[:, h * GLA_DV:(h + 1) * GLA_DV])
        o_ref[:, pl.ds(POOL_WIDTH + h * GLA_DV, GLA_DV)] = o.astype(o_ref.dtype)


def _front_body(x_ref, win_ref, w2_ref, gb_ref, nw_ref, pw_ref, ps_ref, wout_ref, g_ref, b_ref, h_ref, h4_ref,
                pp_s, pq_s, pk_s, pv_s, pr_s, pg_s, mix_s, pbuf, state, kvbuf, sall, obuf, gk_s):
    groups = (pp_s, pq_s, pk_s, pv_s, pr_s, pg_s)
    cols = [0]
    for ref in groups:
        cols.append(cols[-1] + ref.shape[1])
    xb = x_ref[...].astype(BF16)
    for gi in (0, 5, 1, 2, 3, 4):
        groups[gi][...] = jnp.dot(xb, win_ref[:, pl.ds(cols[gi], groups[gi].shape[1])],
                                  preferred_element_type=F32)
    _mixer_body(*groups, w2_ref, gb_ref, nw_ref, pw_ref, ps_ref, mix_s, pbuf, state, kvbuf, sall, obuf, gk_s)
    y = DEEPNORM_ALPHA * x_ref[...] + jnp.dot(mix_s[...], wout_ref[...], preferred_element_type=F32)
    h = _layernorm(y, g_ref[...], b_ref[...])
    h_ref[...] = h
    for s, w in enumerate(_pack_row_words(h)):
        h4_ref[_word_plane(0, h.shape[0], s), :] = w


def _front(x2d, batch, seq, w_in_b, w2p, gate_b, norm_w, pool_w, pool_scale, w_out_b, ln_g, ln_b):
    t, d = x2d.shape
    ts = MIX_TS
    nseq = seq // ts

    def full(shape):
        return pl.BlockSpec(shape, lambda bi, si: (0,) * len(shape))

    return pl.pallas_call(
        _front_body,
        out_shape=(jax.ShapeDtypeStruct((t, d), F32), jax.ShapeDtypeStruct((t * ROW_WORDS, LANES), U32)),
        grid=(batch, nseq),
        in_specs=[pl.BlockSpec((ts, d), lambda bi, si: (bi * nseq + si, 0)),
                  full(w_in_b.shape), full(w2p.shape), full(gate_b.shape), full(norm_w.shape),
                  full(pool_w.shape), full(pool_scale.shape), full(w_out_b.shape), full(ln_g.shape), full(ln_b.shape)],
        out_specs=(pl.BlockSpec((ts, d), lambda bi, si: (bi * nseq + si, 0)),
                   pl.BlockSpec((ts * ROW_WORDS, LANES), lambda bi, si: (bi * nseq + si, 0))),
        scratch_shapes=[pltpu.VMEM((ts, POOL_WIDTH), F32), pltpu.VMEM((ts, GLA_DK_TOTAL), F32),
                        pltpu.VMEM((ts, GLA_DK_TOTAL), F32), pltpu.VMEM((ts, GLA_WIDTH), F32),
                        pltpu.VMEM((ts, GLA_WIDTH), F32), pltpu.VMEM((ts, D_IN_PAD - 2 * GLA_DK_TOTAL
                                                                      - 2 * GLA_WIDTH - POOL_WIDTH), F32),
                        pltpu.VMEM((ts, POOL_WIDTH + GLA_WIDTH), BF16),
                        pltpu.VMEM((ts + POOL_WINDOWS[-1], POOL_WIDTH), F32),
                        pltpu.VMEM((GLA_HEADS // 2, LANES, GLA_DV), F32),
                        pltpu.VMEM((GLA_HEADS // 2, ts // GLA_CHUNK, LANES, GLA_DV), F32),
                        pltpu.VMEM((GLA_HEADS // 2, ts // GLA_CHUNK, LANES, GLA_DV), BF16),
                        pltpu.VMEM((ts, GLA_WIDTH), F32), pltpu.VMEM((ts, GLA_DK_TOTAL), F32)],
        compiler_params=_cparams(("arbitrary", "arbitrary")),
        name="front",
    )(x2d, w_in_b, w2p, gate_b, norm_w, pool_w, pool_scale, w_out_b, ln_g, ln_b)


def _pack_row_words(x):
    half = x.shape[1] // 2
    u = pltpu.bitcast(x.astype(BF16).astype(F32), U32)
    hi_mask = jnp.uint32(HIGH_HALF)
    return [(u[:, half + s * LANES:half + (s + 1) * LANES] & hi_mask) | (u[:, s * LANES:(s + 1) * LANES] >> 16)
            for s in range(ROW_WORDS)]


def _unpack_row_words(w):
    return pltpu.bitcast(w << 16, F32), pltpu.bitcast(w & jnp.uint32(HIGH_HALF), F32)


def _rows(ref, first, n, align=1):
    if isinstance(first, int):
        start = first * ROW_WORDS
    else:
        start = pl.multiple_of(first * ROW_WORDS, ROW_WORDS * align)
    return ref.at[pl.ds(start, n * ROW_WORDS), :]


def _word_plane(first, m, s):
    return pl.ds(first * ROW_WORDS + s, m, stride=ROW_WORDS)


def _first_argmax_rows(val, rowf, nrows):
    m = jnp.max(val, axis=0, keepdims=True)
    first = jnp.min(jnp.where(val == m, rowf, float(nrows)), axis=0, keepdims=True)
    return m, first, rowf == first


def _router_body(h_ref, whi_ref, wlo_ref, bias_ref, idx_ref, gate_ref, rank_ref, cnt_ref, carry):
    tm = h_ref.shape[0]
    i = pl.program_id(0)

    @pl.when(i == 0)
    def _():
        carry[...] = jnp.zeros(carry.shape, F32)

    h = h_ref[...]
    h_hi = h.astype(BF16)
    h_lo = (h - h_hi.astype(F32)).astype(BF16)
    nt = (((1,), (1,)), ((), ()))
    logits = (lax.dot_general(whi_ref[...], h_hi, nt, preferred_element_type=F32)
              + lax.dot_general(whi_ref[...], h_lo, nt, preferred_element_type=F32)
              + lax.dot_general(wlo_ref[...], h_hi, nt, preferred_element_type=F32))
    scores = 1.0 / (1.0 + jnp.exp(-logits))
    biased = scores + bias_ref[...]
    neg = -jnp.inf

    grp = biased.reshape(N_GROUPS, GROUP_SIZE, tm)
    gi = lax.broadcasted_iota(I32, (N_GROUPS, GROUP_SIZE, tm), 1).astype(F32)
    g1 = jnp.max(grp, axis=1, keepdims=True)
    f1 = jnp.min(jnp.where(grp == g1, gi, float(GROUP_SIZE)), axis=1, keepdims=True)
    g2 = jnp.max(jnp.where(gi == f1, neg, grp), axis=1, keepdims=True)
    gscore = (g1 + g2).reshape(N_GROUPS, tm)

    growf = lax.broadcasted_iota(I32, (N_GROUPS, tm), 0).astype(F32)
    gsel = jnp.zeros((N_GROUPS, tm), F32)
    gval = gscore
    for _ in range(TOPK_GROUPS):
        _, _, pick = _first_argmax_rows(gval, growf, N_GROUPS)
        gsel = jnp.where(pick, 1.0, gsel)
        gval = jnp.where(pick, neg, gval)
    emask = jnp.broadcast_to(gsel.reshape(N_GROUPS, 1, tm), (N_GROUPS, GROUP_SIZE, tm)).reshape(N_EXPERTS, tm)

    rowf = lax.broadcasted_iota(I32, (N_EXPERTS, tm), 0).astype(F32)
    val = jnp.where(emask > 0.0, biased, neg)
    onehot = jnp.zeros((N_EXPERTS, tm), F32)
    picks, idxs, ws = [], [], []
    for _ in range(TOP_K):
        _, first, pick = _first_argmax_rows(val, rowf, N_EXPERTS)
        picks.append(pick)
        idxs.append(first)
        ws.append(jnp.sum(jnp.where(pick, scores, 0.0), axis=0, keepdims=True))
        onehot = jnp.where(pick, 1.0, onehot)
        val = jnp.where(pick, neg, val)
    w = jnp.concatenate(ws, axis=0)
    gate_ref[...] = w / jnp.sum(w, axis=0, keepdims=True) * ROUTED_SCALE
    idx_ref[...] = jnp.concatenate(idxs, axis=0).astype(I32)

    ti = lax.broadcasted_iota(I32, (tm, tm), 0)
    tj = lax.broadcasted_iota(I32, (tm, tm), 1)
    upper = jnp.where(ti < tj, 1.0, 0.0).astype(BF16)
    prefix = jnp.dot(onehot.astype(BF16), upper, preferred_element_type=F32) + carry[...]
    ranks = [jnp.sum(jnp.where(pk, prefix, 0.0), axis=0, keepdims=True) for pk in picks]
    rank_ref[...] = jnp.concatenate(ranks, axis=0).astype(I32)
    carry[...] = carry[...] + jnp.sum(onehot, axis=1, keepdims=True)
    cnt_ref[...] = carry[...]


def _router(h, wt, bias_col):
    t, d = h.shape
    tm = ROUTE_TM
    wt_hi = wt.astype(BF16)
    wt_lo = (wt - wt_hi.astype(F32)).astype(BF16)
    return pl.pallas_call(
        _router_body,
        out_shape=(jax.ShapeDtypeStruct((TOP_K, t), I32), jax.ShapeDtypeStruct((TOP_K, t), F32),
                   jax.ShapeDtypeStruct((TOP_K, t), I32), jax.ShapeDtypeStruct((N_EXPERTS, 1), F32)),
        grid=(t // tm,),
        in_specs=[pl.BlockSpec((tm, d), lambda i: (i, 0)),
                  pl.BlockSpec((N_EXPERTS, d), lambda i: (0, 0)),
                  pl.BlockSpec((N_EXPERTS, d), lambda i: (0, 0)),
                  pl.BlockSpec((N_EXPERTS, 1), lambda i: (0, 0))],
        out_specs=(pl.BlockSpec((TOP_K, tm), lambda i: (0, i)),
                   pl.BlockSpec((TOP_K, tm), lambda i: (0, i)),
                   pl.BlockSpec((TOP_K, tm), lambda i: (0, i)),
                   pl.BlockSpec((N_EXPERTS, 1), lambda i: (0, 0))),
        scratch_shapes=[pltpu.VMEM((N_EXPERTS, 1), F32)],
        compiler_params=_cparams(("arbitrary",)),
        name="router",
    )(h, wt_hi, wt_lo, bias_col)


def _positions_body(idx_ref, rank_ref, start_ref, pos_ref):
    tm = idx_ref.shape[1]
    rowi = lax.broadcasted_iota(I32, (N_EXPERTS, tm), 0)
    start = start_ref[...]
    idx = idx_ref[...]
    rows = [jnp.sum(jnp.where(rowi == idx[k:k + 1, :], start, 0.0), axis=0, keepdims=True) for k in range(TOP_K)]
    pos_ref[...] = jnp.concatenate(rows, axis=0).astype(I32) + rank_ref[...]


def _positions(idx_t, rank_t, start_col):
    t = idx_t.shape[1]
    tm = POS_TM
    return pl.pallas_call(
        _positions_body,
        out_shape=jax.ShapeDtypeStruct((TOP_K, t), I32),
        grid=(t // tm,),
        in_specs=[pl.BlockSpec((TOP_K, tm), lambda i: (0, i)),
                  pl.BlockSpec((TOP_K, tm), lambda i: (0, i)),
                  pl.BlockSpec((N_EXPERTS, 1), lambda i: (0, 0))],
        out_specs=pl.BlockSpec((TOP_K, tm), lambda i: (0, i)),
        compiler_params=_cparams(("arbitrary",)),
        name="positions",
    )(idx_t, rank_t, start_col)


def _pad_fill_copy(zeros, xs_ref, sem, row, nrows):
    return pltpu.make_async_copy(_rows(zeros, 0, nrows), _rows(xs_ref, row, nrows), sem)


def _dispatch_body(pad_row_ref, pad_n_ref, nu_ref, pos_ref, h4_ref, xs_ref, zeros, sem, pad_sem):
    i = pl.program_id(0)
    tm = h4_ref.shape[0] // ROW_WORDS
    half = MOE_BM // 2
    pad_bits = [1 << j for j in range(MOE_BM.bit_length() - 1)]
    n_half_blocks = xs_ref.shape[0] // (half * ROW_WORDS)

    def pad_pass(wait):
        def go(cp):
            if wait:
                cp.wait()
            else:
                cp.start()

        def body(e, carry):
            row = pad_row_ref[e]
            n = pad_n_ref[e]
            for bit in pad_bits:
                @pl.when((n & bit) != 0)
                def _():
                    go(_pad_fill_copy(zeros, xs_ref, pad_sem, row + (n & (bit - 1)), bit))
            return carry
        lax.fori_loop(0, N_EXPERTS, body, 0)

        def tail(hb, carry):
            go(_pad_fill_copy(zeros, xs_ref, pad_sem, hb * half, half))
            return carry
        lax.fori_loop(nu_ref[0] * 2, n_half_blocks, tail, 0)

    @pl.when(i == 0)
    def _():
        zeros[...] = jnp.zeros(zeros.shape, U32)
        pad_pass(False)

    def body(tt, carry):
        for dt in range(DISP_UNROLL):
            t = tt * DISP_UNROLL + dt
            for k in range(TOP_K):
                p = pos_ref[0, 0, t * TOP_K + k]
                pltpu.make_async_copy(_rows(h4_ref, t, 1), _rows(xs_ref, p, 1), sem).start(priority=k % 2)
        return carry
    lax.fori_loop(0, tm // DISP_UNROLL, body, 0)

    for k in range(TOP_K):
        pltpu.make_async_copy(h4_ref, _rows(xs_ref, 0, tm), sem).wait()

    @pl.when(i == pl.num_programs(0) - 1)
    def _():
        pad_pass(True)


def _dispatch(pad_row, pad_n, n_used, pos3, h4, n_rows):
    nt, _, per_step = pos3.shape
    tm = per_step // TOP_K
    grid_spec = pltpu.PrefetchScalarGridSpec(
        num_scalar_prefetch=3,
        grid=(nt,),
        in_specs=[pl.BlockSpec((1, 1, per_step), lambda i, a, b, c: (i, 0, 0), memory_space=pltpu.SMEM),
                  pl.BlockSpec((tm * ROW_WORDS, LANES), lambda i, a, b, c: (i, 0))],
        out_specs=pl.BlockSpec(memory_space=pl.ANY),
        scratch_shapes=[pltpu.VMEM((MOE_BM // 2 * ROW_WORDS, LANES), U32),
                        pltpu.SemaphoreType.DMA, pltpu.SemaphoreType.DMA],
    )
    return pl.pallas_call(
        _dispatch_body,
        out_shape=jax.ShapeDtypeStruct((n_rows * ROW_WORDS, LANES), U32),
        grid_spec=grid_spec,
        compiler_params=_cparams(("arbitrary",)),
        name="dispatch",
    )(pad_row, pad_n, n_used, pos3, h4)


def _weight_copies(e, w_hbm, stage, sem, slot):
    return [pltpu.make_async_copy(w.at[e], st.at[slot], sem.at[slot, j])
            for j, (w, st) in enumerate(zip(w_hbm, stage))]


def _row_block_copy(xs_hbm, xbuf, xsem, j):
    slot = j % X_RING
    return pltpu.make_async_copy(_rows(xs_hbm, j * MOE_BM, MOE_BM, align=MOE_BM), xbuf.at[slot], xsem.at[slot])


def _experts_body(be_ref, nu_ref, nx_ref, ord_ref, xs_hbm, wg_hbm, wu_hbm, wd_hbm, y4_ref,
                  xbuf, xsem, sg, su, sd, wsem, wg_b, wu_b, wd_b):
    w_hbm = (wg_hbm, wu_hbm, wd_hbm)
    stage = (sg, su, sd)
    n_used = nu_ref[0]

    def do_block(i, out_row):
        @pl.when(i < n_used)
        def _():
            e = be_ref[i]
            prev_e = be_ref[jnp.maximum(i - 1, 0)]

            @pl.when(i == 0)
            def _():
                for j in range(X_RING - 1):
                    @pl.when(j < n_used)
                    def _():
                        _row_block_copy(xs_hbm, xbuf, xsem, j).start()

            @pl.when(i + X_RING - 1 < n_used)
            def _():
                _row_block_copy(xs_hbm, xbuf, xsem, i + X_RING - 1).start()

            @pl.when((i == 0) | (e != prev_e))
            def _():
                wslot = ord_ref[e] % 2
                n1 = nx_ref[e]
                n2 = jnp.where(n1 >= 0, nx_ref[jnp.maximum(n1, 0)], -1)

                @pl.when(i == 0)
                def _():
                    for cp in _weight_copies(e, w_hbm, stage, wsem, wslot):
                        cp.start()

                    @pl.when(n1 >= 0)
                    def _():
                        for cp in _weight_copies(n1, w_hbm, stage, wsem, 1 - wslot):
                            cp.start()
                for cp in _weight_copies(e, w_hbm, stage, wsem, wslot):
                    cp.wait()
                wg_b[...] = sg[wslot].astype(BF16)
                wu_b[...] = su[wslot].astype(BF16)
                wd_b[...] = sd[wslot].astype(BF16)

                @pl.when(n2 >= 0)
                def _():
                    for cp in _weight_copies(n2, w_hbm, stage, wsem, wslot):
                        cp.start()

            _row_block_copy(xs_hbm, xbuf, xsem, i).wait()
            slot = i % X_RING
            parts = [_unpack_row_words(xbuf[slot, _word_plane(0, MOE_BM, s), :]) for s in range(ROW_WORDS)]
            x = jnp.concatenate([p[0] for p in parts] + [p[1] for p in parts], axis=1).astype(BF16)
            g = jnp.dot(x, wg_b[...], preferred_element_type=F32)
            u = jnp.dot(x, wu_b[...], preferred_element_type=F32)
            a = (_silu(g) * u).astype(BF16)
            y = jnp.dot(a, wd_b[...], preferred_element_type=F32)
            for s, w in enumerate(_pack_row_words(y)):
                y4_ref[_word_plane(out_row, MOE_BM, s), :] = w

        @pl.when(i >= n_used)
        def _():
            y4_ref[pl.ds(out_row * ROW_WORDS, MOE_BM * ROW_WORDS), :] = jnp.zeros((MOE_BM * ROW_WORDS, LANES), U32)

    for sub in range(EXP_BLOCKS_PER_STEP):
        do_block(pl.program_id(0) * EXP_BLOCKS_PER_STEP + sub, sub * MOE_BM)


def _experts(block_e, n_used, next_e, ord_e, xs, wg, wu, wd):
    nblk = block_e.shape[0]
    bm = MOE_BM * EXP_BLOCKS_PER_STEP
    e, d, de = wg.shape

    grid_spec = pltpu.PrefetchScalarGridSpec(
        num_scalar_prefetch=4,
        grid=(nblk // EXP_BLOCKS_PER_STEP,),
        in_specs=[pl.BlockSpec(memory_space=pl.ANY),
                  pl.BlockSpec(memory_space=pl.ANY),
                  pl.BlockSpec(memory_space=pl.ANY),
                  pl.BlockSpec(memory_space=pl.ANY)],
        out_specs=pl.BlockSpec((bm * ROW_WORDS, LANES), lambda i, be, nu, nx, od: (i, 0)),
        scratch_shapes=[pltpu.VMEM((X_RING, MOE_BM * ROW_WORDS, LANES), U32), pltpu.SemaphoreType.DMA((X_RING,)),
                        pltpu.VMEM((2, d, de), F32), pltpu.VMEM((2, d, de), F32), pltpu.VMEM((2, de, d), F32),
                        pltpu.SemaphoreType.DMA((2, 3)),
                        pltpu.VMEM((d, de), BF16), pltpu.VMEM((d, de), BF16), pltpu.VMEM((de, d), BF16)],
    )
    return pl.pallas_call(
        _experts_body,
        out_shape=jax.ShapeDtypeStruct((nblk * MOE_BM * ROW_WORDS, LANES), U32),
        grid_spec=grid_spec,
        compiler_params=_cparams(("arbitrary",)),
        name="experts",
    )(block_e, n_used, next_e, ord_e, xs, wg, wu, wd)


def _combine_body(pos_cur, pos_nxt, y4_ref, h_ref, gate_ref, wsg_ref, wsu_ref, wsd_ref, g_ref, b_ref,
                  o_ref, buf, sem):
    i = pl.program_id(0)
    nb = pl.num_programs(0)
    tm, d = h_ref.shape
    slot = i % 2

    def issue_rows(pos_ref, sl, t):
        for k in range(TOP_K):
            p = pos_ref[0, 0, t * TOP_K + k]
            pltpu.make_async_copy(_rows(y4_ref, p, 1), _rows(buf.at[sl], k * tm + t, 1),
                                  sem.at[sl]).start(priority=k % 2)

    @pl.when(i == 0)
    def _():
        def body(t, carry):
            issue_rows(pos_cur, 0, t)
            return carry
        lax.fori_loop(0, tm, body, 0)

    pltpu.make_async_copy(_rows(y4_ref, 0, tm * TOP_K), buf.at[slot], sem.at[slot]).wait()

    grp = COMB_GROUP

    def finish_rows(tg):
        r0 = pl.multiple_of(tg * grp, grp)
        gates = gate_ref[pl.ds(r0, grp), :]
        lo_cols, hi_cols = [], []
        for s in range(ROW_WORDS):
            lo_acc = hi_acc = None
            for k in range(TOP_K):
                w = buf[slot, pl.ds((k * tm + r0) * ROW_WORDS + s, grp, stride=ROW_WORDS), :]
                lo, hi = _unpack_row_words(w)
                gk = gates[:, k:k + 1]
                lo_acc = gk * lo if lo_acc is None else lo_acc + gk * lo
                hi_acc = gk * hi if hi_acc is None else hi_acc + gk * hi
            lo_cols.append(lo_acc)
            hi_cols.append(hi_acc)
        routed = jnp.concatenate(lo_cols + hi_cols, axis=1)
        h = h_ref[pl.ds(r0, grp), :]
        hb = h.astype(BF16)
        sg = jnp.dot(hb, wsg_ref[...], preferred_element_type=F32)
        su = jnp.dot(hb, wsu_ref[...], preferred_element_type=F32)
        shared = jnp.dot((_silu(sg) * su).astype(BF16), wsd_ref[...], preferred_element_type=F32)
        o_ref[pl.ds(r0, grp), :] = _layernorm(DEEPNORM_ALPHA * h + (routed + shared), g_ref[...], b_ref[...])

    @pl.when(i + 1 < nb)
    def _():
        def body(tg, carry):
            for dt in range(grp):
                issue_rows(pos_nxt, 1 - slot, tg * grp + dt)
            finish_rows(tg)
            return carry
        lax.fori_loop(0, tm // grp, body, 0)

    @pl.when(i + 1 >= nb)
    def _():
        def body(tg, carry):
            finish_rows(tg)
            return carry
        lax.fori_loop(0, tm // grp, body, 0)


def _combine(pos3, y4, h, gates, wsg, wsu, wsd, g, b):
    t, d = h.shape
    tm = COMB_TM
    nt = t // tm
    ds_ = wsg.shape[1]
    return pl.pallas_call(
        _combine_body,
        out_shape=jax.ShapeDtypeStruct((t, d), F32),
        grid=(nt,),
        in_specs=[pl.BlockSpec((1, 1, tm * TOP_K), lambda i: (i, 0, 0), memory_space=pltpu.SMEM),
                  pl.BlockSpec((1, 1, tm * TOP_K), lambda i: (jnp.minimum(i + 1, nt - 1), 0, 0),
                               memory_space=pltpu.SMEM),
                  pl.BlockSpec(memory_space=pl.ANY),
                  pl.BlockSpec((tm, d), lambda i: (i, 0)),
                  pl.BlockSpec((tm, TOP_K), lambda i: (i, 0)),
                  pl.BlockSpec((d, ds_), lambda i: (0, 0)),
                  pl.BlockSpec((d, ds_), lambda i: (0, 0)),
                  pl.BlockSpec((ds_, d), lambda i: (0, 0)),
                  pl.BlockSpec((1, d), lambda i: (0, 0)),
                  pl.BlockSpec((1, d), lambda i: (0, 0))],
        out_specs=pl.BlockSpec((tm, d), lambda i: (i, 0)),
        scratch_shapes=[pltpu.VMEM((2, tm * TOP_K * ROW_WORDS, LANES), U32),
                        pltpu.SemaphoreType.DMA((2,))],
        compiler_params=_cparams(("arbitrary",)),
        name="combine",
    )(pos3, pos3, y4, h, gates, wsg, wsu, wsd, g, b)


def _expert_tables(counts, nblk):
    bm = MOE_BM
    cnt = counts.reshape(N_EXPERTS).astype(I32)
    padded = (cnt + bm - 1) // bm * bm
    padded_end = jnp.cumsum(padded)
    padded_start = padded_end - padded
    block_rows = jnp.arange(nblk, dtype=I32) * bm
    block_e = jnp.sum((padded_end[None, :] <= block_rows[:, None]).astype(I32), axis=1)
    block_e = jnp.minimum(block_e, N_EXPERTS - 1)
    n_used = (padded_end[-1:] // bm).astype(I32)
    ids = jnp.where(cnt > 0, jnp.arange(N_EXPERTS, dtype=I32), N_EXPERTS)
    after = jnp.concatenate([lax.cummin(ids, reverse=True)[1:], jnp.full((1,), N_EXPERTS, I32)])
    next_e = jnp.where(after < N_EXPERTS, after, -1).astype(I32)
    ord_e = (jnp.cumsum((cnt > 0).astype(I32)) - 1).astype(I32)
    return padded_start, padded_start + cnt, padded - cnt, block_e, n_used, next_e, ord_e


def kernel(x, w_in, gla_gate_w2, gla_gate_b, gla_norm_w, pool_w_group, pool_scale, w_out, ln1_g, ln1_b,
           router_w, router_bias, w_exp_gate, w_exp_up, w_exp_down, w_sh_gate, w_sh_up, w_sh_down, ln2_g, ln2_b):
    batch, seq, d = x.shape
    t = batch * seq
    h2d = x.reshape(t, d)
    for l in range(DEPTH):
        d_in = w_in.shape[2]
        w_in_b = jnp.pad(w_in[l], ((0, 0), (0, D_IN_PAD - d_in))).astype(BF16)
        w2p = jnp.pad(gla_gate_w2[l], ((0, LANES - GLA_GATE_RANK), (0, 0))).astype(BF16)
        h, h4 = _front(h2d, batch, seq, w_in_b, w2p, gla_gate_b[l].reshape(1, -1), gla_norm_w[l].reshape(1, -1),
                       pool_w_group[l].astype(BF16), pool_scale[l].reshape(1, -1),
                       w_out[l].astype(BF16), ln1_g[l].reshape(1, -1), ln1_b[l].reshape(1, -1))
        idx_t, gate_t, rank_t, counts = _router(h, router_w[l].T, router_bias[l].reshape(-1, 1))
        nblk = (t * TOP_K + N_EXPERTS * (MOE_BM - 1)) // MOE_BM
        nblk = -(-nblk // EXP_BLOCKS_PER_STEP) * EXP_BLOCKS_PER_STEP
        start, pad_row, pad_n, block_e, n_used, next_e, ord_e = _expert_tables(counts, nblk)
        pos_t = _positions(idx_t, rank_t, start.astype(F32).reshape(-1, 1))
        pos_tok = pos_t.T
        xs = _dispatch(pad_row, pad_n, n_used, pos_tok.reshape(t // DISP_TM, 1, DISP_TM * TOP_K), h4, nblk * MOE_BM)
        y4 = _experts(block_e, n_used, next_e, ord_e, xs, w_exp_gate[l], w_exp_up[l], w_exp_down[l])
        h2d = _combine(pos_tok.reshape(t // COMB_TM, 1, COMB_TM * TOP_K), y4, h, gate_t.T,
                       w_sh_gate[l].astype(BF16), w_sh_up[l].astype(BF16), w_sh_down[l].astype(BF16),
                       ln2_g[l].reshape(1, -1), ln2_b[l].reshape(1, -1))
    return h2d.reshape(batch, seq, d)
```

```python
import jax
import jax.numpy as jnp
from jax import lax
from jax.experimental import pallas as pl
from jax.experimental.pallas import tpu as pltpu

F32 = jnp.float32
BF16 = jnp.bfloat16
I32 = jnp.int32
U32 = jnp.uint32
HIGH_HALF = 0xFFFF0000

POOL_WINDOWS = (2, 4, 8, 16)
POOL_GROUP_DIM = 128
POOL_WIDTH = 512
GLA_HEADS = 4
GLA_DK = 64
GLA_DV = 128
GLA_DK_TOTAL = 256
GLA_WIDTH = 512
GLA_GATE_RANK = 16
GLA_GATE_NORMALIZER = 16.0
GLA_CHUNK = 16
GLA_SAFE_EXP = 60.0
N_EXPERTS = 256
TOP_K = 8
N_GROUPS = 8
GROUP_SIZE = N_EXPERTS // N_GROUPS
TOPK_GROUPS = 4
ROUTED_SCALE = 2.5
DEPTH = 1
DEEPNORM_ALPHA = (2.0 * DEPTH) ** 0.25
LN_EPS = 1e-5
RMS_EPS = 1e-5

LANES = 128
SUBLANES = 8
VMEM_LIMIT = 56 * 1024 * 1024

MIX_TS = 512
ROUTE_TM = 512
MOE_BM = 256
COMB_TM = 512
POS_TM = 1024
DISP_TM = 1024
DISP_UNROLL = 4
ROW_WORDS = 4
X_RING = 6
EXP_BLOCKS_PER_STEP = 8
D_IN_PAD = 2176


def _cparams(sem):
    return pltpu.CompilerParams(dimension_semantics=sem, vmem_limit_bytes=VMEM_LIMIT)


def _silu(x):
    return x * (1.0 / (1.0 + jnp.exp(-x)))


def _layernorm(y, g, b):
    mu = jnp.mean(y, axis=-1, keepdims=True)
    yc = y - mu
    var = jnp.mean(yc * yc, axis=-1, keepdims=True)
    return yc * lax.rsqrt(var + LN_EPS) * g + b


def _mixer_body(p_ref, q_ref, k_ref, v_ref, r_ref, gl_ref, w2_ref, gb_ref, nw_ref, pw_ref, ps_ref,
                o_ref, pbuf, state, kvbuf, sall, obuf, gk_s):
    ts = p_ref.shape[0]
    s_idx = pl.program_id(1)
    halo = POOL_WINDOWS[-1]

    @pl.when(s_idx == 0)
    def _():
        pbuf[pl.ds(0, halo), :] = jnp.zeros((halo, POOL_WIDTH), F32)
        state[...] = jnp.zeros(state.shape, F32)

    p = p_ref[...]
    pbuf[pl.ds(halo, ts), :] = p
    pos = s_idx * ts + lax.broadcasted_iota(I32, (ts, 1), 0)
    for g, w in enumerate(POOL_WINDOWS):
        c0 = g * POOL_GROUP_DIM
        ext = pbuf[:, pl.ds(c0, POOL_GROUP_DIM)]
        sh = 1
        while sh < w:
            ext = ext + pltpu.roll(ext, sh, axis=0)
            sh *= 2
        acc = ext[halo:, :]
        cnt = jnp.minimum(pos + 1, w).astype(F32)
        mixed = acc / cnt - p[:, c0:c0 + POOL_GROUP_DIM]
        og = jnp.dot(mixed.astype(BF16), pw_ref[g], preferred_element_type=F32)
        o_ref[:, pl.ds(c0, POOL_GROUP_DIM)] = (og * ps_ref[:, pl.ds(c0, POOL_GROUP_DIM)]).astype(o_ref.dtype)
    pbuf[pl.ds(0, halo), :] = pbuf[pl.ds(ts, halo), :]

    nchunk = ts // GLA_CHUNK
    glog = jnp.dot(gl_ref[...].astype(BF16), w2_ref[...], preferred_element_type=F32) + gb_ref[...]
    gk = (jnp.minimum(glog, 0.0) - jnp.log(1.0 + jnp.exp(-jnp.abs(glog)))) * (1.0 / GLA_GATE_NORMALIZER)
    row = lax.broadcasted_iota(I32, (ts, 1), 0)
    rin = row % GLA_CHUNK
    b = gk
    sh = 1
    while sh < GLA_CHUNK:
        b = b + jnp.where(rin >= sh, pltpu.roll(b, sh, axis=0), 0.0)
        sh *= 2
    b3 = b.reshape(nchunk, GLA_CHUNK, GLA_DK_TOTAL)
    bmid = b3[:, GLA_CHUNK // 2 - 1:GLA_CHUNK // 2, :]
    blast = b3[:, GLA_CHUNK - 1:GLA_CHUNK, :]
    lane = lax.broadcasted_iota(I32, (1, LANES), 1)
    head_lane = [lane < GLA_DK, lane >= GLA_DK]
    q_scale = GLA_DK ** -0.5
    safe = jnp.max(jnp.abs(b3 - bmid)) <= GLA_SAFE_EXP

    @pl.when(safe)
    def _chunked():
        v = v_ref[...]
        vb = v.astype(BF16)
        q3 = (q_ref[...] * q_scale).reshape(nchunk, GLA_CHUNK, GLA_DK_TOTAL)
        k3 = k_ref[...].reshape(nchunk, GLA_CHUNK, GLA_DK_TOTAL)
        qs = (q3 * jnp.exp(b3 - bmid)).reshape(ts, GLA_DK_TOTAL)
        ks = (k3 * jnp.exp(bmid - b3)).reshape(ts, GLA_DK_TOTAL)
        qd = (q3 * jnp.exp(b3)).reshape(ts, GLA_DK_TOTAL)
        kd = (k3 * jnp.exp(blast - b3)).reshape(ts, GLA_DK_TOTAL)
        cdec = jnp.exp(blast).reshape(nchunk, GLA_DK_TOTAL)

        blk = LANES
        ri = lax.broadcasted_iota(I32, (blk, blk), 0)
        ci = lax.broadcasted_iota(I32, (blk, blk), 1)
        causal = (ri // GLA_CHUNK == ci // GLA_CHUNK) & (ri >= ci)
        o_intra = [[None] * (ts // blk) for _ in range(GLA_HEADS)]
        for rb in range(ts // blk):
            rs = slice(rb * blk, (rb + 1) * blk)
            for pair in range(GLA_HEADS // 2):
                ls = slice(pair * LANES, (pair + 1) * LANES)
                ks_p = ks[rs, ls].astype(BF16)
                q_p = qs[rs, ls]
                q2 = jnp.concatenate([jnp.where(head_lane[sub], q_p, 0.0) for sub in range(2)], axis=0).astype(BF16)
                sc2 = lax.dot_general(q2, ks_p, (((1,), (1,)), ((), ())), preferred_element_type=F32)
                for sub in range(2):
                    h = pair * 2 + sub
                    sc = jnp.where(causal, sc2[sub * blk:(sub + 1) * blk], 0.0).astype(BF16)
                    o_intra[h][rb] = jnp.dot(sc, vb[rs, h * GLA_DV:(h + 1) * GLA_DV], preferred_element_type=F32)

        cpb = blk // GLA_CHUNK
        chunk_of_col = lax.broadcasted_iota(I32, (1, blk), 1) // GLA_CHUNK
        kdb = kd.astype(BF16)
        for rb in range(ts // blk):
            rs = slice(rb * blk, (rb + 1) * blk)
            for pair in range(GLA_HEADS // 2):
                ls = slice(pair * LANES, (pair + 1) * LANES)
                lhs = []
                for sub in range(2):
                    h = pair * 2 + sub
                    v_t = v[rs, h * GLA_DV:(h + 1) * GLA_DV].T
                    lhs += [jnp.where(chunk_of_col == c, v_t, 0.0) for c in range(cpb)]
                both = jnp.dot(jnp.concatenate(lhs, axis=0).astype(BF16), kdb[rs, ls],
                               preferred_element_type=F32)
                stacked = jnp.where(head_lane[0], both[:cpb * LANES], both[cpb * LANES:])
                for c in range(cpb):
                    kvbuf[pair, rb * cpb + c] = stacked[c * LANES:(c + 1) * LANES]
        for pair in range(GLA_HEADS // 2):
            ls = slice(pair * LANES, (pair + 1) * LANES)
            st = state[pair]
            for c in range(nchunk):
                sall[pair, c] = st.astype(BF16)
                st = st * cdec[c:c + 1, ls] + kvbuf[pair, c]
            state[pair] = st
        o_inter = [[None] * nchunk for _ in range(GLA_HEADS)]
        for pair in range(GLA_HEADS // 2):
            ls = slice(pair * LANES, (pair + 1) * LANES)
            for c in range(nchunk):
                rs = slice(c * GLA_CHUNK, (c + 1) * GLA_CHUNK)
                q_c = qd[rs, ls]
                q2 = jnp.concatenate([jnp.where(head_lane[sub], q_c, 0.0) for sub in range(2)], axis=0).astype(BF16)
                res = lax.dot_general(q2, sall[pair, c], (((1,), (1,)), ((), ())), preferred_element_type=F32)
                for sub in range(2):
                    o_inter[pair * 2 + sub][c] = res[sub * GLA_CHUNK:(sub + 1) * GLA_CHUNK]
        for h in range(GLA_HEADS):
            obuf[:, pl.ds(h * GLA_DV, GLA_DV)] = (jnp.concatenate(o_intra[h], axis=0)
                                                  + jnp.concatenate(o_inter[h], axis=0))

    @pl.when(jnp.logical_not(safe))
    def _row_by_row():
        gk_s[...] = gk
        row8 = lax.broadcasted_iota(I32, (SUBLANES, 1), 0)

        def slab(i8, carry):
            r0 = pl.multiple_of(i8 * SUBLANES, SUBLANES)
            q8 = q_ref[pl.ds(r0, SUBLANES), :] * q_scale
            k8 = k_ref[pl.ds(r0, SUBLANES), :]
            v8 = v_ref[pl.ds(r0, SUBLANES), :]
            g8 = jnp.exp(gk_s[pl.ds(r0, SUBLANES), :])
            outs = [jnp.zeros((SUBLANES, GLA_DV), F32) for _ in range(GLA_HEADS)]
            for pair in range(GLA_HEADS // 2):
                ls = slice(pair * LANES, (pair + 1) * LANES)
                st = state[pair]
                for r in range(SUBLANES):
                    sel = row8 == r
                    k_r = jnp.where(sel, k8[:, ls], 0.0).astype(BF16)
                    inc = []
                    for sub in range(2):
                        h = pair * 2 + sub
                        v_r = jnp.where(sel, v8[:, h * GLA_DV:(h + 1) * GLA_DV], 0.0).astype(BF16)
                        inc.append(lax.dot_general(v_r, k_r, (((0,), (0,)), ((), ())), preferred_element_type=F32))
                    st = st * g8[r:r + 1, ls] + jnp.where(head_lane[0], inc[0], inc[1])
                    q2 = jnp.concatenate([jnp.where(sel & head_lane[sub], q8[:, ls], 0.0) for sub in range(2)],
                                         axis=0).astype(BF16)
                    res = lax.dot_general(q2, st.astype(BF16), (((1,), (1,)), ((), ())),
                                          preferred_element_type=F32)
                    for sub in range(2):
                        outs[pair * 2 + sub] = outs[pair * 2 + sub] + res[sub * SUBLANES:(sub + 1) * SUBLANES]
                state[pair] = st
            obuf[pl.ds(r0, SUBLANES), :] = jnp.concatenate(outs, axis=1)
            return carry
        lax.fori_loop(0, ts // SUBLANES, slab, 0)

    nw = nw_ref[...]
    r = r_ref[...]
    for h in range(GLA_HEADS):
        o = obuf[:, pl.ds(h * GLA_DV, GLA_DV)]
        o = o * lax.rsqrt(jnp.mean(o * o, axis=-1, keepdims=True) + RMS_EPS) * nw
        o = o * _silu(r[:, h * GLA_DV:(h + 1) * GLA_DV])
        o_ref[:, pl.ds(POOL_WIDTH + h * GLA_DV, GLA_DV)] = o.astype(o_ref.dtype)


def _front_body(x_ref, win_ref, w2_ref, gb_ref, nw_ref, pw_ref, ps_ref, wout_ref, g_ref, b_ref, h_ref, h4_ref,
                pp_s, pq_s, pk_s, pv_s, pr_s, pg_s, mix_s, pbuf, state, kvbuf, sall, obuf, gk_s):
    groups = (pp_s, pq_s, pk_s, pv_s, pr_s, pg_s)
    cols = [0]
    for ref in groups:
        cols.append(cols[-1] + ref.shape[1])
    xb = x_ref[...].astype(BF16)
    for gi in (0, 5, 1, 2, 3, 4):
        groups[gi][...] = jnp.dot(xb, win_ref[:, pl.ds(cols[gi], groups[gi].shape[1])],
                                  preferred_element_type=F32)
    _mixer_body(*groups, w2_ref, gb_ref, nw_ref, pw_ref, ps_ref, mix_s, pbuf, state, kvbuf, sall, obuf, gk_s)
    y = DEEPNORM_ALPHA * x_ref[...] + jnp.dot(mix_s[...], wout_ref[...], preferred_element_type=F32)
    h = _layernorm(y, g_ref[...], b_ref[...])
    h_ref[...] = h
    for s, w in enumerate(_pack_row_words(h)):
        h4_ref[_word_plane(0, h.shape[0], s), :] = w


def _front(x2d, batch, seq, w_in_b, w2p, gate_b, norm_w, pool_w, pool_scale, w_out_b, ln_g, ln_b):
    t, d = x2d.shape
    ts = MIX_TS
    nseq = seq // ts

    def full(shape):
        return pl.BlockSpec(shape, lambda bi, si: (0,) * len(shape))

    return pl.pallas_call(
        _front_body,
        out_shape=(jax.ShapeDtypeStruct((t, d), F32), jax.ShapeDtypeStruct((t * ROW_WORDS, LANES), U32)),
        grid=(batch, nseq),
        in_specs=[pl.BlockSpec((ts, d), lambda bi, si: (bi * nseq + si, 0)),
                  full(w_in_b.shape), full(w2p.shape), full(gate_b.shape), full(norm_w.shape),
                  full(pool_w.shape), full(pool_scale.shape), full(w_out_b.shape), full(ln_g.shape), full(ln_b.shape)],
        out_specs=(pl.BlockSpec((ts, d), lambda bi, si: (bi * nseq + si, 0)),
                   pl.BlockSpec((ts * ROW_WORDS, LANES), lambda bi, si: (bi * nseq + si, 0))),
        scratch_shapes=[pltpu.VMEM((ts, POOL_WIDTH), F32), pltpu.VMEM((ts, GLA_DK_TOTAL), F32),
                        pltpu.VMEM((ts, GLA_DK_TOTAL), F32), pltpu.VMEM((ts, GLA_WIDTH), F32),
                        pltpu.VMEM((ts, GLA_WIDTH), F32), pltpu.VMEM((ts, D_IN_PAD - 2 * GLA_DK_TOTAL
                                                                      - 2 * GLA_WIDTH - POOL_WIDTH), F32),
                        pltpu.VMEM((ts, POOL_WIDTH + GLA_WIDTH), BF16),
                        pltpu.VMEM((ts + POOL_WINDOWS[-1], POOL_WIDTH), F32),
                        pltpu.VMEM((GLA_HEADS // 2, LANES, GLA_DV), F32),
                        pltpu.VMEM((GLA_HEADS // 2, ts // GLA_CHUNK, LANES, GLA_DV), F32),
                        pltpu.VMEM((GLA_HEADS // 2, ts // GLA_CHUNK, LANES, GLA_DV), BF16),
                        pltpu.VMEM((ts, GLA_WIDTH), F32), pltpu.VMEM((ts, GLA_DK_TOTAL), F32)],
        compiler_params=_cparams(("arbitrary", "arbitrary")),
        name="front",
    )(x2d, w_in_b, w2p, gate_b, norm_w, pool_w, pool_scale, w_out_b, ln_g, ln_b)


def _pack_row_words(x):
    half = x.shape[1] // 2
    u = pltpu.bitcast(x.astype(BF16).astype(F32), U32)
    hi_mask = jnp.uint32(HIGH_HALF)
    return [(u[:, half + s * LANES:half + (s + 1) * LANES] & hi_mask) | (u[:, s * LANES:(s + 1) * LANES] >> 16)
            for s in range(ROW_WORDS)]


def _unpack_row_words(w):
    return pltpu.bitcast(w << 16, F32), pltpu.bitcast(w & jnp.uint32(HIGH_HALF), F32)


def _rows(ref, first, n, align=1):
    if isinstance(first, int):
        start = first * ROW_WORDS
    else:
        start = pl.multiple_of(first * ROW_WORDS, ROW_WORDS * align)
    return ref.at[pl.ds(start, n * ROW_WORDS), :]


def _word_plane(first, m, s):
    return pl.ds(first * ROW_WORDS + s, m, stride=ROW_WORDS)


def _first_argmax_rows(val, rowf, nrows):
    m = jnp.max(val, axis=0, keepdims=True)
    first = jnp.min(jnp.where(val == m, rowf, float(nrows)), axis=0, keepdims=True)
    return m, first, rowf == first


def _router_body(h_ref, whi_ref, wlo_ref, bias_ref, idx_ref, gate_ref, rank_ref, cnt_ref, carry):
    tm = h_ref.shape[0]
    i = pl.program_id(0)

    @pl.when(i == 0)
    def _():
        carry[...] = jnp.zeros(carry.shape, F32)

    h = h_ref[...]
    h_hi = h.astype(BF16)
    h_lo = (h - h_hi.astype(F32)).astype(BF16)
    nt = (((1,), (1,)), ((), ()))
    logits = (lax.dot_general(whi_ref[...], h_hi, nt, preferred_element_type=F32)
              + lax.dot_general(whi_ref[...], h_lo, nt, preferred_element_type=F32)
              + lax.dot_general(wlo_ref[...], h_hi, nt, preferred_element_type=F32))
    scores = 1.0 / (1.0 + jnp.exp(-logits))
    biased = scores + bias_ref[...]
    neg = -jnp.inf

    grp = biased.reshape(N_GROUPS, GROUP_SIZE, tm)
    gi = lax.broadcasted_iota(I32, (N_GROUPS, GROUP_SIZE, tm), 1).astype(F32)
    g1 = jnp.max(grp, axis=1, keepdims=True)
    f1 = jnp.min(jnp.where(grp == g1, gi, float(GROUP_SIZE)), axis=1, keepdims=True)
    g2 = jnp.max(jnp.where(gi == f1, neg, grp), axis=1, keepdims=True)
    gscore = (g1 + g2).reshape(N_GROUPS, tm)

    growf = lax.broadcasted_iota(I32, (N_GROUPS, tm), 0).astype(F32)
    gsel = jnp.zeros((N_GROUPS, tm), F32)
    gval = gscore
    for _ in range(TOPK_GROUPS):
        _, _, pick = _first_argmax_rows(gval, growf, N_GROUPS)
        gsel = jnp.where(pick, 1.0, gsel)
        gval = jnp.where(pick, neg, gval)
    emask = jnp.broadcast_to(gsel.reshape(N_GROUPS, 1, tm), (N_GROUPS, GROUP_SIZE, tm)).reshape(N_EXPERTS, tm)

    rowf = lax.broadcasted_iota(I32, (N_EXPERTS, tm), 0).astype(F32)
    val = jnp.where(emask > 0.0, biased, neg)
    onehot = jnp.zeros((N_EXPERTS, tm), F32)
    picks, idxs, ws = [], [], []
    for _ in range(TOP_K):
        _, first, pick = _first_argmax_rows(val, rowf, N_EXPERTS)
        picks.append(pick)
        idxs.append(first)
        ws.append(jnp.sum(jnp.where(pick, scores, 0.0), axis=0, keepdims=True))
        onehot = jnp.where(pick, 1.0, onehot)
        val = jnp.where(pick, neg, val)
    w = jnp.concatenate(ws, axis=0)
    gate_ref[...] = w / jnp.sum(w, axis=0, keepdims=True) * ROUTED_SCALE
    idx_ref[...] = jnp.concatenate(idxs, axis=0).astype(I32)

    ti = lax.broadcasted_iota(I32, (tm, tm), 0)
    tj = lax.broadcasted_iota(I32, (tm, tm), 1)
    upper = jnp.where(ti < tj, 1.0, 0.0).astype(BF16)
    prefix = jnp.dot(onehot.astype(BF16), upper, preferred_element_type=F32) + carry[...]
    ranks = [jnp.sum(jnp.where(pk, prefix, 0.0), axis=0, keepdims=True) for pk in picks]
    rank_ref[...] = jnp.concatenate(ranks, axis=0).astype(I32)
    carry[...] = carry[...] + jnp.sum(onehot, axis=1, keepdims=True)
    cnt_ref[...] = carry[...]


def _router(h, wt, bias_col):
    t, d = h.shape
    tm = ROUTE_TM
    wt_hi = wt.astype(BF16)
    wt_lo = (wt - wt_hi.astype(F32)).astype(BF16)
    return pl.pallas_call(
        _router_body,
        out_shape=(jax.ShapeDtypeStruct((TOP_K, t), I32), jax.ShapeDtypeStruct((TOP_K, t), F32),
                   jax.ShapeDtypeStruct((TOP_K, t), I32), jax.ShapeDtypeStruct((N_EXPERTS, 1), F32)),
        grid=(t // tm,),
        in_specs=[pl.BlockSpec((tm, d), lambda i: (i, 0)),
                  pl.BlockSpec((N_EXPERTS, d), lambda i: (0, 0)),
                  pl.BlockSpec((N_EXPERTS, d), lambda i: (0, 0)),
                  pl.BlockSpec((N_EXPERTS, 1), lambda i: (0, 0))],
        out_specs=(pl.BlockSpec((TOP_K, tm), lambda i: (0, i)),
                   pl.BlockSpec((TOP_K, tm), lambda i: (0, i)),
                   pl.BlockSpec((TOP_K, tm), lambda i: (0, i)),
                   pl.BlockSpec((N_EXPERTS, 1), lambda i: (0, 0))),
        scratch_shapes=[pltpu.VMEM((N_EXPERTS, 1), F32)],
        compiler_params=_cparams(("arbitrary",)),
        name="router",
    )(h, wt_hi, wt_lo, bias_col)


def _positions_body(idx_ref, rank_ref, start_ref, pos_ref):
    tm = idx_ref.shape[1]
    rowi = lax.broadcasted_iota(I32, (N_EXPERTS, tm), 0)
    start = start_ref[...]
    idx = idx_ref[...]
    rows = [jnp.sum(jnp.where(rowi == idx[k:k + 1, :], start, 0.0), axis=0, keepdims=True) for k in range(TOP_K)]
    pos_ref[...] = jnp.concatenate(rows, axis=0).astype(I32) + rank_ref[...]


def _positions(idx_t, rank_t, start_col):
    t = idx_t.shape[1]
    tm = POS_TM
    return pl.pallas_call(
        _positions_body,
        out_shape=jax.ShapeDtypeStruct((TOP_K, t), I32),
        grid=(t // tm,),
        in_specs=[pl.BlockSpec((TOP_K, tm), lambda i: (0, i)),
                  pl.BlockSpec((TOP_K, tm), lambda i: (0, i)),
                  pl.BlockSpec((N_EXPERTS, 1), lambda i: (0, 0))],
        out_specs=pl.BlockSpec((TOP_K, tm), lambda i: (0, i)),
        compiler_params=_cparams(("arbitrary",)),
        name="positions",
    )(idx_t, rank_t, start_col)


def _pad_fill_copy(zeros, xs_ref, sem, row, nrows):
    return pltpu.make_async_copy(_rows(zeros, 0, nrows), _rows(xs_ref, row, nrows), sem)


def _dispatch_body(pad_row_ref, pad_n_ref, nu_ref, pos_ref, h4_ref, xs_ref, zeros, sem, pad_sem):
    i = pl.program_id(0)
    tm = h4_ref.shape[0] // ROW_WORDS
    half = MOE_BM // 2
    pad_bits = [1 << j for j in range(MOE_BM.bit_length() - 1)]
    n_half_blocks = xs_ref.shape[0] // (half * ROW_WORDS)

    def pad_pass(wait):
        def go(cp):
            if wait:
                cp.wait()
            else:
                cp.start()

        def body(e, carry):
            row = pad_row_ref[e]
            n = pad_n_ref[e]
            for bit in pad_bits:
                @pl.when((n & bit) != 0)
                def _():
                    go(_pad_fill_copy(zeros, xs_ref, pad_sem, row + (n & (bit - 1)), bit))
            return carry
        lax.fori_loop(0, N_EXPERTS, body, 0)

        def tail(hb, carry):
            go(_pad_fill_copy(zeros, xs_ref, pad_sem, hb * half, half))
            return carry
        lax.fori_loop(nu_ref[0] * 2, n_half_blocks, tail, 0)

    @pl.when(i == 0)
    def _():
        zeros[...] = jnp.zeros(zeros.shape, U32)
        pad_pass(False)

    def body(tt, carry):
        for dt in range(DISP_UNROLL):
            t = tt * DISP_UNROLL + dt
            for k in range(TOP_K):
                p = pos_ref[0, 0, t * TOP_K + k]
                pltpu.make_async_copy(_rows(h4_ref, t, 1), _rows(xs_ref, p, 1), sem).start(priority=k % 2)
        return carry
    lax.fori_loop(0, tm // DISP_UNROLL, body, 0)

    for k in range(TOP_K):
        pltpu.make_async_copy(h4_ref, _rows(xs_ref, 0, tm), sem).wait()

    @pl.when(i == pl.num_programs(0) - 1)
    def _():
        pad_pass(True)


def _dispatch(pad_row, pad_n, n_used, pos3, h4, n_rows):
    nt, _, per_step = pos3.shape
    tm = per_step // TOP_K
    grid_spec = pltpu.PrefetchScalarGridSpec(
        num_scalar_prefetch=3,
        grid=(nt,),
        in_specs=[pl.BlockSpec((1, 1, per_step), lambda i, a, b, c: (i, 0, 0), memory_space=pltpu.SMEM),
                  pl.BlockSpec((tm * ROW_WORDS, LANES), lambda i, a, b, c: (i, 0))],
        out_specs=pl.BlockSpec(memory_space=pl.ANY),
        scratch_shapes=[pltpu.VMEM((MOE_BM // 2 * ROW_WORDS, LANES), U32),
                        pltpu.SemaphoreType.DMA, pltpu.SemaphoreType.DMA],
    )
    return pl.pallas_call(
        _dispatch_body,
        out_shape=jax.ShapeDtypeStruct((n_rows * ROW_WORDS, LANES), U32),
        grid_spec=grid_spec,
        compiler_params=_cparams(("arbitrary",)),
        name="dispatch",
    )(pad_row, pad_n, n_used, pos3, h4)


def _weight_copies(e, w_hbm, stage, sem, slot):
    return [pltpu.make_async_copy(w.at[e], st.at[slot], sem.at[slot, j])
            for j, (w, st) in enumerate(zip(w_hbm, stage))]


def _row_block_copy(xs_hbm, xbuf, xsem, j):
    slot = j % X_RING
    return pltpu.make_async_copy(_rows(xs_hbm, j * MOE_BM, MOE_BM, align=MOE_BM), xbuf.at[slot], xsem.at[slot])


def _experts_body(be_ref, nu_ref, nx_ref, ord_ref, xs_hbm, wg_hbm, wu_hbm, wd_hbm, y4_ref,
                  xbuf, xsem, sg, su, sd, wsem, wg_b, wu_b, wd_b):
    w_hbm = (wg_hbm, wu_hbm, wd_hbm)
    stage = (sg, su, sd)
    n_used = nu_ref[0]

    def do_block(i, out_row):
        @pl.when(i < n_used)
        def _():
            e = be_ref[i]
            prev_e = be_ref[jnp.maximum(i - 1, 0)]

            @pl.when(i == 0)
            def _():
                for j in range(X_RING - 1):
                    @pl.when(j < n_used)
                    def _():
                        _row_block_copy(xs_hbm, xbuf, xsem, j).start()

            @pl.when(i + X_RING - 1 < n_used)
            def _():
                _row_block_copy(xs_hbm, xbuf, xsem, i + X_RING - 1).start()

            @pl.when((i == 0) | (e != prev_e))
            def _():
                wslot = ord_ref[e] % 2
                n1 = nx_ref[e]
                n2 = jnp.where(n1 >= 0, nx_ref[jnp.maximum(n1, 0)], -1)

                @pl.when(i == 0)
                def _():
                    for cp in _weight_copies(e, w_hbm, stage, wsem, wslot):
                        cp.start()

                    @pl.when(n1 >= 0)
                    def _():
                        for cp in _weight_copies(n1, w_hbm, stage, wsem, 1 - wslot):
                            cp.start()
                for cp in _weight_copies(e, w_hbm, stage, wsem, wslot):
                    cp.wait()
                wg_b[...] = sg[wslot].astype(BF16)
                wu_b[...] = su[wslot].astype(BF16)
                wd_b[...] = sd[wslot].astype(BF16)

                @pl.when(n2 >= 0)
                def _():
                    for cp in _weight_copies(n2, w_hbm, stage, wsem, wslot):
                        cp.start()

            _row_block_copy(xs_hbm, xbuf, xsem, i).wait()
            slot = i % X_RING
            parts = [_unpack_row_words(xbuf[slot, _word_plane(0, MOE_BM, s), :]) for s in range(ROW_WORDS)]
            x = jnp.concatenate([p[0] for p in parts] + [p[1] for p in parts], axis=1).astype(BF16)
            g = jnp.dot(x, wg_b[...], preferred_element_type=F32)
            u = jnp.dot(x, wu_b[...], preferred_element_type=F32)
            a = (_silu(g) * u).astype(BF16)
            y = jnp.dot(a, wd_b[...], preferred_element_type=F32)
            for s, w in enumerate(_pack_row_words(y)):
                y4_ref[_word_plane(out_row, MOE_BM, s), :] = w

        @pl.when(i >= n_used)
        def _():
            y4_ref[pl.ds(out_row * ROW_WORDS, MOE_BM * ROW_WORDS), :] = jnp.zeros((MOE_BM * ROW_WORDS, LANES), U32)

    for sub in range(EXP_BLOCKS_PER_STEP):
        do_block(pl.program_id(0) * EXP_BLOCKS_PER_STEP + sub, sub * MOE_BM)


def _experts(block_e, n_used, next_e, ord_e, xs, wg, wu, wd):
    nblk = block_e.shape[0]
    bm = MOE_BM * EXP_BLOCKS_PER_STEP
    e, d, de = wg.shape

    grid_spec = pltpu.PrefetchScalarGridSpec(
        num_scalar_prefetch=4,
        grid=(nblk // EXP_BLOCKS_PER_STEP,),
        in_specs=[pl.BlockSpec(memory_space=pl.ANY),
                  pl.BlockSpec(memory_space=pl.ANY),
                  pl.BlockSpec(memory_space=pl.ANY),
                  pl.BlockSpec(memory_space=pl.ANY)],
        out_specs=pl.BlockSpec((bm * ROW_WORDS, LANES), lambda i, be, nu, nx, od: (i, 0)),
        scratch_shapes=[pltpu.VMEM((X_RING, MOE_BM * ROW_WORDS, LANES), U32), pltpu.SemaphoreType.DMA((X_RING,)),
                        pltpu.VMEM((2, d, de), F32), pltpu.VMEM((2, d, de), F32), pltpu.VMEM((2, de, d), F32),
                        pltpu.SemaphoreType.DMA((2, 3)),
                        pltpu.VMEM((d, de), BF16), pltpu.VMEM((d, de), BF16), pltpu.VMEM((de, d), BF16)],
    )
    return pl.pallas_call(
        _experts_body,
        out_shape=jax.ShapeDtypeStruct((nblk * MOE_BM * ROW_WORDS, LANES), U32),
        grid_spec=grid_spec,
        compiler_params=_cparams(("arbitrary",)),
        name="experts",
    )(block_e, n_used, next_e, ord_e, xs, wg, wu, wd)


def _combine_body(pos_cur, pos_nxt, y4_ref, h_ref, gate_ref, wsg_ref, wsu_ref, wsd_ref, g_ref, b_ref,
                  o_ref, buf, sem, racc):
    i = pl.program_id(0)
    nb = pl.num_programs(0)
    tm, d = h_ref.shape
    half = d // 2
    slot = i % 2

    def issue_rows(pos_ref, sl, t):
        for k in range(TOP_K):
            p = pos_ref[0, 0, t * TOP_K + k]
            pltpu.make_async_copy(_rows(y4_ref, p, 1), _rows(buf.at[sl], k * tm + t, 1),
                                  sem.at[sl]).start(priority=k % 2)

    @pl.when(i == 0)
    def _():
        def body(t, carry):
            issue_rows(pos_cur, 0, t)
            return carry
        lax.fori_loop(0, tm, body, 0)

    pltpu.make_async_copy(_rows(y4_ref, 0, tm * TOP_K), buf.at[slot], sem.at[slot]).wait()

    def combine_rows(t8):
        r0 = pl.multiple_of(t8 * SUBLANES, SUBLANES)
        gates = gate_ref[pl.ds(r0, SUBLANES), :]
        for s in range(ROW_WORDS):
            lo_acc = hi_acc = None
            for k in range(TOP_K):
                w = buf[slot, pl.ds((k * tm + r0) * ROW_WORDS + s, SUBLANES, stride=ROW_WORDS), :]
                lo, hi = _unpack_row_words(w)
                gk = gates[:, k:k + 1]
                lo_acc = gk * lo if lo_acc is None else lo_acc + gk * lo
                hi_acc = gk * hi if hi_acc is None else hi_acc + gk * hi
            racc[pl.ds(r0, SUBLANES), pl.ds(s * LANES, LANES)] = lo_acc
            racc[pl.ds(r0, SUBLANES), pl.ds(half + s * LANES, LANES)] = hi_acc

    @pl.when(i + 1 < nb)
    def _():
        def body(t8, carry):
            for dt in range(SUBLANES):
                issue_rows(pos_nxt, 1 - slot, t8 * SUBLANES + dt)
            combine_rows(t8)
            return carry
        lax.fori_loop(0, tm // SUBLANES, body, 0)

    @pl.when(i + 1 >= nb)
    def _():
        def body(t8, carry):
            combine_rows(t8)
            return carry
        lax.fori_loop(0, tm // SUBLANES, body, 0)

    h = h_ref[...]
    hb = h.astype(BF16)
    sg = jnp.dot(hb, wsg_ref[...], preferred_element_type=F32)
    su = jnp.dot(hb, wsu_ref[...], preferred_element_type=F32)
    shared = jnp.dot((_silu(sg) * su).astype(BF16), wsd_ref[...], preferred_element_type=F32)
    o_ref[...] = _layernorm(DEEPNORM_ALPHA * h + (racc[...] + shared), g_ref[...], b_ref[...])


def _combine(pos3, y4, h, gates, wsg, wsu, wsd, g, b):
    t, d = h.shape
    tm = COMB_TM
    nt = t // tm
    ds_ = wsg.shape[1]
    return pl.pallas_call(
        _combine_body,
        out_shape=jax.ShapeDtypeStruct((t, d), F32),
        grid=(nt,),
        in_specs=[pl.BlockSpec((1, 1, tm * TOP_K), lambda i: (i, 0, 0), memory_space=pltpu.SMEM),
                  pl.BlockSpec((1, 1, tm * TOP_K), lambda i: (jnp.minimum(i + 1, nt - 1), 0, 0),
                               memory_space=pltpu.SMEM),
                  pl.BlockSpec(memory_space=pl.ANY),
                  pl.BlockSpec((tm, d), lambda i: (i, 0)),
                  pl.BlockSpec((tm, TOP_K), lambda i: (i, 0)),
                  pl.BlockSpec((d, ds_), lambda i: (0, 0)),
                  pl.BlockSpec((d, ds_), lambda i: (0, 0)),
                  pl.BlockSpec((ds_, d), lambda i: (0, 0)),
                  pl.BlockSpec((1, d), lambda i: (0, 0)),
                  pl.BlockSpec((1, d), lambda i: (0, 0))],
        out_specs=pl.BlockSpec((tm, d), lambda i: (i, 0)),
        scratch_shapes=[pltpu.VMEM((2, tm * TOP_K * ROW_WORDS, LANES), U32),
                        pltpu.SemaphoreType.DMA((2,)),
                        pltpu.VMEM((tm, d), F32)],
        compiler_params=_cparams(("arbitrary",)),
        name="combine",
    )(pos3, pos3, y4, h, gates, wsg, wsu, wsd, g, b)


def _expert_tables(counts, nblk):
    bm = MOE_BM
    cnt = counts.reshape(N_EXPERTS).astype(I32)
    padded = (cnt + bm - 1) // bm * bm
    padded_end = jnp.cumsum(padded)
    padded_start = padded_end - padded
    block_rows = jnp.arange(nblk, dtype=I32) * bm
    block_e = jnp.sum((padded_end[None, :] <= block_rows[:, None]).astype(I32), axis=1)
    block_e = jnp.minimum(block_e, N_EXPERTS - 1)
    n_used = (padded_end[-1:] // bm).astype(I32)
    ids = jnp.where(cnt > 0, jnp.arange(N_EXPERTS, dtype=I32), N_EXPERTS)
    after = jnp.concatenate([lax.cummin(ids, reverse=True)[1:], jnp.full((1,), N_EXPERTS, I32)])
    next_e = jnp.where(after < N_EXPERTS, after, -1).astype(I32)
    ord_e = (jnp.cumsum((cnt > 0).astype(I32)) - 1).astype(I32)
    return padded_start, padded_start + cnt, padded - cnt, block_e, n_used, next_e, ord_e


def kernel(x, w_in, gla_gate_w2, gla_gate_b, gla_norm_w, pool_w_group, pool_scale, w_out, ln1_g, ln1_b,
           router_w, router_bias, w_exp_gate, w_exp_up, w_exp_down, w_sh_gate, w_sh_up, w_sh_down, ln2_g, ln2_b):
    batch, seq, d = x.shape
    t = batch * seq
    h2d = x.reshape(t, d)
    for l in range(DEPTH):
        d_in = w_in.shape[2]
        w_in_b = jnp.pad(w_in[l], ((0, 0), (0, D_IN_PAD - d_in))).astype(BF16)
        w2p = jnp.pad(gla_gate_w2[l], ((0, LANES - GLA_GATE_RANK), (0, 0))).astype(BF16)
        h, h4 = _front(h2d, batch, seq, w_in_b, w2p, gla_gate_b[l].reshape(1, -1), gla_norm_w[l].reshape(1, -1),
                       pool_w_group[l].astype(BF16), pool_scale[l].reshape(1, -1),
                       w_out[l].astype(BF16), ln1_g[l].reshape(1, -1), ln1_b[l].reshape(1, -1))
        idx_t, gate_t, rank_t, counts = _router(h, router_w[l].T, router_bias[l].reshape(-1, 1))
        nblk = (t * TOP_K + N_EXPERTS * (MOE_BM - 1)) // MOE_BM
        nblk = -(-nblk // EXP_BLOCKS_PER_STEP) * EXP_BLOCKS_PER_STEP
        start, pad_row, pad_n, block_e, n_used, next_e, ord_e = _expert_tables(counts, nblk)
        pos_t = _positions(idx_t, rank_t, start.astype(F32).reshape(-1, 1))
        pos_tok = pos_t.T
        xs = _dispatch(pad_row, pad_n, n_used, pos_tok.reshape(t // DISP_TM, 1, DISP_TM * TOP_K), h4, nblk * MOE_BM)
        y4 = _experts(block_e, n_used, next_e, ord_e, xs, w_exp_gate[l], w_exp_up[l], w_exp_down[l])
        h2d = _combine(pos_tok.reshape(t // COMB_TM, 1, COMB_TM * TOP_K), y4, h, gate_t.T,
                       w_sh_gate[l].astype(BF16), w_sh_up[l].astype(BF16), w_sh_down[l].astype(BF16),
                       ln2_g[l].reshape(1, -1), ln2_b[l].reshape(1, -1))
    return h2d.reshape(batch, seq, d)
```

```python
import jax
import jax.numpy as jnp
from jax import lax
from jax.experimental import pallas as pl
from jax.experimental.pallas import tpu as pltpu

F32 = jnp.float32
BF16 = jnp.bfloat16
I32 = jnp.int32
U32 = jnp.uint32
HIGH_HALF = 0xFFFF0000

POOL_WINDOWS = (2, 4, 8, 16)
POOL_GROUP_DIM = 128
POOL_WIDTH = 512
GLA_HEADS = 4
GLA_DK = 64
GLA_DV = 128
GLA_DK_TOTAL = 256
GLA_WIDTH = 512
GLA_GATE_RANK = 16
GLA_GATE_NORMALIZER = 16.0
GLA_CHUNK = 16
GLA_SAFE_EXP = 60.0
N_EXPERTS = 256
TOP_K = 8
N_GROUPS = 8
GROUP_SIZE = N_EXPERTS // N_GROUPS
TOPK_GROUPS = 4
ROUTED_SCALE = 2.5
DEPTH = 1
DEEPNORM_ALPHA = (2.0 * DEPTH) ** 0.25
LN_EPS = 1e-5
RMS_EPS = 1e-5

LANES = 128
SUBLANES = 8
VMEM_LIMIT = 56 * 1024 * 1024

MIX_TS = 512
ROUTE_TM = 512
MOE_BM = 256
COMB_TM = 512
POS_TM = 1024
DISP_TM = 1024
DISP_UNROLL = 4
ROW_WORDS = 4
X_RING = 6
EXP_BLOCKS_PER_STEP = 8
D_IN_PAD = 2176


def _cparams(sem):
    return pltpu.CompilerParams(dimension_semantics=sem, vmem_limit_bytes=VMEM_LIMIT)


def _silu(x):
    return x * (1.0 / (1.0 + jnp.exp(-x)))


def _layernorm(y, g, b):
    mu = jnp.mean(y, axis=-1, keepdims=True)
    yc = y - mu
    var = jnp.mean(yc * yc, axis=-1, keepdims=True)
    return yc * lax.rsqrt(var + LN_EPS) * g + b


def _mixer_body(p_ref, q_ref, k_ref, v_ref, r_ref, gl_ref, w2_ref, gb_ref, nw_ref, pw_ref, ps_ref,
                o_ref, pbuf, state, kvbuf, sall, obuf, gk_s):
    ts = p_ref.shape[0]
    s_idx = pl.program_id(1)
    halo = POOL_WINDOWS[-1]

    @pl.when(s_idx == 0)
    def _():
        pbuf[pl.ds(0, halo), :] = jnp.zeros((halo, POOL_WIDTH), F32)
        state[...] = jnp.zeros(state.shape, F32)

    p = p_ref[...]
    pbuf[pl.ds(halo, ts), :] = p
    pos = s_idx * ts + lax.broadcasted_iota(I32, (ts, 1), 0)
    for g, w in enumerate(POOL_WINDOWS):
        c0 = g * POOL_GROUP_DIM
        ext = pbuf[:, pl.ds(c0, POOL_GROUP_DIM)]
        sh = 1
        while sh < w:
            ext = ext + pltpu.roll(ext, sh, axis=0)
            sh *= 2
        acc = ext[halo:, :]
        cnt = jnp.minimum(pos + 1, w).astype(F32)
        mixed = acc / cnt - p[:, c0:c0 + POOL_GROUP_DIM]
        og = jnp.dot(mixed.astype(BF16), pw_ref[g], preferred_element_type=F32)
        o_ref[:, pl.ds(c0, POOL_GROUP_DIM)] = (og * ps_ref[:, pl.ds(c0, POOL_GROUP_DIM)]).astype(o_ref.dtype)
    pbuf[pl.ds(0, halo), :] = pbuf[pl.ds(ts, halo), :]

    nchunk = ts // GLA_CHUNK
    glog = jnp.dot(gl_ref[...].astype(BF16), w2_ref[...], preferred_element_type=F32) + gb_ref[...]
    gk = (jnp.minimum(glog, 0.0) - jnp.log(1.0 + jnp.exp(-jnp.abs(glog)))) * (1.0 / GLA_GATE_NORMALIZER)
    row = lax.broadcasted_iota(I32, (ts, 1), 0)
    rin = row % GLA_CHUNK
    b = gk
    sh = 1
    while sh < GLA_CHUNK:
        b = b + jnp.where(rin >= sh, pltpu.roll(b, sh, axis=0), 0.0)
        sh *= 2
    b3 = b.reshape(nchunk, GLA_CHUNK, GLA_DK_TOTAL)
    bmid = b3[:, GLA_CHUNK // 2 - 1:GLA_CHUNK // 2, :]
    blast = b3[:, GLA_CHUNK - 1:GLA_CHUNK, :]
    lane = lax.broadcasted_iota(I32, (1, LANES), 1)
    head_lane = [lane < GLA_DK, lane >= GLA_DK]
    q_scale = GLA_DK ** -0.5
    safe = jnp.max(jnp.abs(b3 - bmid)) <= GLA_SAFE_EXP

    @pl.when(safe)
    def _chunked():
        v = v_ref[...]
        vb = v.astype(BF16)
        q3 = (q_ref[...] * q_scale).reshape(nchunk, GLA_CHUNK, GLA_DK_TOTAL)
        k3 = k_ref[...].reshape(nchunk, GLA_CHUNK, GLA_DK_TOTAL)
        qs = (q3 * jnp.exp(b3 - bmid)).reshape(ts, GLA_DK_TOTAL)
        ks = (k3 * jnp.exp(bmid - b3)).reshape(ts, GLA_DK_TOTAL)
        qd = (q3 * jnp.exp(b3)).reshape(ts, GLA_DK_TOTAL)
        kd = (k3 * jnp.exp(blast - b3)).reshape(ts, GLA_DK_TOTAL)
        cdec = jnp.exp(blast).reshape(nchunk, GLA_DK_TOTAL)

        blk = LANES
        ri = lax.broadcasted_iota(I32, (blk, blk), 0)
        ci = lax.broadcasted_iota(I32, (blk, blk), 1)
        causal = (ri // GLA_CHUNK == ci // GLA_CHUNK) & (ri >= ci)
        o_intra = [[None] * (ts // blk) for _ in range(GLA_HEADS)]
        for rb in range(ts // blk):
            rs = slice(rb * blk, (rb + 1) * blk)
            for pair in range(GLA_HEADS // 2):
                ls = slice(pair * LANES, (pair + 1) * LANES)
                ks_p = ks[rs, ls].astype(BF16)
                q_p = qs[rs, ls]
                q2 = jnp.concatenate([jnp.where(head_lane[sub], q_p, 0.0) for sub in range(2)], axis=0).astype(BF16)
                sc2 = lax.dot_general(q2, ks_p, (((1,), (1,)), ((), ())), preferred_element_type=F32)
                for sub in range(2):
                    h = pair * 2 + sub
                    sc = jnp.where(causal, sc2[sub * blk:(sub + 1) * blk], 0.0).astype(BF16)
                    o_intra[h][rb] = jnp.dot(sc, vb[rs, h * GLA_DV:(h + 1) * GLA_DV], preferred_element_type=F32)

        cpb = blk // GLA_CHUNK
        chunk_of_col = lax.broadcasted_iota(I32, (1, blk), 1) // GLA_CHUNK
        kdb = kd.astype(BF16)
        for rb in range(ts // blk):
            rs = slice(rb * blk, (rb + 1) * blk)
            for pair in range(GLA_HEADS // 2):
                ls = slice(pair * LANES, (pair + 1) * LANES)
                lhs = []
                for sub in range(2):
                    h = pair * 2 + sub
                    v_t = v[rs, h * GLA_DV:(h + 1) * GLA_DV].T
                    lhs += [jnp.where(chunk_of_col == c, v_t, 0.0) for c in range(cpb)]
                both = jnp.dot(jnp.concatenate(lhs, axis=0).astype(BF16), kdb[rs, ls],
                               preferred_element_type=F32)
                stacked = jnp.where(head_lane[0], both[:cpb * LANES], both[cpb * LANES:])
                for c in range(cpb):
                    kvbuf[pair, rb * cpb + c] = stacked[c * LANES:(c + 1) * LANES]
        for pair in range(GLA_HEADS // 2):
            ls = slice(pair * LANES, (pair + 1) * LANES)
            st = state[pair]
            for c in range(nchunk):
                sall[pair, c] = st.astype(BF16)
                st = st * cdec[c:c + 1, ls] + kvbuf[pair, c]
            state[pair] = st
        o_inter = [[None] * nchunk for _ in range(GLA_HEADS)]
        for pair in range(GLA_HEADS // 2):
            ls = slice(pair * LANES, (pair + 1) * LANES)
            for c in range(nchunk):
                rs = slice(c * GLA_CHUNK, (c + 1) * GLA_CHUNK)
                q_c = qd[rs, ls]
                q2 = jnp.concatenate([jnp.where(head_lane[sub], q_c, 0.0) for sub in range(2)], axis=0).astype(BF16)
                res = lax.dot_general(q2, sall[pair, c], (((1,), (1,)), ((), ())), preferred_element_type=F32)
                for sub in range(2):
                    o_inter[pair * 2 + sub][c] = res[sub * GLA_CHUNK:(sub + 1) * GLA_CHUNK]
        for h in range(GLA_HEADS):
            obuf[:, pl.ds(h * GLA_DV, GLA_DV)] = (jnp.concatenate(o_intra[h], axis=0)
                                                  + jnp.concatenate(o_inter[h], axis=0))

    @pl.when(jnp.logical_not(safe))
    def _row_by_row():
        gk_s[...] = gk
        row8 = lax.broadcasted_iota(I32, (SUBLANES, 1), 0)

        def slab(i8, carry):
            r0 = pl.multiple_of(i8 * SUBLANES, SUBLANES)
            q8 = q_ref[pl.ds(r0, SUBLANES), :] * q_scale
            k8 = k_ref[pl.ds(r0, SUBLANES), :]
            v8 = v_ref[pl.ds(r0, SUBLANES), :]
            g8 = jnp.exp(gk_s[pl.ds(r0, SUBLANES), :])
            outs = [jnp.zeros((SUBLANES, GLA_DV), F32) for _ in range(GLA_HEADS)]
            for pair in range(GLA_HEADS // 2):
                ls = slice(pair * LANES, (pair + 1) * LANES)
                st = state[pair]
                for r in range(SUBLANES):
                    sel = row8 == r
                    k_r = jnp.where(sel, k8[:, ls], 0.0).astype(BF16)
                    inc = []
                    for sub in range(2):
                        h = pair * 2 + sub
                        v_r = jnp.where(sel, v8[:, h * GLA_DV:(h + 1) * GLA_DV], 0.0).astype(BF16)
                        inc.append(lax.dot_general(v_r, k_r, (((0,), (0,)), ((), ())), preferred_element_type=F32))
                    st = st * g8[r:r + 1, ls] + jnp.where(head_lane[0], inc[0], inc[1])
                    q2 = jnp.concatenate([jnp.where(sel & head_lane[sub], q8[:, ls], 0.0) for sub in range(2)],
                                         axis=0).astype(BF16)
                    res = lax.dot_general(q2, st.astype(BF16), (((1,), (1,)), ((), ())),
                                          preferred_element_type=F32)
                    for sub in range(2):
                        outs[pair * 2 + sub] = outs[pair * 2 + sub] + res[sub * SUBLANES:(sub + 1) * SUBLANES]
                state[pair] = st
            obuf[pl.ds(r0, SUBLANES), :] = jnp.concatenate(outs, axis=1)
            return carry
        lax.fori_loop(0, ts // SUBLANES, slab, 0)

    nw = nw_ref[...]
    r = r_ref[...]
    for h in range(GLA_HEADS):
        o = obuf[:, pl.ds(h * GLA_DV, GLA_DV)]
        o = o * lax.rsqrt(jnp.mean(o * o, axis=-1, keepdims=True) + RMS_EPS) * nw
        o = o * _silu(r[:, h * GLA_DV:(h + 1) * GLA_DV])
        o_ref[:, pl.ds(POOL_WIDTH + h * GLA_DV, GLA_DV)] = o.astype(o_ref.dtype)


def _front_body(x_ref, win_ref, w2_ref, gb_ref, nw_ref, pw_ref, ps_ref, wout_ref, g_ref, b_ref, h_ref, h4_ref,
                pp_s, pq_s, pk_s, pv_s, pr_s, pg_s, mix_s, pbuf, state, kvbuf, sall, obuf, gk_s):
    groups = (pp_s, pq_s, pk_s, pv_s, pr_s, pg_s)
    cols = [0]
    for ref in groups:
        cols.append(cols[-1] + ref.shape[1])
    xb = x_ref[...].astype(BF16)
    for gi in (0, 5, 1, 2, 3, 4):
        groups[gi][...] = jnp.dot(xb, win_ref[:, pl.ds(cols[gi], groups[gi].shape[1])],
                                  preferred_element_type=F32)
    _mixer_body(*groups, w2_ref, gb_ref, nw_ref, pw_ref, ps_ref, mix_s, pbuf, state, kvbuf, sall, obuf, gk_s)
    y = DEEPNORM_ALPHA * x_ref[...] + jnp.dot(mix_s[...], wout_ref[...], preferred_element_type=F32)
    h = _layernorm(y, g_ref[...], b_ref[...])
    h_ref[...] = h
    for s, w in enumerate(_pack_row_words(h)):
        h4_ref[_word_plane(0, h.shape[0], s), :] = w


def _front(x2d, batch, seq, w_in_b, w2p, gate_b, norm_w, pool_w, pool_scale, w_out_b, ln_g, ln_b):
    t, d = x2d.shape
    ts = MIX_TS
    nseq = seq // ts

    def full(shape):
        return pl.BlockSpec(shape, lambda bi, si: (0,) * len(shape))

    return pl.pallas_call(
        _front_body,
        out_shape=(jax.ShapeDtypeStruct((t, d), F32), jax.ShapeDtypeStruct((t * ROW_WORDS, LANES), U32)),
        grid=(batch, nseq),
        in_specs=[pl.BlockSpec((ts, d), lambda bi, si: (bi * nseq + si, 0)),
                  full(w_in_b.shape), full(w2p.shape), full(gate_b.shape), full(norm_w.shape),
                  full(pool_w.shape), full(pool_scale.shape), full(w_out_b.shape), full(ln_g.shape), full(ln_b.shape)],
        out_specs=(pl.BlockSpec((ts, d), lambda bi, si: (bi * nseq + si, 0)),
                   pl.BlockSpec((ts * ROW_WORDS, LANES), lambda bi, si: (bi * nseq + si, 0))),
        scratch_shapes=[pltpu.VMEM((ts, POOL_WIDTH), F32), pltpu.VMEM((ts, GLA_DK_TOTAL), F32),
                        pltpu.VMEM((ts, GLA_DK_TOTAL), F32), pltpu.VMEM((ts, GLA_WIDTH), F32),
                        pltpu.VMEM((ts, GLA_WIDTH), F32), pltpu.VMEM((ts, D_IN_PAD - 2 * GLA_DK_TOTAL
                                                                      - 2 * GLA_WIDTH - POOL_WIDTH), F32),
                        pltpu.VMEM((ts, POOL_WIDTH + GLA_WIDTH), BF16),
                        pltpu.VMEM((ts + POOL_WINDOWS[-1], POOL_WIDTH), F32),
                        pltpu.VMEM((GLA_HEADS // 2, LANES, GLA_DV), F32),
                        pltpu.VMEM((GLA_HEADS // 2, ts // GLA_CHUNK, LANES, GLA_DV), F32),
                        pltpu.VMEM((GLA_HEADS // 2, ts // GLA_CHUNK, LANES, GLA_DV), BF16),
                        pltpu.VMEM((ts, GLA_WIDTH), F32), pltpu.VMEM((ts, GLA_DK_TOTAL), F32)],
        compiler_params=_cparams(("arbitrary", "arbitrary")),
        name="front",
    )(x2d, w_in_b, w2p, gate_b, norm_w, pool_w, pool_scale, w_out_b, ln_g, ln_b)


def _pack_row_words(x):
    half = x.shape[1] // 2
    u = pltpu.bitcast(x.astype(BF16).astype(F32), U32)
    hi_mask = jnp.uint32(HIGH_HALF)
    return [(u[:, half + s * LANES:half + (s + 1) * LANES] & hi_mask) | (u[:, s * LANES:(s + 1) * LANES] >> 16)
            for s in range(ROW_WORDS)]


def _unpack_row_words(w):
    return pltpu.bitcast(w << 16, F32), pltpu.bitcast(w & jnp.uint32(HIGH_HALF), F32)


def _rows(ref, first, n, align=1):
    if isinstance(first, int):
        start = first * ROW_WORDS
    else:
        start = pl.multiple_of(first * ROW_WORDS, ROW_WORDS * align)
    return ref.at[pl.ds(start, n * ROW_WORDS), :]


def _word_plane(first, m, s):
    return pl.ds(first * ROW_WORDS + s, m, stride=ROW_WORDS)


def _first_argmax_rows(val, rowf, nrows):
    m = jnp.max(val, axis=0, keepdims=True)
    first = jnp.min(jnp.where(val == m, rowf, float(nrows)), axis=0, keepdims=True)
    return m, first, rowf == first


def _router_body(h_ref, whi_ref, wlo_ref, bias_ref, idx_ref, gate_ref, rank_ref, cnt_ref, carry):
    tm = h_ref.shape[0]
    i = pl.program_id(0)

    @pl.when(i == 0)
    def _():
        carry[...] = jnp.zeros(carry.shape, F32)

    h = h_ref[...]
    h_hi = h.astype(BF16)
    h_lo = (h - h_hi.astype(F32)).astype(BF16)
    nt = (((1,), (1,)), ((), ()))
    logits = (lax.dot_general(whi_ref[...], h_hi, nt, preferred_element_type=F32)
              + lax.dot_general(whi_ref[...], h_lo, nt, preferred_element_type=F32)
              + lax.dot_general(wlo_ref[...], h_hi, nt, preferred_element_type=F32))
    scores = 1.0 / (1.0 + jnp.exp(-logits))
    biased = scores + bias_ref[...]
    neg = -jnp.inf

    grp = biased.reshape(N_GROUPS, GROUP_SIZE, tm)
    gi = lax.broadcasted_iota(I32, (N_GROUPS, GROUP_SIZE, tm), 1).astype(F32)
    g1 = jnp.max(grp, axis=1, keepdims=True)
    f1 = jnp.min(jnp.where(grp == g1, gi, float(GROUP_SIZE)), axis=1, keepdims=True)
    g2 = jnp.max(jnp.where(gi == f1, neg, grp), axis=1, keepdims=True)
    gscore = (g1 + g2).reshape(N_GROUPS, tm)

    growf = lax.broadcasted_iota(I32, (N_GROUPS, tm), 0).astype(F32)
    gsel = jnp.zeros((N_GROUPS, tm), F32)
    gval = gscore
    for _ in range(TOPK_GROUPS):
        _, _, pick = _first_argmax_rows(gval, growf, N_GROUPS)
        gsel = jnp.where(pick, 1.0, gsel)
        gval = jnp.where(pick, neg, gval)
    emask = jnp.broadcast_to(gsel.reshape(N_GROUPS, 1, tm), (N_GROUPS, GROUP_SIZE, tm)).reshape(N_EXPERTS, tm)

    rowf = lax.broadcasted_iota(I32, (N_EXPERTS, tm), 0).astype(F32)
    val = jnp.where(emask > 0.0, biased, neg)
    onehot = jnp.zeros((N_EXPERTS, tm), F32)
    picks, idxs, ws = [], [], []
    for _ in range(TOP_K):
        _, first, pick = _first_argmax_rows(val, rowf, N_EXPERTS)
        picks.append(pick)
        idxs.append(first)
        ws.append(jnp.sum(jnp.where(pick, scores, 0.0), axis=0, keepdims=True))
        onehot = jnp.where(pick, 1.0, onehot)
        val = jnp.where(pick, neg, val)
    w = jnp.concatenate(ws, axis=0)
    gate_ref[...] = w / jnp.sum(w, axis=0, keepdims=True) * ROUTED_SCALE
    idx_ref[...] = jnp.concatenate(idxs, axis=0).astype(I32)

    ti = lax.broadcasted_iota(I32, (tm, tm), 0)
    tj = lax.broadcasted_iota(I32, (tm, tm), 1)
    upper = jnp.where(ti < tj, 1.0, 0.0).astype(BF16)
    prefix = jnp.dot(onehot.astype(BF16), upper, preferred_element_type=F32) + carry[...]
    ranks = [jnp.sum(jnp.where(pk, prefix, 0.0), axis=0, keepdims=True) for pk in picks]
    rank_ref[...] = jnp.concatenate(ranks, axis=0).astype(I32)
    carry[...] = carry[...] + jnp.sum(onehot, axis=1, keepdims=True)
    cnt_ref[...] = carry[...]


def _router(h, wt, bias_col):
    t, d = h.shape
    tm = ROUTE_TM
    wt_hi = wt.astype(BF16)
    wt_lo = (wt - wt_hi.astype(F32)).astype(BF16)
    return pl.pallas_call(
        _router_body,
        out_shape=(jax.ShapeDtypeStruct((TOP_K, t), I32), jax.ShapeDtypeStruct((TOP_K, t), F32),
                   jax.ShapeDtypeStruct((TOP_K, t), I32), jax.ShapeDtypeStruct((N_EXPERTS, 1), F32)),
        grid=(t // tm,),
        in_specs=[pl.BlockSpec((tm, d), lambda i: (i, 0)),
                  pl.BlockSpec((N_EXPERTS, d), lambda i: (0, 0)),
                  pl.BlockSpec((N_EXPERTS, d), lambda i: (0, 0)),
                  pl.BlockSpec((N_EXPERTS, 1), lambda i: (0, 0))],
        out_specs=(pl.BlockSpec((TOP_K, tm), lambda i: (0, i)),
                   pl.BlockSpec((TOP_K, tm), lambda i: (0, i)),
                   pl.BlockSpec((TOP_K, tm), lambda i: (0, i)),
                   pl.BlockSpec((N_EXPERTS, 1), lambda i: (0, 0))),
        scratch_shapes=[pltpu.VMEM((N_EXPERTS, 1), F32)],
        compiler_params=_cparams(("arbitrary",)),
        name="router",
    )(h, wt_hi, wt_lo, bias_col)


def _positions_body(idx_ref, rank_ref, start_ref, pos_ref):
    tm = idx_ref.shape[1]
    rowi = lax.broadcasted_iota(I32, (N_EXPERTS, tm), 0)
    start = start_ref[...]
    idx = idx_ref[...]
    rows = [jnp.sum(jnp.where(rowi == idx[k:k + 1, :], start, 0.0), axis=0, keepdims=True) for k in range(TOP_K)]
    pos_ref[...] = jnp.concatenate(rows, axis=0).astype(I32) + rank_ref[...]


def _positions(idx_t, rank_t, start_col):
    t = idx_t.shape[1]
    tm = POS_TM
    return pl.pallas_call(
        _positions_body,
        out_shape=jax.ShapeDtypeStruct((TOP_K, t), I32),
        grid=(t // tm,),
        in_specs=[pl.BlockSpec((TOP_K, tm), lambda i: (0, i)),
                  pl.BlockSpec((TOP_K, tm), lambda i: (0, i)),
                  pl.BlockSpec((N_EXPERTS, 1), lambda i: (0, 0))],
        out_specs=pl.BlockSpec((TOP_K, tm), lambda i: (0, i)),
        compiler_params=_cparams(("arbitrary",)),
        name="positions",
    )(idx_t, rank_t, start_col)


def _pad_fill_copy(zeros, xs_ref, sem, row, nrows):
    return pltpu.make_async_copy(_rows(zeros, 0, nrows), _rows(xs_ref, row, nrows), sem)


def _dispatch_body(pad_row_ref, pad_n_ref, nu_ref, pos_ref, h4_ref, xs_ref, zeros, sem, pad_sem):
    i = pl.program_id(0)
    tm = h4_ref.shape[0] // ROW_WORDS
    half = MOE_BM // 2
    pad_bits = [1 << j for j in range(MOE_BM.bit_length() - 1)]
    n_half_blocks = xs_ref.shape[0] // (half * ROW_WORDS)

    def pad_pass(wait):
        def go(cp):
            if wait:
                cp.wait()
            else:
                cp.start()

        def body(e, carry):
            row = pad_row_ref[e]
            n = pad_n_ref[e]
            for bit in pad_bits:
                @pl.when((n & bit) != 0)
                def _():
                    go(_pad_fill_copy(zeros, xs_ref, pad_sem, row + (n & (bit - 1)), bit))
            return carry
        lax.fori_loop(0, N_EXPERTS, body, 0)

        def tail(hb, carry):
            go(_pad_fill_copy(zeros, xs_ref, pad_sem, hb * half, half))
            return carry
        lax.fori_loop(nu_ref[0] * 2, n_half_blocks, tail, 0)

    @pl.when(i == 0)
    def _():
        zeros[...] = jnp.zeros(zeros.shape, U32)
        pad_pass(False)

    def body(tt, carry):
        for dt in range(DISP_UNROLL):
            t = tt * DISP_UNROLL + dt
            for k in range(TOP_K):
                p = pos_ref[k, 0, 0, t]
                pltpu.make_async_copy(_rows(h4_ref, t, 1), _rows(xs_ref, p, 1), sem).start(priority=k % 2)
        return carry
    lax.fori_loop(0, tm // DISP_UNROLL, body, 0)

    for k in range(TOP_K):
        pltpu.make_async_copy(h4_ref, _rows(xs_ref, 0, tm), sem).wait()

    @pl.when(i == pl.num_programs(0) - 1)
    def _():
        pad_pass(True)


def _dispatch(pad_row, pad_n, n_used, pos3, h4, n_rows):
    _, nt, _, tm = pos3.shape
    grid_spec = pltpu.PrefetchScalarGridSpec(
        num_scalar_prefetch=3,
        grid=(nt,),
        in_specs=[pl.BlockSpec((TOP_K, 1, 1, tm), lambda i, a, b, c: (0, i, 0, 0), memory_space=pltpu.SMEM),
                  pl.BlockSpec((tm * ROW_WORDS, LANES), lambda i, a, b, c: (i, 0))],
        out_specs=pl.BlockSpec(memory_space=pl.ANY),
        scratch_shapes=[pltpu.VMEM((MOE_BM // 2 * ROW_WORDS, LANES), U32),
                        pltpu.SemaphoreType.DMA, pltpu.SemaphoreType.DMA],
    )
    return pl.pallas_call(
        _dispatch_body,
        out_shape=jax.ShapeDtypeStruct((n_rows * ROW_WORDS, LANES), U32),
        grid_spec=grid_spec,
        compiler_params=_cparams(("arbitrary",)),
        name="dispatch",
    )(pad_row, pad_n, n_used, pos3, h4)


def _weight_copies(e, w_hbm, stage, sem, slot):
    return [pltpu.make_async_copy(w.at[e], st.at[slot], sem.at[slot, j])
            for j, (w, st) in enumerate(zip(w_hbm, stage))]


def _row_block_copy(xs_hbm, xbuf, xsem, j):
    slot = j % X_RING
    return pltpu.make_async_copy(_rows(xs_hbm, j * MOE_BM, MOE_BM, align=MOE_BM), xbuf.at[slot], xsem.at[slot])


def _experts_body(be_ref, nu_ref, nx_ref, ord_ref, xs_hbm, wg_hbm, wu_hbm, wd_hbm, y4_ref,
                  xbuf, xsem, sg, su, sd, wsem, wg_b, wu_b, wd_b):
    w_hbm = (wg_hbm, wu_hbm, wd_hbm)
    stage = (sg, su, sd)
    n_used = nu_ref[0]

    def do_block(i, out_row):
        @pl.when(i < n_used)
        def _():
            e = be_ref[i]
            prev_e = be_ref[jnp.maximum(i - 1, 0)]

            @pl.when(i == 0)
            def _():
                for j in range(X_RING - 1):
                    @pl.when(j < n_used)
                    def _():
                        _row_block_copy(xs_hbm, xbuf, xsem, j).start()

            @pl.when(i + X_RING - 1 < n_used)
            def _():
                _row_block_copy(xs_hbm, xbuf, xsem, i + X_RING - 1).start()

            @pl.when((i == 0) | (e != prev_e))
            def _():
                wslot = ord_ref[e] % 2
                n1 = nx_ref[e]
                n2 = jnp.where(n1 >= 0, nx_ref[jnp.maximum(n1, 0)], -1)

                @pl.when(i == 0)
                def _():
                    for cp in _weight_copies(e, w_hbm, stage, wsem, wslot):
                        cp.start()

                    @pl.when(n1 >= 0)
                    def _():
                        for cp in _weight_copies(n1, w_hbm, stage, wsem, 1 - wslot):
                            cp.start()
                for cp in _weight_copies(e, w_hbm, stage, wsem, wslot):
                    cp.wait()
                wg_b[...] = sg[wslot].astype(BF16)
                wu_b[...] = su[wslot].astype(BF16)
                wd_b[...] = sd[wslot].astype(BF16)

                @pl.when(n2 >= 0)
                def _():
                    for cp in _weight_copies(n2, w_hbm, stage, wsem, wslot):
                        cp.start()

            _row_block_copy(xs_hbm, xbuf, xsem, i).wait()
            slot = i % X_RING
            parts = [_unpack_row_words(xbuf[slot, _word_plane(0, MOE_BM, s), :]) for s in range(ROW_WORDS)]
            x = jnp.concatenate([p[0] for p in parts] + [p[1] for p in parts], axis=1).astype(BF16)
            g = jnp.dot(x, wg_b[...], preferred_element_type=F32)
            u = jnp.dot(x, wu_b[...], preferred_element_type=F32)
            a = (_silu(g) * u).astype(BF16)
            y = jnp.dot(a, wd_b[...], preferred_element_type=F32)
            for s, w in enumerate(_pack_row_words(y)):
                y4_ref[_word_plane(out_row, MOE_BM, s), :] = w

        @pl.when(i >= n_used)
        def _():
            y4_ref[pl.ds(out_row * ROW_WORDS, MOE_BM * ROW_WORDS), :] = jnp.zeros((MOE_BM * ROW_WORDS, LANES), U32)

    for sub in range(EXP_BLOCKS_PER_STEP):
        do_block(pl.program_id(0) * EXP_BLOCKS_PER_STEP + sub, sub * MOE_BM)


def _experts(block_e, n_used, next_e, ord_e, xs, wg, wu, wd):
    nblk = block_e.shape[0]
    bm = MOE_BM * EXP_BLOCKS_PER_STEP
    e, d, de = wg.shape

    grid_spec = pltpu.PrefetchScalarGridSpec(
        num_scalar_prefetch=4,
        grid=(nblk // EXP_BLOCKS_PER_STEP,),
        in_specs=[pl.BlockSpec(memory_space=pl.ANY),
                  pl.BlockSpec(memory_space=pl.ANY),
                  pl.BlockSpec(memory_space=pl.ANY),
                  pl.BlockSpec(memory_space=pl.ANY)],
        out_specs=pl.BlockSpec((bm * ROW_WORDS, LANES), lambda i, be, nu, nx, od: (i, 0)),
        scratch_shapes=[pltpu.VMEM((X_RING, MOE_BM * ROW_WORDS, LANES), U32), pltpu.SemaphoreType.DMA((X_RING,)),
                        pltpu.VMEM((2, d, de), F32), pltpu.VMEM((2, d, de), F32), pltpu.VMEM((2, de, d), F32),
                        pltpu.SemaphoreType.DMA((2, 3)),
                        pltpu.VMEM((d, de), BF16), pltpu.VMEM((d, de), BF16), pltpu.VMEM((de, d), BF16)],
    )
    return pl.pallas_call(
        _experts_body,
        out_shape=jax.ShapeDtypeStruct((nblk * MOE_BM * ROW_WORDS, LANES), U32),
        grid_spec=grid_spec,
        compiler_params=_cparams(("arbitrary",)),
        name="experts",
    )(block_e, n_used, next_e, ord_e, xs, wg, wu, wd)


def _combine_body(pos_cur, pos_nxt, y4_ref, h_ref, gate_ref, wsg_ref, wsu_ref, wsd_ref, g_ref, b_ref,
                  o_ref, buf, sem, racc):
    i = pl.program_id(0)
    nb = pl.num_programs(0)
    tm, d = h_ref.shape
    half = d // 2
    slot = i % 2

    def issue_rows(pos_ref, sl, t):
        for k in range(TOP_K):
            p = pos_ref[k, 0, 0, t]
            pltpu.make_async_copy(_rows(y4_ref, p, 1), _rows(buf.at[sl], k * tm + t, 1),
                                  sem.at[sl]).start(priority=k % 2)

    @pl.when(i == 0)
    def _():
        def body(t, carry):
            issue_rows(pos_cur, 0, t)
            return carry
        lax.fori_loop(0, tm, body, 0)

    pltpu.make_async_copy(_rows(y4_ref, 0, tm * TOP_K), buf.at[slot], sem.at[slot]).wait()

    def combine_rows(t8):
        r0 = pl.multiple_of(t8 * SUBLANES, SUBLANES)
        gates = gate_ref[pl.ds(r0, SUBLANES), :]
        for s in range(ROW_WORDS):
            lo_acc = hi_acc = None
            for k in range(TOP_K):
                w = buf[slot, pl.ds((k * tm + r0) * ROW_WORDS + s, SUBLANES, stride=ROW_WORDS), :]
                lo, hi = _unpack_row_words(w)
                gk = gates[:, k:k + 1]
                lo_acc = gk * lo if lo_acc is None else lo_acc + gk * lo
                hi_acc = gk * hi if hi_acc is None else hi_acc + gk * hi
            racc[pl.ds(r0, SUBLANES), pl.ds(s * LANES, LANES)] = lo_acc
            racc[pl.ds(r0, SUBLANES), pl.ds(half + s * LANES, LANES)] = hi_acc

    @pl.when(i + 1 < nb)
    def _():
        def body(t8, carry):
            for dt in range(SUBLANES):
                issue_rows(pos_nxt, 1 - slot, t8 * SUBLANES + dt)
            combine_rows(t8)
            return carry
        lax.fori_loop(0, tm // SUBLANES, body, 0)

    @pl.when(i + 1 >= nb)
    def _():
        def body(t8, carry):
            combine_rows(t8)
            return carry
        lax.fori_loop(0, tm // SUBLANES, body, 0)

    h = h_ref[...]
    hb = h.astype(BF16)
    sg = jnp.dot(hb, wsg_ref[...], preferred_element_type=F32)
    su = jnp.dot(hb, wsu_ref[...], preferred_element_type=F32)
    shared = jnp.dot((_silu(sg) * su).astype(BF16), wsd_ref[...], preferred_element_type=F32)
    o_ref[...] = _layernorm(DEEPNORM_ALPHA * h + (racc[...] + shared), g_ref[...], b_ref[...])


def _combine(pos3, y4, h, gates, wsg, wsu, wsd, g, b):
    t, d = h.shape
    tm = COMB_TM
    nt = t // tm
    ds_ = wsg.shape[1]
    return pl.pallas_call(
        _combine_body,
        out_shape=jax.ShapeDtypeStruct((t, d), F32),
        grid=(nt,),
        in_specs=[pl.BlockSpec((TOP_K, 1, 1, tm), lambda i: (0, i, 0, 0), memory_space=pltpu.SMEM),
                  pl.BlockSpec((TOP_K, 1, 1, tm), lambda i: (0, jnp.minimum(i + 1, nt - 1), 0, 0),
                               memory_space=pltpu.SMEM),
                  pl.BlockSpec(memory_space=pl.ANY),
                  pl.BlockSpec((tm, d), lambda i: (i, 0)),
                  pl.BlockSpec((tm, TOP_K), lambda i: (i, 0)),
                  pl.BlockSpec((d, ds_), lambda i: (0, 0)),
                  pl.BlockSpec((d, ds_), lambda i: (0, 0)),
                  pl.BlockSpec((ds_, d), lambda i: (0, 0)),
                  pl.BlockSpec((1, d), lambda i: (0, 0)),
                  pl.BlockSpec((1, d), lambda i: (0, 0))],
        out_specs=pl.BlockSpec((tm, d), lambda i: (i, 0)),
        scratch_shapes=[pltpu.VMEM((2, tm * TOP_K * ROW_WORDS, LANES), U32),
                        pltpu.SemaphoreType.DMA((2,)),
                        pltpu.VMEM((tm, d), F32)],
        compiler_params=_cparams(("arbitrary",)),
        name="combine",
    )(pos3, pos3, y4, h, gates, wsg, wsu, wsd, g, b)


def _expert_tables(counts, nblk):
    bm = MOE_BM
    cnt = counts.reshape(N_EXPERTS).astype(I32)
    padded = (cnt + bm - 1) // bm * bm
    padded_end = jnp.cumsum(padded)
    padded_start = padded_end - padded
    block_rows = jnp.arange(nblk, dtype=I32) * bm
    block_e = jnp.sum((padded_end[None, :] <= block_rows[:, None]).astype(I32), axis=1)
    block_e = jnp.minimum(block_e, N_EXPERTS - 1)
    n_used = (padded_end[-1:] // bm).astype(I32)
    ids = jnp.where(cnt > 0, jnp.arange(N_EXPERTS, dtype=I32), N_EXPERTS)
    after = jnp.concatenate([lax.cummin(ids, reverse=True)[1:], jnp.full((1,), N_EXPERTS, I32)])
    next_e = jnp.where(after < N_EXPERTS, after, -1).astype(I32)
    ord_e = (jnp.cumsum((cnt > 0).astype(I32)) - 1).astype(I32)
    return padded_start, padded_start + cnt, padded - cnt, block_e, n_used, next_e, ord_e


def kernel(x, w_in, gla_gate_w2, gla_gate_b, gla_norm_w, pool_w_group, pool_scale, w_out, ln1_g, ln1_b,
           router_w, router_bias, w_exp_gate, w_exp_up, w_exp_down, w_sh_gate, w_sh_up, w_sh_down, ln2_g, ln2_b):
    batch, seq, d = x.shape
    t = batch * seq
    h2d = x.reshape(t, d)
    for l in range(DEPTH):
        d_in = w_in.shape[2]
        w_in_b = jnp.pad(w_in[l], ((0, 0), (0, D_IN_PAD - d_in))).astype(BF16)
        w2p = jnp.pad(gla_gate_w2[l], ((0, LANES - GLA_GATE_RANK), (0, 0))).astype(BF16)
        h, h4 = _front(h2d, batch, seq, w_in_b, w2p, gla_gate_b[l].reshape(1, -1), gla_norm_w[l].reshape(1, -1),
                       pool_w_group[l].astype(BF16), pool_scale[l].reshape(1, -1),
                       w_out[l].astype(BF16), ln1_g[l].reshape(1, -1), ln1_b[l].reshape(1, -1))
        idx_t, gate_t, rank_t, counts = _router(h, router_w[l].T, router_bias[l].reshape(-1, 1))
        nblk = (t * TOP_K + N_EXPERTS * (MOE_BM - 1)) // MOE_BM
        nblk = -(-nblk // EXP_BLOCKS_PER_STEP) * EXP_BLOCKS_PER_STEP
        start, pad_row, pad_n, block_e, n_used, next_e, ord_e = _expert_tables(counts, nblk)
        pos_t = _positions(idx_t, rank_t, start.astype(F32).reshape(-1, 1))
        xs = _dispatch(pad_row, pad_n, n_used, pos_t.reshape(TOP_K, t // DISP_TM, 1, DISP_TM), h4, nblk * MOE_BM)
        y4 = _experts(block_e, n_used, next_e, ord_e, xs, w_exp_gate[l], w_exp_up[l], w_exp_down[l])
        h2d = _combine(pos_t.reshape(TOP_K, t // COMB_TM, 1, COMB_TM), y4, h, gate_t.T,
                       w_sh_gate[l].astype(BF16), w_sh_up[l].astype(BF16), w_sh_down[l].astype(BF16),
                       ln2_g[l].reshape(1, -1), ln2_b[l].reshape(1, -1))
    return h2d.reshape(batch, seq, d)
```

```python
import jax
import jax.numpy as jnp
from jax import lax
from jax.experimental import pallas as pl
from jax.experimental.pallas import tpu as pltpu

F32 = jnp.float32
BF16 = jnp.bfloat16
I32 = jnp.int32
U32 = jnp.uint32
HIGH_HALF = 0xFFFF0000

POOL_WINDOWS = (2, 4, 8, 16)
POOL_GROUP_DIM = 128
POOL_WIDTH = 512
GLA_HEADS = 4
GLA_DK = 64
GLA_DV = 128
GLA_DK_TOTAL = 256
GLA_WIDTH = 512
GLA_GATE_RANK = 16
GLA_GATE_NORMALIZER = 16.0
GLA_CHUNK = 16
GLA_SAFE_EXP = 60.0
N_EXPERTS = 256
TOP_K = 8
N_GROUPS = 8
GROUP_SIZE = N_EXPERTS // N_GROUPS
TOPK_GROUPS = 4
ROUTED_SCALE = 2.5
DEPTH = 1
DEEPNORM_ALPHA = (2.0 * DEPTH) ** 0.25
LN_EPS = 1e-5
RMS_EPS = 1e-5

LANES = 128
SUBLANES = 8
VMEM_LIMIT = 56 * 1024 * 1024

MIX_TS = 512
ROUTE_TM = 512
MOE_BM = 128
COMB_TM = 512
POS_TM = 1024
DISP_TM = 1024
DISP_UNROLL = 4
ROW_WORDS = 4
X_RING = 8
EXP_BLOCKS_PER_STEP = 16
D_IN_PAD = 2176


def _cparams(sem):
    return pltpu.CompilerParams(dimension_semantics=sem, vmem_limit_bytes=VMEM_LIMIT)


def _silu(x):
    return x * (1.0 / (1.0 + jnp.exp(-x)))


def _layernorm(y, g, b):
    mu = jnp.mean(y, axis=-1, keepdims=True)
    yc = y - mu
    var = jnp.mean(yc * yc, axis=-1, keepdims=True)
    return yc * lax.rsqrt(var + LN_EPS) * g + b


def _mixer_body(p_ref, q_ref, k_ref, v_ref, r_ref, gl_ref, w2_ref, gb_ref, nw_ref, pw_ref, ps_ref,
                o_ref, pbuf, state, kvbuf, sall, obuf, gk_s):
    ts = p_ref.shape[0]
    s_idx = pl.program_id(1)
    halo = POOL_WINDOWS[-1]

    @pl.when(s_idx == 0)
    def _():
        pbuf[pl.ds(0, halo), :] = jnp.zeros((halo, POOL_WIDTH), F32)
        state[...] = jnp.zeros(state.shape, F32)

    p = p_ref[...]
    pbuf[pl.ds(halo, ts), :] = p
    pos = s_idx * ts + lax.broadcasted_iota(I32, (ts, 1), 0)
    for g, w in enumerate(POOL_WINDOWS):
        c0 = g * POOL_GROUP_DIM
        ext = pbuf[:, pl.ds(c0, POOL_GROUP_DIM)]
        sh = 1
        while sh < w:
            ext = ext + pltpu.roll(ext, sh, axis=0)
            sh *= 2
        acc = ext[halo:, :]
        cnt = jnp.minimum(pos + 1, w).astype(F32)
        mixed = acc / cnt - p[:, c0:c0 + POOL_GROUP_DIM]
        og = jnp.dot(mixed.astype(BF16), pw_ref[g], preferred_element_type=F32)
        o_ref[:, pl.ds(c0, POOL_GROUP_DIM)] = (og * ps_ref[:, pl.ds(c0, POOL_GROUP_DIM)]).astype(o_ref.dtype)
    pbuf[pl.ds(0, halo), :] = pbuf[pl.ds(ts, halo), :]

    nchunk = ts // GLA_CHUNK
    glog = jnp.dot(gl_ref[...].astype(BF16), w2_ref[...], preferred_element_type=F32) + gb_ref[...]
    gk = (jnp.minimum(glog, 0.0) - jnp.log(1.0 + jnp.exp(-jnp.abs(glog)))) * (1.0 / GLA_GATE_NORMALIZER)
    row = lax.broadcasted_iota(I32, (ts, 1), 0)
    rin = row % GLA_CHUNK
    b = gk
    sh = 1
    while sh < GLA_CHUNK:
        b = b + jnp.where(rin >= sh, pltpu.roll(b, sh, axis=0), 0.0)
        sh *= 2
    b3 = b.reshape(nchunk, GLA_CHUNK, GLA_DK_TOTAL)
    bmid = b3[:, GLA_CHUNK // 2 - 1:GLA_CHUNK // 2, :]
    blast = b3[:, GLA_CHUNK - 1:GLA_CHUNK, :]
    lane = lax.broadcasted_iota(I32, (1, LANES), 1)
    head_lane = [lane < GLA_DK, lane >= GLA_DK]
    q_scale = GLA_DK ** -0.5
    safe = jnp.max(jnp.abs(b3 - bmid)) <= GLA_SAFE_EXP

    @pl.when(safe)
    def _chunked():
        v = v_ref[...]
        vb = v.astype(BF16)
        q3 = (q_ref[...] * q_scale).reshape(nchunk, GLA_CHUNK, GLA_DK_TOTAL)
        k3 = k_ref[...].reshape(nchunk, GLA_CHUNK, GLA_DK_TOTAL)
        qs = (q3 * jnp.exp(b3 - bmid)).reshape(ts, GLA_DK_TOTAL)
        ks = (k3 * jnp.exp(bmid - b3)).reshape(ts, GLA_DK_TOTAL)
        qd = (q3 * jnp.exp(b3)).reshape(ts, GLA_DK_TOTAL)
        kd = (k3 * jnp.exp(blast - b3)).reshape(ts, GLA_DK_TOTAL)
        cdec = jnp.exp(blast).reshape(nchunk, GLA_DK_TOTAL)

        blk = LANES
        ri = lax.broadcasted_iota(I32, (blk, blk), 0)
        ci = lax.broadcasted_iota(I32, (blk, blk), 1)
        causal = (ri // GLA_CHUNK == ci // GLA_CHUNK) & (ri >= ci)
        o_intra = [[None] * (ts // blk) for _ in range(GLA_HEADS)]
        for rb in range(ts // blk):
            rs = slice(rb * blk, (rb + 1) * blk)
            for pair in range(GLA_HEADS // 2):
                ls = slice(pair * LANES, (pair + 1) * LANES)
                ks_p = ks[rs, ls].astype(BF16)
                q_p = qs[rs, ls]
                q2 = jnp.concatenate([jnp.where(head_lane[sub], q_p, 0.0) for sub in range(2)], axis=0).astype(BF16)
                sc2 = lax.dot_general(q2, ks_p, (((1,), (1,)), ((), ())), preferred_element_type=F32)
                for sub in range(2):
                    h = pair * 2 + sub
                    sc = jnp.where(causal, sc2[sub * blk:(sub + 1) * blk], 0.0).astype(BF16)
                    o_intra[h][rb] = jnp.dot(sc, vb[rs, h * GLA_DV:(h + 1) * GLA_DV], preferred_element_type=F32)

        cpb = blk // GLA_CHUNK
        chunk_of_col = lax.broadcasted_iota(I32, (1, blk), 1) // GLA_CHUNK
        kdb = kd.astype(BF16)
        for rb in range(ts // blk):
            rs = slice(rb * blk, (rb + 1) * blk)
            for pair in range(GLA_HEADS // 2):
                ls = slice(pair * LANES, (pair + 1) * LANES)
                lhs = []
                for sub in range(2):
                    h = pair * 2 + sub
                    v_t = v[rs, h * GLA_DV:(h + 1) * GLA_DV].T
                    lhs += [jnp.where(chunk_of_col == c, v_t, 0.0) for c in range(cpb)]
                both = jnp.dot(jnp.concatenate(lhs, axis=0).astype(BF16), kdb[rs, ls],
                               preferred_element_type=F32)
                stacked = jnp.where(head_lane[0], both[:cpb * LANES], both[cpb * LANES:])
                for c in range(cpb):
                    kvbuf[pair, rb * cpb + c] = stacked[c * LANES:(c + 1) * LANES]
        for pair in range(GLA_HEADS // 2):
            ls = slice(pair * LANES, (pair + 1) * LANES)
            st = state[pair]
            for c in range(nchunk):
                sall[pair, c] = st.astype(BF16)
                st = st * cdec[c:c + 1, ls] + kvbuf[pair, c]
            state[pair] = st
        o_inter = [[None] * nchunk for _ in range(GLA_HEADS)]
        for pair in range(GLA_HEADS // 2):
            ls = slice(pair * LANES, (pair + 1) * LANES)
            for c in range(nchunk):
                rs = slice(c * GLA_CHUNK, (c + 1) * GLA_CHUNK)
                q_c = qd[rs, ls]
                q2 = jnp.concatenate([jnp.where(head_lane[sub], q_c, 0.0) for sub in range(2)], axis=0).astype(BF16)
                res = lax.dot_general(q2, sall[pair, c], (((1,), (1,)), ((), ())), preferred_element_type=F32)
                for sub in range(2):
                    o_inter[pair * 2 + sub][c] = res[sub * GLA_CHUNK:(sub + 1) * GLA_CHUNK]
        for h in range(GLA_HEADS):
            obuf[:, pl.ds(h * GLA_DV, GLA_DV)] = (jnp.concatenate(o_intra[h], axis=0)
                                                  + jnp.concatenate(o_inter[h], axis=0))

    @pl.when(jnp.logical_not(safe))
    def _row_by_row():
        gk_s[...] = gk
        row8 = lax.broadcasted_iota(I32, (SUBLANES, 1), 0)

        def slab(i8, carry):
            r0 = pl.multiple_of(i8 * SUBLANES, SUBLANES)
            q8 = q_ref[pl.ds(r0, SUBLANES), :] * q_scale
            k8 = k_ref[pl.ds(r0, SUBLANES), :]
            v8 = v_ref[pl.ds(r0, SUBLANES), :]
            g8 = jnp.exp(gk_s[pl.ds(r0, SUBLANES), :])
            outs = [jnp.zeros((SUBLANES, GLA_DV), F32) for _ in range(GLA_HEADS)]
            for pair in range(GLA_HEADS // 2):
                ls = slice(pair * LANES, (pair + 1) * LANES)
                st = state[pair]
                for r in range(SUBLANES):
                    sel = row8 == r
                    k_r = jnp.where(sel, k8[:, ls], 0.0).astype(BF16)
                    inc = []
                    for sub in range(2):
                        h = pair * 2 + sub
                        v_r = jnp.where(sel, v8[:, h * GLA_DV:(h + 1) * GLA_DV], 0.0).astype(BF16)
                        inc.append(lax.dot_general(v_r, k_r, (((0,), (0,)), ((), ())), preferred_element_type=F32))
                    st = st * g8[r:r + 1, ls] + jnp.where(head_lane[0], inc[0], inc[1])
                    q2 = jnp.concatenate([jnp.where(sel & head_lane[sub], q8[:, ls], 0.0) for sub in range(2)],
                                         axis=0).astype(BF16)
                    res = lax.dot_general(q2, st.astype(BF16), (((1,), (1,)), ((), ())),
                                          preferred_element_type=F32)
                    for sub in range(2):
                        outs[pair * 2 + sub] = outs[pair * 2 + sub] + res[sub * SUBLANES:(sub + 1) * SUBLANES]
                state[pair] = st
            obuf[pl.ds(r0, SUBLANES), :] = jnp.concatenate(outs, axis=1)
            return carry
        lax.fori_loop(0, ts // SUBLANES, slab, 0)

    nw = nw_ref[...]
    r = r_ref[...]
    for h in range(GLA_HEADS):
        o = obuf[:, pl.ds(h * GLA_DV, GLA_DV)]
        o = o * lax.rsqrt(jnp.mean(o * o, axis=-1, keepdims=True) + RMS_EPS) * nw
        o = o * _silu(r[:, h * GLA_DV:(h + 1) * GLA_DV])
        o_ref[:, pl.ds(POOL_WIDTH + h * GLA_DV, GLA_DV)] = o.astype(o_ref.dtype)


def _front_body(x_ref, win_ref, w2_ref, gb_ref, nw_ref, pw_ref, ps_ref, wout_ref, g_ref, b_ref, h_ref, h4_ref,
                pp_s, pq_s, pk_s, pv_s, pr_s, pg_s, mix_s, pbuf, state, kvbuf, sall, obuf, gk_s):
    groups = (pp_s, pq_s, pk_s, pv_s, pr_s, pg_s)
    cols = [0]
    for ref in groups:
        cols.append(cols[-1] + ref.shape[1])
    xb = x_ref[...].astype(BF16)
    for gi in (0, 5, 1, 2, 3, 4):
        groups[gi][...] = jnp.dot(xb, win_ref[:, pl.ds(cols[gi], groups[gi].shape[1])],
                                  preferred_element_type=F32)
    _mixer_body(*groups, w2_ref, gb_ref, nw_ref, pw_ref, ps_ref, mix_s, pbuf, state, kvbuf, sall, obuf, gk_s)
    y = DEEPNORM_ALPHA * x_ref[...] + jnp.dot(mix_s[...], wout_ref[...], preferred_element_type=F32)
    h = _layernorm(y, g_ref[...], b_ref[...])
    h_ref[...] = h
    for s, w in enumerate(_pack_row_words(h)):
        h4_ref[_word_plane(0, h.shape[0], s), :] = w


def _front(x2d, batch, seq, w_in_b, w2p, gate_b, norm_w, pool_w, pool_scale, w_out_b, ln_g, ln_b):
    t, d = x2d.shape
    ts = MIX_TS
    nseq = seq // ts

    def full(shape):
        return pl.BlockSpec(shape, lambda bi, si: (0,) * len(shape))

    return pl.pallas_call(
        _front_body,
        out_shape=(jax.ShapeDtypeStruct((t, d), F32), jax.ShapeDtypeStruct((t * ROW_WORDS, LANES), U32)),
        grid=(batch, nseq),
        in_specs=[pl.BlockSpec((ts, d), lambda bi, si: (bi * nseq + si, 0)),
                  full(w_in_b.shape), full(w2p.shape), full(gate_b.shape), full(norm_w.shape),
                  full(pool_w.shape), full(pool_scale.shape), full(w_out_b.shape), full(ln_g.shape), full(ln_b.shape)],
        out_specs=(pl.BlockSpec((ts, d), lambda bi, si: (bi * nseq + si, 0)),
                   pl.BlockSpec((ts * ROW_WORDS, LANES), lambda bi, si: (bi * nseq + si, 0))),
        scratch_shapes=[pltpu.VMEM((ts, POOL_WIDTH), F32), pltpu.VMEM((ts, GLA_DK_TOTAL), F32),
                        pltpu.VMEM((ts, GLA_DK_TOTAL), F32), pltpu.VMEM((ts, GLA_WIDTH), F32),
                        pltpu.VMEM((ts, GLA_WIDTH), F32), pltpu.VMEM((ts, D_IN_PAD - 2 * GLA_DK_TOTAL
                                                                      - 2 * GLA_WIDTH - POOL_WIDTH), F32),
                        pltpu.VMEM((ts, POOL_WIDTH + GLA_WIDTH), BF16),
                        pltpu.VMEM((ts + POOL_WINDOWS[-1], POOL_WIDTH), F32),
                        pltpu.VMEM((GLA_HEADS // 2, LANES, GLA_DV), F32),
                        pltpu.VMEM((GLA_HEADS // 2, ts // GLA_CHUNK, LANES, GLA_DV), F32),
                        pltpu.VMEM((GLA_HEADS // 2, ts // GLA_CHUNK, LANES, GLA_DV), BF16),
                        pltpu.VMEM((ts, GLA_WIDTH), F32), pltpu.VMEM((ts, GLA_DK_TOTAL), F32)],
        compiler_params=_cparams(("arbitrary", "arbitrary")),
        name="front",
    )(x2d, w_in_b, w2p, gate_b, norm_w, pool_w, pool_scale, w_out_b, ln_g, ln_b)


def _pack_row_words(x):
    half = x.shape[1] // 2
    u = pltpu.bitcast(x.astype(BF16).astype(F32), U32)
    hi_mask = jnp.uint32(HIGH_HALF)
    return [(u[:, half + s * LANES:half + (s + 1) * LANES] & hi_mask) | (u[:, s * LANES:(s + 1) * LANES] >> 16)
            for s in range(ROW_WORDS)]


def _unpack_row_words(w):
    return pltpu.bitcast(w << 16, F32), pltpu.bitcast(w & jnp.uint32(HIGH_HALF), F32)


def _rows(ref, first, n, align=1):
    if isinstance(first, int):
        start = first * ROW_WORDS
    else:
        start = pl.multiple_of(first * ROW_WORDS, ROW_WORDS * align)
    return ref.at[pl.ds(start, n * ROW_WORDS), :]


def _word_plane(first, m, s):
    return pl.ds(first * ROW_WORDS + s, m, stride=ROW_WORDS)


def _first_argmax_rows(val, rowf, nrows):
    m = jnp.max(val, axis=0, keepdims=True)
    first = jnp.min(jnp.where(val == m, rowf, float(nrows)), axis=0, keepdims=True)
    return m, first, rowf == first


def _router_body(h_ref, whi_ref, wlo_ref, bias_ref, idx_ref, gate_ref, rank_ref, cnt_ref, carry):
    tm = h_ref.shape[0]
    i = pl.program_id(0)

    @pl.when(i == 0)
    def _():
        carry[...] = jnp.zeros(carry.shape, F32)

    h = h_ref[...]
    h_hi = h.astype(BF16)
    h_lo = (h - h_hi.astype(F32)).astype(BF16)
    nt = (((1,), (1,)), ((), ()))
    logits = (lax.dot_general(whi_ref[...], h_hi, nt, preferred_element_type=F32)
              + lax.dot_general(whi_ref[...], h_lo, nt, preferred_element_type=F32)
              + lax.dot_general(wlo_ref[...], h_hi, nt, preferred_element_type=F32))
    scores = 1.0 / (1.0 + jnp.exp(-logits))
    biased = scores + bias_ref[...]
    neg = -jnp.inf

    grp = biased.reshape(N_GROUPS, GROUP_SIZE, tm)
    gi = lax.broadcasted_iota(I32, (N_GROUPS, GROUP_SIZE, tm), 1).astype(F32)
    g1 = jnp.max(grp, axis=1, keepdims=True)
    f1 = jnp.min(jnp.where(grp == g1, gi, float(GROUP_SIZE)), axis=1, keepdims=True)
    g2 = jnp.max(jnp.where(gi == f1, neg, grp), axis=1, keepdims=True)
    gscore = (g1 + g2).reshape(N_GROUPS, tm)

    growf = lax.broadcasted_iota(I32, (N_GROUPS, tm), 0).astype(F32)
    gsel = jnp.zeros((N_GROUPS, tm), F32)
    gval = gscore
    for _ in range(TOPK_GROUPS):
        _, _, pick = _first_argmax_rows(gval, growf, N_GROUPS)
        gsel = jnp.where(pick, 1.0, gsel)
        gval = jnp.where(pick, neg, gval)
    emask = jnp.broadcast_to(gsel.reshape(N_GROUPS, 1, tm), (N_GROUPS, GROUP_SIZE, tm)).reshape(N_EXPERTS, tm)

    rowf = lax.broadcasted_iota(I32, (N_EXPERTS, tm), 0).astype(F32)
    val = jnp.where(emask > 0.0, biased, neg)
    onehot = jnp.zeros((N_EXPERTS, tm), F32)
    picks, idxs, ws = [], [], []
    for _ in range(TOP_K):
        _, first, pick = _first_argmax_rows(val, rowf, N_EXPERTS)
        picks.append(pick)
        idxs.append(first)
        ws.append(jnp.sum(jnp.where(pick, scores, 0.0), axis=0, keepdims=True))
        onehot = jnp.where(pick, 1.0, onehot)
        val = jnp.where(pick, neg, val)
    w = jnp.concatenate(ws, axis=0)
    gate_ref[...] = w / jnp.sum(w, axis=0, keepdims=True) * ROUTED_SCALE
    idx_ref[...] = jnp.concatenate(idxs, axis=0).astype(I32)

    ti = lax.broadcasted_iota(I32, (tm, tm), 0)
    tj = lax.broadcasted_iota(I32, (tm, tm), 1)
    upper = jnp.where(ti < tj, 1.0, 0.0).astype(BF16)
    prefix = jnp.dot(onehot.astype(BF16), upper, preferred_element_type=F32) + carry[...]
    ranks = [jnp.sum(jnp.where(pk, prefix, 0.0), axis=0, keepdims=True) for pk in picks]
    rank_ref[...] = jnp.concatenate(ranks, axis=0).astype(I32)
    carry[...] = carry[...] + jnp.sum(onehot, axis=1, keepdims=True)
    cnt_ref[...] = carry[...]


def _router(h, wt, bias_col):
    t, d = h.shape
    tm = ROUTE_TM
    wt_hi = wt.astype(BF16)
    wt_lo = (wt - wt_hi.astype(F32)).astype(BF16)
    return pl.pallas_call(
        _router_body,
        out_shape=(jax.ShapeDtypeStruct((TOP_K, t), I32), jax.ShapeDtypeStruct((TOP_K, t), F32),
                   jax.ShapeDtypeStruct((TOP_K, t), I32), jax.ShapeDtypeStruct((N_EXPERTS, 1), F32)),
        grid=(t // tm,),
        in_specs=[pl.BlockSpec((tm, d), lambda i: (i, 0)),
                  pl.BlockSpec((N_EXPERTS, d), lambda i: (0, 0)),
                  pl.BlockSpec((N_EXPERTS, d), lambda i: (0, 0)),
                  pl.BlockSpec((N_EXPERTS, 1), lambda i: (0, 0))],
        out_specs=(pl.BlockSpec((TOP_K, tm), lambda i: (0, i)),
                   pl.BlockSpec((TOP_K, tm), lambda i: (0, i)),
                   pl.BlockSpec((TOP_K, tm), lambda i: (0, i)),
                   pl.BlockSpec((N_EXPERTS, 1), lambda i: (0, 0))),
        scratch_shapes=[pltpu.VMEM((N_EXPERTS, 1), F32)],
        compiler_params=_cparams(("arbitrary",)),
        name="router",
    )(h, wt_hi, wt_lo, bias_col)


def _positions_body(idx_ref, rank_ref, start_ref, pos_ref):
    tm = idx_ref.shape[1]
    rowi = lax.broadcasted_iota(I32, (N_EXPERTS, tm), 0)
    start = start_ref[...]
    idx = idx_ref[...]
    rows = [jnp.sum(jnp.where(rowi == idx[k:k + 1, :], start, 0.0), axis=0, keepdims=True) for k in range(TOP_K)]
    pos_ref[...] = jnp.concatenate(rows, axis=0).astype(I32) + rank_ref[...]


def _positions(idx_t, rank_t, start_col):
    t = idx_t.shape[1]
    tm = POS_TM
    return pl.pallas_call(
        _positions_body,
        out_shape=jax.ShapeDtypeStruct((TOP_K, t), I32),
        grid=(t // tm,),
        in_specs=[pl.BlockSpec((TOP_K, tm), lambda i: (0, i)),
                  pl.BlockSpec((TOP_K, tm), lambda i: (0, i)),
                  pl.BlockSpec((N_EXPERTS, 1), lambda i: (0, 0))],
        out_specs=pl.BlockSpec((TOP_K, tm), lambda i: (0, i)),
        compiler_params=_cparams(("arbitrary",)),
        name="positions",
    )(idx_t, rank_t, start_col)


def _pad_fill_copy(zeros, xs_ref, sem, row, nrows):
    return pltpu.make_async_copy(_rows(zeros, 0, nrows), _rows(xs_ref, row, nrows), sem)


def _dispatch_body(pad_row_ref, pad_n_ref, nu_ref, pos_ref, h4_ref, xs_ref, zeros, sem, pad_sem):
    i = pl.program_id(0)
    tm = h4_ref.shape[0] // ROW_WORDS
    half = MOE_BM // 2
    pad_bits = [1 << j for j in range(MOE_BM.bit_length() - 1)]
    n_half_blocks = xs_ref.shape[0] // (half * ROW_WORDS)

    def pad_pass(wait):
        def go(cp):
            if wait:
                cp.wait()
            else:
                cp.start()

        def body(e, carry):
            row = pad_row_ref[e]
            n = pad_n_ref[e]
            for bit in pad_bits:
                @pl.when((n & bit) != 0)
                def _():
                    go(_pad_fill_copy(zeros, xs_ref, pad_sem, row + (n & (bit - 1)), bit))
            return carry
        lax.fori_loop(0, N_EXPERTS, body, 0)

        def tail(hb, carry):
            go(_pad_fill_copy(zeros, xs_ref, pad_sem, hb * half, half))
            return carry
        lax.fori_loop(nu_ref[0] * 2, n_half_blocks, tail, 0)

    @pl.when(i == 0)
    def _():
        zeros[...] = jnp.zeros(zeros.shape, U32)
        pad_pass(False)

    def body(tt, carry):
        for dt in range(DISP_UNROLL):
            t = tt * DISP_UNROLL + dt
            for k in range(TOP_K):
                p = pos_ref[0, 0, t * TOP_K + k]
                pltpu.make_async_copy(_rows(h4_ref, t, 1), _rows(xs_ref, p, 1), sem).start(priority=k % 2)
        return carry
    lax.fori_loop(0, tm // DISP_UNROLL, body, 0)

    for k in range(TOP_K):
        pltpu.make_async_copy(h4_ref, _rows(xs_ref, 0, tm), sem).wait()

    @pl.when(i == pl.num_programs(0) - 1)
    def _():
        pad_pass(True)


def _dispatch(pad_row, pad_n, n_used, pos3, h4, n_rows):
    nt, _, per_step = pos3.shape
    tm = per_step // TOP_K
    grid_spec = pltpu.PrefetchScalarGridSpec(
        num_scalar_prefetch=3,
        grid=(nt,),
        in_specs=[pl.BlockSpec((1, 1, per_step), lambda i, a, b, c: (i, 0, 0), memory_space=pltpu.SMEM),
                  pl.BlockSpec((tm * ROW_WORDS, LANES), lambda i, a, b, c: (i, 0))],
        out_specs=pl.BlockSpec(memory_space=pl.ANY),
        scratch_shapes=[pltpu.VMEM((MOE_BM // 2 * ROW_WORDS, LANES), U32),
                        pltpu.SemaphoreType.DMA, pltpu.SemaphoreType.DMA],
    )
    return pl.pallas_call(
        _dispatch_body,
        out_shape=jax.ShapeDtypeStruct((n_rows * ROW_WORDS, LANES), U32),
        grid_spec=grid_spec,
        compiler_params=_cparams(("arbitrary",)),
        name="dispatch",
    )(pad_row, pad_n, n_used, pos3, h4)


def _weight_copies(e, w_hbm, stage, sem, slot):
    return [pltpu.make_async_copy(w.at[e], st.at[slot], sem.at[slot, j])
            for j, (w, st) in enumerate(zip(w_hbm, stage))]


def _row_block_copy(xs_hbm, xbuf, xsem, j):
    slot = j % X_RING
    return pltpu.make_async_copy(_rows(xs_hbm, j * MOE_BM, MOE_BM, align=MOE_BM), xbuf.at[slot], xsem.at[slot])


def _experts_body(be_ref, nu_ref, nx_ref, ord_ref, xs_hbm, wg_hbm, wu_hbm, wd_hbm, y4_ref,
                  xbuf, xsem, sg, su, sd, wsem, wg_b, wu_b, wd_b):
    w_hbm = (wg_hbm, wu_hbm, wd_hbm)
    stage = (sg, su, sd)
    n_used = nu_ref[0]

    def do_block(i, out_row):
        @pl.when(i < n_used)
        def _():
            e = be_ref[i]
            prev_e = be_ref[jnp.maximum(i - 1, 0)]

            @pl.when(i == 0)
            def _():
                for j in range(X_RING - 1):
                    @pl.when(j < n_used)
                    def _():
                        _row_block_copy(xs_hbm, xbuf, xsem, j).start()

            @pl.when(i + X_RING - 1 < n_used)
            def _():
                _row_block_copy(xs_hbm, xbuf, xsem, i + X_RING - 1).start()

            @pl.when((i == 0) | (e != prev_e))
            def _():
                wslot = ord_ref[e] % 2
                n1 = nx_ref[e]
                n2 = jnp.where(n1 >= 0, nx_ref[jnp.maximum(n1, 0)], -1)

                @pl.when(i == 0)
                def _():
                    for cp in _weight_copies(e, w_hbm, stage, wsem, wslot):
                        cp.start()

                    @pl.when(n1 >= 0)
                    def _():
                        for cp in _weight_copies(n1, w_hbm, stage, wsem, 1 - wslot):
                            cp.start()
                for cp in _weight_copies(e, w_hbm, stage, wsem, wslot):
                    cp.wait()
                wg_b[...] = sg[wslot].astype(BF16)
                wu_b[...] = su[wslot].astype(BF16)
                wd_b[...] = sd[wslot].astype(BF16)

                @pl.when(n2 >= 0)
                def _():
                    for cp in _weight_copies(n2, w_hbm, stage, wsem, wslot):
                        cp.start()

            _row_block_copy(xs_hbm, xbuf, xsem, i).wait()
            slot = i % X_RING
            parts = [_unpack_row_words(xbuf[slot, _word_plane(0, MOE_BM, s), :]) for s in range(ROW_WORDS)]
            x = jnp.concatenate([p[0] for p in parts] + [p[1] for p in parts], axis=1).astype(BF16)
            g = jnp.dot(x, wg_b[...], preferred_element_type=F32)
            u = jnp.dot(x, wu_b[...], preferred_element_type=F32)
            a = (_silu(g) * u).astype(BF16)
            y = jnp.dot(a, wd_b[...], preferred_element_type=F32)
            for s, w in enumerate(_pack_row_words(y)):
                y4_ref[_word_plane(out_row, MOE_BM, s), :] = w

        @pl.when(i >= n_used)
        def _():
            y4_ref[pl.ds(out_row * ROW_WORDS, MOE_BM * ROW_WORDS), :] = jnp.zeros((MOE_BM * ROW_WORDS, LANES), U32)

    for sub in range(EXP_BLOCKS_PER_STEP):
        do_block(pl.program_id(0) * EXP_BLOCKS_PER_STEP + sub, sub * MOE_BM)


def _experts(block_e, n_used, next_e, ord_e, xs, wg, wu, wd):
    nblk = block_e.shape[0]
    bm = MOE_BM * EXP_BLOCKS_PER_STEP
    e, d, de = wg.shape

    grid_spec = pltpu.PrefetchScalarGridSpec(
        num_scalar_prefetch=4,
        grid=(nblk // EXP_BLOCKS_PER_STEP,),
        in_specs=[pl.BlockSpec(memory_space=pl.ANY),
                  pl.BlockSpec(memory_space=pl.ANY),
                  pl.BlockSpec(memory_space=pl.ANY),
                  pl.BlockSpec(memory_space=pl.ANY)],
        out_specs=pl.BlockSpec((bm * ROW_WORDS, LANES), lambda i, be, nu, nx, od: (i, 0)),
        scratch_shapes=[pltpu.VMEM((X_RING, MOE_BM * ROW_WORDS, LANES), U32), pltpu.SemaphoreType.DMA((X_RING,)),
                        pltpu.VMEM((2, d, de), F32), pltpu.VMEM((2, d, de), F32), pltpu.VMEM((2, de, d), F32),
                        pltpu.SemaphoreType.DMA((2, 3)),
                        pltpu.VMEM((d, de), BF16), pltpu.VMEM((d, de), BF16), pltpu.VMEM((de, d), BF16)],
    )
    return pl.pallas_call(
        _experts_body,
        out_shape=jax.ShapeDtypeStruct((nblk * MOE_BM * ROW_WORDS, LANES), U32),
        grid_spec=grid_spec,
        compiler_params=_cparams(("arbitrary",)),
        name="experts",
    )(block_e, n_used, next_e, ord_e, xs, wg, wu, wd)


def _combine_body(pos_cur, pos_nxt, y4_ref, h_ref, gate_ref, wsg_ref, wsu_ref, wsd_ref, g_ref, b_ref,
                  o_ref, buf, sem, racc):
    i = pl.program_id(0)
    nb = pl.num_programs(0)
    tm, d = h_ref.shape
    half = d // 2
    slot = i % 2

    def issue_rows(pos_ref, sl, t):
        for k in range(TOP_K):
            p = pos_ref[0, 0, t * TOP_K + k]
            pltpu.make_async_copy(_rows(y4_ref, p, 1), _rows(buf.at[sl], k * tm + t, 1),
                                  sem.at[sl]).start(priority=k % 2)

    @pl.when(i == 0)
    def _():
        def body(t, carry):
            issue_rows(pos_cur, 0, t)
            return carry
        lax.fori_loop(0, tm, body, 0)

    pltpu.make_async_copy(_rows(y4_ref, 0, tm * TOP_K), buf.at[slot], sem.at[slot]).wait()

    def combine_rows(t8):
        r0 = pl.multiple_of(t8 * SUBLANES, SUBLANES)
        gates = gate_ref[pl.ds(r0, SUBLANES), :]
        for s in range(ROW_WORDS):
            lo_acc = hi_acc = None
            for k in range(TOP_K):
                w = buf[slot, pl.ds((k * tm + r0) * ROW_WORDS + s, SUBLANES, stride=ROW_WORDS), :]
                lo, hi = _unpack_row_words(w)
                gk = gates[:, k:k + 1]
                lo_acc = gk * lo if lo_acc is None else lo_acc + gk * lo
                hi_acc = gk * hi if hi_acc is None else hi_acc + gk * hi
            racc[pl.ds(r0, SUBLANES), pl.ds(s * LANES, LANES)] = lo_acc
            racc[pl.ds(r0, SUBLANES), pl.ds(half + s * LANES, LANES)] = hi_acc

    @pl.when(i + 1 < nb)
    def _():
        def body(t8, carry):
            for dt in range(SUBLANES):
                issue_rows(pos_nxt, 1 - slot, t8 * SUBLANES + dt)
            combine_rows(t8)
            return carry
        lax.fori_loop(0, tm // SUBLANES, body, 0)

    @pl.when(i + 1 >= nb)
    def _():
        def body(t8, carry):
            combine_rows(t8)
            return carry
        lax.fori_loop(0, tm // SUBLANES, body, 0)

    h = h_ref[...]
    hb = h.astype(BF16)
    sg = jnp.dot(hb, wsg_ref[...], preferred_element_type=F32)
    su = jnp.dot(hb, wsu_ref[...], preferred_element_type=F32)
    shared = jnp.dot((_silu(sg) * su).astype(BF16), wsd_ref[...], preferred_element_type=F32)
    o_ref[...] = _layernorm(DEEPNORM_ALPHA * h + (racc[...] + shared), g_ref[...], b_ref[...])


def _combine(pos3, y4, h, gates, wsg, wsu, wsd, g, b):
    t, d = h.shape
    tm = COMB_TM
    nt = t // tm
    ds_ = wsg.shape[1]
    return pl.pallas_call(
        _combine_body,
        out_shape=jax.ShapeDtypeStruct((t, d), F32),
        grid=(nt,),
        in_specs=[pl.BlockSpec((1, 1, tm * TOP_K), lambda i: (i, 0, 0), memory_space=pltpu.SMEM),
                  pl.BlockSpec((1, 1, tm * TOP_K), lambda i: (jnp.minimum(i + 1, nt - 1), 0, 0),
                               memory_space=pltpu.SMEM),
                  pl.BlockSpec(memory_space=pl.ANY),
                  pl.BlockSpec((tm, d), lambda i: (i, 0)),
                  pl.BlockSpec((tm, TOP_K), lambda i: (i, 0)),
                  pl.BlockSpec((d, ds_), lambda i: (0, 0)),
                  pl.BlockSpec((d, ds_), lambda i: (0, 0)),
                  pl.BlockSpec((ds_, d), lambda i: (0, 0)),
                  pl.BlockSpec((1, d), lambda i: (0, 0)),
                  pl.BlockSpec((1, d), lambda i: (0, 0))],
        out_specs=pl.BlockSpec((tm, d), lambda i: (i, 0)),
        scratch_shapes=[pltpu.VMEM((2, tm * TOP_K * ROW_WORDS, LANES), U32),
                        pltpu.SemaphoreType.DMA((2,)),
                        pltpu.VMEM((tm, d), F32)],
        compiler_params=_cparams(("arbitrary",)),
        name="combine",
    )(pos3, pos3, y4, h, gates, wsg, wsu, wsd, g, b)


def _expert_tables(counts, nblk):
    bm = MOE_BM
    cnt = counts.reshape(N_EXPERTS).astype(I32)
    padded = (cnt + bm - 1) // bm * bm
    padded_end = jnp.cumsum(padded)
    padded_start = padded_end - padded
    block_rows = jnp.arange(nblk, dtype=I32) * bm
    block_e = jnp.sum((padded_end[None, :] <= block_rows[:, None]).astype(I32), axis=1)
    block_e = jnp.minimum(block_e, N_EXPERTS - 1)
    n_used = (padded_end[-1:] // bm).astype(I32)
    ids = jnp.where(cnt > 0, jnp.arange(N_EXPERTS, dtype=I32), N_EXPERTS)
    after = jnp.concatenate([lax.cummin(ids, reverse=True)[1:], jnp.full((1,), N_EXPERTS, I32)])
    next_e = jnp.where(after < N_EXPERTS, after, -1).astype(I32)
    ord_e = (jnp.cumsum((cnt > 0).astype(I32)) - 1).astype(I32)
    return padded_start, padded_start + cnt, padded - cnt, block_e, n_used, next_e, ord_e


def kernel(x, w_in, gla_gate_w2, gla_gate_b, gla_norm_w, pool_w_group, pool_scale, w_out, ln1_g, ln1_b,
           router_w, router_bias, w_exp_gate, w_exp_up, w_exp_down, w_sh_gate, w_sh_up, w_sh_down, ln2_g, ln2_b):
    batch, seq, d = x.shape
    t = batch * seq
    h2d = x.reshape(t, d)
    for l in range(DEPTH):
        d_in = w_in.shape[2]
        w_in_b = jnp.pad(w_in[l], ((0, 0), (0, D_IN_PAD - d_in))).astype(BF16)
        w2p = jnp.pad(gla_gate_w2[l], ((0, LANES - GLA_GATE_RANK), (0, 0))).astype(BF16)
        h, h4 = _front(h2d, batch, seq, w_in_b, w2p, gla_gate_b[l].reshape(1, -1), gla_norm_w[l].reshape(1, -1),
                       pool_w_group[l].astype(BF16), pool_scale[l].reshape(1, -1),
                       w_out[l].astype(BF16), ln1_g[l].reshape(1, -1), ln1_b[l].reshape(1, -1))
        idx_t, gate_t, rank_t, counts = _router(h, router_w[l].T, router_bias[l].reshape(-1, 1))
        nblk = (t * TOP_K + N_EXPERTS * (MOE_BM - 1)) // MOE_BM
        nblk = -(-nblk // EXP_BLOCKS_PER_STEP) * EXP_BLOCKS_PER_STEP
        start, pad_row, pad_n, block_e, n_used, next_e, ord_e = _expert_tables(counts, nblk)
        pos_t = _positions(idx_t, rank_t, start.astype(F32).reshape(-1, 1))
        pos_tok = pos_t.T
        xs = _dispatch(pad_row, pad_n, n_used, pos_tok.reshape(t // DISP_TM, 1, DISP_TM * TOP_K), h4, nblk * MOE_BM)
        y4 = _experts(block_e, n_used, next_e, ord_e, xs, w_exp_gate[l], w_exp_up[l], w_exp_down[l])
        h2d = _combine(pos_tok.reshape(t // COMB_TM, 1, COMB_TM * TOP_K), y4, h, gate_t.T,
                       w_sh_gate[l].astype(BF16), w_sh_up[l].astype(BF16), w_sh_down[l].astype(BF16),
                       ln2_g[l].reshape(1, -1), ln2_b[l].reshape(1, -1))
    return h2d.reshape(batch, seq, d)
```

```python
import jax
import jax.numpy as jnp
from jax import lax
from jax.experimental import pallas as pl
from jax.experimental.pallas import tpu as pltpu

F32 = jnp.float32
BF16 = jnp.bfloat16
I32 = jnp.int32
U32 = jnp.uint32
HIGH_HALF = 0xFFFF0000

POOL_WINDOWS = (2, 4, 8, 16)
POOL_GROUP_DIM = 128
POOL_WIDTH = 512
GLA_HEADS = 4
GLA_DK = 64
GLA_DV = 128
GLA_DK_TOTAL = 256
GLA_WIDTH = 512
GLA_GATE_RANK = 16
GLA_GATE_NORMALIZER = 16.0
GLA_CHUNK = 16
GLA_SAFE_EXP = 60.0
N_EXPERTS = 256
TOP_K = 8
N_GROUPS = 8
GROUP_SIZE = N_EXPERTS // N_GROUPS
TOPK_GROUPS = 4
ROUTED_SCALE = 2.5
DEPTH = 1
DEEPNORM_ALPHA = (2.0 * DEPTH) ** 0.25
LN_EPS = 1e-5
RMS_EPS = 1e-5

LANES = 128
SUBLANES = 8
VMEM_LIMIT = 56 * 1024 * 1024

MIX_TS = 512
ROUTE_TM = 512
MOE_BM = 512
COMB_TM = 512
POS_TM = 1024
DISP_TM = 1024
DISP_UNROLL = 4
ROW_WORDS = 4
X_RING = 4
EXP_BLOCKS_PER_STEP = 4
D_IN_PAD = 2176


def _cparams(sem):
    return pltpu.CompilerParams(dimension_semantics=sem, vmem_limit_bytes=VMEM_LIMIT)


def _silu(x):
    return x * (1.0 / (1.0 + jnp.exp(-x)))


def _layernorm(y, g, b):
    mu = jnp.mean(y, axis=-1, keepdims=True)
    yc = y - mu
    var = jnp.mean(yc * yc, axis=-1, keepdims=True)
    return yc * lax.rsqrt(var + LN_EPS) * g + b


def _mixer_body(p_ref, q_ref, k_ref, v_ref, r_ref, gl_ref, w2_ref, gb_ref, nw_ref, pw_ref, ps_ref,
                o_ref, pbuf, state, kvbuf, sall, obuf, gk_s):
    ts = p_ref.shape[0]
    s_idx = pl.program_id(1)
    halo = POOL_WINDOWS[-1]

    @pl.when(s_idx == 0)
    def _():
        pbuf[pl.ds(0, halo), :] = jnp.zeros((halo, POOL_WIDTH), F32)
        state[...] = jnp.zeros(state.shape, F32)

    p = p_ref[...]
    pbuf[pl.ds(halo, ts), :] = p
    pos = s_idx * ts + lax.broadcasted_iota(I32, (ts, 1), 0)
    for g, w in enumerate(POOL_WINDOWS):
        c0 = g * POOL_GROUP_DIM
        ext = pbuf[:, pl.ds(c0, POOL_GROUP_DIM)]
        sh = 1
        while sh < w:
            ext = ext + pltpu.roll(ext, sh, axis=0)
            sh *= 2
        acc = ext[halo:, :]
        cnt = jnp.minimum(pos + 1, w).astype(F32)
        mixed = acc / cnt - p[:, c0:c0 + POOL_GROUP_DIM]
        og = jnp.dot(mixed.astype(BF16), pw_ref[g], preferred_element_type=F32)
        o_ref[:, pl.ds(c0, POOL_GROUP_DIM)] = (og * ps_ref[:, pl.ds(c0, POOL_GROUP_DIM)]).astype(o_ref.dtype)
    pbuf[pl.ds(0, halo), :] = pbuf[pl.ds(ts, halo), :]

    nchunk = ts // GLA_CHUNK
    glog = jnp.dot(gl_ref[...].astype(BF16), w2_ref[...], preferred_element_type=F32) + gb_ref[...]
    gk = (jnp.minimum(glog, 0.0) - jnp.log(1.0 + jnp.exp(-jnp.abs(glog)))) * (1.0 / GLA_GATE_NORMALIZER)
    row = lax.broadcasted_iota(I32, (ts, 1), 0)
    rin = row % GLA_CHUNK
    b = gk
    sh = 1
    while sh < GLA_CHUNK:
        b = b + jnp.where(rin >= sh, pltpu.roll(b, sh, axis=0), 0.0)
        sh *= 2
    b3 = b.reshape(nchunk, GLA_CHUNK, GLA_DK_TOTAL)
    bmid = b3[:, GLA_CHUNK // 2 - 1:GLA_CHUNK // 2, :]
    blast = b3[:, GLA_CHUNK - 1:GLA_CHUNK, :]
    lane = lax.broadcasted_iota(I32, (1, LANES), 1)
    head_lane = [lane < GLA_DK, lane >= GLA_DK]
    q_scale = GLA_DK ** -0.5
    safe = jnp.max(jnp.abs(b3 - bmid)) <= GLA_SAFE_EXP

    @pl.when(safe)
    def _chunked():
        v = v_ref[...]
        vb = v.astype(BF16)
        q3 = (q_ref[...] * q_scale).reshape(nchunk, GLA_CHUNK, GLA_DK_TOTAL)
        k3 = k_ref[...].reshape(nchunk, GLA_CHUNK, GLA_DK_TOTAL)
        qs = (q3 * jnp.exp(b3 - bmid)).reshape(ts, GLA_DK_TOTAL)
        ks = (k3 * jnp.exp(bmid - b3)).reshape(ts, GLA_DK_TOTAL)
        qd = (q3 * jnp.exp(b3)).reshape(ts, GLA_DK_TOTAL)
        kd = (k3 * jnp.exp(blast - b3)).reshape(ts, GLA_DK_TOTAL)
        cdec = jnp.exp(blast).reshape(nchunk, GLA_DK_TOTAL)

        blk = LANES
        ri = lax.broadcasted_iota(I32, (blk, blk), 0)
        ci = lax.broadcasted_iota(I32, (blk, blk), 1)
        causal = (ri // GLA_CHUNK == ci // GLA_CHUNK) & (ri >= ci)
        o_intra = [[None] * (ts // blk) for _ in range(GLA_HEADS)]
        for rb in range(ts // blk):
            rs = slice(rb * blk, (rb + 1) * blk)
            for pair in range(GLA_HEADS // 2):
                ls = slice(pair * LANES, (pair + 1) * LANES)
                ks_p = ks[rs, ls].astype(BF16)
                q_p = qs[rs, ls]
                q2 = jnp.concatenate([jnp.where(head_lane[sub], q_p, 0.0) for sub in range(2)], axis=0).astype(BF16)
                sc2 = lax.dot_general(q2, ks_p, (((1,), (1,)), ((), ())), preferred_element_type=F32)
                for sub in range(2):
                    h = pair * 2 + sub
                    sc = jnp.where(causal, sc2[sub * blk:(sub + 1) * blk], 0.0).astype(BF16)
                    o_intra[h][rb] = jnp.dot(sc, vb[rs, h * GLA_DV:(h + 1) * GLA_DV], preferred_element_type=F32)

        cpb = blk // GLA_CHUNK
        chunk_of_col = lax.broadcasted_iota(I32, (1, blk), 1) // GLA_CHUNK
        kdb = kd.astype(BF16)
        for rb in range(ts // blk):
            rs = slice(rb * blk, (rb + 1) * blk)
            for pair in range(GLA_HEADS // 2):
                ls = slice(pair * LANES, (pair + 1) * LANES)
                lhs = []
                for sub in range(2):
                    h = pair * 2 + sub
                    v_t = v[rs, h * GLA_DV:(h + 1) * GLA_DV].T
                    lhs += [jnp.where(chunk_of_col == c, v_t, 0.0) for c in range(cpb)]
                both = jnp.dot(jnp.concatenate(lhs, axis=0).astype(BF16), kdb[rs, ls],
                               preferred_element_type=F32)
                stacked = jnp.where(head_lane[0], both[:cpb * LANES], both[cpb * LANES:])
                for c in range(cpb):
                    kvbuf[pair, rb * cpb + c] = stacked[c * LANES:(c + 1) * LANES]
        for pair in range(GLA_HEADS // 2):
            ls = slice(pair * LANES, (pair + 1) * LANES)
            st = state[pair]
            for c in range(nchunk):
                sall[pair, c] = st.astype(BF16)
                st = st * cdec[c:c + 1, ls] + kvbuf[pair, c]
            state[pair] = st
        o_inter = [[None] * nchunk for _ in range(GLA_HEADS)]
        for pair in range(GLA_HEADS // 2):
            ls = slice(pair * LANES, (pair + 1) * LANES)
            for c in range(nchunk):
                rs = slice(c * GLA_CHUNK, (c + 1) * GLA_CHUNK)
                q_c = qd[rs, ls]
                q2 = jnp.concatenate([jnp.where(head_lane[sub], q_c, 0.0) for sub in range(2)], axis=0).astype(BF16)
                res = lax.dot_general(q2, sall[pair, c], (((1,), (1,)), ((), ())), preferred_element_type=F32)
                for sub in range(2):
                    o_inter[pair * 2 + sub][c] = res[sub * GLA_CHUNK:(sub + 1) * GLA_CHUNK]
        for h in range(GLA_HEADS):
            obuf[:, pl.ds(h * GLA_DV, GLA_DV)] = (jnp.concatenate(o_intra[h], axis=0)
                                                  + jnp.concatenate(o_inter[h], axis=0))

    @pl.when(jnp.logical_not(safe))
    def _row_by_row():
        gk_s[...] = gk
        row8 = lax.broadcasted_iota(I32, (SUBLANES, 1), 0)

        def slab(i8, carry):
            r0 = pl.multiple_of(i8 * SUBLANES, SUBLANES)
            q8 = q_ref[pl.ds(r0, SUBLANES), :] * q_scale
            k8 = k_ref[pl.ds(r0, SUBLANES), :]
            v8 = v_ref[pl.ds(r0, SUBLANES), :]
            g8 = jnp.exp(gk_s[pl.ds(r0, SUBLANES), :])
            outs = [jnp.zeros((SUBLANES, GLA_DV), F32) for _ in range(GLA_HEADS)]
            for pair in range(GLA_HEADS // 2):
                ls = slice(pair * LANES, (pair + 1) * LANES)
                st = state[pair]
                for r in range(SUBLANES):
                    sel = row8 == r
                    k_r = jnp.where(sel, k8[:, ls], 0.0).astype(BF16)
                    inc = []
                    for sub in range(2):
                        h = pair * 2 + sub
                        v_r = jnp.where(sel, v8[:, h * GLA_DV:(h + 1) * GLA_DV], 0.0).astype(BF16)
                        inc.append(lax.dot_general(v_r, k_r, (((0,), (0,)), ((), ())), preferred_element_type=F32))
                    st = st * g8[r:r + 1, ls] + jnp.where(head_lane[0], inc[0], inc[1])
                    q2 = jnp.concatenate([jnp.where(sel & head_lane[sub], q8[:, ls], 0.0) for sub in range(2)],
                                         axis=0).astype(BF16)
                    res = lax.dot_general(q2, st.astype(BF16), (((1,), (1,)), ((), ())),
                                          preferred_element_type=F32)
                    for sub in range(2):
                        outs[pair * 2 + sub] = outs[pair * 2 + sub] + res[sub * SUBLANES:(sub + 1) * SUBLANES]
                state[pair] = st
            obuf[pl.ds(r0, SUBLANES), :] = jnp.concatenate(outs, axis=1)
            return carry
        lax.fori_loop(0, ts // SUBLANES, slab, 0)

    nw = nw_ref[...]
    r = r_ref[...]
    for h in range(GLA_HEADS):
        o = obuf[:, pl.ds(h * GLA_DV, GLA_DV)]
        o = o * lax.rsqrt(jnp.mean(o * o, axis=-1, keepdims=True) + RMS_EPS) * nw
        o = o * _silu(r[:, h * GLA_DV:(h + 1) * GLA_DV])
        o_ref[:, pl.ds(POOL_WIDTH + h * GLA_DV, GLA_DV)] = o.astype(o_ref.dtype)


def _front_body(x_ref, win_ref, w2_ref, gb_ref, nw_ref, pw_ref, ps_ref, wout_ref, g_ref, b_ref, h_ref, h4_ref,
                pp_s, pq_s, pk_s, pv_s, pr_s, pg_s, mix_s, pbuf, state, kvbuf, sall, obuf, gk_s):
    groups = (pp_s, pq_s, pk_s, pv_s, pr_s, pg_s)
    cols = [0]
    for ref in groups:
        cols.append(cols[-1] + ref.shape[1])
    xb = x_ref[...].astype(BF16)
    for gi in (0, 5, 1, 2, 3, 4):
        groups[gi][...] = jnp.dot(xb, win_ref[:, pl.ds(cols[gi], groups[gi].shape[1])],
                                  preferred_element_type=F32)
    _mixer_body(*groups, w2_ref, gb_ref, nw_ref, pw_ref, ps_ref, mix_s, pbuf, state, kvbuf, sall, obuf, gk_s)
    y = DEEPNORM_ALPHA * x_ref[...] + jnp.dot(mix_s[...], wout_ref[...], preferred_element_type=F32)
    h = _layernorm(y, g_ref[...], b_ref[...])
    h_ref[...] = h
    for s, w in enumerate(_pack_row_words(h)):
        h4_ref[_word_plane(0, h.shape[0], s), :] = w


def _front(x2d, batch, seq, w_in_b, w2p, gate_b, norm_w, pool_w, pool_scale, w_out_b, ln_g, ln_b):
    t, d = x2d.shape
    ts = MIX_TS
    nseq = seq // ts

    def full(shape):
        return pl.BlockSpec(shape, lambda bi, si: (0,) * len(shape))

    return pl.pallas_call(
        _front_body,
        out_shape=(jax.ShapeDtypeStruct((t, d), F32), jax.ShapeDtypeStruct((t * ROW_WORDS, LANES), U32)),
        grid=(batch, nseq),
        in_specs=[pl.BlockSpec((ts, d), lambda bi, si: (bi * nseq + si, 0)),
                  full(w_in_b.shape), full(w2p.shape), full(gate_b.shape), full(norm_w.shape),
                  full(pool_w.shape), full(pool_scale.shape), full(w_out_b.shape), full(ln_g.shape), full(ln_b.shape)],
        out_specs=(pl.BlockSpec((ts, d), lambda bi, si: (bi * nseq + si, 0)),
                   pl.BlockSpec((ts * ROW_WORDS, LANES), lambda bi, si: (bi * nseq + si, 0))),
        scratch_shapes=[pltpu.VMEM((ts, POOL_WIDTH), F32), pltpu.VMEM((ts, GLA_DK_TOTAL), F32),
                        pltpu.VMEM((ts, GLA_DK_TOTAL), F32), pltpu.VMEM((ts, GLA_WIDTH), F32),
                        pltpu.VMEM((ts, GLA_WIDTH), F32), pltpu.VMEM((ts, D_IN_PAD - 2 * GLA_DK_TOTAL
                                                                      - 2 * GLA_WIDTH - POOL_WIDTH), F32),
                        pltpu.VMEM((ts, POOL_WIDTH + GLA_WIDTH), BF16),
                        pltpu.VMEM((ts + POOL_WINDOWS[-1], POOL_WIDTH), F32),
                        pltpu.VMEM((GLA_HEADS // 2, LANES, GLA_DV), F32),
                        pltpu.VMEM((GLA_HEADS // 2, ts // GLA_CHUNK, LANES, GLA_DV), F32),
                        pltpu.VMEM((GLA_HEADS // 2, ts // GLA_CHUNK, LANES, GLA_DV), BF16),
                        pltpu.VMEM((ts, GLA_WIDTH), F32), pltpu.VMEM((ts, GLA_DK_TOTAL), F32)],
        compiler_params=_cparams(("arbitrary", "arbitrary")),
        name="front",
    )(x2d, w_in_b, w2p, gate_b, norm_w, pool_w, pool_scale, w_out_b, ln_g, ln_b)


def _pack_row_words(x):
    half = x.shape[1] // 2
    u = pltpu.bitcast(x.astype(BF16).astype(F32), U32)
    hi_mask = jnp.uint32(HIGH_HALF)
    return [(u[:, half + s * LANES:half + (s + 1) * LANES] & hi_mask) | (u[:, s * LANES:(s + 1) * LANES] >> 16)
            for s in range(ROW_WORDS)]


def _unpack_row_words(w):
    return pltpu.bitcast(w << 16, F32), pltpu.bitcast(w & jnp.uint32(HIGH_HALF), F32)


def _rows(ref, first, n, align=1):
    if isinstance(first, int):
        start = first * ROW_WORDS
    else:
        start = pl.multiple_of(first * ROW_WORDS, ROW_WORDS * align)
    return ref.at[pl.ds(start, n * ROW_WORDS), :]


def _word_plane(first, m, s):
    return pl.ds(first * ROW_WORDS + s, m, stride=ROW_WORDS)


def _first_argmax_rows(val, rowf, nrows):
    m = jnp.max(val, axis=0, keepdims=True)
    first = jnp.min(jnp.where(val == m, rowf, float(nrows)), axis=0, keepdims=True)
    return m, first, rowf == first


def _router_body(h_ref, whi_ref, wlo_ref, bias_ref, idx_ref, gate_ref, rank_ref, cnt_ref, carry):
    tm = h_ref.shape[0]
    i = pl.program_id(0)

    @pl.when(i == 0)
    def _():
        carry[...] = jnp.zeros(carry.shape, F32)

    h = h_ref[...]
    h_hi = h.astype(BF16)
    h_lo = (h - h_hi.astype(F32)).astype(BF16)
    nt = (((1,), (1,)), ((), ()))
    logits = (lax.dot_general(whi_ref[...], h_hi, nt, preferred_element_type=F32)
              + lax.dot_general(whi_ref[...], h_lo, nt, preferred_element_type=F32)
              + lax.dot_general(wlo_ref[...], h_hi, nt, preferred_element_type=F32))
    scores = 1.0 / (1.0 + jnp.exp(-logits))
    biased = scores + bias_ref[...]
    neg = -jnp.inf

    grp = biased.reshape(N_GROUPS, GROUP_SIZE, tm)
    gi = lax.broadcasted_iota(I32, (N_GROUPS, GROUP_SIZE, tm), 1).astype(F32)
    g1 = jnp.max(grp, axis=1, keepdims=True)
    f1 = jnp.min(jnp.where(grp == g1, gi, float(GROUP_SIZE)), axis=1, keepdims=True)
    g2 = jnp.max(jnp.where(gi == f1, neg, grp), axis=1, keepdims=True)
    gscore = (g1 + g2).reshape(N_GROUPS, tm)

    growf = lax.broadcasted_iota(I32, (N_GROUPS, tm), 0).astype(F32)
    gsel = jnp.zeros((N_GROUPS, tm), F32)
    gval = gscore
    for _ in range(TOPK_GROUPS):
        _, _, pick = _first_argmax_rows(gval, growf, N_GROUPS)
        gsel = jnp.where(pick, 1.0, gsel)
        gval = jnp.where(pick, neg, gval)
    emask = jnp.broadcast_to(gsel.reshape(N_GROUPS, 1, tm), (N_GROUPS, GROUP_SIZE, tm)).reshape(N_EXPERTS, tm)

    rowf = lax.broadcasted_iota(I32, (N_EXPERTS, tm), 0).astype(F32)
    val = jnp.where(emask > 0.0, biased, neg)
    onehot = jnp.zeros((N_EXPERTS, tm), F32)
    picks, idxs, ws = [], [], []
    for _ in range(TOP_K):
        _, first, pick = _first_argmax_rows(val, rowf, N_EXPERTS)
        picks.append(pick)
        idxs.append(first)
        ws.append(jnp.sum(jnp.where(pick, scores, 0.0), axis=0, keepdims=True))
        onehot = jnp.where(pick, 1.0, onehot)
        val = jnp.where(pick, neg, val)
    w = jnp.concatenate(ws, axis=0)
    gate_ref[...] = w / jnp.sum(w, axis=0, keepdims=True) * ROUTED_SCALE
    idx_ref[...] = jnp.concatenate(idxs, axis=0).astype(I32)

    ti = lax.broadcasted_iota(I32, (tm, tm), 0)
    tj = lax.broadcasted_iota(I32, (tm, tm), 1)
    upper = jnp.where(ti < tj, 1.0, 0.0).astype(BF16)
    prefix = jnp.dot(onehot.astype(BF16), upper, preferred_element_type=F32) + carry[...]
    ranks = [jnp.sum(jnp.where(pk, prefix, 0.0), axis=0, keepdims=True) for pk in picks]
    rank_ref[...] = jnp.concatenate(ranks, axis=0).astype(I32)
    carry[...] = carry[...] + jnp.sum(onehot, axis=1, keepdims=True)
    cnt_ref[...] = carry[...]


def _router(h, wt, bias_col):
    t, d = h.shape
    tm = ROUTE_TM
    wt_hi = wt.astype(BF16)
    wt_lo = (wt - wt_hi.astype(F32)).astype(BF16)
    return pl.pallas_call(
        _router_body,
        out_shape=(jax.ShapeDtypeStruct((TOP_K, t), I32), jax.ShapeDtypeStruct((TOP_K, t), F32),
                   jax.ShapeDtypeStruct((TOP_K, t), I32), jax.ShapeDtypeStruct((N_EXPERTS, 1), F32)),
        grid=(t // tm,),
        in_specs=[pl.BlockSpec((tm, d), lambda i: (i, 0)),
                  pl.BlockSpec((N_EXPERTS, d), lambda i: (0, 0)),
                  pl.BlockSpec((N_EXPERTS, d), lambda i: (0, 0)),
                  pl.BlockSpec((N_EXPERTS, 1), lambda i: (0, 0))],
        out_specs=(pl.BlockSpec((TOP_K, tm), lambda i: (0, i)),
                   pl.BlockSpec((TOP_K, tm), lambda i: (0, i)),
                   pl.BlockSpec((TOP_K, tm), lambda i: (0, i)),
                   pl.BlockSpec((N_EXPERTS, 1), lambda i: (0, 0))),
        scratch_shapes=[pltpu.VMEM((N_EXPERTS, 1), F32)],
        compiler_params=_cparams(("arbitrary",)),
        name="router",
    )(h, wt_hi, wt_lo, bias_col)


def _positions_body(idx_ref, rank_ref, start_ref, pos_ref):
    tm = idx_ref.shape[1]
    rowi = lax.broadcasted_iota(I32, (N_EXPERTS, tm), 0)
    start = start_ref[...]
    idx = idx_ref[...]
    rows = [jnp.sum(jnp.where(rowi == idx[k:k + 1, :], start, 0.0), axis=0, keepdims=True) for k in range(TOP_K)]
    pos_ref[...] = jnp.concatenate(rows, axis=0).astype(I32) + rank_ref[...]


def _positions(idx_t, rank_t, start_col):
    t = idx_t.shape[1]
    tm = POS_TM
    return pl.pallas_call(
        _positions_body,
        out_shape=jax.ShapeDtypeStruct((TOP_K, t), I32),
        grid=(t // tm,),
        in_specs=[pl.BlockSpec((TOP_K, tm), lambda i: (0, i)),
                  pl.BlockSpec((TOP_K, tm), lambda i: (0, i)),
                  pl.BlockSpec((N_EXPERTS, 1), lambda i: (0, 0))],
        out_specs=pl.BlockSpec((TOP_K, tm), lambda i: (0, i)),
        compiler_params=_cparams(("arbitrary",)),
        name="positions",
    )(idx_t, rank_t, start_col)


def _pad_fill_copy(zeros, xs_ref, sem, row, nrows):
    return pltpu.make_async_copy(_rows(zeros, 0, nrows), _rows(xs_ref, row, nrows), sem)


def _dispatch_body(pad_row_ref, pad_n_ref, nu_ref, pos_ref, h4_ref, xs_ref, zeros, sem, pad_sem):
    i = pl.program_id(0)
    tm = h4_ref.shape[0] // ROW_WORDS
    half = MOE_BM // 2
    pad_bits = [1 << j for j in range(MOE_BM.bit_length() - 1)]
    n_half_blocks = xs_ref.shape[0] // (half * ROW_WORDS)

    def pad_pass(wait):
        def go(cp):
            if wait:
                cp.wait()
            else:
                cp.start()

        def body(e, carry):
            row = pad_row_ref[e]
            n = pad_n_ref[e]
            for bit in pad_bits:
                @pl.when((n & bit) != 0)
                def _():
                    go(_pad_fill_copy(zeros, xs_ref, pad_sem, row + (n & (bit - 1)), bit))
            return carry
        lax.fori_loop(0, N_EXPERTS, body, 0)

        def tail(hb, carry):
            go(_pad_fill_copy(zeros, xs_ref, pad_sem, hb * half, half))
            return carry
        lax.fori_loop(nu_ref[0] * 2, n_half_blocks, tail, 0)

    @pl.when(i == 0)
    def _():
        zeros[...] = jnp.zeros(zeros.shape, U32)
        pad_pass(False)

    def body(tt, carry):
        for dt in range(DISP_UNROLL):
            t = tt * DISP_UNROLL + dt
            for k in range(TOP_K):
                p = pos_ref[0, 0, t * TOP_K + k]
                pltpu.make_async_copy(_rows(h4_ref, t, 1), _rows(xs_ref, p, 1), sem).start(priority=k % 2)
        return carry
    lax.fori_loop(0, tm // DISP_UNROLL, body, 0)

    for k in range(TOP_K):
        pltpu.make_async_copy(h4_ref, _rows(xs_ref, 0, tm), sem).wait()

    @pl.when(i == pl.num_programs(0) - 1)
    def _():
        pad_pass(True)


def _dispatch(pad_row, pad_n, n_used, pos3, h4, n_rows):
    nt, _, per_step = pos3.shape
    tm = per_step // TOP_K
    grid_spec = pltpu.PrefetchScalarGridSpec(
        num_scalar_prefetch=3,
        grid=(nt,),
        in_specs=[pl.BlockSpec((1, 1, per_step), lambda i, a, b, c: (i, 0, 0), memory_space=pltpu.SMEM),
                  pl.BlockSpec((tm * ROW_WORDS, LANES), lambda i, a, b, c: (i, 0))],
        out_specs=pl.BlockSpec(memory_space=pl.ANY),
        scratch_shapes=[pltpu.VMEM((MOE_BM // 2 * ROW_WORDS, LANES), U32),
                        pltpu.SemaphoreType.DMA, pltpu.SemaphoreType.DMA],
    )
    return pl.pallas_call(
        _dispatch_body,
        out_shape=jax.ShapeDtypeStruct((n_rows * ROW_WORDS, LANES), U32),
        grid_spec=grid_spec,
        compiler_params=_cparams(("arbitrary",)),
        name="dispatch",
    )(pad_row, pad_n, n_used, pos3, h4)


def _weight_copies(e, w_hbm, stage, sem, slot):
    return [pltpu.make_async_copy(w.at[e], st.at[slot], sem.at[slot, j])
            for j, (w, st) in enumerate(zip(w_hbm, stage))]


def _row_block_copy(xs_hbm, xbuf, xsem, j):
    slot = j % X_RING
    return pltpu.make_async_copy(_rows(xs_hbm, j * MOE_BM, MOE_BM, align=MOE_BM), xbuf.at[slot], xsem.at[slot])


def _experts_body(be_ref, nu_ref, nx_ref, ord_ref, xs_hbm, wg_hbm, wu_hbm, wd_hbm, y4_ref,
                  xbuf, xsem, sg, su, sd, wsem, wg_b, wu_b, wd_b):
    w_hbm = (wg_hbm, wu_hbm, wd_hbm)
    stage = (sg, su, sd)
    n_used = nu_ref[0]

    def do_block(i, out_row):
        @pl.when(i < n_used)
        def _():
            e = be_ref[i]
            prev_e = be_ref[jnp.maximum(i - 1, 0)]

            @pl.when(i == 0)
            def _():
                for j in range(X_RING - 1):
                    @pl.when(j < n_used)
                    def _():
                        _row_block_copy(xs_hbm, xbuf, xsem, j).start()

            @pl.when(i + X_RING - 1 < n_used)
            def _():
                _row_block_copy(xs_hbm, xbuf, xsem, i + X_RING - 1).start()

            @pl.when((i == 0) | (e != prev_e))
            def _():
                wslot = ord_ref[e] % 2
                n1 = nx_ref[e]
                n2 = jnp.where(n1 >= 0, nx_ref[jnp.maximum(n1, 0)], -1)

                @pl.when(i == 0)
                def _():
                    for cp in _weight_copies(e, w_hbm, stage, wsem, wslot):
                        cp.start()

                    @pl.when(n1 >= 0)
                    def _():
                        for cp in _weight_copies(n1, w_hbm, stage, wsem, 1 - wslot):
                            cp.start()
                for cp in _weight_copies(e, w_hbm, stage, wsem, wslot):
                    cp.wait()
                wg_b[...] = sg[wslot].astype(BF16)
                wu_b[...] = su[wslot].astype(BF16)
                wd_b[...] = sd[wslot].astype(BF16)

                @pl.when(n2 >= 0)
                def _():
                    for cp in _weight_copies(n2, w_hbm, stage, wsem, wslot):
                        cp.start()

            _row_block_copy(xs_hbm, xbuf, xsem, i).wait()
            slot = i % X_RING
            parts = [_unpack_row_words(xbuf[slot, _word_plane(0, MOE_BM, s), :]) for s in range(ROW_WORDS)]
            x = jnp.concatenate([p[0] for p in parts] + [p[1] for p in parts], axis=1).astype(BF16)
            g = jnp.dot(x, wg_b[...], preferred_element_type=F32)
            u = jnp.dot(x, wu_b[...], preferred_element_type=F32)
            a = (_silu(g) * u).astype(BF16)
            y = jnp.dot(a, wd_b[...], preferred_element_type=F32)
            for s, w in enumerate(_pack_row_words(y)):
                y4_ref[_word_plane(out_row, MOE_BM, s), :] = w

        @pl.when(i >= n_used)
        def _():
            y4_ref[pl.ds(out_row * ROW_WORDS, MOE_BM * ROW_WORDS), :] = jnp.zeros((MOE_BM * ROW_WORDS, LANES), U32)

    for sub in range(EXP_BLOCKS_PER_STEP):
        do_block(pl.program_id(0) * EXP_BLOCKS_PER_STEP + sub, sub * MOE_BM)


def _experts(block_e, n_used, next_e, ord_e, xs, wg, wu, wd):
    nblk = block_e.shape[0]
    bm = MOE_BM * EXP_BLOCKS_PER_STEP
    e, d, de = wg.shape

    grid_spec = pltpu.PrefetchScalarGridSpec(
        num_scalar_prefetch=4,
        grid=(nblk // EXP_BLOCKS_PER_STEP,),
        in_specs=[pl.BlockSpec(memory_space=pl.ANY),
                  pl.BlockSpec(memory_space=pl.ANY),
                  pl.BlockSpec(memory_space=pl.ANY),
                  pl.BlockSpec(memory_space=pl.ANY)],
        out_specs=pl.BlockSpec((bm * ROW_WORDS, LANES), lambda i, be, nu, nx, od: (i, 0)),
        scratch_shapes=[pltpu.VMEM((X_RING, MOE_BM * ROW_WORDS, LANES), U32), pltpu.SemaphoreType.DMA((X_RING,)),
                        pltpu.VMEM((2, d, de), F32), pltpu.VMEM((2, d, de), F32), pltpu.VMEM((2, de, d), F32),
                        pltpu.SemaphoreType.DMA((2, 3)),
                        pltpu.VMEM((d, de), BF16), pltpu.VMEM((d, de), BF16), pltpu.VMEM((de, d), BF16)],
    )
    return pl.pallas_call(
        _experts_body,
        out_shape=jax.ShapeDtypeStruct((nblk * MOE_BM * ROW_WORDS, LANES), U32),
        grid_spec=grid_spec,
        compiler_params=_cparams(("arbitrary",)),
        name="experts",
    )(block_e, n_used, next_e, ord_e, xs, wg, wu, wd)


def _combine_body(pos_cur, pos_nxt, y4_ref, h_ref, gate_ref, wsg_ref, wsu_ref, wsd_ref, g_ref, b_ref,
                  o_ref, buf, sem, racc):
    i = pl.program_id(0)
    nb = pl.num_programs(0)
    tm, d = h_ref.shape
    half = d // 2
    slot = i % 2

    def issue_rows(pos_ref, sl, t):
        for k in range(TOP_K):
            p = pos_ref[0, 0, t * TOP_K + k]
            pltpu.make_async_copy(_rows(y4_ref, p, 1), _rows(buf.at[sl], k * tm + t, 1),
                                  sem.at[sl]).start(priority=k % 2)

    @pl.when(i == 0)
    def _():
        def body(t, carry):
            issue_rows(pos_cur, 0, t)
            return carry
        lax.fori_loop(0, tm, body, 0)

    pltpu.make_async_copy(_rows(y4_ref, 0, tm * TOP_K), buf.at[slot], sem.at[slot]).wait()

    def combine_rows(t8):
        r0 = pl.multiple_of(t8 * SUBLANES, SUBLANES)
        gates = gate_ref[pl.ds(r0, SUBLANES), :]
        for s in range(ROW_WORDS):
            lo_acc = hi_acc = None
            for k in range(TOP_K):
                w = buf[slot, pl.ds((k * tm + r0) * ROW_WORDS + s, SUBLANES, stride=ROW_WORDS), :]
                lo, hi = _unpack_row_words(w)
                gk = gates[:, k:k + 1]
                lo_acc = gk * lo if lo_acc is None else lo_acc + gk * lo
                hi_acc = gk * hi if hi_acc is None else hi_acc + gk * hi
            racc[pl.ds(r0, SUBLANES), pl.ds(s * LANES, LANES)] = lo_acc
            racc[pl.ds(r0, SUBLANES), pl.ds(half + s * LANES, LANES)] = hi_acc

    @pl.when(i + 1 < nb)
    def _():
        def body(t8, carry):
            for dt in range(SUBLANES):
                issue_rows(pos_nxt, 1 - slot, t8 * SUBLANES + dt)
            combine_rows(t8)
            return carry
        lax.fori_loop(0, tm // SUBLANES, body, 0)

    @pl.when(i + 1 >= nb)
    def _():
        def body(t8, carry):
            combine_rows(t8)
            return carry
        lax.fori_loop(0, tm // SUBLANES, body, 0)

    h = h_ref[...]
    hb = h.astype(BF16)
    sg = jnp.dot(hb, wsg_ref[...], preferred_element_type=F32)
    su = jnp.dot(hb, wsu_ref[...], preferred_element_type=F32)
    shared = jnp.dot((_silu(sg) * su).astype(BF16), wsd_ref[...], preferred_element_type=F32)
    o_ref[...] = _layernorm(DEEPNORM_ALPHA * h + (racc[...] + shared), g_ref[...], b_ref[...])


def _combine(pos3, y4, h, gates, wsg, wsu, wsd, g, b):
    t, d = h.shape
    tm = COMB_TM
    nt = t // tm
    ds_ = wsg.shape[1]
    return pl.pallas_call(
        _combine_body,
        out_shape=jax.ShapeDtypeStruct((t, d), F32),
        grid=(nt,),
        in_specs=[pl.BlockSpec((1, 1, tm * TOP_K), lambda i: (i, 0, 0), memory_space=pltpu.SMEM),
                  pl.BlockSpec((1, 1, tm * TOP_K), lambda i: (jnp.minimum(i + 1, nt - 1), 0, 0),
                               memory_space=pltpu.SMEM),
                  pl.BlockSpec(memory_space=pl.ANY),
                  pl.BlockSpec((tm, d), lambda i: (i, 0)),
                  pl.BlockSpec((tm, TOP_K), lambda i: (i, 0)),
                  pl.BlockSpec((d, ds_), lambda i: (0, 0)),
                  pl.BlockSpec((d, ds_), lambda i: (0, 0)),
                  pl.BlockSpec((ds_, d), lambda i: (0, 0)),
                  pl.BlockSpec((1, d), lambda i: (0, 0)),
                  pl.BlockSpec((1, d), lambda i: (0, 0))],
        out_specs=pl.BlockSpec((tm, d), lambda i: (i, 0)),
        scratch_shapes=[pltpu.VMEM((2, tm * TOP_K * ROW_WORDS, LANES), U32),
                        pltpu.SemaphoreType.DMA((2,)),
                        pltpu.VMEM((tm, d), F32)],
        compiler_params=_cparams(("arbitrary",)),
        name="combine",
    )(pos3, pos3, y4, h, gates, wsg, wsu, wsd, g, b)


def _expert_tables(counts, nblk):
    bm = MOE_BM
    cnt = counts.reshape(N_EXPERTS).astype(I32)
    padded = (cnt + bm - 1) // bm * bm
    padded_end = jnp.cumsum(padded)
    padded_start = padded_end - padded
    block_rows = jnp.arange(nblk, dtype=I32) * bm
    block_e = jnp.sum((padded_end[None, :] <= block_rows[:, None]).astype(I32), axis=1)
    block_e = jnp.minimum(block_e, N_EXPERTS - 1)
    n_used = (padded_end[-1:] // bm).astype(I32)
    ids = jnp.where(cnt > 0, jnp.arange(N_EXPERTS, dtype=I32), N_EXPERTS)
    after = jnp.concatenate([lax.cummin(ids, reverse=True)[1:], jnp.full((1,), N_EXPERTS, I32)])
    next_e = jnp.where(after < N_EXPERTS, after, -1).astype(I32)
    ord_e = (jnp.cumsum((cnt > 0).astype(I32)) - 1).astype(I32)
    return padded_start, padded_start + cnt, padded - cnt, block_e, n_used, next_e, ord_e


def kernel(x, w_in, gla_gate_w2, gla_gate_b, gla_norm_w, pool_w_group, pool_scale, w_out, ln1_g, ln1_b,
           router_w, router_bias, w_exp_gate, w_exp_up, w_exp_down, w_sh_gate, w_sh_up, w_sh_down, ln2_g, ln2_b):
    batch, seq, d = x.shape
    t = batch * seq
    h2d = x.reshape(t, d)
    for l in range(DEPTH):
        d_in = w_in.shape[2]
        w_in_b = jnp.pad(w_in[l], ((0, 0), (0, D_IN_PAD - d_in))).astype(BF16)
        w2p = jnp.pad(gla_gate_w2[l], ((0, LANES - GLA_GATE_RANK), (0, 0))).astype(BF16)
        h, h4 = _front(h2d, batch, seq, w_in_b, w2p, gla_gate_b[l].reshape(1, -1), gla_norm_w[l].reshape(1, -1),
                       pool_w_group[l].astype(BF16), pool_scale[l].reshape(1, -1),
                       w_out[l].astype(BF16), ln1_g[l].reshape(1, -1), ln1_b[l].reshape(1, -1))
        idx_t, gate_t, rank_t, counts = _router(h, router_w[l].T, router_bias[l].reshape(-1, 1))
        nblk = (t * TOP_K + N_EXPERTS * (MOE_BM - 1)) // MOE_BM
        nblk = -(-nblk // EXP_BLOCKS_PER_STEP) * EXP_BLOCKS_PER_STEP
        start, pad_row, pad_n, block_e, n_used, next_e, ord_e = _expert_tables(counts, nblk)
        pos_t = _positions(idx_t, rank_t, start.astype(F32).reshape(-1, 1))
        pos_tok = pos_t.T
        xs = _dispatch(pad_row, pad_n, n_used, pos_tok.reshape(t // DISP_TM, 1, DISP_TM * TOP_K), h4, nblk * MOE_BM)
        y4 = _experts(block_e, n_used, next_e, ord_e, xs, w_exp_gate[l], w_exp_up[l], w_exp_down[l])
        h2d = _combine(pos_tok.reshape(t // COMB_TM, 1, COMB_TM * TOP_K), y4, h, gate_t.T,
                       w_sh_gate[l].astype(BF16), w_sh_up[l].astype(BF16), w_sh_down[l].astype(BF16),
                       ln2_g[l].reshape(1, -1), ln2_b[l].reshape(1, -1))
    return h2d.reshape(batch, seq, d)
```
